```python
import jax, jax.numpy as jnp
from jax import lax
import numpy as np

D_MODEL = 1024
BATCH = 2
SEQ = 8192
DEPTH = 2

GRID_W = 64
CTX_LEN = 256
N_GROUPS = 4
GROUP_HEADS = 4
GROUP_WIDTH = D_MODEL // N_GROUPS
HEAD_DIM = GROUP_WIDTH // GROUP_HEADS
D_MIX = N_GROUPS * GROUP_WIDTH
A_KV_HEADS = 2
B_KV_HEADS = 2
Q_BLOCK = 128
WINDOW = 128
WIN_BLOCK = 128
MLSTM_CHUNK = 64
NA_KH = 8
NA_KW = 16
D_FF = ((8 * D_MODEL // 3 + 127) // 128) * 128
CONV_W = 3
ROPE_THETA = 10000.0
EPS = 1e-6
NEG_INF = -1e30
SPLIT_SIZES = (GROUP_WIDTH, A_KV_HEADS * HEAD_DIM, A_KV_HEADS * HEAD_DIM,
               GROUP_WIDTH, B_KV_HEADS * HEAD_DIM, B_KV_HEADS * HEAD_DIM,
               GROUP_WIDTH, GROUP_WIDTH, GROUP_WIDTH, GROUP_WIDTH, 4 * GROUP_HEADS,
               GROUP_WIDTH, GROUP_WIDTH, GROUP_WIDTH)
D_IN = sum(SPLIT_SIZES)

kernel_name = 'hybrid_parallel_heads_dit_block'


def rms_norm(x, g):
    xf = x.astype(jnp.float32)
    y = xf * lax.rsqrt(jnp.mean(xf * xf, axis=-1, keepdims=True) + EPS)
    return (y * g.astype(jnp.float32)).astype(x.dtype)


def to_heads(x, n):
    b, t, _ = x.shape
    return x.reshape(b, t, n, -1).transpose(0, 2, 1, 3)


def from_heads(x):
    b, h, t, d = x.shape
    return x.transpose(0, 2, 1, 3).reshape(b, t, h * d)


def split_proj(p):
    outs, start = [], 0
    for size in SPLIT_SIZES:
        outs.append(p[..., start:start + size])
        start += size
    return outs


def axial_angles(n_tokens):
    t = jnp.arange(n_tokens)
    row = (t // GRID_W).astype(jnp.float32)
    col = (t % GRID_W).astype(jnp.float32)
    n_freq = HEAD_DIM // 4
    freqs = ROPE_THETA ** (-jnp.arange(n_freq, dtype=jnp.float32) / n_freq)
    return row[:, None] * freqs, col[:, None] * freqs


def _rotate(x, ang):
    x1, x2 = jnp.split(x, 2, axis=-1)
    cos, sin = jnp.cos(ang).astype(x.dtype), jnp.sin(ang).astype(x.dtype)
    return jnp.concatenate([x1 * cos - x2 * sin, x1 * sin + x2 * cos], axis=-1)


def rope_2d(x, ang_r, ang_c):
    xr, xc = jnp.split(x, 2, axis=-1)
    return jnp.concatenate([_rotate(xr, ang_r), _rotate(xc, ang_c)], axis=-1)


def ctx_attend(qc, kc, vc, sink=None):
    b, hq, n, d = qc.shape
    kv = kc.shape[1]
    g = hq // kv
    qg = qc.reshape(b, kv, g, n, d)
    s = jnp.einsum('bkgqd,bksd->bkgqs', qg, kc).astype(jnp.float32) * d ** -0.5
    if sink is not None:
        sk = jnp.broadcast_to(sink.astype(jnp.float32).reshape(1, kv, g, 1, 1), (b, kv, g, n, 1))
        p = jax.nn.softmax(jnp.concatenate([s, sk], axis=-1), axis=-1)[..., :-1]
    else:
        p = jax.nn.softmax(s, axis=-1)
    o = jnp.einsum('bkgqs,bksd->bkgqd', p.astype(vc.dtype), vc)
    return o.reshape(b, hq, n, d)


def mixer_global(q, k, v, qc, kc, vc, gq, gk, ang_r, ang_c, need_ctx):
    q = rope_2d(rms_norm(q, gq), ang_r, ang_c)
    k = rope_2d(rms_norm(k, gk), ang_r, ang_c)
    qc, kc = rms_norm(qc, gq), rms_norm(kc, gk)
    b, h, t, d = q.shape
    kv = k.shape[1]
    g = h // kv
    nb = t // Q_BLOCK
    keys = jnp.concatenate([kc, k], axis=2)
    vals = jnp.concatenate([vc, v], axis=2)
    qb = q.reshape(b, kv, g, nb, Q_BLOCK, d).transpose(3, 0, 1, 2, 4, 5)

    def block(qi):
        s = jnp.einsum('bkgqd,bksd->bkgqs', qi, keys).astype(jnp.float32) * d ** -0.5
        p = jax.nn.softmax(s, axis=-1).astype(vals.dtype)
        return jnp.einsum('bkgqs,bksd->bkgqd', p, vals)

    o = lax.map(block, qb)
    o = o.transpose(1, 2, 3, 0, 4, 5).reshape(b, h, t, d)
    oc = ctx_attend(qc, kc, vc) if need_ctx else None
    return o, oc


def mixer_window(q, k, v, qc, kc, vc, sink, ang_r, ang_c, need_ctx):
    q = rope_2d(q, ang_r, ang_c)
    k = rope_2d(k, ang_r, ang_c)
    b, h, t, d = q.shape
    kv = k.shape[1]
    g = h // kv
    w = WIN_BLOCK
    nb = t // w

    def band(x):
        xb = x.reshape(b, kv, nb, w, d)
        xp = jnp.pad(xb, ((0, 0), (0, 0), (1, 1), (0, 0), (0, 0)))
        return jnp.concatenate([xp[:, :, :-2], xp[:, :, 1:-1], xp[:, :, 2:]], axis=3)

    kb, vb = band(k), band(v)
    qg = q.reshape(b, kv, g, nb, w, d)
    scale = d ** -0.5
    s_loc = jnp.einsum('bkgnqd,bknsd->bkgnqs', qg, kb).astype(jnp.float32) * scale
    q_pos = jnp.arange(nb)[:, None, None] * w + jnp.arange(w)[None, :, None]
    k_pos = jnp.arange(nb)[:, None, None] * w - w + jnp.arange(3 * w)[None, None, :]
    valid = (jnp.abs(k_pos - q_pos) <= WINDOW) & (k_pos >= 0) & (k_pos < t)
    s_loc = jnp.where(valid, s_loc, NEG_INF)
    s_ctx = jnp.einsum('bkgnqd,bkcd->bkgnqc', qg, kc).astype(jnp.float32) * scale
    sk = jnp.broadcast_to(sink.astype(jnp.float32).reshape(1, kv, g, 1, 1, 1), (b, kv, g, nb, w, 1))
    p = jax.nn.softmax(jnp.concatenate([s_loc, s_ctx, sk], axis=-1), axis=-1).astype(v.dtype)
    n_loc, n_ctx = 3 * w, kc.shape[2]
    o = (jnp.einsum('bkgnqs,bknsd->bkgnqd', p[..., :n_loc], vb)
         + jnp.einsum('bkgnqc,bkcd->bkgnqd', p[..., n_loc:n_loc + n_ctx], vc))
    o = o.reshape(b, h, t, d)
    oc = ctx_attend(qc, kc, vc, sink) if need_ctx else None
    return o, oc


def mlstm_chunkwise(q, k, v, log_i, log_f, state):
    b, h, t, d = q.shape
    nc = t // MLSTM_CHUNK
    lc = MLSTM_CHUNK

    def chunks(x):
        return jnp.moveaxis(x.reshape((b, h, nc, lc) + x.shape[3:]), 2, 0)

    causal = jnp.tril(jnp.ones((lc, lc), dtype=bool))

    def step(carry, inp):
        c_mat, n_vec, m = carry
        qb, kb, vb, li, lf = inp
        cum = jnp.cumsum(lf, axis=-1)
        logw = jnp.where(causal, cum[..., :, None] - cum[..., None, :] + li[..., None, :], NEG_INF)
        inter = cum + m[..., None]
        m_t = jnp.maximum(inter, logw.max(axis=-1))
        w_inter = jnp.exp(inter - m_t)
        s = jnp.einsum('bhtk,bhsk->bhts', qb, kb) * jnp.exp(logw - m_t[..., None])
        num = (jnp.einsum('bhts,bhsv->bhtv', s, vb)
               + w_inter[..., None] * jnp.einsum('bhvk,bhtk->bhtv', c_mat, qb))
        den = s.sum(axis=-1) + w_inter * jnp.einsum('bhk,bhtk->bht', n_vec, qb)
        hout = num / jnp.maximum(jnp.abs(den), jnp.exp(-m_t))[..., None]
        cum_end = cum[..., -1]
        log_end = cum_end[..., None] - cum + li
        m_new = jnp.maximum(cum_end + m, log_end.max(axis=-1))
        decay = jnp.exp(cum_end + m - m_new)
        w_end = jnp.exp(log_end - m_new[..., None])
        c_mat = decay[..., None, None] * c_mat + jnp.einsum('bhs,bhsv,bhsk->bhvk', w_end, vb, kb)
        n_vec = decay[..., None] * n_vec + jnp.einsum('bhs,bhsk->bhk', w_end, kb)
        return (c_mat, n_vec, m_new), hout

    state, hs = lax.scan(step, state, (chunks(q), chunks(k), chunks(v), chunks(log_i), chunks(log_f)))
    hs = jnp.moveaxis(hs, 0, 2).reshape(b, h, t, d)
    return hs, state


def mixer_mlstm(q, k, v, o, gates, qc, kc, vc, oc, gates_c, gate_bias, need_ctx):
    b, h, t, d = q.shape
    f32 = jnp.float32

    def prep_gates(g):
        g = (g + gate_bias).astype(f32)
        g = g.reshape(g.shape[0], g.shape[1], 4, h).transpose(2, 0, 3, 1)
        return g[0], jax.nn.log_sigmoid(g[1]), g[2], jax.nn.log_sigmoid(g[3])

    li_f, lf_f, li_b, lf_b = prep_gates(gates)
    lic_f, lfc_f, lic_b, lfc_b = prep_gates(gates_c)
    sc = d ** -0.5
    q32, k32, v32 = (q * sc).astype(f32), k.astype(f32), v.astype(f32)
    qc32, kc32, vc32 = (qc * sc).astype(f32), kc.astype(f32), vc.astype(f32)
    zero = (jnp.zeros((b, h, d, d), f32), jnp.zeros((b, h, d), f32), jnp.zeros((b, h), f32))

    def flip(x):
        return jnp.flip(x, axis=2)

    hc_f, st_f = mlstm_chunkwise(qc32, kc32, vc32, lic_f, lfc_f, zero)
    hc_b, st_b = mlstm_chunkwise(flip(qc32), flip(kc32), flip(vc32), flip(lic_b), flip(lfc_b), zero)
    h_f, _ = mlstm_chunkwise(q32, k32, v32, li_f, lf_f, st_f)
    h_b, _ = mlstm_chunkwise(flip(q32), flip(k32), flip(v32), flip(li_b), flip(lf_b), st_b)
    y = (jax.nn.sigmoid(o.astype(f32)) * (h_f + flip(h_b))).astype(v.dtype)
    yc = (jax.nn.sigmoid(oc.astype(f32)) * (hc_f + flip(hc_b))).astype(vc.dtype) if need_ctx else None
    return y, yc


def mixer_neighbourhood(q, k, v, qc, kc, vc, rpb, need_ctx):
    b, h, t, d = q.shape
    rows = t // GRID_W
    kh, kw = min(NA_KH, rows), NA_KW
    qg = q.reshape(b, h, rows, GRID_W, d)
    kg = k.reshape(b, h, rows, GRID_W, d)
    vg = v.reshape(b, h, rows, GRID_W, d)
    cols = jnp.arange(GRID_W)
    col_start = jnp.clip(cols - kw // 2, 0, GRID_W - kw)
    col_idx = col_start[:, None] + jnp.arange(kw)[None, :]
    dc_idx = col_idx - cols[:, None] + (NA_KW - 1)
    scale = d ** -0.5
    n_nb = kh * kw

    def row_fn(r):
        rs = jnp.clip(r - kh // 2, 0, rows - kh)
        kr = lax.dynamic_slice_in_dim(kg, rs, kh, axis=2)[:, :, :, col_idx]
        vr = lax.dynamic_slice_in_dim(vg, rs, kh, axis=2)[:, :, :, col_idx]
        qr = lax.dynamic_index_in_dim(qg, r, axis=2, keepdims=False)
        dr_idx = rs + jnp.arange(kh) - r + (NA_KH - 1)
        bias = rpb[:, dr_idx][:, :, dc_idx].transpose(0, 2, 1, 3)
        s_nb = jnp.einsum('bhwd,bhiwjd->bhwij', qr, kr).astype(jnp.float32) * scale + bias.astype(jnp.float32)
        s_c = jnp.einsum('bhwd,bhcd->bhwc', qr, kc).astype(jnp.float32) * scale
        p = jax.nn.softmax(jnp.concatenate([s_nb.reshape(b, h, GRID_W, n_nb), s_c], axis=-1), axis=-1)
        p = p.astype(v.dtype)
        p_nb = p[..., :n_nb].reshape(b, h, GRID_W, kh, kw)
        return (jnp.einsum('bhwij,bhiwjd->bhwd', p_nb, vr)
                + jnp.einsum('bhwc,bhcd->bhwd', p[..., n_nb:], vc))

    o = lax.map(row_fn, jnp.arange(rows))
    o = o.transpose(1, 2, 0, 3, 4).reshape(b, h, t, d)
    oc = ctx_attend(qc, kc, vc) if need_ctx else None
    return o, oc


def mix_out(ys, g_group, w_out):
    y = jnp.concatenate([from_heads(t) for t in ys], axis=-1)
    b, t, _ = y.shape
    y = rms_norm(y.reshape(b, t, N_GROUPS, GROUP_WIDTH), g_group.reshape(N_GROUPS, GROUP_WIDTH))
    return y.reshape(b, t, D_MIX) @ w_out


def dwconv_centred(x, w, bias):
    t = x.shape[1]
    xp = jnp.pad(x, ((0, 0), (1, 1), (0, 0)))
    return xp[:, :t] * w[0] + xp[:, 1:t + 1] * w[1] + xp[:, 2:] * w[2] + bias


def conv_ffn(h, w_up, conv_w, conv_b, w_down):
    u = dwconv_centred(h @ w_up, conv_w, conv_b)
    gate, val = jnp.split(u, 2, axis=-1)
    return (jax.nn.silu(gate) * val) @ w_down


def trunk_layer(x, ctx, mod_lat, mod_ctx, g1, g2, w_in, a_qg, a_kg, sink, c_gb, rpb, g_group,
                w_out, w_up, conv_w, conv_b, w_down, ang_r, ang_c, need_ctx):
    sh1, sc1, gt1, sh2, sc2, gt2 = jnp.split(mod_lat, 6, axis=-1)
    sh1c, sc1c, gt1c, sh2c, sc2c, gt2c = jnp.split(mod_ctx, 6, axis=-1)
    h = rms_norm(x, g1) * (1 + sc1) + sh1
    hc = rms_norm(ctx, g1) * (1 + sc1c) + sh1c
    aq, ak, av, bq, bk, bv, cq, ck, cv, co, cg, dq, dk, dv = split_proj(h @ w_in)
    aqc, akc, avc, bqc, bkc, bvc, cqc, ckc, cvc, coc, cgc, dqc, dkc, dvc = split_proj(hc @ w_in)
    nh = GROUP_HEADS
    ya, yac = mixer_global(to_heads(aq, nh), to_heads(ak, A_KV_HEADS), to_heads(av, A_KV_HEADS),
                           to_heads(aqc, nh), to_heads(akc, A_KV_HEADS), to_heads(avc, A_KV_HEADS),
                           a_qg, a_kg, ang_r, ang_c, need_ctx)
    yb, ybc = mixer_window(to_heads(bq, nh), to_heads(bk, B_KV_HEADS), to_heads(bv, B_KV_HEADS),
                           to_heads(bqc, nh), to_heads(bkc, B_KV_HEADS), to_heads(bvc, B_KV_HEADS),
                           sink, ang_r, ang_c, need_ctx)
    yc, ycc = mixer_mlstm(to_heads(cq, nh), to_heads(ck, nh), to_heads(cv, nh), to_heads(co, nh), cg,
                          to_heads(cqc, nh), to_heads(ckc, nh), to_heads(cvc, nh), to_heads(coc, nh), cgc,
                          c_gb, need_ctx)
    yd, ydc = mixer_neighbourhood(to_heads(dq, nh), to_heads(dk, nh), to_heads(dv, nh),
                                  to_heads(dqc, nh), to_heads(dkc, nh), to_heads(dvc, nh), rpb, need_ctx)
    x = x + gt1 * mix_out([ya, yb, yc, yd], g_group, w_out)
    h2 = rms_norm(x, g2) * (1 + sc2) + sh2
    x = x + gt2 * conv_ffn(h2, w_up, conv_w, conv_b, w_down)
    if need_ctx:
        ctx = ctx + gt1c * mix_out([yac, ybc, ycc, ydc], g_group, w_out)
        h2c = rms_norm(ctx, g2) * (1 + sc2c) + sh2c
        ctx = ctx + gt2c * conv_ffn(h2c, w_up, conv_w, conv_b, w_down)
    return x, ctx


def setup_inputs(seed: int = 0) -> dict:
    key = jax.random.key(seed)
    ks = jax.random.split(key, 24)
    nrm = jax.random.normal
    f32 = jnp.float32
    gate_noise = 0.1 * nrm(ks[10], (DEPTH, 4, GROUP_HEADS), f32)
    forget_bias = jnp.linspace(3.0, 6.0, GROUP_HEADS, dtype=f32)
    gate_bias = gate_noise + jnp.array([0.0, 1.0, 0.0, 1.0], f32)[None, :, None] * forget_bias[None, None, :]
    return {
        'x': nrm(ks[0], (BATCH, SEQ, D_MODEL), f32),
        'c': nrm(ks[1], (BATCH, D_MODEL), f32),
        'ctx': nrm(ks[2], (BATCH, CTX_LEN, D_MODEL), f32),
        'c_ctx': nrm(ks[3], (D_MODEL,), f32),
        'w_mod': nrm(ks[4], (DEPTH, D_MODEL, 6 * D_MODEL), f32) * (0.5 * D_MODEL ** -0.5),
        'b_mod': 0.02 * nrm(ks[5], (DEPTH, 6 * D_MODEL), f32),
        'g_norm1': 1.0 + 0.05 * nrm(ks[6], (DEPTH, D_MODEL), f32),
        'g_norm2': 1.0 + 0.05 * nrm(ks[7], (DEPTH, D_MODEL), f32),
        'w_in': nrm(ks[8], (DEPTH, D_MODEL, D_IN), f32) * D_MODEL ** -0.5,
        'a_q_gain': 1.0 + 0.05 * nrm(ks[9], (DEPTH, HEAD_DIM), f32),
        'a_k_gain': 1.0 + 0.05 * nrm(ks[11], (DEPTH, HEAD_DIM), f32),
        'b_sink': 0.5 * nrm(ks[12], (DEPTH, GROUP_HEADS), f32),
        'c_gate_bias': gate_bias.reshape(DEPTH, 4 * GROUP_HEADS),
        'd_rel_bias': 0.2 * nrm(ks[13], (DEPTH, GROUP_HEADS, 2 * NA_KH - 1, 2 * NA_KW - 1), f32),
        'g_group': 1.0 + 0.05 * nrm(ks[14], (DEPTH, D_MIX), f32),
        'w_out': nrm(ks[15], (DEPTH, D_MIX, D_MODEL), f32) * D_MIX ** -0.5,
        'w_up': nrm(ks[16], (DEPTH, D_MODEL, 2 * D_FF), f32) * D_MODEL ** -0.5,
        'conv_w': nrm(ks[17], (DEPTH, CONV_W, 2 * D_FF), f32) * CONV_W ** -0.5,
        'conv_b': 0.02 * nrm(ks[18], (DEPTH, 2 * D_FF), f32),
        'w_down': nrm(ks[19], (DEPTH, D_FF, D_MODEL), f32) * D_FF ** -0.5,
        'g_final': 1.0 + 0.05 * nrm(ks[20], (D_MODEL,), f32),
    }


def reference(x, c, ctx, c_ctx, w_mod, b_mod, g_norm1, g_norm2, w_in, a_q_gain, a_k_gain, b_sink,
              c_gate_bias, d_rel_bias, g_group, w_out, w_up, conv_w, conv_b, w_down, g_final):
    t = x.shape[1]
    ang_r, ang_c = axial_angles(t)
    for layer in range(DEPTH):
        mod_lat = (jax.nn.silu(c) @ w_mod[layer] + b_mod[layer])[:, None, :]
        mod_ctx = (jax.nn.silu(c_ctx) @ w_mod[layer] + b_mod[layer])[None, None, :]
        x, ctx = trunk_layer(x, ctx, mod_lat, mod_ctx, g_norm1[layer], g_norm2[layer], w_in[layer],
                             a_q_gain[layer], a_k_gain[layer], b_sink[layer], c_gate_bias[layer],
                             d_rel_bias[layer], g_group[layer], w_out[layer], w_up[layer],
                             conv_w[layer], conv_b[layer], w_down[layer], ang_r, ang_c,
                             layer < DEPTH - 1)
    return rms_norm(x, g_final)
```

```python
import functools

import jax
import jax.numpy as jnp
from jax import lax
from jax.experimental import pallas as pl
from jax.experimental.pallas import tpu as pltpu

F32 = jnp.float32
BF16 = jnp.bfloat16

N_HEADS = 4
HEAD_DIM = 64
GROUP_W = N_HEADS * HEAD_DIM
KV_W = 2 * HEAD_DIM
GRID_W = 64
WINDOW = 128
NA_KH = 8
NA_KW = 16
ROPE_THETA = 10000.0
EPS = 1e-6
NEG_INF = -1e30
LANES = 128
MLSTM_L = 256
FFN_CHUNK = 256
HALO = 8
MIB = 1024 * 1024


def _params(sem, vmem_mib):
    return pltpu.CompilerParams(dimension_semantics=sem, vmem_limit_bytes=vmem_mib * MIB)


def _dot(a, b):
    return jnp.dot(a, b, preferred_element_type=F32)


def _dot_nt(a, b):
    return lax.dot_general(a, b, (((1,), (1,)), ((), ())), preferred_element_type=F32)


def _const_spec(shape):
    return pl.BlockSpec(shape, lambda *_: (0,) * len(shape), pipeline_mode=pl.Buffered(1))


def _mod_kernel(c_ref, w_ref, b_ref, o_ref):
    c = c_ref[...]
    a = (c * jax.nn.sigmoid(c)).astype(BF16)
    o_ref[0] = _dot(a, w_ref[0].astype(BF16)) + b_ref[0]


def _modulation(c_rows, w_mod, b_mod):
    depth, d, n = w_mod.shape
    tn = 1536
    return pl.pallas_call(
        _mod_kernel,
        grid=(depth, n // tn),
        in_specs=[pl.BlockSpec((8, d), lambda l, j: (0, 0)),
                  pl.BlockSpec((1, d, tn), lambda l, j: (l, 0, j)),
                  pl.BlockSpec((1, 1, tn), lambda l, j: (l, 0, j))],
        out_specs=pl.BlockSpec((1, 8, tn), lambda l, j: (l, 0, j)),
        out_shape=jax.ShapeDtypeStruct((depth, 8, n), F32),
        compiler_params=_params(("parallel", "parallel"), 40),
        name="modulation",
    )(c_rows, w_mod, b_mod.reshape(depth, 1, n))


def _rms_rows(x):
    return x * lax.rsqrt(jnp.mean(x * x, axis=-1, keepdims=True) + EPS)


def _head_rms(z, pool, gain):
    z2 = z * z
    hi = z2.astype(BF16)
    lo = (z2 - hi.astype(F32)).astype(BF16)
    ms = _dot(hi, pool) + _dot(lo, pool)
    return z * lax.rsqrt(ms + EPS) * gain


def _rope(z, cos, sin):
    lane = lax.broadcasted_iota(jnp.int32, z.shape, 1)
    first = (lane % 32) < 16
    partner = jnp.where(first, pltpu.roll(z, LANES - 16, 1), pltpu.roll(z, 16, 1))
    return z * cos + partner * sin


def _inproj_kernel(x_ref, mod_ref, g1_ref, w_ref, pool_ref, gq_ref, gk_ref, cos_ref, sin_ref, gb_ref,
                   aq_ref, ak_ref, av_ref, bq_ref, bk_ref, bv_ref, cq_ref, ck_ref, cv_ref, co_ref, cg_ref,
                   dq_ref, dk_ref, dv_ref, *, d_model, rope):
    d = d_model
    mod = mod_ref[0]
    xn = _rms_rows(x_ref[0]) * g1_ref[...]
    hb = (xn * (1.0 + mod[:, d:2 * d]) + mod[:, 0:d]).astype(BF16)

    def proj(seg, width=GROUP_W):
        return _dot(hb, w_ref[:, seg * GROUP_W:seg * GROUP_W + width])

    def rot(z):
        if not rope:
            return z
        cos, sin = cos_ref[...], sin_ref[...]
        return jnp.concatenate([_rope(z[:, i:i + LANES], cos, sin) for i in range(0, z.shape[1], LANES)], axis=1)

    scale = HEAD_DIM ** -0.5
    pool = pool_ref[...]
    aq_ref[0] = (rot(_head_rms(proj(0), pool, gq_ref[...])) * scale).astype(BF16)
    akv = proj(1)
    ak_ref[0] = rot(_head_rms(akv[:, :KV_W], pool[:KV_W, :KV_W], gk_ref[...])).astype(BF16)
    av_ref[0] = akv[:, KV_W:].astype(BF16)
    bq_ref[0] = (rot(proj(2)) * scale).astype(BF16)
    bkv = proj(3)
    bk_ref[0] = rot(bkv[:, :KV_W]).astype(BF16)
    bv_ref[0] = bkv[:, KV_W:].astype(BF16)
    cq_ref[0] = (proj(4) * scale).astype(BF16)
    ck_ref[0] = proj(5).astype(BF16)
    cv_ref[0] = proj(6).astype(BF16)
    co_ref[0] = proj(7)
    dq_ref[0] = (proj(8) * scale).astype(BF16)
    dk_ref[0] = proj(9).astype(BF16)
    dv_ref[0] = proj(10).astype(BF16)
    cg_ref[0] = proj(11, LANES) + gb_ref[...]


def _in_proj(x, mod, mod_row, g1, w, pool, gq, gk, cos, sin, gbias, *, rope, tm):
    b, t, d = x.shape
    n_w = w.shape[1]
    widths = [(GROUP_W, BF16), (KV_W, BF16), (KV_W, BF16),
              (GROUP_W, BF16), (KV_W, BF16), (KV_W, BF16),
              (GROUP_W, BF16), (GROUP_W, BF16), (GROUP_W, BF16), (GROUP_W, F32), (LANES, F32),
              (GROUP_W, BF16), (GROUP_W, BF16), (GROUP_W, BF16)]
    row = lambda wd: pl.BlockSpec((1, wd), lambda bi, j: (0, 0))
    return pl.pallas_call(
        functools.partial(_inproj_kernel, d_model=d, rope=rope),
        grid=(b, t // tm),
        in_specs=[pl.BlockSpec((1, tm, d), lambda bi, j: (bi, j, 0)),
                  pl.BlockSpec((1, 1, 6 * d), lambda bi, j: (mod_row(bi), 0, 0)),
                  row(d),
                  _const_spec((d, n_w)),
                  _const_spec((GROUP_W, GROUP_W)),
                  row(GROUP_W), row(KV_W),
                  pl.BlockSpec((tm, LANES), lambda bi, j: (j, 0)),
                  pl.BlockSpec((tm, LANES), lambda bi, j: (j, 0)),
                  row(LANES)],
        out_specs=[pl.BlockSpec((1, tm, wd), lambda bi, j: (bi, j, 0)) for wd, _ in widths],
        out_shape=[jax.ShapeDtypeStruct((b, t, wd), dt) for wd, dt in widths],
        compiler_params=_params(("parallel", "parallel"), 48),
        name="in_proj",
    )(x, mod, g1, w, pool, gq, gk, cos, sin, gbias)


def _stack_gqa(q):
    qa, qb = q[:, :LANES], q[:, LANES:]
    left = lax.broadcasted_iota(jnp.int32, qa.shape, 1) < HEAD_DIM
    zero = jnp.zeros_like(qa)
    return jnp.concatenate([jnp.where(left, qa, zero), jnp.where(left, qb, zero),
                            jnp.where(left, zero, qa), jnp.where(left, zero, qb)], axis=0)


def _unstack_gqa(o, t):
    left = lax.broadcasted_iota(jnp.int32, (t, LANES), 1) < HEAD_DIM
    return jnp.concatenate([jnp.where(left, o[0:t], o[2 * t:3 * t]),
                            jnp.where(left, o[t:2 * t], o[3 * t:4 * t])], axis=1)


def _attn_a_kernel(q_ref, k_ref, v_ref, kc_ref, vc_ref, o_ref, m_ref, l_ref, acc_ref, *, tq, tk, n_kb):
    qs = _stack_gqa(q_ref[0])
    s = _dot_nt(qs, kc_ref[0])
    m0 = jnp.max(s, axis=-1, keepdims=True)
    p = jnp.exp(s - m0)
    m_ref[...] = m0
    l_ref[...] = jnp.sum(p, axis=-1, keepdims=True)
    acc_ref[...] = _dot(p.astype(BF16), vc_ref[0])

    def body(kb, carry):
        start = pl.multiple_of(kb * tk, tk)
        s = _dot_nt(qs, k_ref[0, pl.ds(start, tk), :])
        m_prev = m_ref[...]
        m_new = jnp.maximum(m_prev, jnp.max(s, axis=-1, keepdims=True))
        alpha = jnp.exp(m_prev - m_new)
        p = jnp.exp(s - m_new)
        l_ref[...] = alpha * l_ref[...] + jnp.sum(p, axis=-1, keepdims=True)
        acc_ref[...] = alpha * acc_ref[...] + _dot(p.astype(BF16), v_ref[0, pl.ds(start, tk), :])
        m_ref[...] = m_new
        return carry

    lax.fori_loop(0, n_kb, body, 0)
    o_ref[0] = _unstack_gqa(acc_ref[...] / l_ref[...], tq)


def _attn_global(q, k, v, kc, vc, *, tq, tk):
    b, t, _ = q.shape
    n_ctx = kc.shape[1]
    per_b = lambda n, wd: pl.BlockSpec((1, n, wd), lambda bi, i: (bi, 0, 0))
    return pl.pallas_call(
        functools.partial(_attn_a_kernel, tq=tq, tk=tk, n_kb=t // tk),
        grid=(b, t // tq),
        in_specs=[pl.BlockSpec((1, tq, GROUP_W), lambda bi, i: (bi, i, 0)),
                  per_b(t, KV_W), per_b(t, KV_W), per_b(n_ctx, KV_W), per_b(n_ctx, KV_W)],
        out_specs=pl.BlockSpec((1, tq, GROUP_W), lambda bi, i: (bi, i, 0)),
        out_shape=jax.ShapeDtypeStruct((b, t, GROUP_W), F32),
        scratch_shapes=[pltpu.VMEM((4 * tq, 1), F32), pltpu.VMEM((4 * tq, 1), F32),
                        pltpu.VMEM((4 * tq, LANES), F32)],
        compiler_params=_params(("parallel", "parallel"), 48),
        name="attn_global",
    )(q, k, v, kc, vc)


def _sink_column(sink_ref, t):
    row = lax.broadcasted_iota(jnp.int32, (4 * t, 1), 0)
    col = jnp.full((4 * t, 1), sink_ref[3], F32)
    for h in (2, 1, 0):
        col = jnp.where(row < (h + 1) * t, sink_ref[h], col)
    return col


def _attn_b_kernel(sink_ref, q_ref, k_ref, v_ref, kc_ref, vc_ref, o_ref, *, tq, t_total):
    span = tq + 2 * WINDOW
    q0 = pl.program_id(1) * tq
    ks = pl.multiple_of(jnp.clip(q0 - WINDOW, 0, t_total - span), WINDOW)
    qs = _stack_gqa(q_ref[0])
    s = _dot_nt(qs, k_ref[0, pl.ds(ks, span), :])
    q_pos = q0 + lax.broadcasted_iota(jnp.int32, s.shape, 0) % tq
    k_pos = ks + lax.broadcasted_iota(jnp.int32, s.shape, 1)
    s = jnp.where(jnp.abs(k_pos - q_pos) <= WINDOW, s, NEG_INF)
    sc = _dot_nt(qs, kc_ref[0])
    sink = _sink_column(sink_ref, tq)
    m = jnp.maximum(jnp.maximum(jnp.max(s, axis=-1, keepdims=True), jnp.max(sc, axis=-1, keepdims=True)), sink)
    p = jnp.exp(s - m)
    pc = jnp.exp(sc - m)
    l = jnp.sum(p, axis=-1, keepdims=True) + jnp.sum(pc, axis=-1, keepdims=True) + jnp.exp(sink - m)
    o = _dot(p.astype(BF16), v_ref[0, pl.ds(ks, span), :]) + _dot(pc.astype(BF16), vc_ref[0])
    o_ref[0] = _unstack_gqa(o / l, tq)


def _attn_window(sink, q, k, v, kc, vc, *, tq):
    b, t, _ = q.shape
    n_ctx = kc.shape[1]
    assert t >= tq + 2 * WINDOW
    per_b = lambda n, wd: pl.BlockSpec((1, n, wd), lambda bi, i: (bi, 0, 0))
    return pl.pallas_call(
        functools.partial(_attn_b_kernel, tq=tq, t_total=t),
        grid=(b, t // tq),
        in_specs=[pl.BlockSpec(memory_space=pltpu.SMEM),
                  pl.BlockSpec((1, tq, GROUP_W), lambda bi, i: (bi, i, 0)),
                  per_b(t, KV_W), per_b(t, KV_W), per_b(n_ctx, KV_W), per_b(n_ctx, KV_W)],
        out_specs=pl.BlockSpec((1, tq, GROUP_W), lambda bi, i: (bi, i, 0)),
        out_shape=jax.ShapeDtypeStruct((b, t, GROUP_W), F32),
        compiler_params=_params(("parallel", "parallel"), 48),
        name="attn_window",
    )(sink, q, k, v, kc, vc)


def _head_masks(shape):
    lane = lax.broadcasted_iota(jnp.int32, shape, 1)
    return [(lane >= h * HEAD_DIM) & (lane < (h + 1) * HEAD_DIM) for h in range(N_HEADS)]


def _stack_mha(q):
    zero = jnp.zeros_like(q)
    return jnp.concatenate([jnp.where(mk, q, zero) for mk in _head_masks(q.shape)], axis=0)


def _unstack_mha(o, t):
    masks = _head_masks((t, GROUP_W))
    out = jnp.where(masks[0], o[0:t], 0.0)
    for h in range(1, N_HEADS):
        out = jnp.where(masks[h], o[h * t:(h + 1) * t], out)
    return out


def _attn_d_kernel(q_ref, k_ref, v_ref, kc_ref, vc_ref, bias_ref, o_ref, *, rows_per_step, rows):
    kc = kc_ref[0]
    vc = vc_ref[0]
    span = NA_KH * GRID_W

    def body(rr, carry):
        r = pl.program_id(1) * rows_per_step + rr
        rs = jnp.clip(r - NA_KH // 2, 0, rows - NA_KH)
        k0 = pl.multiple_of(rs * GRID_W, GRID_W)
        q0 = pl.multiple_of(rr * GRID_W, GRID_W)
        qs = _stack_mha(q_ref[0, pl.ds(q0, GRID_W), :])
        s = _dot_nt(qs, k_ref[0, pl.ds(k0, span), :]) + bias_ref[r - rs]
        sc = _dot_nt(qs, kc)
        m = jnp.maximum(jnp.max(s, axis=-1, keepdims=True), jnp.max(sc, axis=-1, keepdims=True))
        p = jnp.exp(s - m)
        pc = jnp.exp(sc - m)
        l = jnp.sum(p, axis=-1, keepdims=True) + jnp.sum(pc, axis=-1, keepdims=True)
        o = _dot(p.astype(BF16), v_ref[0, pl.ds(k0, span), :]) + _dot(pc.astype(BF16), vc)
        o_ref[0, pl.ds(q0, GRID_W), :] = _unstack_mha(o / l, GRID_W)
        return carry

    lax.fori_loop(0, rows_per_step, body, 0)


def _attn_neighbour(q, k, v, kc, vc, bias, *, rows_per_step):
    b, t, _ = q.shape
    n_ctx = kc.shape[1]
    rows = t // GRID_W
    assert rows >= NA_KH and rows % rows_per_step == 0
    tq = rows_per_step * GRID_W
    per_b = lambda n: pl.BlockSpec((1, n, GROUP_W), lambda bi, i: (bi, 0, 0))
    return pl.pallas_call(
        functools.partial(_attn_d_kernel, rows_per_step=rows_per_step, rows=rows),
        grid=(b, t // tq),
        in_specs=[pl.BlockSpec((1, tq, GROUP_W), lambda bi, i: (bi, i, 0)),
                  per_b(t), per_b(t), per_b(n_ctx), per_b(n_ctx),
                  _const_spec(bias.shape)],
        out_specs=pl.BlockSpec((1, tq, GROUP_W), lambda bi, i: (bi, i, 0)),
        out_shape=jax.ShapeDtypeStruct((b, t, GROUP_W), F32),
        compiler_params=_params(("parallel", "parallel"), 48),
        name="attn_neighbour",
    )(q, k, v, kc, vc, bias)


def _neighbour_bias(rpb):
    w = jnp.arange(GRID_W)
    cs = jnp.clip(w - NA_KW // 2, 0, GRID_W - NA_KW)
    col = jnp.arange(GRID_W)
    valid = (col[None, :] >= cs[:, None]) & (col[None, :] < cs[:, None] + NA_KW)
    dc = jnp.clip(col[None, :] - w[:, None] + (NA_KW - 1), 0, 2 * NA_KW - 2)
    dr = jnp.arange(NA_KH)[None, :] - jnp.arange(NA_KH)[:, None] + (NA_KH - 1)
    tab = rpb[:, dr][:, :, :, dc]
    tab = jnp.where(valid[None, None, None], tab, NEG_INF)
    tab = tab.transpose(1, 0, 3, 2, 4)
    return tab.reshape(NA_KH, N_HEADS * GRID_W, NA_KH * GRID_W).astype(F32)


def _ctx_attn_kernel(sink_ref, aq_ref, ak_ref, av_ref, bq_ref, bk_ref, bv_ref, dq_ref, dk_ref, dv_ref,
                     ya_ref, yb_ref, yd_ref, *, n):
    def attend(qs, k, v, sink=None):
        s = _dot_nt(qs, k)
        m = jnp.max(s, axis=-1, keepdims=True)
        if sink is not None:
            m = jnp.maximum(m, sink)
        p = jnp.exp(s - m)
        l = jnp.sum(p, axis=-1, keepdims=True)
        if sink is not None:
            l = l + jnp.exp(sink - m)
        return _dot(p.astype(BF16), v) / l

    ya_ref[0] = _unstack_gqa(attend(_stack_gqa(aq_ref[0]), ak_ref[0], av_ref[0]), n)
    yb_ref[0] = _unstack_gqa(attend(_stack_gqa(bq_ref[0]), bk_ref[0], bv_ref[0], _sink_column(sink_ref, n)), n)
    yd_ref[0] = _unstack_mha(attend(_stack_mha(dq_ref[0]), dk_ref[0], dv_ref[0]), n)


def _ctx_attention(sink, aq, ak, av, bq, bk, bv, dq, dk, dv):
    b, n, _ = aq.shape
    spec = lambda wd: pl.BlockSpec((1, n, wd), lambda bi: (bi, 0, 0))
    return pl.pallas_call(
        functools.partial(_ctx_attn_kernel, n=n),
        grid=(b,),
        in_specs=[pl.BlockSpec(memory_space=pltpu.SMEM),
                  spec(GROUP_W), spec(KV_W), spec(KV_W), spec(GROUP_W), spec(KV_W), spec(KV_W),
                  spec(GROUP_W), spec(GROUP_W), spec(GROUP_W)],
        out_specs=[spec(GROUP_W)] * 3,
        out_shape=[jax.ShapeDtypeStruct((b, n, GROUP_W), F32)] * 3,
        compiler_params=_params(("parallel",), 32),
        name="ctx_attention",
    )(sink, aq, ak, av, bq, bk, bv, dq, dk, dv)


def _log_sigmoid(x):
    return jnp.minimum(x, 0.0) - jnp.log1p(jnp.exp(-jnp.abs(x)))


def _exact_dot_01(tri, x):
    x1 = x.astype(BF16)
    r1 = x - x1.astype(F32)
    x2 = r1.astype(BF16)
    x3 = (r1 - x2.astype(F32)).astype(BF16)
    return _dot(tri, x1) + _dot(tri, x2) + _dot(tri, x3)


def _mlstm_kernel(*refs, reverse, combine, chunk):
    if combine:
        (q_ref, k_ref, v_ref, g_ref, s0_ref, n0_ref, m0_ref, hf_ref, o_gate_ref,
         y_ref, s_out, n_out, m_out, s_ref, n_ref, m_ref) = refs
    else:
        (q_ref, k_ref, v_ref, g_ref, s0_ref, n0_ref, m0_ref,
         y_ref, s_out, n_out, m_out, s_ref, n_ref, m_ref) = refs
    L = chunk
    j = pl.program_id(1)

    @pl.when(j == 0)
    def _():
        s_ref[...] = s0_ref[0]
        n_ref[...] = n0_ref[0]
        m_ref[...] = m0_ref[0]

    q = q_ref[0]
    k = k_ref[0]
    v = v_ref[0]
    gates = g_ref[0]
    t_idx = lax.broadcasted_iota(jnp.int32, (L, L), 0)
    s_idx = lax.broadcasted_iota(jnp.int32, (L, L), 1)
    seen = (s_idx >= t_idx) if reverse else (s_idx <= t_idx)
    tri = jnp.where(seen, 1.0, 0.0).astype(BF16)
    cum = _exact_dot_01(tri, _log_sigmoid(gates))
    a = pltpu.roll(gates, 4, 1) - cum
    a_t = a.T
    end_row = 0 if reverse else L - 1
    lane_base = 8 if reverse else 0

    state_b = s_ref[...].astype(BF16)
    q_state = _dot(q, state_b)
    q_norm = _dot(q, n_ref[...].astype(BF16))
    m_all = m_ref[...]

    masks = _head_masks((L, GROUP_W))
    lane_n = lax.broadcasted_iota(jnp.int32, (L, LANES), 1)
    lane_row = lax.broadcasted_iota(jnp.int32, (1, GROUP_W), 1)
    lane_row_n = lax.broadcasted_iota(jnp.int32, (1, LANES), 1)
    zero_q = jnp.zeros_like(q)
    out = jnp.zeros((L, GROUP_W), F32)
    w_full = jnp.zeros((L, GROUP_W), F32)
    w_cols = jnp.zeros((L, LANES), F32)
    decay_row = jnp.zeros((1, GROUP_W), F32)
    decay_row_n = jnp.zeros((1, LANES), F32)
    m_next = m_all
    for h in range(N_HEADS):
        c = lane_base + 4 + h
        cum_c = cum[:, c:c + 1]
        li_c = gates[:, lane_base + h:lane_base + h + 1]
        m_prev = m_all[0:1, h:h + 1]
        logw = jnp.where(seen, cum_c + a_t[c:c + 1, :], NEG_INF)
        inter = cum_c + m_prev
        m_t = jnp.maximum(inter, jnp.max(logw, axis=-1, keepdims=True))
        w_inter = jnp.exp(inter - m_t)
        sc = _dot_nt(jnp.where(masks[h], q, zero_q), k) * jnp.exp(logw - m_t)
        num = _dot(sc.astype(BF16), v) + w_inter * q_state
        den = jnp.sum(sc, axis=-1, keepdims=True) + w_inter * q_norm[:, h:h + 1]
        out = jnp.where(masks[h], num / jnp.maximum(jnp.abs(den), jnp.exp(-m_t)), out)
        cum_end = cum[end_row:end_row + 1, c:c + 1]
        log_end = cum_end - cum_c + li_c
        m_new = jnp.maximum(cum_end + m_prev, jnp.max(log_end, axis=0, keepdims=True))
        decay = jnp.exp(cum_end + m_prev - m_new)
        w_end = jnp.exp(log_end - m_new)
        w_full = jnp.where(masks[h], w_end, w_full)
        w_cols = jnp.where(lane_n == h, w_end, w_cols)
        in_head = (lane_row >= h * HEAD_DIM) & (lane_row < (h + 1) * HEAD_DIM)
        decay_row = jnp.where(in_head, decay, decay_row)
        decay_row_n = jnp.where(lane_row_n == h, decay, decay_row_n)
        m_next = jnp.where(lax.broadcasted_iota(jnp.int32, m_all.shape, 1) == h, m_new, m_next)

    k_t = k.astype(F32).T.astype(BF16)
    upd = _dot(k_t, jnp.concatenate([(v.astype(F32) * w_full).astype(BF16), w_cols.astype(BF16)], axis=1))
    row_head = lax.broadcasted_iota(jnp.int32, (GROUP_W, GROUP_W), 0) // HEAD_DIM
    col_head = lax.broadcasted_iota(jnp.int32, (GROUP_W, GROUP_W), 1) // HEAD_DIM
    s_ref[...] = jnp.where(row_head == col_head, s_ref[...] * decay_row + upd[:, :GROUP_W], 0.0)
    row_head_n = lax.broadcasted_iota(jnp.int32, (GROUP_W, LANES), 0) // HEAD_DIM
    col_n = lax.broadcasted_iota(jnp.int32, (GROUP_W, LANES), 1)
    n_ref[...] = jnp.where(row_head_n == col_n, n_ref[...] * decay_row_n + upd[:, GROUP_W:], 0.0)
    m_ref[...] = m_next

    if combine:
        y_ref[0] = jax.nn.sigmoid(o_gate_ref[0]) * (hf_ref[0] + out)
    else:
        y_ref[0] = out
    s_out[0] = s_ref[...]
    n_out[0] = n_ref[...]
    m_out[0] = m_ref[...]


def _mlstm_scan(q, k, v, gates, state, *, reverse, hf=None, o_gate=None):
    b, t, _ = q.shape
    L = MLSTM_L
    nc = t // L
    combine = hf is not None
    chunk_of = (lambda j: nc - 1 - j) if reverse else (lambda j: j)
    tok = lambda wd: pl.BlockSpec((1, L, wd), lambda bi, j: (bi, chunk_of(j), 0))
    st = lambda r, wd: pl.BlockSpec((1, r, wd), lambda bi, j: (bi, 0, 0))
    in_specs = [tok(GROUP_W), tok(GROUP_W), tok(GROUP_W), tok(LANES),
                st(GROUP_W, GROUP_W), st(GROUP_W, LANES), st(8, LANES)]
    args = [q, k, v, gates, *state]
    if combine:
        in_specs += [tok(GROUP_W), tok(GROUP_W)]
        args += [hf, o_gate]
    y, s_fin, n_fin, m_fin = pl.pallas_call(
        functools.partial(_mlstm_kernel, reverse=reverse, combine=combine, chunk=L),
        grid=(b, nc),
        in_specs=in_specs,
        out_specs=[tok(GROUP_W), st(GROUP_W, GROUP_W), st(GROUP_W, LANES), st(8, LANES)],
        out_shape=[jax.ShapeDtypeStruct((b, t, GROUP_W), F32),
                   jax.ShapeDtypeStruct((b, GROUP_W, GROUP_W), F32),
                   jax.ShapeDtypeStruct((b, GROUP_W, LANES), F32),
                   jax.ShapeDtypeStruct((b, 8, LANES), F32)],
        scratch_shapes=[pltpu.VMEM((GROUP_W, GROUP_W), F32), pltpu.VMEM((GROUP_W, LANES), F32),
                        pltpu.VMEM((8, LANES), F32)],
        compiler_params=_params(("parallel", "arbitrary"), 48),
        name="mlstm_bwd" if reverse else "mlstm_fwd",
    )(*args)
    return y, (s_fin, n_fin, m_fin)


def _outproj_kernel(x_ref, ya_ref, yb_ref, yc_ref, yd_ref, mod_ref, gg_ref, w_ref, o_ref, *, d_model):
    d = d_model
    parts = []
    for i, y_ref in enumerate((ya_ref, yb_ref, yc_ref, yd_ref)):
        parts.append((_rms_rows(y_ref[0]) * gg_ref[i:i + 1, :]).astype(BF16))
    res = _dot(jnp.concatenate(parts, axis=1), w_ref[...])
    o_ref[0] = x_ref[0] + mod_ref[0][:, 2 * d:3 * d] * res


def _out_proj(x, ys, mod, mod_row, gg, w, *, tm):
    b, t, d = x.shape
    tokd = pl.BlockSpec((1, tm, d), lambda bi, j: (bi, j, 0))
    tokg = pl.BlockSpec((1, tm, GROUP_W), lambda bi, j: (bi, j, 0))
    return pl.pallas_call(
        functools.partial(_outproj_kernel, d_model=d),
        grid=(b, t // tm),
        in_specs=[tokd, tokg, tokg, tokg, tokg,
                  pl.BlockSpec((1, 1, 6 * d), lambda bi, j: (mod_row(bi), 0, 0)),
                  _const_spec(gg.shape), _const_spec(w.shape)],
        out_specs=tokd,
        out_shape=jax.ShapeDtypeStruct((b, t, d), F32),
        compiler_params=_params(("parallel", "parallel"), 48),
        name="out_proj",
    )(x, *ys, mod, gg, w)


def _ffn_kernel(*refs, d_model, tm, n_chunks, final_norm):
    if final_norm:
        (x_ref, xp_ref, xn_ref, mod_ref, g2_ref, wug_ref, wuv_ref, cg_ref, cv_ref, wd_ref, gf_ref,
         o_ref, acc_ref) = refs
    else:
        (x_ref, xp_ref, xn_ref, mod_ref, g2_ref, wug_ref, wuv_ref, cg_ref, cv_ref, wd_ref,
         o_ref, acc_ref) = refs
    d = d_model
    j = pl.program_id(1)
    mod = mod_ref[0]
    g2 = g2_ref[...]

    def norm_mod(x):
        return _rms_rows(x) * g2 * (1.0 + mod[:, 4 * d:5 * d]) + mod[:, 3 * d:4 * d]

    keep_prev = (j > 0).astype(F32)
    keep_next = (j < pl.num_programs(1) - 1).astype(F32)
    x = x_ref[0]
    h_ext = jnp.concatenate([norm_mod(xp_ref[0]) * keep_prev, norm_mod(x), norm_mod(xn_ref[0]) * keep_next],
                            axis=0).astype(BF16)
    n_ext = tm + 2 * HALO
    acc_ref[...] = jnp.zeros_like(acc_ref)

    def conv(u, cw):
        y = pltpu.roll(u, 1, 0) * cw[0:1] + u * cw[1:2] + pltpu.roll(u, n_ext - 1, 0) * cw[2:3] + cw[3:4]
        return y[HALO:HALO + tm]

    def body(c, carry):
        gate = conv(_dot(h_ext, wug_ref[c]), cg_ref[c])
        val = conv(_dot(h_ext, wuv_ref[c]), cv_ref[c])
        act = (gate * jax.nn.sigmoid(gate) * val).astype(BF16)
        acc_ref[...] += _dot(act, wd_ref[c])
        return carry

    lax.fori_loop(0, n_chunks, body, 0)
    y = x + mod[:, 5 * d:6 * d] * acc_ref[...]
    if final_norm:
        y = _rms_rows(y) * gf_ref[...]
    o_ref[0] = y


def _conv_ffn(x, mod, mod_row, g2, wug, wuv, cg, cv, wd, g_final, *, tm):
    b, t, d = x.shape
    n_chunks = wug.shape[0]
    final_norm = g_final is not None
    hb = tm // HALO
    last = t // HALO - 1
    row = pl.BlockSpec((1, d), lambda bi, j: (0, 0))
    in_specs = [pl.BlockSpec((1, tm, d), lambda bi, j: (bi, j, 0)),
                pl.BlockSpec((1, HALO, d), lambda bi, j: (bi, jnp.maximum(j * hb - 1, 0), 0)),
                pl.BlockSpec((1, HALO, d), lambda bi, j: (bi, jnp.minimum((j + 1) * hb, last), 0)),
                pl.BlockSpec((1, 1, 6 * d), lambda bi, j: (mod_row(bi), 0, 0)),
                row,
                _const_spec(wug.shape), _const_spec(wuv.shape), _const_spec(cg.shape), _const_spec(cv.shape),
                _const_spec(wd.shape)]
    args = [x, x, x, mod, g2, wug, wuv, cg, cv, wd]
    if final_norm:
        in_specs.append(row)
        args.append(g_final)
    return pl.pallas_call(
        functools.partial(_ffn_kernel, d_model=d, tm=tm, n_chunks=n_chunks, final_norm=final_norm),
        grid=(b, t // tm),
        in_specs=in_specs,
        out_specs=pl.BlockSpec((1, tm, d), lambda bi, j: (bi, j, 0)),
        out_shape=jax.ShapeDtypeStruct((b, t, d), F32),
        scratch_shapes=[pltpu.VMEM((tm, d), F32)],
        compiler_params=_params(("parallel", "parallel"), 56),
        name="conv_ffn",
    )(*args)


_Q_HEAD_ORDER = (0, 2, 1, 3)


def _permute_heads(w, axis):
    parts = [lax.slice_in_dim(w, h * HEAD_DIM, (h + 1) * HEAD_DIM, axis=axis) for h in _Q_HEAD_ORDER]
    return jnp.concatenate(parts, axis=axis)


def _layout_w_in(w):
    g, kv = GROUP_W, KV_W
    o_b = g + 2 * kv
    o_c = 2 * o_b
    o_g = o_c + 4 * g
    o_d = o_g + 4 * N_HEADS
    sl = lambda a, n: w[:, a:a + n]
    pad = jnp.zeros((w.shape[0], LANES - 4 * N_HEADS), w.dtype)
    return jnp.concatenate([_permute_heads(sl(0, g), 1), sl(g, 2 * kv),
                            _permute_heads(sl(o_b, g), 1), sl(o_b + g, 2 * kv),
                            sl(o_c, 4 * g), sl(o_d, 3 * g), sl(o_g, 4 * N_HEADS), pad], axis=1).astype(BF16)


def _rope_tables(t):
    pos = jnp.arange(t)
    row = (pos // GRID_W).astype(F32)
    col = (pos % GRID_W).astype(F32)
    n_freq = HEAD_DIM // 4
    freqs = ROPE_THETA ** (-jnp.arange(n_freq, dtype=F32) / n_freq)
    ang_r, ang_c = row[:, None] * freqs, col[:, None] * freqs
    cos = jnp.concatenate([jnp.cos(ang_r), jnp.cos(ang_r), jnp.cos(ang_c), jnp.cos(ang_c)], axis=1)
    sin = jnp.concatenate([-jnp.sin(ang_r), jnp.sin(ang_r), -jnp.sin(ang_c), jnp.sin(ang_c)], axis=1)
    reps = LANES // HEAD_DIM
    return jnp.tile(cos, (1, reps)), jnp.tile(sin, (1, reps))


def _pick_tile(t, pref):
    while t % pref:
        pref //= 2
    return pref


def kernel(x, c, ctx, c_ctx, w_mod, b_mod, g_norm1, g_norm2, w_in, a_q_gain, a_k_gain, b_sink, c_gate_bias,
           d_rel_bias, g_group, w_out, w_up, conv_w, conv_b, w_down, g_final):
    batch, t, d = x.shape
    n_ctx = ctx.shape[1]
    depth = w_in.shape[0]
    d_ff = w_down.shape[1]
    n_chunks = d_ff // FFN_CHUNK
    assert batch < 8 and d_ff % FFN_CHUNK == 0 and t % MLSTM_L == 0 and n_ctx % MLSTM_L == 0

    c_rows = jnp.zeros((8, d), F32).at[:batch].set(c).at[batch].set(c_ctx)
    mod_all = _modulation(c_rows, w_mod, b_mod).reshape(depth, 8, 1, 6 * d)
    lat_row = lambda bi: bi
    ctx_row = lambda bi: batch

    cos, sin = _rope_tables(t)
    pool = jnp.where(jnp.arange(GROUP_W)[:, None] // HEAD_DIM == jnp.arange(GROUP_W)[None, :] // HEAD_DIM,
                     1.0 / HEAD_DIM, 0.0).astype(BF16)
    tm_lat = _pick_tile(t, 512)
    tm_ctx = _pick_tile(n_ctx, 256)
    zero_state = (jnp.zeros((batch, GROUP_W, GROUP_W), F32), jnp.zeros((batch, GROUP_W, LANES), F32),
                  jnp.zeros((batch, 8, LANES), F32))

    for layer in range(depth):
        need_ctx = layer < depth - 1
        mod = mod_all[layer]
        g1 = g_norm1[layer][None]
        g2 = g_norm2[layer][None]
        w_in_l = _layout_w_in(w_in[layer])
        gq = jnp.tile(a_q_gain[layer], N_HEADS)[None]
        gk = jnp.tile(a_k_gain[layer], 2)[None]
        gbias = jnp.pad(c_gate_bias[layer], (0, LANES - 4 * N_HEADS))[None]
        gg = g_group[layer].reshape(N_HEADS, GROUP_W)
        gg = jnp.concatenate([_permute_heads(gg[0:2], 1), gg[2:4]], axis=0)
        w_out_l = w_out[layer]
        w_out_l = jnp.concatenate([_permute_heads(w_out_l[0:GROUP_W], 0),
                                   _permute_heads(w_out_l[GROUP_W:2 * GROUP_W], 0),
                                   w_out_l[2 * GROUP_W:]], axis=0).astype(BF16)
        w_up_l = w_up[layer].astype(BF16).reshape(d, 2, n_chunks, FFN_CHUNK).transpose(1, 2, 0, 3)
        conv_l = jnp.concatenate([conv_w[layer], conv_b[layer][None],
                                  jnp.zeros((8 - 1 - conv_w.shape[1], 2 * d_ff), F32)], axis=0)
        conv_l = conv_l.reshape(8, 2, n_chunks, FFN_CHUNK).transpose(1, 2, 0, 3)
        w_down_l = w_down[layer].astype(BF16).reshape(n_chunks, FFN_CHUNK, d)
        sink = b_sink[layer]
        bias_tab = _neighbour_bias(d_rel_bias[layer])

        proj = functools.partial(_in_proj, g1=g1, w=w_in_l, pool=pool, gq=gq, gk=gk, gbias=gbias)
        (aqc, akc, avc, bqc, bkc, bvc, cqc, ckc, cvc, coc, cgc, dqc, dkc, dvc) = proj(
            ctx, mod, ctx_row, cos=cos, sin=sin, rope=False, tm=tm_ctx)
        (aq, ak, av, bq, bk, bv, cq, ck, cv, co, cg, dq, dk, dv) = proj(
            x, mod, lat_row, cos=cos, sin=sin, rope=True, tm=tm_lat)

        ya = _attn_global(aq, ak, av, akc, avc, tq=_pick_tile(t, 256), tk=_pick_tile(t, 512))
        yb = _attn_window(sink, bq, bk, bv, bkc, bvc, tq=_pick_tile(t, 256))
        yd = _attn_neighbour(dq, dk, dv, dkc, dvc, bias_tab, rows_per_step=8)

        hcf, st_f = _mlstm_scan(cqc, ckc, cvc, cgc, zero_state, reverse=False)
        ycc, st_b = _mlstm_scan(cqc, ckc, cvc, cgc, zero_state, reverse=True, hf=hcf, o_gate=coc)
        hf, _ = _mlstm_scan(cq, ck, cv, cg, st_f, reverse=False)
        yc, _ = _mlstm_scan(cq, ck, cv, cg, st_b, reverse=True, hf=hf, o_gate=co)

        x = _out_proj(x, (ya, yb, yc, yd), mod, lat_row, gg, w_out_l, tm=tm_lat)
        ffn = functools.partial(_conv_ffn, g2=g2, wug=w_up_l[0], wuv=w_up_l[1], cg=conv_l[0], cv=conv_l[1],
                                wd=w_down_l)
        x = ffn(x, mod, lat_row, g_final=None if need_ctx else g_final[None], tm=tm_lat)
        if need_ctx:
            yac, ybc, ydc = _ctx_attention(sink, aqc, akc, avc, bqc, bkc, bvc, dqc, dkc, dvc)
            ctx = _out_proj(ctx, (yac, ybc, ycc, ydc), mod, ctx_row, gg, w_out_l, tm=tm_ctx)
            ctx = ffn(ctx, mod, ctx_row, g_final=None, tm=tm_ctx)
    return x
```

```python
import functools

import jax
import jax.numpy as jnp
from jax import lax
from jax.experimental import pallas as pl
from jax.experimental.pallas import tpu as pltpu

F32 = jnp.float32
BF16 = jnp.bfloat16

N_HEADS = 4
HEAD_DIM = 64
GROUP_W = N_HEADS * HEAD_DIM
KV_W = 2 * HEAD_DIM
GRID_W = 64
WINDOW = 128
NA_KH = 8
NA_KW = 16
ROPE_THETA = 10000.0
EPS = 1e-6
NEG_INF = -1e30
LANES = 128
MLSTM_L = 256
FFN_CHUNK = 256
HALO = 8
MIB = 1024 * 1024
LOG2_E = 1.4426950408889634
MAX_UNSHIFTED_LOG2_SCORE = 60.0


def _params(sem, vmem_mib):
    return pltpu.CompilerParams(dimension_semantics=sem, vmem_limit_bytes=vmem_mib * MIB)


def _dot(a, b):
    return jnp.dot(a, b, preferred_element_type=F32)


def _dot_nt(a, b):
    return lax.dot_general(a, b, (((1,), (1,)), ((), ())), preferred_element_type=F32)


def _const_spec(shape):
    return pl.BlockSpec(shape, lambda *_: (0,) * len(shape), pipeline_mode=pl.Buffered(1))


def _mod_kernel(c_ref, w_ref, b_ref, o_ref):
    c = c_ref[...]
    a = (c * jax.nn.sigmoid(c)).astype(BF16)
    o_ref[0] = _dot(a, w_ref[0].astype(BF16)) + b_ref[0]


def _modulation(c_rows, w_mod, b_mod):
    depth, d, n = w_mod.shape
    tn = 1536
    return pl.pallas_call(
        _mod_kernel,
        grid=(depth, n // tn),
        in_specs=[pl.BlockSpec((8, d), lambda l, j: (0, 0)),
                  pl.BlockSpec((1, d, tn), lambda l, j: (l, 0, j)),
                  pl.BlockSpec((1, 1, tn), lambda l, j: (l, 0, j))],
        out_specs=pl.BlockSpec((1, 8, tn), lambda l, j: (l, 0, j)),
        out_shape=jax.ShapeDtypeStruct((depth, 8, n), F32),
        compiler_params=_params(("parallel", "parallel"), 40),
        name="modulation",
    )(c_rows, w_mod, b_mod.reshape(depth, 1, n))


def _rms_rows(x):
    return x * lax.rsqrt(jnp.mean(x * x, axis=-1, keepdims=True) + EPS)


def _head_rms(z, pool, gain):
    z2 = z * z
    hi = z2.astype(BF16)
    lo = (z2 - hi.astype(F32)).astype(BF16)
    ms = _dot(hi, pool) + _dot(lo, pool)
    return z * lax.rsqrt(ms + EPS) * gain


def _rope(z, cos, sin):
    lane = lax.broadcasted_iota(jnp.int32, z.shape, 1)
    first = (lane % 32) < 16
    partner = jnp.where(first, pltpu.roll(z, LANES - 16, 1), pltpu.roll(z, 16, 1))
    return z * cos + partner * sin


def _inproj_kernel(x_ref, mod_ref, g1_ref, w_ref, pool_ref, gq_ref, gk_ref, cos_ref, sin_ref, gb_ref,
                   aq_ref, ak_ref, av_ref, bq_ref, bk_ref, bv_ref, cq_ref, ck_ref, cv_ref, co_ref, cg_ref,
                   dq_ref, dk_ref, dv_ref, *, d_model, rope):
    d = d_model
    mod = mod_ref[0]
    xn = _rms_rows(x_ref[0]) * g1_ref[...]
    hb = (xn * (1.0 + mod[:, d:2 * d]) + mod[:, 0:d]).astype(BF16)

    def proj(seg, width=GROUP_W):
        return _dot(hb, w_ref[:, seg * GROUP_W:seg * GROUP_W + width])

    def rot(z):
        if not rope:
            return z
        cos, sin = cos_ref[...], sin_ref[...]
        return jnp.concatenate([_rope(z[:, i:i + LANES], cos, sin) for i in range(0, z.shape[1], LANES)], axis=1)

    scale = HEAD_DIM ** -0.5
    pool = pool_ref[...]
    aq_ref[0] = (rot(_head_rms(proj(0), pool, gq_ref[...])) * (scale * LOG2_E)).astype(BF16)
    akv = proj(1)
    ak_ref[0] = rot(_head_rms(akv[:, :KV_W], pool[:KV_W, :KV_W], gk_ref[...])).astype(BF16)
    av_ref[0] = jnp.concatenate([akv[:, KV_W:], jnp.ones((akv.shape[0], LANES), F32)], axis=1).astype(BF16)
    bq_ref[0] = (rot(proj(2)) * scale).astype(BF16)
    bkv = proj(3)
    bk_ref[0] = rot(bkv[:, :KV_W]).astype(BF16)
    bv_ref[0] = bkv[:, KV_W:].astype(BF16)
    cq_ref[0] = (proj(4) * scale).astype(BF16)
    ck_ref[0] = proj(5).astype(BF16)
    cv_ref[0] = proj(6).astype(BF16)
    co_ref[0] = proj(7)
    dq_ref[0] = (proj(8) * scale).astype(BF16)
    dk_ref[0] = proj(9).astype(BF16)
    dv_ref[0] = proj(10).astype(BF16)
    cg_ref[0] = proj(11, LANES) + gb_ref[...]


def _in_proj(x, mod, mod_row, g1, w, pool, gq, gk, cos, sin, gbias, *, rope, tm):
    b, t, d = x.shape
    n_w = w.shape[1]
    widths = [(GROUP_W, BF16), (KV_W, BF16), (2 * LANES, BF16),
              (GROUP_W, BF16), (KV_W, BF16), (KV_W, BF16),
              (GROUP_W, BF16), (GROUP_W, BF16), (GROUP_W, BF16), (GROUP_W, F32), (LANES, F32),
              (GROUP_W, BF16), (GROUP_W, BF16), (GROUP_W, BF16)]
    row = lambda wd: pl.BlockSpec((1, wd), lambda bi, j: (0, 0))
    return pl.pallas_call(
        functools.partial(_inproj_kernel, d_model=d, rope=rope),
        grid=(b, t // tm),
        in_specs=[pl.BlockSpec((1, tm, d), lambda bi, j: (bi, j, 0)),
                  pl.BlockSpec((1, 1, 6 * d), lambda bi, j: (mod_row(bi), 0, 0)),
                  row(d),
                  _const_spec((d, n_w)),
                  _const_spec((GROUP_W, GROUP_W)),
                  row(GROUP_W), row(KV_W),
                  pl.BlockSpec((tm, LANES), lambda bi, j: (j, 0)),
                  pl.BlockSpec((tm, LANES), lambda bi, j: (j, 0)),
                  row(LANES)],
        out_specs=[pl.BlockSpec((1, tm, wd), lambda bi, j: (bi, j, 0)) for wd, _ in widths],
        out_shape=[jax.ShapeDtypeStruct((b, t, wd), dt) for wd, dt in widths],
        compiler_params=_params(("parallel", "parallel"), 48),
        name="in_proj",
    )(x, mod, g1, w, pool, gq, gk, cos, sin, gbias)


def _stack_gqa(q):
    qa, qb = q[:, :LANES], q[:, LANES:]
    left = lax.broadcasted_iota(jnp.int32, qa.shape, 1) < HEAD_DIM
    zero = jnp.zeros_like(qa)
    return jnp.concatenate([jnp.where(left, qa, zero), jnp.where(left, qb, zero),
                            jnp.where(left, zero, qa), jnp.where(left, zero, qb)], axis=0)


def _unstack_gqa(o, t):
    left = lax.broadcasted_iota(jnp.int32, (t, LANES), 1) < HEAD_DIM
    return jnp.concatenate([jnp.where(left, o[0:t], o[2 * t:3 * t]),
                            jnp.where(left, o[t:2 * t], o[3 * t:4 * t])], axis=1)


def _attn_a_kernel(safe_ref, q_ref, k_ref, v_ref, kc_ref, vc_ref, o_ref, qs_ref, m_ref, acc_ref, *, tq, tk, n_kb):
    qs_ref[...] = _stack_gqa(q_ref[0])
    heads = [slice(h * tq, (h + 1) * tq) for h in range(N_HEADS)]

    @pl.when(safe_ref[0] != 0)
    def _():
        for rows in heads:
            p = jnp.exp2(_dot_nt(qs_ref[rows], kc_ref[0]))
            acc_ref[rows] = _dot(p.astype(BF16), vc_ref[0])

        def body(kb, carry):
            start = pl.multiple_of(kb * tk, tk)
            for rows in heads:
                p = jnp.exp2(_dot_nt(qs_ref[rows], k_ref[0, pl.ds(start, tk), :]))
                acc_ref[rows] += _dot(p.astype(BF16), v_ref[0, pl.ds(start, tk), :])
            return carry

        lax.fori_loop(0, n_kb, body, 0)

    @pl.when(safe_ref[0] == 0)
    def _():
        for rows in heads:
            s = _dot_nt(qs_ref[rows], kc_ref[0])
            m0 = jnp.max(s, axis=-1, keepdims=True)
            m_ref[rows] = m0
            acc_ref[rows] = _dot(jnp.exp2(s - m0).astype(BF16), vc_ref[0])

        def body(kb, carry):
            start = pl.multiple_of(kb * tk, tk)
            for rows in heads:
                s = _dot_nt(qs_ref[rows], k_ref[0, pl.ds(start, tk), :])
                m_prev = m_ref[rows]
                m_new = jnp.maximum(m_prev, jnp.max(s, axis=-1, keepdims=True))
                p = jnp.exp2(s - m_new)
                acc_ref[rows] = jnp.exp2(m_prev - m_new) * acc_ref[rows] + _dot(
                    p.astype(BF16), v_ref[0, pl.ds(start, tk), :])
                m_ref[rows] = m_new
            return carry

        lax.fori_loop(0, n_kb, body, 0)

    acc = acc_ref[...]
    o_ref[0] = _unstack_gqa(acc[:, :LANES] / acc[:, LANES:], tq)


def _attn_global(safe, q, k, v, kc, vc, *, tq, tk):
    b, t, _ = q.shape
    n_ctx = kc.shape[1]
    per_b = lambda n, wd: pl.BlockSpec((1, n, wd), lambda bi, i: (bi, 0, 0))
    return pl.pallas_call(
        functools.partial(_attn_a_kernel, tq=tq, tk=tk, n_kb=t // tk),
        grid=(b, t // tq),
        in_specs=[pl.BlockSpec(memory_space=pltpu.SMEM),
                  pl.BlockSpec((1, tq, GROUP_W), lambda bi, i: (bi, i, 0)),
                  per_b(t, KV_W), per_b(t, 2 * LANES), per_b(n_ctx, KV_W), per_b(n_ctx, 2 * LANES)],
        out_specs=pl.BlockSpec((1, tq, GROUP_W), lambda bi, i: (bi, i, 0)),
        out_shape=jax.ShapeDtypeStruct((b, t, GROUP_W), F32),
        scratch_shapes=[pltpu.VMEM((4 * tq, LANES), BF16), pltpu.VMEM((4 * tq, 1), F32),
                        pltpu.VMEM((4 * tq, 2 * LANES), F32)],
        compiler_params=_params(("parallel", "parallel"), 48),
        name="attn_global",
    )(safe, q, k, v, kc, vc)


def _sink_column(sink_ref, t):
    row = lax.broadcasted_iota(jnp.int32, (4 * t, 1), 0)
    col = jnp.full((4 * t, 1), sink_ref[3], F32)
    for h in (2, 1, 0):
        col = jnp.where(row < (h + 1) * t, sink_ref[h], col)
    return col


def _attn_b_kernel(sink_ref, q_ref, k_ref, v_ref, kc_ref, vc_ref, o_ref, *, tq, t_total):
    span = tq + 2 * WINDOW
    q0 = pl.program_id(1) * tq
    ks = pl.multiple_of(jnp.clip(q0 - WINDOW, 0, t_total - span), WINDOW)
    qs = _stack_gqa(q_ref[0])
    s = _dot_nt(qs, k_ref[0, pl.ds(ks, span), :])
    q_pos = q0 + lax.broadcasted_iota(jnp.int32, s.shape, 0) % tq
    k_pos = ks + lax.broadcasted_iota(jnp.int32, s.shape, 1)
    s = jnp.where(jnp.abs(k_pos - q_pos) <= WINDOW, s, NEG_INF)
    sc = _dot_nt(qs, kc_ref[0])
    sink = _sink_column(sink_ref, tq)
    m = jnp.maximum(jnp.maximum(jnp.max(s, axis=-1, keepdims=True), jnp.max(sc, axis=-1, keepdims=True)), sink)
    p = jnp.exp(s - m)
    pc = jnp.exp(sc - m)
    l = jnp.sum(p, axis=-1, keepdims=True) + jnp.sum(pc, axis=-1, keepdims=True) + jnp.exp(sink - m)
    o = _dot(p.astype(BF16), v_ref[0, pl.ds(ks, span), :]) + _dot(pc.astype(BF16), vc_ref[0])
    o_ref[0] = _unstack_gqa(o / l, tq)


def _attn_window(sink, q, k, v, kc, vc, *, tq):
    b, t, _ = q.shape
    n_ctx = kc.shape[1]
    assert t >= tq + 2 * WINDOW
    per_b = lambda n, wd: pl.BlockSpec((1, n, wd), lambda bi, i: (bi, 0, 0))
    return pl.pallas_call(
        functools.partial(_attn_b_kernel, tq=tq, t_total=t),
        grid=(b, t // tq),
        in_specs=[pl.BlockSpec(memory_space=pltpu.SMEM),
                  pl.BlockSpec((1, tq, GROUP_W), lambda bi, i: (bi, i, 0)),
                  per_b(t, KV_W), per_b(t, KV_W), per_b(n_ctx, KV_W), per_b(n_ctx, KV_W)],
        out_specs=pl.BlockSpec((1, tq, GROUP_W), lambda bi, i: (bi, i, 0)),
        out_shape=jax.ShapeDtypeStruct((b, t, GROUP_W), F32),
        compiler_params=_params(("parallel", "parallel"), 48),
        name="attn_window",
    )(sink, q, k, v, kc, vc)


def _head_masks(shape):
    lane = lax.broadcasted_iota(jnp.int32, shape, 1)
    return [(lane >= h * HEAD_DIM) & (lane < (h + 1) * HEAD_DIM) for h in range(N_HEADS)]


def _stack_mha(q):
    zero = jnp.zeros_like(q)
    return jnp.concatenate([jnp.where(mk, q, zero) for mk in _head_masks(q.shape)], axis=0)


def _unstack_mha(o, t):
    masks = _head_masks((t, GROUP_W))
    out = jnp.where(masks[0], o[0:t], 0.0)
    for h in range(1, N_HEADS):
        out = jnp.where(masks[h], o[h * t:(h + 1) * t], out)
    return out


def _attn_d_kernel(q_ref, k_ref, v_ref, kc_ref, vc_ref, bias_ref, o_ref, *, rows_per_step, rows):
    kc = kc_ref[0]
    vc = vc_ref[0]
    span = NA_KH * GRID_W

    def body(rr, carry):
        r = pl.program_id(1) * rows_per_step + rr
        rs = jnp.clip(r - NA_KH // 2, 0, rows - NA_KH)
        k0 = pl.multiple_of(rs * GRID_W, GRID_W)
        q0 = pl.multiple_of(rr * GRID_W, GRID_W)
        qs = _stack_mha(q_ref[0, pl.ds(q0, GRID_W), :])
        s = _dot_nt(qs, k_ref[0, pl.ds(k0, span), :]) + bias_ref[r - rs]
        sc = _dot_nt(qs, kc)
        m = jnp.maximum(jnp.max(s, axis=-1, keepdims=True), jnp.max(sc, axis=-1, keepdims=True))
        p = jnp.exp(s - m)
        pc = jnp.exp(sc - m)
        l = jnp.sum(p, axis=-1, keepdims=True) + jnp.sum(pc, axis=-1, keepdims=True)
        o = _dot(p.astype(BF16), v_ref[0, pl.ds(k0, span), :]) + _dot(pc.astype(BF16), vc)
        o_ref[0, pl.ds(q0, GRID_W), :] = _unstack_mha(o / l, GRID_W)
        return carry

    lax.fori_loop(0, rows_per_step, body, 0, unroll=2)


def _attn_neighbour(q, k, v, kc, vc, bias, *, rows_per_step):
    b, t, _ = q.shape
    n_ctx = kc.shape[1]
    rows = t // GRID_W
    assert rows >= NA_KH and rows % rows_per_step == 0
    tq = rows_per_step * GRID_W
    per_b = lambda n: pl.BlockSpec((1, n, GROUP_W), lambda bi, i: (bi, 0, 0))
    return pl.pallas_call(
        functools.partial(_attn_d_kernel, rows_per_step=rows_per_step, rows=rows),
        grid=(b, t // tq),
        in_specs=[pl.BlockSpec((1, tq, GROUP_W), lambda bi, i: (bi, i, 0)),
                  per_b(t), per_b(t), per_b(n_ctx), per_b(n_ctx),
                  _const_spec(bias.shape)],
        out_specs=pl.BlockSpec((1, tq, GROUP_W), lambda bi, i: (bi, i, 0)),
        out_shape=jax.ShapeDtypeStruct((b, t, GROUP_W), F32),
        compiler_params=_params(("parallel", "parallel"), 48),
        name="attn_neighbour",
    )(q, k, v, kc, vc, bias)


def _neighbour_bias(rpb):
    w = jnp.arange(GRID_W)
    cs = jnp.clip(w - NA_KW // 2, 0, GRID_W - NA_KW)
    col = jnp.arange(GRID_W)
    valid = (col[None, :] >= cs[:, None]) & (col[None, :] < cs[:, None] + NA_KW)
    dc = jnp.clip(col[None, :] - w[:, None] + (NA_KW - 1), 0, 2 * NA_KW - 2)
    dr = jnp.arange(NA_KH)[None, :] - jnp.arange(NA_KH)[:, None] + (NA_KH - 1)
    tab = rpb[:, dr][:, :, :, dc]
    tab = jnp.where(valid[None, None, None], tab, NEG_INF)
    tab = tab.transpose(1, 0, 3, 2, 4)
    return tab.reshape(NA_KH, N_HEADS * GRID_W, NA_KH * GRID_W).astype(F32)


def _ctx_attn_kernel(sink_ref, aq_ref, ak_ref, av_ref, bq_ref, bk_ref, bv_ref, dq_ref, dk_ref, dv_ref,
                     ya_ref, yb_ref, yd_ref, *, n):
    def attend(qs, k, v, sink=None):
        s = _dot_nt(qs, k)
        m = jnp.max(s, axis=-1, keepdims=True)
        if sink is not None:
            m = jnp.maximum(m, sink)
        p = jnp.exp(s - m)
        l = jnp.sum(p, axis=-1, keepdims=True)
        if sink is not None:
            l = l + jnp.exp(sink - m)
        return _dot(p.astype(BF16), v) / l

    s = _dot_nt(_stack_gqa(aq_ref[0]), ak_ref[0])
    acc = _dot(jnp.exp2(s - jnp.max(s, axis=-1, keepdims=True)).astype(BF16), av_ref[0])
    ya_ref[0] = _unstack_gqa(acc[:, :LANES] / acc[:, LANES:], n)
    yb_ref[0] = _unstack_gqa(attend(_stack_gqa(bq_ref[0]), bk_ref[0], bv_ref[0], _sink_column(sink_ref, n)), n)
    yd_ref[0] = _unstack_mha(attend(_stack_mha(dq_ref[0]), dk_ref[0], dv_ref[0]), n)


def _ctx_attention(sink, aq, ak, av, bq, bk, bv, dq, dk, dv):
    b, n, _ = aq.shape
    spec = lambda wd: pl.BlockSpec((1, n, wd), lambda bi: (bi, 0, 0))
    return pl.pallas_call(
        functools.partial(_ctx_attn_kernel, n=n),
        grid=(b,),
        in_specs=[pl.BlockSpec(memory_space=pltpu.SMEM),
                  spec(GROUP_W), spec(KV_W), spec(2 * LANES), spec(GROUP_W), spec(KV_W), spec(KV_W),
                  spec(GROUP_W), spec(GROUP_W), spec(GROUP_W)],
        out_specs=[spec(GROUP_W)] * 3,
        out_shape=[jax.ShapeDtypeStruct((b, n, GROUP_W), F32)] * 3,
        compiler_params=_params(("parallel",), 32),
        name="ctx_attention",
    )(sink, aq, ak, av, bq, bk, bv, dq, dk, dv)


def _log_sigmoid(x):
    return jnp.minimum(x, 0.0) - jnp.log1p(jnp.exp(-jnp.abs(x)))


def _exact_dot_01(tri, x):
    x1 = x.astype(BF16)
    r1 = x - x1.astype(F32)
    x2 = r1.astype(BF16)
    x3 = (r1 - x2.astype(F32)).astype(BF16)
    return _dot(tri, x1) + _dot(tri, x2) + _dot(tri, x3)


def _mlstm_kernel(*refs, reverse, combine, chunk):
    if combine:
        (q_ref, k_ref, v_ref, g_ref, s0_ref, n0_ref, m0_ref, hf_ref, o_gate_ref,
         y_ref, s_out, n_out, m_out, s_ref, n_ref, m_ref) = refs
    else:
        (q_ref, k_ref, v_ref, g_ref, s0_ref, n0_ref, m0_ref,
         y_ref, s_out, n_out, m_out, s_ref, n_ref, m_ref) = refs
    L = chunk
    j = pl.program_id(1)

    @pl.when(j == 0)
    def _():
        s_ref[...] = s0_ref[0]
        n_ref[...] = n0_ref[0]
        m_ref[...] = m0_ref[0]

    q = q_ref[0]
    k = k_ref[0]
    v = v_ref[0]
    gates = g_ref[0]
    t_idx = lax.broadcasted_iota(jnp.int32, (L, L), 0)
    s_idx = lax.broadcasted_iota(jnp.int32, (L, L), 1)
    seen = (s_idx >= t_idx) if reverse else (s_idx <= t_idx)
    tri = jnp.where(seen, 1.0, 0.0).astype(BF16)
    cum = _exact_dot_01(tri, _log_sigmoid(gates))
    a = pltpu.roll(gates, 4, 1) - cum
    a_t = a.T
    end_row = 0 if reverse else L - 1
    lane_base = 8 if reverse else 0

    state_b = s_ref[...].astype(BF16)
    q_state = _dot(q, state_b)
    q_norm = _dot(q, n_ref[...].astype(BF16))
    m_all = m_ref[...]

    masks = _head_masks((L, GROUP_W))
    lane_n = lax.broadcasted_iota(jnp.int32, (L, LANES), 1)
    lane_row = lax.broadcasted_iota(jnp.int32, (1, GROUP_W), 1)
    lane_row_n = lax.broadcasted_iota(jnp.int32, (1, LANES), 1)
    zero_q = jnp.zeros_like(q)
    out = jnp.zeros((L, GROUP_W), F32)
    w_full = jnp.zeros((L, GROUP_W), F32)
    w_cols = jnp.zeros((L, LANES), F32)
    decay_row = jnp.zeros((1, GROUP_W), F32)
    decay_row_n = jnp.zeros((1, LANES), F32)
    m_next = m_all
    for h in range(N_HEADS):
        c = lane_base + 4 + h
        cum_c = cum[:, c:c + 1]
        li_c = gates[:, lane_base + h:lane_base + h + 1]
        m_prev = m_all[0:1, h:h + 1]
        logw = jnp.where(seen, cum_c + a_t[c:c + 1, :], NEG_INF)
        inter = cum_c + m_prev
        m_t = jnp.maximum(inter, jnp.max(logw, axis=-1, keepdims=True))
        w_inter = jnp.exp(inter - m_t)
        sc = _dot_nt(jnp.where(masks[h], q, zero_q), k) * jnp.exp(logw - m_t)
        num = _dot(sc.astype(BF16), v) + w_inter * q_state
        den = jnp.sum(sc, axis=-1, keepdims=True) + w_inter * q_norm[:, h:h + 1]
        out = jnp.where(masks[h], num / jnp.maximum(jnp.abs(den), jnp.exp(-m_t)), out)
        cum_end = cum[end_row:end_row + 1, c:c + 1]
        log_end = cum_end - cum_c + li_c
        m_new = jnp.maximum(cum_end + m_prev, jnp.max(log_end, axis=0, keepdims=True))
        decay = jnp.exp(cum_end + m_prev - m_new)
        w_end = jnp.exp(log_end - m_new)
        w_full = jnp.where(masks[h], w_end, w_full)
        w_cols = jnp.where(lane_n == h, w_end, w_cols)
        in_head = (lane_row >= h * HEAD_DIM) & (lane_row < (h + 1) * HEAD_DIM)
        decay_row = jnp.where(in_head, decay, decay_row)
        decay_row_n = jnp.where(lane_row_n == h, decay, decay_row_n)
        m_next = jnp.where(lax.broadcasted_iota(jnp.int32, m_all.shape, 1) == h, m_new, m_next)

    k_t = k.astype(F32).T.astype(BF16)
    upd = _dot(k_t, jnp.concatenate([(v.astype(F32) * w_full).astype(BF16), w_cols.astype(BF16)], axis=1))
    row_head = lax.broadcasted_iota(jnp.int32, (GROUP_W, GROUP_W), 0) // HEAD_DIM
    col_head = lax.broadcasted_iota(jnp.int32, (GROUP_W, GROUP_W), 1) // HEAD_DIM
    s_ref[...] = jnp.where(row_head == col_head, s_ref[...] * decay_row + upd[:, :GROUP_W], 0.0)
    row_head_n = lax.broadcasted_iota(jnp.int32, (GROUP_W, LANES), 0) // HEAD_DIM
    col_n = lax.broadcasted_iota(jnp.int32, (GROUP_W, LANES), 1)
    n_ref[...] = jnp.where(row_head_n == col_n, n_ref[...] * decay_row_n + upd[:, GROUP_W:], 0.0)
    m_ref[...] = m_next

    if combine:
        y_ref[0] = jax.nn.sigmoid(o_gate_ref[0]) * (hf_ref[0] + out)
    else:
        y_ref[0] = out
    s_out[0] = s_ref[...]
    n_out[0] = n_ref[...]
    m_out[0] = m_ref[...]


def _mlstm_scan(q, k, v, gates, state, *, reverse, hf=None, o_gate=None):
    b, t, _ = q.shape
    L = MLSTM_L
    nc = t // L
    combine = hf is not None
    chunk_of = (lambda j: nc - 1 - j) if reverse else (lambda j: j)
    tok = lambda wd: pl.BlockSpec((1, L, wd), lambda bi, j: (bi, chunk_of(j), 0))
    st = lambda r, wd: pl.BlockSpec((1, r, wd), lambda bi, j: (bi, 0, 0))
    in_specs = [tok(GROUP_W), tok(GROUP_W), tok(GROUP_W), tok(LANES),
                st(GROUP_W, GROUP_W), st(GROUP_W, LANES), st(8, LANES)]
    args = [q, k, v, gates, *state]
    if combine:
        in_specs += [tok(GROUP_W), tok(GROUP_W)]
        args += [hf, o_gate]
    y, s_fin, n_fin, m_fin = pl.pallas_call(
        functools.partial(_mlstm_kernel, reverse=reverse, combine=combine, chunk=L),
        grid=(b, nc),
        in_specs=in_specs,
        out_specs=[tok(GROUP_W), st(GROUP_W, GROUP_W), st(GROUP_W, LANES), st(8, LANES)],
        out_shape=[jax.ShapeDtypeStruct((b, t, GROUP_W), F32),
                   jax.ShapeDtypeStruct((b, GROUP_W, GROUP_W), F32),
                   jax.ShapeDtypeStruct((b, GROUP_W, LANES), F32),
                   jax.ShapeDtypeStruct((b, 8, LANES), F32)],
        scratch_shapes=[pltpu.VMEM((GROUP_W, GROUP_W), F32), pltpu.VMEM((GROUP_W, LANES), F32),
                        pltpu.VMEM((8, LANES), F32)],
        compiler_params=_params(("parallel", "arbitrary"), 48),
        name="mlstm_bwd" if reverse else "mlstm_fwd",
    )(*args)
    return y, (s_fin, n_fin, m_fin)


def _outproj_kernel(x_ref, ya_ref, yb_ref, yc_ref, yd_ref, mod_ref, gg_ref, w_ref, o_ref, *, d_model):
    d = d_model
    parts = []
    for i, y_ref in enumerate((ya_ref, yb_ref, yc_ref, yd_ref)):
        parts.append((_rms_rows(y_ref[0]) * gg_ref[i:i + 1, :]).astype(BF16))
    res = _dot(jnp.concatenate(parts, axis=1), w_ref[...])
    o_ref[0] = x_ref[0] + mod_ref[0][:, 2 * d:3 * d] * res


def _out_proj(x, ys, mod, mod_row, gg, w, *, tm):
    b, t, d = x.shape
    tokd = pl.BlockSpec((1, tm, d), lambda bi, j: (bi, j, 0))
    tokg = pl.BlockSpec((1, tm, GROUP_W), lambda bi, j: (bi, j, 0))
    return pl.pallas_call(
        functools.partial(_outproj_kernel, d_model=d),
        grid=(b, t // tm),
        in_specs=[tokd, tokg, tokg, tokg, tokg,
                  pl.BlockSpec((1, 1, 6 * d), lambda bi, j: (mod_row(bi), 0, 0)),
                  _const_spec(gg.shape), _const_spec(w.shape)],
        out_specs=tokd,
        out_shape=jax.ShapeDtypeStruct((b, t, d), F32),
        compiler_params=_params(("parallel", "parallel"), 48),
        name="out_proj",
    )(x, *ys, mod, gg, w)


def _ffn_kernel(*refs, d_model, tm, n_chunks, final_norm):
    if final_norm:
        (x_ref, xp_ref, xn_ref, mod_ref, g2_ref, wug_ref, wuv_ref, cg_ref, cv_ref, wd_ref, gf_ref,
         o_ref, acc_ref) = refs
    else:
        (x_ref, xp_ref, xn_ref, mod_ref, g2_ref, wug_ref, wuv_ref, cg_ref, cv_ref, wd_ref,
         o_ref, acc_ref) = refs
    d = d_model
    j = pl.program_id(1)
    mod = mod_ref[0]
    g2 = g2_ref[...]

    def norm_mod(x):
        return _rms_rows(x) * g2 * (1.0 + mod[:, 4 * d:5 * d]) + mod[:, 3 * d:4 * d]

    keep_prev = (j > 0).astype(F32)
    keep_next = (j < pl.num_programs(1) - 1).astype(F32)
    x = x_ref[0]
    h_ext = jnp.concatenate([norm_mod(xp_ref[0]) * keep_prev, norm_mod(x), norm_mod(xn_ref[0]) * keep_next],
                            axis=0).astype(BF16)
    n_ext = tm + 2 * HALO
    acc_ref[...] = jnp.zeros_like(acc_ref)

    def conv(u, cw):
        y = pltpu.roll(u, 1, 0) * cw[0:1] + u * cw[1:2] + pltpu.roll(u, n_ext - 1, 0) * cw[2:3] + cw[3:4]
        return y[HALO:HALO + tm]

    def body(c, carry):
        gate = conv(_dot(h_ext, wug_ref[c]), cg_ref[c])
        val = conv(_dot(h_ext, wuv_ref[c]), cv_ref[c])
        act = (gate * jax.nn.sigmoid(gate) * val).astype(BF16)
        acc_ref[...] += _dot(act, wd_ref[c])
        return carry

    lax.fori_loop(0, n_chunks, body, 0)
    y = x + mod[:, 5 * d:6 * d] * acc_ref[...]
    if final_norm:
        y = _rms_rows(y) * gf_ref[...]
    o_ref[0] = y


def _conv_ffn(x, mod, mod_row, g2, wug, wuv, cg, cv, wd, g_final, *, tm):
    b, t, d = x.shape
    n_chunks = wug.shape[0]
    final_norm = g_final is not None
    hb = tm // HALO
    last = t // HALO - 1
    row = pl.BlockSpec((1, d), lambda bi, j: (0, 0))
    in_specs = [pl.BlockSpec((1, tm, d), lambda bi, j: (bi, j, 0)),
                pl.BlockSpec((1, HALO, d), lambda bi, j: (bi, jnp.maximum(j * hb - 1, 0), 0)),
                pl.BlockSpec((1, HALO, d), lambda bi, j: (bi, jnp.minimum((j + 1) * hb, last), 0)),
                pl.BlockSpec((1, 1, 6 * d), lambda bi, j: (mod_row(bi), 0, 0)),
                row,
                _const_spec(wug.shape), _const_spec(wuv.shape), _const_spec(cg.shape), _const_spec(cv.shape),
                _const_spec(wd.shape)]
    args = [x, x, x, mod, g2, wug, wuv, cg, cv, wd]
    if final_norm:
        in_specs.append(row)
        args.append(g_final)
    return pl.pallas_call(
        functools.partial(_ffn_kernel, d_model=d, tm=tm, n_chunks=n_chunks, final_norm=final_norm),
        grid=(b, t // tm),
        in_specs=in_specs,
        out_specs=pl.BlockSpec((1, tm, d), lambda bi, j: (bi, j, 0)),
        out_shape=jax.ShapeDtypeStruct((b, t, d), F32),
        scratch_shapes=[pltpu.VMEM((tm, d), F32)],
        compiler_params=_params(("parallel", "parallel"), 56),
        name="conv_ffn",
    )(*args)


_Q_HEAD_ORDER = (0, 2, 1, 3)


def _permute_heads(w, axis):
    parts = [lax.slice_in_dim(w, h * HEAD_DIM, (h + 1) * HEAD_DIM, axis=axis) for h in _Q_HEAD_ORDER]
    return jnp.concatenate(parts, axis=axis)


def _layout_w_in(w):
    g, kv = GROUP_W, KV_W
    o_b = g + 2 * kv
    o_c = 2 * o_b
    o_g = o_c + 4 * g
    o_d = o_g + 4 * N_HEADS
    sl = lambda a, n: w[:, a:a + n]
    pad = jnp.zeros((w.shape[0], LANES - 4 * N_HEADS), w.dtype)
    return jnp.concatenate([_permute_heads(sl(0, g), 1), sl(g, 2 * kv),
                            _permute_heads(sl(o_b, g), 1), sl(o_b + g, 2 * kv),
                            sl(o_c, 4 * g), sl(o_d, 3 * g), sl(o_g, 4 * N_HEADS), pad], axis=1).astype(BF16)


def _rope_tables(t):
    pos = jnp.arange(t)
    row = (pos // GRID_W).astype(F32)
    col = (pos % GRID_W).astype(F32)
    n_freq = HEAD_DIM // 4
    freqs = ROPE_THETA ** (-jnp.arange(n_freq, dtype=F32) / n_freq)
    ang_r, ang_c = row[:, None] * freqs, col[:, None] * freqs
    cos = jnp.concatenate([jnp.cos(ang_r), jnp.cos(ang_r), jnp.cos(ang_c), jnp.cos(ang_c)], axis=1)
    sin = jnp.concatenate([-jnp.sin(ang_r), jnp.sin(ang_r), -jnp.sin(ang_c), jnp.sin(ang_c)], axis=1)
    reps = LANES // HEAD_DIM
    return jnp.tile(cos, (1, reps)), jnp.tile(sin, (1, reps))


def _pick_tile(t, pref):
    while t % pref:
        pref //= 2
    return pref


def kernel(x, c, ctx, c_ctx, w_mod, b_mod, g_norm1, g_norm2, w_in, a_q_gain, a_k_gain, b_sink, c_gate_bias,
           d_rel_bias, g_group, w_out, w_up, conv_w, conv_b, w_down, g_final):
    batch, t, d = x.shape
    n_ctx = ctx.shape[1]
    depth = w_in.shape[0]
    d_ff = w_down.shape[1]
    n_chunks = d_ff // FFN_CHUNK
    assert batch < 8 and d_ff % FFN_CHUNK == 0 and t % MLSTM_L == 0 and n_ctx % MLSTM_L == 0

    c_rows = jnp.zeros((8, d), F32).at[:batch].set(c).at[batch].set(c_ctx)
    mod_all = _modulation(c_rows, w_mod, b_mod).reshape(depth, 8, 1, 6 * d)
    lat_row = lambda bi: bi
    ctx_row = lambda bi: batch

    cos, sin = _rope_tables(t)
    pool = jnp.where(jnp.arange(GROUP_W)[:, None] // HEAD_DIM == jnp.arange(GROUP_W)[None, :] // HEAD_DIM,
                     1.0 / HEAD_DIM, 0.0).astype(BF16)
    tm_lat = _pick_tile(t, 512)
    tm_ctx = _pick_tile(n_ctx, 256)
    zero_state = (jnp.zeros((batch, GROUP_W, GROUP_W), F32), jnp.zeros((batch, GROUP_W, LANES), F32),
                  jnp.zeros((batch, 8, LANES), F32))

    for layer in range(depth):
        need_ctx = layer < depth - 1
        mod = mod_all[layer]
        g1 = g_norm1[layer][None]
        g2 = g_norm2[layer][None]
        w_in_l = _layout_w_in(w_in[layer])
        gq = jnp.tile(a_q_gain[layer], N_HEADS)[None]
        gk = jnp.tile(a_k_gain[layer], 2)[None]
        gbias = jnp.pad(c_gate_bias[layer], (0, LANES - 4 * N_HEADS))[None]
        gg = g_group[layer].reshape(N_HEADS, GROUP_W)
        gg = jnp.concatenate([_permute_heads(gg[0:2], 1), gg[2:4]], axis=0)
        w_out_l = w_out[layer]
        w_out_l = jnp.concatenate([_permute_heads(w_out_l[0:GROUP_W], 0),
                                   _permute_heads(w_out_l[GROUP_W:2 * GROUP_W], 0),
                                   w_out_l[2 * GROUP_W:]], axis=0).astype(BF16)
        w_up_l = w_up[layer].astype(BF16).reshape(d, 2, n_chunks, FFN_CHUNK).transpose(1, 2, 0, 3)
        conv_l = jnp.concatenate([conv_w[layer], conv_b[layer][None],
                                  jnp.zeros((8 - 1 - conv_w.shape[1], 2 * d_ff), F32)], axis=0)
        conv_l = conv_l.reshape(8, 2, n_chunks, FFN_CHUNK).transpose(1, 2, 0, 3)
        w_down_l = w_down[layer].astype(BF16).reshape(n_chunks, FFN_CHUNK, d)
        sink = b_sink[layer]
        bias_tab = _neighbour_bias(d_rel_bias[layer])

        proj = functools.partial(_in_proj, g1=g1, w=w_in_l, pool=pool, gq=gq, gk=gk, gbias=gbias)
        (aqc, akc, avc, bqc, bkc, bvc, cqc, ckc, cvc, coc, cgc, dqc, dkc, dvc) = proj(
            ctx, mod, ctx_row, cos=cos, sin=sin, rope=False, tm=tm_ctx)
        (aq, ak, av, bq, bk, bv, cq, ck, cv, co, cg, dq, dk, dv) = proj(
            x, mod, lat_row, cos=cos, sin=sin, rope=True, tm=tm_lat)

        score_bound = 1.02 * LOG2_E * HEAD_DIM ** 0.5 * jnp.max(jnp.abs(a_q_gain[layer])) * jnp.max(jnp.abs(a_k_gain[layer]))
        safe = (score_bound <= MAX_UNSHIFTED_LOG2_SCORE).astype(jnp.int32).reshape(1)
        ya = _attn_global(safe, aq, ak, av, akc, avc, tq=_pick_tile(t, 256), tk=_pick_tile(t, 1024))
        yb = _attn_window(sink, bq, bk, bv, bkc, bvc, tq=_pick_tile(t, 256))
        yd = _attn_neighbour(dq, dk, dv, dkc, dvc, bias_tab, rows_per_step=8)

        hcf, st_f = _mlstm_scan(cqc, ckc, cvc, cgc, zero_state, reverse=False)
        ycc, st_b = _mlstm_scan(cqc, ckc, cvc, cgc, zero_state, reverse=True, hf=hcf, o_gate=coc)
        hf, _ = _mlstm_scan(cq, ck, cv, cg, st_f, reverse=False)
        yc, _ = _mlstm_scan(cq, ck, cv, cg, st_b, reverse=True, hf=hf, o_gate=co)

        x = _out_proj(x, (ya, yb, yc, yd), mod, lat_row, gg, w_out_l, tm=tm_lat)
        ffn = functools.partial(_conv_ffn, g2=g2, wug=w_up_l[0], wuv=w_up_l[1], cg=conv_l[0], cv=conv_l[1],
                                wd=w_down_l)
        x = ffn(x, mod, lat_row, g_final=None if need_ctx else g_final[None], tm=tm_lat)
        if need_ctx:
            yac, ybc, ydc = _ctx_attention(sink, aqc, akc, avc, bqc, bkc, bvc, dqc, dkc, dvc)
            ctx = _out_proj(ctx, (yac, ybc, ycc, ydc), mod, ctx_row, gg, w_out_l, tm=tm_ctx)
            ctx = ffn(ctx, mod, ctx_row, g_final=None, tm=tm_ctx)
    return x
```

```python
import functools

import jax
import jax.numpy as jnp
from jax import lax
from jax.experimental import pallas as pl
from jax.experimental.pallas import tpu as pltpu

F32 = jnp.float32
BF16 = jnp.bfloat16

N_HEADS = 4
HEAD_DIM = 64
GROUP_W = N_HEADS * HEAD_DIM
KV_W = 2 * HEAD_DIM
GRID_W = 64
WINDOW = 128
NA_KH = 8
NA_KW = 16
ROPE_THETA = 10000.0
EPS = 1e-6
NEG_INF = -1e30
LANES = 128
MLSTM_L = 256
FFN_CHUNK = 256
HALO = 8
MIB = 1024 * 1024
LOG2_E = 1.4426950408889634
MAX_UNSHIFTED_LOG2_SCORE = 60.0


def _params(sem, vmem_mib):
    return pltpu.CompilerParams(dimension_semantics=sem, vmem_limit_bytes=vmem_mib * MIB)


def _dot(a, b):
    return jnp.dot(a, b, preferred_element_type=F32)


def _dot_nt(a, b):
    return lax.dot_general(a, b, (((1,), (1,)), ((), ())), preferred_element_type=F32)


def _const_spec(shape):
    return pl.BlockSpec(shape, lambda *_: (0,) * len(shape), pipeline_mode=pl.Buffered(1))


def _mod_kernel(c_ref, w_ref, b_ref, o_ref):
    c = c_ref[...]
    a = (c * jax.nn.sigmoid(c)).astype(BF16)
    o_ref[0] = _dot(a, w_ref[0].astype(BF16)) + b_ref[0]


def _modulation(c_rows, w_mod, b_mod):
    depth, d, n = w_mod.shape
    tn = 1536
    return pl.pallas_call(
        _mod_kernel,
        grid=(depth, n // tn),
        in_specs=[pl.BlockSpec((8, d), lambda l, j: (0, 0)),
                  pl.BlockSpec((1, d, tn), lambda l, j: (l, 0, j)),
                  pl.BlockSpec((1, 1, tn), lambda l, j: (l, 0, j))],
        out_specs=pl.BlockSpec((1, 8, tn), lambda l, j: (l, 0, j)),
        out_shape=jax.ShapeDtypeStruct((depth, 8, n), F32),
        compiler_params=_params(("parallel", "parallel"), 40),
        name="modulation",
    )(c_rows, w_mod, b_mod.reshape(depth, 1, n))


def _rms_rows(x):
    return x * lax.rsqrt(jnp.mean(x * x, axis=-1, keepdims=True) + EPS)


def _head_rms(z, pool, gain):
    z2 = z * z
    hi = z2.astype(BF16)
    lo = (z2 - hi.astype(F32)).astype(BF16)
    ms = _dot(hi, pool) + _dot(lo, pool)
    return z * lax.rsqrt(ms + EPS) * gain


def _rope(z, cos, sin):
    lane = lax.broadcasted_iota(jnp.int32, z.shape, 1)
    first = (lane % 32) < 16
    partner = jnp.where(first, pltpu.roll(z, LANES - 16, 1), pltpu.roll(z, 16, 1))
    return z * cos + partner * sin


def _inproj_kernel(x_ref, mod_ref, g1_ref, w_ref, pool_ref, gq_ref, gk_ref, cos_ref, sin_ref, gb_ref,
                   aq_ref, ak_ref, av_ref, bq_ref, bk_ref, bv_ref, cq_ref, ck_ref, cv_ref, co_ref, cg_ref,
                   dq_ref, dk_ref, dv_ref, *, d_model, rope):
    d = d_model
    mod = mod_ref[0]
    xn = _rms_rows(x_ref[0]) * g1_ref[...]
    hb = (xn * (1.0 + mod[:, d:2 * d]) + mod[:, 0:d]).astype(BF16)

    def proj(seg, width=GROUP_W):
        return _dot(hb, w_ref[:, seg * GROUP_W:seg * GROUP_W + width])

    def rot(z):
        if not rope:
            return z
        cos, sin = cos_ref[...], sin_ref[...]
        return jnp.concatenate([_rope(z[:, i:i + LANES], cos, sin) for i in range(0, z.shape[1], LANES)], axis=1)

    scale = HEAD_DIM ** -0.5
    pool = pool_ref[...]
    aq_ref[0] = (rot(_head_rms(proj(0), pool, gq_ref[...])) * (scale * LOG2_E)).astype(BF16)
    akv = proj(1)
    ak_ref[0] = rot(_head_rms(akv[:, :KV_W], pool[:KV_W, :KV_W], gk_ref[...])).astype(BF16)
    av_ref[0] = jnp.concatenate([akv[:, KV_W:], jnp.ones((akv.shape[0], LANES), F32)], axis=1).astype(BF16)
    bq_ref[0] = (rot(proj(2)) * scale).astype(BF16)
    bkv = proj(3)
    bk_ref[0] = rot(bkv[:, :KV_W]).astype(BF16)
    bv_ref[0] = bkv[:, KV_W:].astype(BF16)
    cq_ref[0] = (proj(4) * scale).astype(BF16)
    ck_ref[0] = proj(5).astype(BF16)
    cv_ref[0] = proj(6).astype(BF16)
    co_ref[0] = proj(7)
    dq_ref[0] = (proj(8) * scale).astype(BF16)
    dk_ref[0] = proj(9).astype(BF16)
    dv_ref[0] = proj(10).astype(BF16)
    cg_ref[0] = proj(11, LANES) + gb_ref[...]


def _in_proj(x, mod, mod_row, g1, w, pool, gq, gk, cos, sin, gbias, *, rope, tm):
    b, t, d = x.shape
    n_w = w.shape[1]
    widths = [(GROUP_W, BF16), (KV_W, BF16), (2 * LANES, BF16),
              (GROUP_W, BF16), (KV_W, BF16), (KV_W, BF16),
              (GROUP_W, BF16), (GROUP_W, BF16), (GROUP_W, BF16), (GROUP_W, F32), (LANES, F32),
              (GROUP_W, BF16), (GROUP_W, BF16), (GROUP_W, BF16)]
    row = lambda wd: pl.BlockSpec((1, wd), lambda bi, j: (0, 0))
    return pl.pallas_call(
        functools.partial(_inproj_kernel, d_model=d, rope=rope),
        grid=(b, t // tm),
        in_specs=[pl.BlockSpec((1, tm, d), lambda bi, j: (bi, j, 0)),
                  pl.BlockSpec((1, 1, 6 * d), lambda bi, j: (mod_row(bi), 0, 0)),
                  row(d),
                  _const_spec((d, n_w)),
                  _const_spec((GROUP_W, GROUP_W)),
                  row(GROUP_W), row(KV_W),
                  pl.BlockSpec((tm, LANES), lambda bi, j: (j, 0)),
                  pl.BlockSpec((tm, LANES), lambda bi, j: (j, 0)),
                  row(LANES)],
        out_specs=[pl.BlockSpec((1, tm, wd), lambda bi, j: (bi, j, 0)) for wd, _ in widths],
        out_shape=[jax.ShapeDtypeStruct((b, t, wd), dt) for wd, dt in widths],
        compiler_params=_params(("parallel", "parallel"), 48),
        name="in_proj",
    )(x, mod, g1, w, pool, gq, gk, cos, sin, gbias)


def _stack_gqa(q):
    qa, qb = q[:, :LANES], q[:, LANES:]
    left = lax.broadcasted_iota(jnp.int32, qa.shape, 1) < HEAD_DIM
    zero = jnp.zeros_like(qa)
    return jnp.concatenate([jnp.where(left, qa, zero), jnp.where(left, qb, zero),
                            jnp.where(left, zero, qa), jnp.where(left, zero, qb)], axis=0)


def _unstack_gqa(o, t):
    left = lax.broadcasted_iota(jnp.int32, (t, LANES), 1) < HEAD_DIM
    return jnp.concatenate([jnp.where(left, o[0:t], o[2 * t:3 * t]),
                            jnp.where(left, o[t:2 * t], o[3 * t:4 * t])], axis=1)


def _attn_a_kernel(safe_ref, q_ref, k_ref, v_ref, kc_ref, vc_ref, o_ref, qs_ref, m_ref, acc_ref, *, tq, tk, n_kb):
    qs_ref[...] = _stack_gqa(q_ref[0])
    heads = [slice(h * tq, (h + 1) * tq) for h in range(N_HEADS)]

    @pl.when(safe_ref[0] != 0)
    def _():
        for rows in heads:
            p = jnp.exp2(_dot_nt(qs_ref[rows], kc_ref[0]))
            acc_ref[rows] = _dot(p.astype(BF16), vc_ref[0])

        def body(kb, carry):
            start = pl.multiple_of(kb * tk, tk)
            for rows in heads:
                p = jnp.exp2(_dot_nt(qs_ref[rows], k_ref[0, pl.ds(start, tk), :]))
                acc_ref[rows] += _dot(p.astype(BF16), v_ref[0, pl.ds(start, tk), :])
            return carry

        lax.fori_loop(0, n_kb, body, 0)

    @pl.when(safe_ref[0] == 0)
    def _():
        for rows in heads:
            s = _dot_nt(qs_ref[rows], kc_ref[0])
            m0 = jnp.max(s, axis=-1, keepdims=True)
            m_ref[rows] = m0
            acc_ref[rows] = _dot(jnp.exp2(s - m0).astype(BF16), vc_ref[0])

        def body(kb, carry):
            start = pl.multiple_of(kb * tk, tk)
            for rows in heads:
                s = _dot_nt(qs_ref[rows], k_ref[0, pl.ds(start, tk), :])
                m_prev = m_ref[rows]
                m_new = jnp.maximum(m_prev, jnp.max(s, axis=-1, keepdims=True))
                p = jnp.exp2(s - m_new)
                acc_ref[rows] = jnp.exp2(m_prev - m_new) * acc_ref[rows] + _dot(
                    p.astype(BF16), v_ref[0, pl.ds(start, tk), :])
                m_ref[rows] = m_new
            return carry

        lax.fori_loop(0, n_kb, body, 0)

    acc = acc_ref[...]
    o_ref[0] = _unstack_gqa(acc[:, :LANES] / acc[:, LANES:], tq)


def _attn_global(safe, q, k, v, kc, vc, *, tq, tk):
    b, t, _ = q.shape
    n_ctx = kc.shape[1]
    per_b = lambda n, wd: pl.BlockSpec((1, n, wd), lambda bi, i: (bi, 0, 0))
    return pl.pallas_call(
        functools.partial(_attn_a_kernel, tq=tq, tk=tk, n_kb=t // tk),
        grid=(b, t // tq),
        in_specs=[pl.BlockSpec(memory_space=pltpu.SMEM),
                  pl.BlockSpec((1, tq, GROUP_W), lambda bi, i: (bi, i, 0)),
                  per_b(t, KV_W), per_b(t, 2 * LANES), per_b(n_ctx, KV_W), per_b(n_ctx, 2 * LANES)],
        out_specs=pl.BlockSpec((1, tq, GROUP_W), lambda bi, i: (bi, i, 0)),
        out_shape=jax.ShapeDtypeStruct((b, t, GROUP_W), F32),
        scratch_shapes=[pltpu.VMEM((4 * tq, LANES), BF16), pltpu.VMEM((4 * tq, 1), F32),
                        pltpu.VMEM((4 * tq, 2 * LANES), F32)],
        compiler_params=_params(("parallel", "parallel"), 48),
        name="attn_global",
    )(safe, q, k, v, kc, vc)


def _sink_column(sink_ref, t):
    row = lax.broadcasted_iota(jnp.int32, (4 * t, 1), 0)
    col = jnp.full((4 * t, 1), sink_ref[3], F32)
    for h in (2, 1, 0):
        col = jnp.where(row < (h + 1) * t, sink_ref[h], col)
    return col


def _attn_b_kernel(sink_ref, q_ref, k_ref, v_ref, kc_ref, vc_ref, o_ref, *, tq, t_total):
    span = tq + 2 * WINDOW
    q0 = pl.program_id(1) * tq
    ks = pl.multiple_of(jnp.clip(q0 - WINDOW, 0, t_total - span), WINDOW)
    qs = _stack_gqa(q_ref[0])
    kw = k_ref[0, pl.ds(ks, span), :]
    vw = v_ref[0, pl.ds(ks, span), :]
    q_pos = q0 + lax.broadcasted_iota(jnp.int32, (tq, span), 0)
    k_pos = ks + lax.broadcasted_iota(jnp.int32, (tq, span), 1)
    in_window = jnp.abs(k_pos - q_pos) <= WINDOW
    outs = []
    for h in range(N_HEADS):
        qh = qs[h * tq:(h + 1) * tq]
        s = jnp.where(in_window, _dot_nt(qh, kw), NEG_INF)
        sc = _dot_nt(qh, kc_ref[0])
        sink = sink_ref[h]
        m = jnp.maximum(jnp.maximum(jnp.max(s, axis=-1, keepdims=True), jnp.max(sc, axis=-1, keepdims=True)), sink)
        p = jnp.exp(s - m)
        pc = jnp.exp(sc - m)
        l = jnp.sum(p, axis=-1, keepdims=True) + jnp.sum(pc, axis=-1, keepdims=True) + jnp.exp(sink - m)
        outs.append((_dot(p.astype(BF16), vw) + _dot(pc.astype(BF16), vc_ref[0])) / l)
    o_ref[0] = _unstack_gqa(jnp.concatenate(outs, axis=0), tq)


def _attn_window(sink, q, k, v, kc, vc, *, tq):
    b, t, _ = q.shape
    n_ctx = kc.shape[1]
    assert t >= tq + 2 * WINDOW
    per_b = lambda n, wd: pl.BlockSpec((1, n, wd), lambda bi, i: (bi, 0, 0))
    return pl.pallas_call(
        functools.partial(_attn_b_kernel, tq=tq, t_total=t),
        grid=(b, t // tq),
        in_specs=[pl.BlockSpec(memory_space=pltpu.SMEM),
                  pl.BlockSpec((1, tq, GROUP_W), lambda bi, i: (bi, i, 0)),
                  per_b(t, KV_W), per_b(t, KV_W), per_b(n_ctx, KV_W), per_b(n_ctx, KV_W)],
        out_specs=pl.BlockSpec((1, tq, GROUP_W), lambda bi, i: (bi, i, 0)),
        out_shape=jax.ShapeDtypeStruct((b, t, GROUP_W), F32),
        compiler_params=_params(("parallel", "parallel"), 48),
        name="attn_window",
    )(sink, q, k, v, kc, vc)


def _head_masks(shape):
    lane = lax.broadcasted_iota(jnp.int32, shape, 1)
    return [(lane >= h * HEAD_DIM) & (lane < (h + 1) * HEAD_DIM) for h in range(N_HEADS)]


def _stack_mha(q):
    zero = jnp.zeros_like(q)
    return jnp.concatenate([jnp.where(mk, q, zero) for mk in _head_masks(q.shape)], axis=0)


def _unstack_mha(o, t):
    masks = _head_masks((t, GROUP_W))
    out = jnp.where(masks[0], o[0:t], 0.0)
    for h in range(1, N_HEADS):
        out = jnp.where(masks[h], o[h * t:(h + 1) * t], out)
    return out


def _attn_d_kernel(q_ref, k_ref, v_ref, kc_ref, vc_ref, bias_ref, o_ref, *, rows_per_step, rows):
    kc = kc_ref[0]
    vc = vc_ref[0]
    span = NA_KH * GRID_W

    def body(rr, carry):
        r = pl.program_id(1) * rows_per_step + rr
        rs = jnp.clip(r - NA_KH // 2, 0, rows - NA_KH)
        k0 = pl.multiple_of(rs * GRID_W, GRID_W)
        q0 = pl.multiple_of(rr * GRID_W, GRID_W)
        qs = _stack_mha(q_ref[0, pl.ds(q0, GRID_W), :])
        s = _dot_nt(qs, k_ref[0, pl.ds(k0, span), :]) + bias_ref[r - rs]
        sc = _dot_nt(qs, kc)
        m = jnp.maximum(jnp.max(s, axis=-1, keepdims=True), jnp.max(sc, axis=-1, keepdims=True))
        p = jnp.exp(s - m)
        pc = jnp.exp(sc - m)
        l = jnp.sum(p, axis=-1, keepdims=True) + jnp.sum(pc, axis=-1, keepdims=True)
        o = _dot(p.astype(BF16), v_ref[0, pl.ds(k0, span), :]) + _dot(pc.astype(BF16), vc)
        o_ref[0, pl.ds(q0, GRID_W), :] = _unstack_mha(o / l, GRID_W)
        return carry

    lax.fori_loop(0, rows_per_step, body, 0, unroll=2)


def _attn_neighbour(q, k, v, kc, vc, bias, *, rows_per_step):
    b, t, _ = q.shape
    n_ctx = kc.shape[1]
    rows = t // GRID_W
    assert rows >= NA_KH and rows % rows_per_step == 0
    tq = rows_per_step * GRID_W
    per_b = lambda n: pl.BlockSpec((1, n, GROUP_W), lambda bi, i: (bi, 0, 0))
    return pl.pallas_call(
        functools.partial(_attn_d_kernel, rows_per_step=rows_per_step, rows=rows),
        grid=(b, t // tq),
        in_specs=[pl.BlockSpec((1, tq, GROUP_W), lambda bi, i: (bi, i, 0)),
                  per_b(t), per_b(t), per_b(n_ctx), per_b(n_ctx),
                  _const_spec(bias.shape)],
        out_specs=pl.BlockSpec((1, tq, GROUP_W), lambda bi, i: (bi, i, 0)),
        out_shape=jax.ShapeDtypeStruct((b, t, GROUP_W), F32),
        compiler_params=_params(("parallel", "parallel"), 48),
        name="attn_neighbour",
    )(q, k, v, kc, vc, bias)


def _neighbour_bias(rpb):
    w = jnp.arange(GRID_W)
    cs = jnp.clip(w - NA_KW // 2, 0, GRID_W - NA_KW)
    col = jnp.arange(GRID_W)
    valid = (col[None, :] >= cs[:, None]) & (col[None, :] < cs[:, None] + NA_KW)
    dc = jnp.clip(col[None, :] - w[:, None] + (NA_KW - 1), 0, 2 * NA_KW - 2)
    dr = jnp.arange(NA_KH)[None, :] - jnp.arange(NA_KH)[:, None] + (NA_KH - 1)
    pick_r = (dr[:, :, None] == jnp.arange(2 * NA_KH - 1)).astype(F32)
    pick_c = (jnp.arange(2 * NA_KW - 1)[:, None, None] == dc[None]).astype(F32)
    tab = jnp.einsum('hrc,oir,cwk->ohwik', rpb.astype(F32), pick_r, pick_c, precision=lax.Precision.HIGHEST)
    tab = jnp.where(valid[None, None, :, None, :], tab, NEG_INF)
    return tab.reshape(NA_KH, N_HEADS * GRID_W, NA_KH * GRID_W)


def _ctx_attn_kernel(sink_ref, aq_ref, ak_ref, av_ref, bq_ref, bk_ref, bv_ref, dq_ref, dk_ref, dv_ref,
                     ya_ref, yb_ref, yd_ref, *, n):
    def attend(qs, k, v, sink=None):
        s = _dot_nt(qs, k)
        m = jnp.max(s, axis=-1, keepdims=True)
        if sink is not None:
            m = jnp.maximum(m, sink)
        p = jnp.exp(s - m)
        l = jnp.sum(p, axis=-1, keepdims=True)
        if sink is not None:
            l = l + jnp.exp(sink - m)
        return _dot(p.astype(BF16), v) / l

    s = _dot_nt(_stack_gqa(aq_ref[0]), ak_ref[0])
    acc = _dot(jnp.exp2(s - jnp.max(s, axis=-1, keepdims=True)).astype(BF16), av_ref[0])
    ya_ref[0] = _unstack_gqa(acc[:, :LANES] / acc[:, LANES:], n)
    yb_ref[0] = _unstack_gqa(attend(_stack_gqa(bq_ref[0]), bk_ref[0], bv_ref[0], _sink_column(sink_ref, n)), n)
    yd_ref[0] = _unstack_mha(attend(_stack_mha(dq_ref[0]), dk_ref[0], dv_ref[0]), n)


def _ctx_attention(sink, aq, ak, av, bq, bk, bv, dq, dk, dv):
    b, n, _ = aq.shape
    spec = lambda wd: pl.BlockSpec((1, n, wd), lambda bi: (bi, 0, 0))
    return pl.pallas_call(
        functools.partial(_ctx_attn_kernel, n=n),
        grid=(b,),
        in_specs=[pl.BlockSpec(memory_space=pltpu.SMEM),
                  spec(GROUP_W), spec(KV_W), spec(2 * LANES), spec(GROUP_W), spec(KV_W), spec(KV_W),
                  spec(GROUP_W), spec(GROUP_W), spec(GROUP_W)],
        out_specs=[spec(GROUP_W)] * 3,
        out_shape=[jax.ShapeDtypeStruct((b, n, GROUP_W), F32)] * 3,
        compiler_params=_params(("parallel",), 32),
        name="ctx_attention",
    )(sink, aq, ak, av, bq, bk, bv, dq, dk, dv)


def _log_sigmoid(x):
    return jnp.minimum(x, 0.0) - jnp.log1p(jnp.exp(-jnp.abs(x)))


def _split3(x):
    x1 = x.astype(BF16)
    r1 = x - x1.astype(F32)
    x2 = r1.astype(BF16)
    return x1, x2, (r1 - x2.astype(F32)).astype(BF16)


def _exact_dot_01(sel, x):
    return sum(_dot(sel, piece) for piece in _split3(x))


def _exact_dot_01_rhs(x, sel):
    return sum(_dot(piece, sel) for piece in _split3(x))


def _mlstm_chunk(q, k, v, gates, expand, tri, neg_mask, ones_cat, block, s_ref, n_ref, m_ref, *, reverse, L):
    ge = _exact_dot_01_rhs(gates, expand)
    li = ge[:, :GROUP_W] * LOG2_E
    cum = _exact_dot_01(tri, _log_sigmoid(ge[:, GROUP_W:]) * LOG2_E)
    a_t = (li - cum).T
    m_prev = m_ref[0:1, :]
    masks = _head_masks((L, GROUP_W))
    zero_q = jnp.zeros_like(q)
    zero_v = jnp.zeros_like(v)
    mu = jnp.zeros((L, GROUP_W), F32)
    scores = []
    for h in range(N_HEADS):
        lane0 = h * HEAD_DIM
        a_m = a_t[lane0:lane0 + 1, :] + neg_mask
        mu_h = jnp.maximum(jnp.max(a_m, axis=-1, keepdims=True), m_prev[:, lane0:lane0 + 1])
        qk = _dot_nt(jnp.where(masks[h], q, zero_q), k)
        scores.append((qk * jnp.exp2(a_m - mu_h)).astype(BF16))
        mu = jnp.where(masks[h], mu_h, mu)
    s_cat = jnp.concatenate(scores, axis=1)
    v_cat = jnp.concatenate([jnp.where(mk, v, zero_v) for mk in masks], axis=0)
    w_inter = jnp.exp2(m_prev - mu)
    num = _dot(s_cat, v_cat) + w_inter * _dot(q, s_ref[...].astype(BF16))
    den = _dot(s_cat, ones_cat) + w_inter * _dot(q, n_ref[...].astype(BF16))
    h_out = num / jnp.maximum(jnp.abs(den), jnp.exp2(-(cum + mu)))

    end_row = 0 if reverse else L - 1
    cum_end = cum[end_row:end_row + 1, :]
    log_end = cum_end - cum + li
    m_new = jnp.maximum(cum_end + m_prev, jnp.max(log_end, axis=0, keepdims=True))
    decay = jnp.exp2(cum_end + m_prev - m_new)
    w_end = jnp.exp2(log_end - m_new)
    k_t = k.astype(F32).T.astype(BF16)
    upd = _dot(k_t, jnp.concatenate([(v.astype(F32) * w_end).astype(BF16), w_end.astype(BF16)], axis=1))
    s_ref[...] = (s_ref[...] * decay + upd[:, :GROUP_W]) * block
    n_ref[...] = (n_ref[...] * decay + upd[:, GROUP_W:]) * block
    m_ref[...] = jnp.broadcast_to(m_new, m_ref.shape)
    return h_out


def _mlstm_kernel(qf_ref, kf_ref, vf_ref, gf_ref, qb_ref, kb_ref, vb_ref, gb_ref,
                  expand_ref, tri_ref, neg_ref, ones_ref, block_ref, s0_ref, n0_ref, m0_ref,
                  hf_ref, hb_ref, s_out, n_out, m_out, s_ref, n_ref, m_ref, *, chunk):
    @pl.when(pl.program_id(1) == 0)
    def _():
        s_ref[...] = s0_ref[0]
        n_ref[...] = n0_ref[0]
        m_ref[...] = m0_ref[0]

    sides = ((qf_ref, kf_ref, vf_ref, gf_ref, hf_ref), (qb_ref, kb_ref, vb_ref, gb_ref, hb_ref))
    for d, (q_ref, k_ref, v_ref, g_ref, h_ref) in enumerate(sides):
        h = _mlstm_chunk(q_ref[0], k_ref[0], v_ref[0], g_ref[0], expand_ref[d], tri_ref[d], neg_ref[d],
                         ones_ref[...], block_ref[...], s_ref.at[d], n_ref.at[d], m_ref.at[d],
                         reverse=bool(d), L=chunk)
        h_ref[0] = h.astype(h_ref.dtype)
    s_out[0] = s_ref[...]
    n_out[0] = n_ref[...]
    m_out[0] = m_ref[...]


def _mlstm_consts(L):
    lane = jnp.arange(LANES)[:, None]
    col = jnp.arange(2 * GROUP_W)[None, :]
    col_head = (col % GROUP_W) // HEAD_DIM
    expand = jnp.stack([(lane == 8 * d + 4 * (col // GROUP_W) + col_head) for d in range(2)]).astype(BF16)
    t_idx = jnp.arange(L)[:, None]
    s_idx = jnp.arange(L)[None, :]
    seen = jnp.stack([s_idx <= t_idx, s_idx >= t_idx])
    head = jnp.arange(GROUP_W) // HEAD_DIM
    ones_cat = (jnp.repeat(jnp.arange(N_HEADS), L)[:, None] == head[None, :]).astype(BF16)
    block = (head[:, None] == head[None, :]).astype(F32)
    return expand, seen.astype(BF16), jnp.where(seen, 0.0, NEG_INF).astype(F32), ones_cat, block


def _mlstm_scan(q, k, v, gates, state, consts):
    b, t, _ = q.shape
    L = MLSTM_L
    nc = t // L
    fwd = lambda wd: pl.BlockSpec((1, L, wd), lambda bi, j: (bi, j, 0))
    bwd = lambda wd: pl.BlockSpec((1, L, wd), lambda bi, j: (bi, nc - 1 - j, 0))
    st = lambda r: pl.BlockSpec((1, 2, r, GROUP_W), lambda bi, j: (bi, 0, 0, 0))
    tok_specs = [fwd(GROUP_W), fwd(GROUP_W), fwd(GROUP_W), fwd(LANES),
                 bwd(GROUP_W), bwd(GROUP_W), bwd(GROUP_W), bwd(LANES)]
    hf, hb, s_fin, n_fin, m_fin = pl.pallas_call(
        functools.partial(_mlstm_kernel, chunk=L),
        grid=(b, nc),
        in_specs=tok_specs + [_const_spec(c.shape) for c in consts] + [st(GROUP_W), st(GROUP_W), st(8)],
        out_specs=[fwd(GROUP_W), bwd(GROUP_W), st(GROUP_W), st(GROUP_W), st(8)],
        out_shape=[jax.ShapeDtypeStruct((b, t, GROUP_W), BF16), jax.ShapeDtypeStruct((b, t, GROUP_W), BF16),
                   jax.ShapeDtypeStruct((b, 2, GROUP_W, GROUP_W), F32),
                   jax.ShapeDtypeStruct((b, 2, GROUP_W, GROUP_W), F32),
                   jax.ShapeDtypeStruct((b, 2, 8, GROUP_W), F32)],
        scratch_shapes=[pltpu.VMEM((2, GROUP_W, GROUP_W), F32), pltpu.VMEM((2, GROUP_W, GROUP_W), F32),
                        pltpu.VMEM((2, 8, GROUP_W), F32)],
        compiler_params=_params(("parallel", "arbitrary"), 48),
        name="mlstm",
    )(q, k, v, gates, q, k, v, gates, *consts, *state)
    return hf, hb, (s_fin, n_fin, m_fin)


def _outproj_kernel(x_ref, ya_ref, yb_ref, hf_ref, hb_ref, og_ref, yd_ref, mod_ref, gg_ref, w_ref, o_ref, *, d_model):
    d = d_model
    yc = jax.nn.sigmoid(og_ref[0]) * (hf_ref[0].astype(F32) + hb_ref[0].astype(F32))
    parts = []
    for i, y in enumerate((ya_ref[0], yb_ref[0], yc, yd_ref[0])):
        parts.append((_rms_rows(y) * gg_ref[i:i + 1, :]).astype(BF16))
    res = _dot(jnp.concatenate(parts, axis=1), w_ref[...])
    o_ref[0] = x_ref[0] + mod_ref[0][:, 2 * d:3 * d] * res


def _out_proj(x, ya, yb, hf, hb, o_gate, yd, mod, mod_row, gg, w, *, tm):
    b, t, d = x.shape
    tokd = pl.BlockSpec((1, tm, d), lambda bi, j: (bi, j, 0))
    tokg = pl.BlockSpec((1, tm, GROUP_W), lambda bi, j: (bi, j, 0))
    return pl.pallas_call(
        functools.partial(_outproj_kernel, d_model=d),
        grid=(b, t // tm),
        in_specs=[tokd, tokg, tokg, tokg, tokg, tokg, tokg,
                  pl.BlockSpec((1, 1, 6 * d), lambda bi, j: (mod_row(bi), 0, 0)),
                  _const_spec(gg.shape), _const_spec(w.shape)],
        out_specs=tokd,
        out_shape=jax.ShapeDtypeStruct((b, t, d), F32),
        compiler_params=_params(("parallel", "parallel"), 48),
        name="out_proj",
    )(x, ya, yb, hf, hb, o_gate, yd, mod, gg, w)


def _ffn_kernel(*refs, d_model, d_ff, tm, final_norm):
    if final_norm:
        x_ref, xp_ref, xn_ref, mod_ref, g2_ref, wu_ref, cw_ref, wd_ref, gf_ref, o_ref, acc_ref = refs
    else:
        x_ref, xp_ref, xn_ref, mod_ref, g2_ref, wu_ref, cw_ref, wd_ref, o_ref, acc_ref = refs
    d = d_model
    j = pl.program_id(1)
    mod = mod_ref[0]
    g2 = g2_ref[...]

    def norm_mod(x):
        return _rms_rows(x) * g2 * (1.0 + mod[:, 4 * d:5 * d]) + mod[:, 3 * d:4 * d]

    keep_prev = (j > 0).astype(F32)
    keep_next = (j < pl.num_programs(1) - 1).astype(F32)
    x = x_ref[0]
    h_ext = jnp.concatenate([norm_mod(xp_ref[0]) * keep_prev, norm_mod(x), norm_mod(xn_ref[0]) * keep_next],
                            axis=0).astype(BF16)
    acc_ref[...] = jnp.zeros_like(acc_ref)

    n_ext = tm + 2 * HALO

    def conv(u, cw):
        y = pltpu.roll(u, 1, 0) * cw[0:1] + u * cw[1:2] + pltpu.roll(u, n_ext - 1, 0) * cw[2:3] + cw[3:4]
        return y[HALO:HALO + tm]

    for lo in range(0, d_ff, FFN_CHUNK):
        g_cols = slice(lo, lo + FFN_CHUNK)
        v_cols = slice(d_ff + lo, d_ff + lo + FFN_CHUNK)
        gate = conv(_dot(h_ext, wu_ref[:, g_cols]), cw_ref[:, g_cols])
        val = conv(_dot(h_ext, wu_ref[:, v_cols]), cw_ref[:, v_cols])
        act = (gate * jax.nn.sigmoid(gate) * val).astype(BF16)
        acc_ref[...] += _dot(act, wd_ref[g_cols, :])
    y = x + mod[:, 5 * d:6 * d] * acc_ref[...]
    if final_norm:
        y = _rms_rows(y) * gf_ref[...]
    o_ref[0] = y


def _conv_ffn(x, mod, mod_row, g2, wu, cw, wd, g_final, *, tm):
    b, t, d = x.shape
    d_ff = wd.shape[0]
    final_norm = g_final is not None
    hb = tm // HALO
    last = t // HALO - 1
    row = pl.BlockSpec((1, d), lambda bi, j: (0, 0))
    in_specs = [pl.BlockSpec((1, tm, d), lambda bi, j: (bi, j, 0)),
                pl.BlockSpec((1, HALO, d), lambda bi, j: (bi, jnp.maximum(j * hb - 1, 0), 0)),
                pl.BlockSpec((1, HALO, d), lambda bi, j: (bi, jnp.minimum((j + 1) * hb, last), 0)),
                pl.BlockSpec((1, 1, 6 * d), lambda bi, j: (mod_row(bi), 0, 0)),
                row,
                _const_spec(wu.shape), _const_spec(cw.shape), _const_spec(wd.shape)]
    args = [x, x, x, mod, g2, wu, cw, wd]
    if final_norm:
        in_specs.append(row)
        args.append(g_final)
    return pl.pallas_call(
        functools.partial(_ffn_kernel, d_model=d, d_ff=d_ff, tm=tm, final_norm=final_norm),
        grid=(b, t // tm),
        in_specs=in_specs,
        out_specs=pl.BlockSpec((1, tm, d), lambda bi, j: (bi, j, 0)),
        out_shape=jax.ShapeDtypeStruct((b, t, d), F32),
        scratch_shapes=[pltpu.VMEM((tm, d), F32)],
        compiler_params=_params(("parallel", "parallel"), 56),
        name="conv_ffn",
    )(*args)


_Q_HEAD_ORDER = (0, 2, 1, 3)


def _permute_heads(w, axis):
    parts = [lax.slice_in_dim(w, h * HEAD_DIM, (h + 1) * HEAD_DIM, axis=axis) for h in _Q_HEAD_ORDER]
    return jnp.concatenate(parts, axis=axis)


def _layout_w_in(w):
    g, kv = GROUP_W, KV_W
    o_b = g + 2 * kv
    o_c = 2 * o_b
    o_g = o_c + 4 * g
    o_d = o_g + 4 * N_HEADS
    sl = lambda a, n: w[:, a:a + n]
    pad = jnp.zeros((w.shape[0], LANES - 4 * N_HEADS), w.dtype)
    return jnp.concatenate([_permute_heads(sl(0, g), 1), sl(g, 2 * kv),
                            _permute_heads(sl(o_b, g), 1), sl(o_b + g, 2 * kv),
                            sl(o_c, 4 * g), sl(o_d, 3 * g), sl(o_g, 4 * N_HEADS), pad], axis=1).astype(BF16)


def _rope_tables(t):
    pos = jnp.arange(t)
    row = (pos // GRID_W).astype(F32)
    col = (pos % GRID_W).astype(F32)
    n_freq = HEAD_DIM // 4
    freqs = ROPE_THETA ** (-jnp.arange(n_freq, dtype=F32) / n_freq)
    ang_r, ang_c = row[:, None] * freqs, col[:, None] * freqs
    cos = jnp.concatenate([jnp.cos(ang_r), jnp.cos(ang_r), jnp.cos(ang_c), jnp.cos(ang_c)], axis=1)
    sin = jnp.concatenate([-jnp.sin(ang_r), jnp.sin(ang_r), -jnp.sin(ang_c), jnp.sin(ang_c)], axis=1)
    reps = LANES // HEAD_DIM
    return jnp.tile(cos, (1, reps)), jnp.tile(sin, (1, reps))


def _pick_tile(t, pref):
    while t % pref:
        pref //= 2
    return pref


def kernel(x, c, ctx, c_ctx, w_mod, b_mod, g_norm1, g_norm2, w_in, a_q_gain, a_k_gain, b_sink, c_gate_bias,
           d_rel_bias, g_group, w_out, w_up, conv_w, conv_b, w_down, g_final):
    batch, t, d = x.shape
    n_ctx = ctx.shape[1]
    depth = w_in.shape[0]
    d_ff = w_down.shape[1]
    n_chunks = d_ff // FFN_CHUNK
    assert batch < 8 and d_ff % FFN_CHUNK == 0 and t % MLSTM_L == 0 and n_ctx % MLSTM_L == 0

    c_rows = jnp.zeros((8, d), F32).at[:batch].set(c).at[batch].set(c_ctx)
    mod_all = _modulation(c_rows, w_mod, b_mod).reshape(depth, 8, 1, 6 * d)
    lat_row = lambda bi: bi
    ctx_row = lambda bi: batch

    cos, sin = _rope_tables(t)
    pool = jnp.where(jnp.arange(GROUP_W)[:, None] // HEAD_DIM == jnp.arange(GROUP_W)[None, :] // HEAD_DIM,
                     1.0 / HEAD_DIM, 0.0).astype(BF16)
    tm_lat = _pick_tile(t, 512)
    tm_ctx = _pick_tile(n_ctx, 256)
    zero_state = (jnp.zeros((batch, 2, GROUP_W, GROUP_W), F32), jnp.zeros((batch, 2, GROUP_W, GROUP_W), F32),
                  jnp.zeros((batch, 2, 8, GROUP_W), F32))
    mlstm_consts = _mlstm_consts(MLSTM_L)

    for layer in range(depth):
        need_ctx = layer < depth - 1
        mod = mod_all[layer]
        g1 = g_norm1[layer][None]
        g2 = g_norm2[layer][None]
        w_in_l = _layout_w_in(w_in[layer])
        gq = jnp.tile(a_q_gain[layer], N_HEADS)[None]
        gk = jnp.tile(a_k_gain[layer], 2)[None]
        gbias = jnp.pad(c_gate_bias[layer], (0, LANES - 4 * N_HEADS))[None]
        gg = g_group[layer].reshape(N_HEADS, GROUP_W)
        gg = jnp.concatenate([_permute_heads(gg[0:2], 1), gg[2:4]], axis=0)
        w_out_l = w_out[layer]
        w_out_l = jnp.concatenate([_permute_heads(w_out_l[0:GROUP_W], 0),
                                   _permute_heads(w_out_l[GROUP_W:2 * GROUP_W], 0),
                                   w_out_l[2 * GROUP_W:]], axis=0).astype(BF16)
        conv_l = jnp.concatenate([conv_w[layer], conv_b[layer][None],
                                  jnp.zeros((8 - 1 - conv_w.shape[1], 2 * d_ff), F32)], axis=0)
        sink = b_sink[layer]
        bias_tab = _neighbour_bias(d_rel_bias[layer])

        proj = functools.partial(_in_proj, g1=g1, w=w_in_l, pool=pool, gq=gq, gk=gk, gbias=gbias)
        (aqc, akc, avc, bqc, bkc, bvc, cqc, ckc, cvc, coc, cgc, dqc, dkc, dvc) = proj(
            ctx, mod, ctx_row, cos=cos, sin=sin, rope=False, tm=tm_ctx)
        (aq, ak, av, bq, bk, bv, cq, ck, cv, co, cg, dq, dk, dv) = proj(
            x, mod, lat_row, cos=cos, sin=sin, rope=True, tm=tm_lat)

        score_bound = 1.02 * LOG2_E * HEAD_DIM ** 0.5 * jnp.max(jnp.abs(a_q_gain[layer])) * jnp.max(jnp.abs(a_k_gain[layer]))
        safe = (score_bound <= MAX_UNSHIFTED_LOG2_SCORE).astype(jnp.int32).reshape(1)
        ya = _attn_global(safe, aq, ak, av, akc, avc, tq=_pick_tile(t, 256), tk=_pick_tile(t, 1024))
        yb = _attn_window(sink, bq, bk, bv, bkc, bvc, tq=_pick_tile(t, 256))
        yd = _attn_neighbour(dq, dk, dv, dkc, dvc, bias_tab, rows_per_step=8)

        hcf, hcb, ctx_state = _mlstm_scan(cqc, ckc, cvc, cgc, zero_state, mlstm_consts)
        hf, hb, _ = _mlstm_scan(cq, ck, cv, cg, ctx_state, mlstm_consts)

        x = _out_proj(x, ya, yb, hf, hb, co, yd, mod, lat_row, gg, w_out_l, tm=tm_lat)
        ffn = functools.partial(_conv_ffn, g2=g2, wu=w_up[layer].astype(BF16), cw=conv_l,
                                wd=w_down[layer].astype(BF16))
        x = ffn(x, mod, lat_row, g_final=None if need_ctx else g_final[None], tm=tm_lat)
        if need_ctx:
            yac, ybc, ydc = _ctx_attention(sink, aqc, akc, avc, bqc, bkc, bvc, dqc, dkc, dvc)
            ctx = _out_proj(ctx, yac, ybc, hcf, hcb, coc, ydc, mod, ctx_row, gg, w_out_l, tm=tm_ctx)
            ctx = ffn(ctx, mod, ctx_row, g_final=None, tm=tm_ctx)
    return x
```

```python
import functools

import jax
import jax.numpy as jnp
from jax import lax
from jax.experimental import pallas as pl
from jax.experimental.pallas import tpu as pltpu

F32 = jnp.float32
BF16 = jnp.bfloat16

N_HEADS = 4
HEAD_DIM = 64
GROUP_W = N_HEADS * HEAD_DIM
KV_W = 2 * HEAD_DIM
GRID_W = 64
WINDOW = 128
NA_KH = 8
NA_KW = 16
ROPE_THETA = 10000.0
EPS = 1e-6
NEG_INF = -1e30
LANES = 128
MLSTM_L = 256
HALO = 8
MIB = 1024 * 1024
LOG2_E = 1.4426950408889634
MAX_UNSHIFTED_LOG2_SCORE = 60.0


def _params(sem, vmem_mib):
    return pltpu.CompilerParams(dimension_semantics=sem, vmem_limit_bytes=vmem_mib * MIB)


def _dot(a, b):
    return jnp.dot(a, b, preferred_element_type=F32)


def _dot_nt(a, b):
    return lax.dot_general(a, b, (((1,), (1,)), ((), ())), preferred_element_type=F32)


def _const_spec(shape):
    return pl.BlockSpec(shape, lambda *_: (0,) * len(shape), pipeline_mode=pl.Buffered(1))


def _mod_kernel(c_ref, w_ref, b_ref, o_ref):
    c = c_ref[...]
    a = (c * jax.nn.sigmoid(c)).astype(BF16)
    o_ref[0] = _dot(a, w_ref[0].astype(BF16)) + b_ref[0]


def _modulation(c_rows, w_mod, b_mod):
    depth, d, n = w_mod.shape
    tn = 1536
    return pl.pallas_call(
        _mod_kernel,
        grid=(depth, n // tn),
        in_specs=[pl.BlockSpec((8, d), lambda l, j: (0, 0)),
                  pl.BlockSpec((1, d, tn), lambda l, j: (l, 0, j)),
                  pl.BlockSpec((1, 1, tn), lambda l, j: (l, 0, j))],
        out_specs=pl.BlockSpec((1, 8, tn), lambda l, j: (l, 0, j)),
        out_shape=jax.ShapeDtypeStruct((depth, 8, n), F32),
        compiler_params=_params(("parallel", "parallel"), 40),
        name="modulation",
    )(c_rows, w_mod, b_mod.reshape(depth, 1, n))


def _rms_rows(x):
    return x * lax.rsqrt(jnp.mean(x * x, axis=-1, keepdims=True) + EPS)


def _head_rms(z, pool, gain):
    z2 = z * z
    hi = z2.astype(BF16)
    lo = (z2 - hi.astype(F32)).astype(BF16)
    ms = _dot(hi, pool) + _dot(lo, pool)
    return z * lax.rsqrt(ms + EPS) * gain


def _rope(z, cos, sin):
    lane = lax.broadcasted_iota(jnp.int32, z.shape, 1)
    first = (lane % 32) < 16
    partner = jnp.where(first, pltpu.roll(z, LANES - 16, 1), pltpu.roll(z, 16, 1))
    return z * cos + partner * sin


def _inproj_kernel(x_ref, mod_ref, g1_ref, w_ref, pool_ref, gq_ref, gk_ref, cos_ref, sin_ref, gb_ref,
                   aq_ref, ak_ref, av_ref, bq_ref, bk_ref, bv_ref, cq_ref, ck_ref, cv_ref, co_ref, cg_ref,
                   dq_ref, dk_ref, dv_ref, *, d_model, rope):
    d = d_model
    mod = mod_ref[0]
    xn = _rms_rows(x_ref[0]) * g1_ref[...]
    hb = (xn * (1.0 + mod[:, d:2 * d]) + mod[:, 0:d]).astype(BF16)
    projected = _dot(hb, w_ref[...])

    def proj(seg, width=GROUP_W):
        return projected[:, seg * GROUP_W:seg * GROUP_W + width]

    def rot(z):
        if not rope:
            return z
        cos, sin = cos_ref[...], sin_ref[...]
        return jnp.concatenate([_rope(z[:, i:i + LANES], cos, sin) for i in range(0, z.shape[1], LANES)], axis=1)

    scale = HEAD_DIM ** -0.5
    pool = pool_ref[...]
    aq_ref[0] = (rot(_head_rms(proj(0), pool, gq_ref[...])) * (scale * LOG2_E)).astype(BF16)
    akv = proj(1)
    ak_ref[0] = rot(_head_rms(akv[:, :KV_W], pool[:KV_W, :KV_W], gk_ref[...])).astype(BF16)
    av_ref[0] = jnp.concatenate([akv[:, KV_W:], jnp.ones((akv.shape[0], LANES), F32)], axis=1).astype(BF16)
    bq_ref[0] = (rot(proj(2)) * scale).astype(BF16)
    bkv = proj(3)
    bk_ref[0] = rot(bkv[:, :KV_W]).astype(BF16)
    bv_ref[0] = bkv[:, KV_W:].astype(BF16)
    cq_ref[0] = (proj(4) * scale).astype(BF16)
    ck_ref[0] = proj(5).astype(BF16)
    cv_ref[0] = proj(6).astype(BF16)
    co_ref[0] = proj(7)
    dq_ref[0] = (proj(8) * scale).astype(BF16)
    dk_ref[0] = proj(9).astype(BF16)
    dv_ref[0] = proj(10).astype(BF16)
    cg_ref[0] = proj(11, LANES) + gb_ref[...]


def _in_proj(x, mod, mod_row, g1, w, pool, gq, gk, cos, sin, gbias, *, rope, tm):
    b, t, d = x.shape
    n_w = w.shape[1]
    widths = [(GROUP_W, BF16), (KV_W, BF16), (2 * LANES, BF16),
              (GROUP_W, BF16), (KV_W, BF16), (KV_W, BF16),
              (GROUP_W, BF16), (GROUP_W, BF16), (GROUP_W, BF16), (GROUP_W, F32), (LANES, F32),
              (GROUP_W, BF16), (GROUP_W, BF16), (GROUP_W, BF16)]
    row = lambda wd: pl.BlockSpec((1, wd), lambda bi, j: (0, 0))
    return pl.pallas_call(
        functools.partial(_inproj_kernel, d_model=d, rope=rope),
        grid=(b, t // tm),
        in_specs=[pl.BlockSpec((1, tm, d), lambda bi, j: (bi, j, 0)),
                  pl.BlockSpec((1, 1, 6 * d), lambda bi, j: (mod_row(bi), 0, 0)),
                  row(d),
                  _const_spec((d, n_w)),
                  _const_spec((GROUP_W, GROUP_W)),
                  row(GROUP_W), row(KV_W),
                  pl.BlockSpec((tm, LANES), lambda bi, j: (j, 0)),
                  pl.BlockSpec((tm, LANES), lambda bi, j: (j, 0)),
                  row(LANES)],
        out_specs=[pl.BlockSpec((1, tm, wd), lambda bi, j: (bi, j, 0)) for wd, _ in widths],
        out_shape=[jax.ShapeDtypeStruct((b, t, wd), dt) for wd, dt in widths],
        compiler_params=_params(("parallel", "parallel"), 48),
        name="in_proj",
    )(x, mod, g1, w, pool, gq, gk, cos, sin, gbias)


def _stack_gqa(q):
    qa, qb = q[:, :LANES], q[:, LANES:]
    left = lax.broadcasted_iota(jnp.int32, qa.shape, 1) < HEAD_DIM
    zero = jnp.zeros_like(qa)
    return jnp.concatenate([jnp.where(left, qa, zero), jnp.where(left, qb, zero),
                            jnp.where(left, zero, qa), jnp.where(left, zero, qb)], axis=0)


def _unstack_gqa(o, t):
    left = lax.broadcasted_iota(jnp.int32, (t, LANES), 1) < HEAD_DIM
    return jnp.concatenate([jnp.where(left, o[0:t], o[2 * t:3 * t]),
                            jnp.where(left, o[t:2 * t], o[3 * t:4 * t])], axis=1)


def _attn_a_kernel(safe_ref, q_ref, k_ref, v_ref, kc_ref, vc_ref, o_ref, qs_ref, m_ref, acc_ref, *, tq, tk, n_kb):
    qs_ref[...] = _stack_gqa(q_ref[0])
    heads = [slice(h * tq, (h + 1) * tq) for h in range(N_HEADS)]

    @pl.when(safe_ref[0] != 0)
    def _():
        for rows in heads:
            p = jnp.exp2(_dot_nt(qs_ref[rows], kc_ref[0]))
            acc_ref[rows] = _dot(p.astype(BF16), vc_ref[0])

        def body(kb, carry):
            start = pl.multiple_of(kb * tk, tk)
            for rows in heads:
                p = jnp.exp2(_dot_nt(qs_ref[rows], k_ref[0, pl.ds(start, tk), :]))
                acc_ref[rows] += _dot(p.astype(BF16), v_ref[0, pl.ds(start, tk), :])
            return carry

        lax.fori_loop(0, n_kb, body, 0)

    @pl.when(safe_ref[0] == 0)
    def _():
        for rows in heads:
            s = _dot_nt(qs_ref[rows], kc_ref[0])
            m0 = jnp.max(s, axis=-1, keepdims=True)
            m_ref[rows] = m0
            acc_ref[rows] = _dot(jnp.exp2(s - m0).astype(BF16), vc_ref[0])

        def body(kb, carry):
            start = pl.multiple_of(kb * tk, tk)
            for rows in heads:
                s = _dot_nt(qs_ref[rows], k_ref[0, pl.ds(start, tk), :])
                m_prev = m_ref[rows]
                m_new = jnp.maximum(m_prev, jnp.max(s, axis=-1, keepdims=True))
                p = jnp.exp2(s - m_new)
                acc_ref[rows] = jnp.exp2(m_prev - m_new) * acc_ref[rows] + _dot(
                    p.astype(BF16), v_ref[0, pl.ds(start, tk), :])
                m_ref[rows] = m_new
            return carry

        lax.fori_loop(0, n_kb, body, 0)

    acc = acc_ref[...]
    o_ref[0] = _unstack_gqa(acc[:, :LANES] / acc[:, LANES:], tq)


def _attn_global(safe, q, k, v, kc, vc, *, tq, tk):
    b, t, _ = q.shape
    n_ctx = kc.shape[1]
    per_b = lambda n, wd: pl.BlockSpec((1, n, wd), lambda bi, i: (bi, 0, 0))
    return pl.pallas_call(
        functools.partial(_attn_a_kernel, tq=tq, tk=tk, n_kb=t // tk),
        grid=(b, t // tq),
        in_specs=[pl.BlockSpec(memory_space=pltpu.SMEM),
                  pl.BlockSpec((1, tq, GROUP_W), lambda bi, i: (bi, i, 0)),
                  per_b(t, KV_W), per_b(t, 2 * LANES), per_b(n_ctx, KV_W), per_b(n_ctx, 2 * LANES)],
        out_specs=pl.BlockSpec((1, tq, GROUP_W), lambda bi, i: (bi, i, 0)),
        out_shape=jax.ShapeDtypeStruct((b, t, GROUP_W), F32),
        scratch_shapes=[pltpu.VMEM((4 * tq, LANES), BF16), pltpu.VMEM((4 * tq, 1), F32),
                        pltpu.VMEM((4 * tq, 2 * LANES), F32)],
        compiler_params=_params(("parallel", "parallel"), 48),
        name="attn_global",
    )(safe, q, k, v, kc, vc)


def _sink_column(sink_ref, t):
    row = lax.broadcasted_iota(jnp.int32, (4 * t, 1), 0)
    col = jnp.full((4 * t, 1), sink_ref[3], F32)
    for h in (2, 1, 0):
        col = jnp.where(row < (h + 1) * t, sink_ref[h], col)
    return col


def _attn_b_kernel(sink_ref, q_ref, k_ref, v_ref, kc_ref, vc_ref, o_ref, *, tq, t_total):
    span = tq + 2 * WINDOW
    q0 = pl.program_id(1) * tq
    ks = pl.multiple_of(jnp.clip(q0 - WINDOW, 0, t_total - span), WINDOW)
    qs = _stack_gqa(q_ref[0])
    kw = k_ref[0, pl.ds(ks, span), :]
    vw = v_ref[0, pl.ds(ks, span), :]
    q_pos = q0 + lax.broadcasted_iota(jnp.int32, (tq, span), 0)
    k_pos = ks + lax.broadcasted_iota(jnp.int32, (tq, span), 1)
    in_window = jnp.abs(k_pos - q_pos) <= WINDOW
    outs = []
    for h in range(N_HEADS):
        qh = qs[h * tq:(h + 1) * tq]
        s = jnp.where(in_window, _dot_nt(qh, kw), NEG_INF)
        sc = _dot_nt(qh, kc_ref[0])
        sink = sink_ref[h]
        m = jnp.maximum(jnp.maximum(jnp.max(s, axis=-1, keepdims=True), jnp.max(sc, axis=-1, keepdims=True)), sink)
        p = jnp.exp(s - m)
        pc = jnp.exp(sc - m)
        l = jnp.sum(p, axis=-1, keepdims=True) + jnp.sum(pc, axis=-1, keepdims=True) + jnp.exp(sink - m)
        outs.append((_dot(p.astype(BF16), vw) + _dot(pc.astype(BF16), vc_ref[0])) / l)
    o_ref[0] = _unstack_gqa(jnp.concatenate(outs, axis=0), tq)


def _attn_window(sink, q, k, v, kc, vc, *, tq):
    b, t, _ = q.shape
    n_ctx = kc.shape[1]
    assert t >= tq + 2 * WINDOW
    per_b = lambda n, wd: pl.BlockSpec((1, n, wd), lambda bi, i: (bi, 0, 0))
    return pl.pallas_call(
        functools.partial(_attn_b_kernel, tq=tq, t_total=t),
        grid=(b, t // tq),
        in_specs=[pl.BlockSpec(memory_space=pltpu.SMEM),
                  pl.BlockSpec((1, tq, GROUP_W), lambda bi, i: (bi, i, 0)),
                  per_b(t, KV_W), per_b(t, KV_W), per_b(n_ctx, KV_W), per_b(n_ctx, KV_W)],
        out_specs=pl.BlockSpec((1, tq, GROUP_W), lambda bi, i: (bi, i, 0)),
        out_shape=jax.ShapeDtypeStruct((b, t, GROUP_W), F32),
        compiler_params=_params(("parallel", "parallel"), 48),
        name="attn_window",
    )(sink, q, k, v, kc, vc)


def _head_masks(shape):
    lane = lax.broadcasted_iota(jnp.int32, shape, 1)
    return [(lane >= h * HEAD_DIM) & (lane < (h + 1) * HEAD_DIM) for h in range(N_HEADS)]


def _stack_mha(q):
    zero = jnp.zeros_like(q)
    return jnp.concatenate([jnp.where(mk, q, zero) for mk in _head_masks(q.shape)], axis=0)


def _unstack_mha(o, t):
    masks = _head_masks((t, GROUP_W))
    out = jnp.where(masks[0], o[0:t], 0.0)
    for h in range(1, N_HEADS):
        out = jnp.where(masks[h], o[h * t:(h + 1) * t], out)
    return out


def _attn_d_kernel(q_ref, k_ref, v_ref, kc_ref, vc_ref, bias_ref, o_ref, *, rows_per_step, rows):
    kc = kc_ref[0]
    vc = vc_ref[0]
    span = NA_KH * GRID_W

    def body(rr, carry):
        r = pl.program_id(1) * rows_per_step + rr
        rs = jnp.clip(r - NA_KH // 2, 0, rows - NA_KH)
        k0 = pl.multiple_of(rs * GRID_W, GRID_W)
        q0 = pl.multiple_of(rr * GRID_W, GRID_W)
        qs = _stack_mha(q_ref[0, pl.ds(q0, GRID_W), :])
        s = _dot_nt(qs, k_ref[0, pl.ds(k0, span), :]) + bias_ref[r - rs]
        sc = _dot_nt(qs, kc)
        m = jnp.maximum(jnp.max(s, axis=-1, keepdims=True), jnp.max(sc, axis=-1, keepdims=True))
        p = jnp.exp(s - m)
        pc = jnp.exp(sc - m)
        l = jnp.sum(p, axis=-1, keepdims=True) + jnp.sum(pc, axis=-1, keepdims=True)
        o = _dot(p.astype(BF16), v_ref[0, pl.ds(k0, span), :]) + _dot(pc.astype(BF16), vc)
        o_ref[0, pl.ds(q0, GRID_W), :] = _unstack_mha(o / l, GRID_W)
        return carry

    lax.fori_loop(0, rows_per_step, body, 0, unroll=True)


def _attn_neighbour(q, k, v, kc, vc, bias, *, rows_per_step):
    b, t, _ = q.shape
    n_ctx = kc.shape[1]
    rows = t // GRID_W
    assert rows >= NA_KH and rows % rows_per_step == 0
    tq = rows_per_step * GRID_W
    per_b = lambda n: pl.BlockSpec((1, n, GROUP_W), lambda bi, i: (bi, 0, 0))
    return pl.pallas_call(
        functools.partial(_attn_d_kernel, rows_per_step=rows_per_step, rows=rows),
        grid=(b, t // tq),
        in_specs=[pl.BlockSpec((1, tq, GROUP_W), lambda bi, i: (bi, i, 0)),
                  per_b(t), per_b(t), per_b(n_ctx), per_b(n_ctx),
                  _const_spec(bias.shape)],
        out_specs=pl.BlockSpec((1, tq, GROUP_W), lambda bi, i: (bi, i, 0)),
        out_shape=jax.ShapeDtypeStruct((b, t, GROUP_W), F32),
        compiler_params=_params(("parallel", "parallel"), 48),
        name="attn_neighbour",
    )(q, k, v, kc, vc, bias)


def _neighbour_bias(rpb):
    w = jnp.arange(GRID_W)
    cs = jnp.clip(w - NA_KW // 2, 0, GRID_W - NA_KW)
    col = jnp.arange(GRID_W)
    valid = (col[None, :] >= cs[:, None]) & (col[None, :] < cs[:, None] + NA_KW)
    dc = jnp.clip(col[None, :] - w[:, None] + (NA_KW - 1), 0, 2 * NA_KW - 2)
    dr = jnp.arange(NA_KH)[None, :] - jnp.arange(NA_KH)[:, None] + (NA_KH - 1)
    pick_r = (dr[:, :, None] == jnp.arange(2 * NA_KH - 1)).astype(F32)
    pick_c = (jnp.arange(2 * NA_KW - 1)[:, None, None] == dc[None]).astype(F32)
    tab = jnp.einsum('hrc,oir,cwk->ohwik', rpb.astype(F32), pick_r, pick_c, precision=lax.Precision.HIGHEST)
    tab = jnp.where(valid[None, None, :, None, :], tab, NEG_INF)
    return tab.reshape(NA_KH, N_HEADS * GRID_W, NA_KH * GRID_W)


def _ctx_attn_kernel(sink_ref, aq_ref, ak_ref, av_ref, bq_ref, bk_ref, bv_ref, dq_ref, dk_ref, dv_ref,
                     ya_ref, yb_ref, yd_ref, *, n):
    def attend(qs, k, v, sink=None):
        s = _dot_nt(qs, k)
        m = jnp.max(s, axis=-1, keepdims=True)
        if sink is not None:
            m = jnp.maximum(m, sink)
        p = jnp.exp(s - m)
        l = jnp.sum(p, axis=-1, keepdims=True)
        if sink is not None:
            l = l + jnp.exp(sink - m)
        return _dot(p.astype(BF16), v) / l

    s = _dot_nt(_stack_gqa(aq_ref[0]), ak_ref[0])
    acc = _dot(jnp.exp2(s - jnp.max(s, axis=-1, keepdims=True)).astype(BF16), av_ref[0])
    ya_ref[0] = _unstack_gqa(acc[:, :LANES] / acc[:, LANES:], n)
    yb_ref[0] = _unstack_gqa(attend(_stack_gqa(bq_ref[0]), bk_ref[0], bv_ref[0], _sink_column(sink_ref, n)), n)
    yd_ref[0] = _unstack_mha(attend(_stack_mha(dq_ref[0]), dk_ref[0], dv_ref[0]), n)


def _ctx_attention(sink, aq, ak, av, bq, bk, bv, dq, dk, dv):
    b, n, _ = aq.shape
    spec = lambda wd: pl.BlockSpec((1, n, wd), lambda bi: (bi, 0, 0))
    return pl.pallas_call(
        functools.partial(_ctx_attn_kernel, n=n),
        grid=(b,),
        in_specs=[pl.BlockSpec(memory_space=pltpu.SMEM),
                  spec(GROUP_W), spec(KV_W), spec(2 * LANES), spec(GROUP_W), spec(KV_W), spec(KV_W),
                  spec(GROUP_W), spec(GROUP_W), spec(GROUP_W)],
        out_specs=[spec(GROUP_W)] * 3,
        out_shape=[jax.ShapeDtypeStruct((b, n, GROUP_W), F32)] * 3,
        compiler_params=_params(("parallel",), 32),
        name="ctx_attention",
    )(sink, aq, ak, av, bq, bk, bv, dq, dk, dv)


def _log_sigmoid(x):
    return jnp.minimum(x, 0.0) - jnp.log1p(jnp.exp(-jnp.abs(x)))


def _split3(x):
    x1 = x.astype(BF16)
    r1 = x - x1.astype(F32)
    x2 = r1.astype(BF16)
    return x1, x2, (r1 - x2.astype(F32)).astype(BF16)


def _exact_dot_01(sel, x):
    return sum(_dot(sel, piece) for piece in _split3(x))


def _exact_dot_01_rhs(x, sel):
    return sum(_dot(piece, sel) for piece in _split3(x))


def _mlstm_chunk(q, k, v, gates, expand, tri, neg_mask, ones_cat, block, s_ref, n_ref, m_ref, *, reverse, L):
    ge = _exact_dot_01_rhs(gates, expand)
    li = ge[:, :GROUP_W] * LOG2_E
    cum = _exact_dot_01(tri, _log_sigmoid(ge[:, GROUP_W:]) * LOG2_E)
    a_t = (li - cum).T
    m_prev = m_ref[0:1, :]
    masks = _head_masks((L, GROUP_W))
    zero_q = jnp.zeros_like(q)
    zero_v = jnp.zeros_like(v)
    mu = jnp.zeros((L, GROUP_W), F32)
    scores = []
    for h in range(N_HEADS):
        lane0 = h * HEAD_DIM
        a_m = a_t[lane0:lane0 + 1, :] + neg_mask
        mu_h = jnp.maximum(jnp.max(a_m, axis=-1, keepdims=True), m_prev[:, lane0:lane0 + 1])
        qk = _dot_nt(jnp.where(masks[h], q, zero_q), k)
        scores.append((qk * jnp.exp2(a_m - mu_h)).astype(BF16))
        mu = jnp.where(masks[h], mu_h, mu)
    s_cat = jnp.concatenate(scores, axis=1)
    v_cat = jnp.concatenate([jnp.where(mk, v, zero_v) for mk in masks], axis=0)
    w_inter = jnp.exp2(m_prev - mu)
    num = _dot(s_cat, v_cat) + w_inter * _dot(q, s_ref[...].astype(BF16))
    den = _dot(s_cat, ones_cat) + w_inter * _dot(q, n_ref[...].astype(BF16))
    h_out = num / jnp.maximum(jnp.abs(den), jnp.exp2(-(cum + mu)))

    end_row = 0 if reverse else L - 1
    cum_end = cum[end_row:end_row + 1, :]
    log_end = cum_end - cum + li
    m_new = jnp.maximum(cum_end + m_prev, jnp.max(log_end, axis=0, keepdims=True))
    decay = jnp.exp2(cum_end + m_prev - m_new)
    w_end = jnp.exp2(log_end - m_new)
    k_t = k.astype(F32).T.astype(BF16)
    upd = _dot(k_t, jnp.concatenate([(v.astype(F32) * w_end).astype(BF16), w_end.astype(BF16)], axis=1))
    s_ref[...] = (s_ref[...] * decay + upd[:, :GROUP_W]) * block
    n_ref[...] = (n_ref[...] * decay + upd[:, GROUP_W:]) * block
    m_ref[...] = jnp.broadcast_to(m_new, m_ref.shape)
    return h_out


def _mlstm_kernel(qf_ref, kf_ref, vf_ref, gf_ref, qb_ref, kb_ref, vb_ref, gb_ref,
                  expand_ref, tri_ref, neg_ref, ones_ref, block_ref, s0_ref, n0_ref, m0_ref,
                  hf_ref, hb_ref, s_out, n_out, m_out, s_ref, n_ref, m_ref, *, chunk):
    @pl.when(pl.program_id(1) == 0)
    def _():
        s_ref[...] = s0_ref[0]
        n_ref[...] = n0_ref[0]
        m_ref[...] = m0_ref[0]

    sides = ((qf_ref, kf_ref, vf_ref, gf_ref, hf_ref), (qb_ref, kb_ref, vb_ref, gb_ref, hb_ref))
    for d, (q_ref, k_ref, v_ref, g_ref, h_ref) in enumerate(sides):
        h = _mlstm_chunk(q_ref[0], k_ref[0], v_ref[0], g_ref[0], expand_ref[d], tri_ref[d], neg_ref[d],
                         ones_ref[...], block_ref[...], s_ref.at[d], n_ref.at[d], m_ref.at[d],
                         reverse=bool(d), L=chunk)
        h_ref[0] = h.astype(h_ref.dtype)
    s_out[0] = s_ref[...]
    n_out[0] = n_ref[...]
    m_out[0] = m_ref[...]


def _mlstm_consts(L):
    lane = jnp.arange(LANES)[:, None]
    col = jnp.arange(2 * GROUP_W)[None, :]
    col_head = (col % GROUP_W) // HEAD_DIM
    expand = jnp.stack([(lane == 8 * d + 4 * (col // GROUP_W) + col_head) for d in range(2)]).astype(BF16)
    t_idx = jnp.arange(L)[:, None]
    s_idx = jnp.arange(L)[None, :]
    seen = jnp.stack([s_idx <= t_idx, s_idx >= t_idx])
    head = jnp.arange(GROUP_W) // HEAD_DIM
    ones_cat = (jnp.repeat(jnp.arange(N_HEADS), L)[:, None] == head[None, :]).astype(BF16)
    block = (head[:, None] == head[None, :]).astype(F32)
    return expand, seen.astype(BF16), jnp.where(seen, 0.0, NEG_INF).astype(F32), ones_cat, block


def _mlstm_scan(q, k, v, gates, state, consts):
    b, t, _ = q.shape
    L = MLSTM_L
    nc = t // L
    fwd = lambda wd: pl.BlockSpec((1, L, wd), lambda bi, j: (bi, j, 0))
    bwd = lambda wd: pl.BlockSpec((1, L, wd), lambda bi, j: (bi, nc - 1 - j, 0))
    st = lambda r: pl.BlockSpec((1, 2, r, GROUP_W), lambda bi, j: (bi, 0, 0, 0))
    tok_specs = [fwd(GROUP_W), fwd(GROUP_W), fwd(GROUP_W), fwd(LANES),
                 bwd(GROUP_W), bwd(GROUP_W), bwd(GROUP_W), bwd(LANES)]
    hf, hb, s_fin, n_fin, m_fin = pl.pallas_call(
        functools.partial(_mlstm_kernel, chunk=L),
        grid=(b, nc),
        in_specs=tok_specs + [_const_spec(c.shape) for c in consts] + [st(GROUP_W), st(GROUP_W), st(8)],
        out_specs=[fwd(GROUP_W), bwd(GROUP_W), st(GROUP_W), st(GROUP_W), st(8)],
        out_shape=[jax.ShapeDtypeStruct((b, t, GROUP_W), BF16), jax.ShapeDtypeStruct((b, t, GROUP_W), BF16),
                   jax.ShapeDtypeStruct((b, 2, GROUP_W, GROUP_W), F32),
                   jax.ShapeDtypeStruct((b, 2, GROUP_W, GROUP_W), F32),
                   jax.ShapeDtypeStruct((b, 2, 8, GROUP_W), F32)],
        scratch_shapes=[pltpu.VMEM((2, GROUP_W, GROUP_W), F32), pltpu.VMEM((2, GROUP_W, GROUP_W), F32),
                        pltpu.VMEM((2, 8, GROUP_W), F32)],
        compiler_params=_params(("parallel", "arbitrary"), 48),
        name="mlstm",
    )(q, k, v, gates, q, k, v, gates, *consts, *state)
    return hf, hb, (s_fin, n_fin, m_fin)


def _outproj_kernel(x_ref, ya_ref, yb_ref, hf_ref, hb_ref, og_ref, yd_ref, mod_ref, gg_ref, w_ref, o_ref, *, d_model):
    d = d_model
    yc = jax.nn.sigmoid(og_ref[0]) * (hf_ref[0].astype(F32) + hb_ref[0].astype(F32))
    parts = []
    for i, y in enumerate((ya_ref[0], yb_ref[0], yc, yd_ref[0])):
        parts.append((_rms_rows(y) * gg_ref[i:i + 1, :]).astype(BF16))
    res = _dot(jnp.concatenate(parts, axis=1), w_ref[...])
    o_ref[0] = x_ref[0] + mod_ref[0][:, 2 * d:3 * d] * res


def _out_proj(x, ya, yb, hf, hb, o_gate, yd, mod, mod_row, gg, w, *, tm):
    b, t, d = x.shape
    tokd = pl.BlockSpec((1, tm, d), lambda bi, j: (bi, j, 0))
    tokg = pl.BlockSpec((1, tm, GROUP_W), lambda bi, j: (bi, j, 0))
    return pl.pallas_call(
        functools.partial(_outproj_kernel, d_model=d),
        grid=(b, t // tm),
        in_specs=[tokd, tokg, tokg, tokg, tokg, tokg, tokg,
                  pl.BlockSpec((1, 1, 6 * d), lambda bi, j: (mod_row(bi), 0, 0)),
                  _const_spec(gg.shape), _const_spec(w.shape)],
        out_specs=tokd,
        out_shape=jax.ShapeDtypeStruct((b, t, d), F32),
        compiler_params=_params(("parallel", "parallel"), 48),
        name="out_proj",
    )(x, ya, yb, hf, hb, o_gate, yd, mod, gg, w)


def _ffn_kernel(*refs, d_model, d_ff, tm, final_norm):
    if final_norm:
        x_ref, xp_ref, xn_ref, mod_ref, g2_ref, wu_ref, cw_ref, wd_ref, gf_ref, o_ref = refs
    else:
        x_ref, xp_ref, xn_ref, mod_ref, g2_ref, wu_ref, cw_ref, wd_ref, o_ref = refs
    d = d_model
    j = pl.program_id(1)
    mod = mod_ref[0]
    g2 = g2_ref[...]

    def norm_mod(x):
        return _rms_rows(x) * g2 * (1.0 + mod[:, 4 * d:5 * d]) + mod[:, 3 * d:4 * d]

    keep_prev = (j > 0).astype(F32)
    keep_next = (j < pl.num_programs(1) - 1).astype(F32)
    x = x_ref[0]
    h_ext = jnp.concatenate([norm_mod(xp_ref[0]) * keep_prev, norm_mod(x), norm_mod(xn_ref[0]) * keep_next],
                            axis=0).astype(BF16)
    n_ext = tm + 2 * HALO

    u = _dot(h_ext, wu_ref[...])
    cw = cw_ref[...]
    u = (pltpu.roll(u, 1, 0) * cw[0:1] + u * cw[1:2] + pltpu.roll(u, n_ext - 1, 0) * cw[2:3] + cw[3:4])[HALO:HALO + tm]
    gate, val = u[:, :d_ff], u[:, d_ff:]
    act = (gate * jax.nn.sigmoid(gate) * val).astype(BF16)
    y = x + mod[:, 5 * d:6 * d] * _dot(act, wd_ref[...])
    if final_norm:
        y = _rms_rows(y) * gf_ref[...]
    o_ref[0] = y


def _conv_ffn(x, mod, mod_row, g2, wu, cw, wd, g_final, *, tm):
    b, t, d = x.shape
    d_ff = wd.shape[0]
    final_norm = g_final is not None
    hb = tm // HALO
    last = t // HALO - 1
    row = pl.BlockSpec((1, d), lambda bi, j: (0, 0))
    in_specs = [pl.BlockSpec((1, tm, d), lambda bi, j: (bi, j, 0)),
                pl.BlockSpec((1, HALO, d), lambda bi, j: (bi, jnp.maximum(j * hb - 1, 0), 0)),
                pl.BlockSpec((1, HALO, d), lambda bi, j: (bi, jnp.minimum((j + 1) * hb, last), 0)),
                pl.BlockSpec((1, 1, 6 * d), lambda bi, j: (mod_row(bi), 0, 0)),
                row,
                _const_spec(wu.shape), _const_spec(cw.shape), _const_spec(wd.shape)]
    args = [x, x, x, mod, g2, wu, cw, wd]
    if final_norm:
        in_specs.append(row)
        args.append(g_final)
    return pl.pallas_call(
        functools.partial(_ffn_kernel, d_model=d, d_ff=d_ff, tm=tm, final_norm=final_norm),
        grid=(b, t // tm),
        in_specs=in_specs,
        out_specs=pl.BlockSpec((1, tm, d), lambda bi, j: (bi, j, 0)),
        out_shape=jax.ShapeDtypeStruct((b, t, d), F32),
        compiler_params=_params(("parallel", "parallel"), 56),
        name="conv_ffn",
    )(*args)


_Q_HEAD_ORDER = (0, 2, 1, 3)


def _permute_heads(w, axis):
    parts = [lax.slice_in_dim(w, h * HEAD_DIM, (h + 1) * HEAD_DIM, axis=axis) for h in _Q_HEAD_ORDER]
    return jnp.concatenate(parts, axis=axis)


def _layout_w_in(w):
    g, kv = GROUP_W, KV_W
    o_b = g + 2 * kv
    o_c = 2 * o_b
    o_g = o_c + 4 * g
    o_d = o_g + 4 * N_HEADS
    sl = lambda a, n: w[:, a:a + n]
    pad = jnp.zeros((w.shape[0], LANES - 4 * N_HEADS), w.dtype)
    return jnp.concatenate([_permute_heads(sl(0, g), 1), sl(g, 2 * kv),
                            _permute_heads(sl(o_b, g), 1), sl(o_b + g, 2 * kv),
                            sl(o_c, 4 * g), sl(o_d, 3 * g), sl(o_g, 4 * N_HEADS), pad], axis=1).astype(BF16)


def _rope_tables(t):
    pos = jnp.arange(t)
    row = (pos // GRID_W).astype(F32)
    col = (pos % GRID_W).astype(F32)
    n_freq = HEAD_DIM // 4
    freqs = ROPE_THETA ** (-jnp.arange(n_freq, dtype=F32) / n_freq)
    ang_r, ang_c = row[:, None] * freqs, col[:, None] * freqs
    cos = jnp.concatenate([jnp.cos(ang_r), jnp.cos(ang_r), jnp.cos(ang_c), jnp.cos(ang_c)], axis=1)
    sin = jnp.concatenate([-jnp.sin(ang_r), jnp.sin(ang_r), -jnp.sin(ang_c), jnp.sin(ang_c)], axis=1)
    reps = LANES // HEAD_DIM
    return jnp.tile(cos, (1, reps)), jnp.tile(sin, (1, reps))


def _pick_tile(t, pref):
    while t % pref:
        pref //= 2
    return pref


def kernel(x, c, ctx, c_ctx, w_mod, b_mod, g_norm1, g_norm2, w_in, a_q_gain, a_k_gain, b_sink, c_gate_bias,
           d_rel_bias, g_group, w_out, w_up, conv_w, conv_b, w_down, g_final):
    batch, t, d = x.shape
    n_ctx = ctx.shape[1]
    depth = w_in.shape[0]
    d_ff = w_down.shape[1]
    assert batch < 8 and d_ff % LANES == 0 and t % MLSTM_L == 0 and n_ctx % MLSTM_L == 0

    c_rows = jnp.zeros((8, d), F32).at[:batch].set(c).at[batch].set(c_ctx)
    mod_all = _modulation(c_rows, w_mod, b_mod).reshape(depth, 8, 1, 6 * d)
    lat_row = lambda bi: bi
    ctx_row = lambda bi: batch

    cos, sin = _rope_tables(t)
    pool = jnp.where(jnp.arange(GROUP_W)[:, None] // HEAD_DIM == jnp.arange(GROUP_W)[None, :] // HEAD_DIM,
                     1.0 / HEAD_DIM, 0.0).astype(BF16)
    tm_lat = _pick_tile(t, 512)
    tm_ctx = _pick_tile(n_ctx, 256)
    zero_state = (jnp.zeros((batch, 2, GROUP_W, GROUP_W), F32), jnp.zeros((batch, 2, GROUP_W, GROUP_W), F32),
                  jnp.zeros((batch, 2, 8, GROUP_W), F32))
    mlstm_consts = _mlstm_consts(MLSTM_L)

    for layer in range(depth):
        need_ctx = layer < depth - 1
        mod = mod_all[layer]
        g1 = g_norm1[layer][None]
        g2 = g_norm2[layer][None]
        w_in_l = _layout_w_in(w_in[layer])
        gq = jnp.tile(a_q_gain[layer], N_HEADS)[None]
        gk = jnp.tile(a_k_gain[layer], 2)[None]
        gbias = jnp.pad(c_gate_bias[layer], (0, LANES - 4 * N_HEADS))[None]
        gg = g_group[layer].reshape(N_HEADS, GROUP_W)
        gg = jnp.concatenate([_permute_heads(gg[0:2], 1), gg[2:4]], axis=0)
        w_out_l = w_out[layer]
        w_out_l = jnp.concatenate([_permute_heads(w_out_l[0:GROUP_W], 0),
                                   _permute_heads(w_out_l[GROUP_W:2 * GROUP_W], 0),
                                   w_out_l[2 * GROUP_W:]], axis=0).astype(BF16)
        conv_l = jnp.concatenate([conv_w[layer], conv_b[layer][None],
                                  jnp.zeros((8 - 1 - conv_w.shape[1], 2 * d_ff), F32)], axis=0)
        sink = b_sink[layer]
        bias_tab = _neighbour_bias(d_rel_bias[layer])

        proj = functools.partial(_in_proj, g1=g1, w=w_in_l, pool=pool, gq=gq, gk=gk, gbias=gbias)
        (aqc, akc, avc, bqc, bkc, bvc, cqc, ckc, cvc, coc, cgc, dqc, dkc, dvc) = proj(
            ctx, mod, ctx_row, cos=cos, sin=sin, rope=False, tm=tm_ctx)
        (aq, ak, av, bq, bk, bv, cq, ck, cv, co, cg, dq, dk, dv) = proj(
            x, mod, lat_row, cos=cos, sin=sin, rope=True, tm=tm_lat)

        score_bound = 1.02 * LOG2_E * HEAD_DIM ** 0.5 * jnp.max(jnp.abs(a_q_gain[layer])) * jnp.max(jnp.abs(a_k_gain[layer]))
        safe = (score_bound <= MAX_UNSHIFTED_LOG2_SCORE).astype(jnp.int32).reshape(1)
        ya = _attn_global(safe, aq, ak, av, akc, avc, tq=_pick_tile(t, 256), tk=_pick_tile(t, 1024))
        yb = _attn_window(sink, bq, bk, bv, bkc, bvc, tq=_pick_tile(t, 256))
        yd = _attn_neighbour(dq, dk, dv, dkc, dvc, bias_tab, rows_per_step=8)

        hcf, hcb, ctx_state = _mlstm_scan(cqc, ckc, cvc, cgc, zero_state, mlstm_consts)
        hf, hb, _ = _mlstm_scan(cq, ck, cv, cg, ctx_state, mlstm_consts)

        x = _out_proj(x, ya, yb, hf, hb, co, yd, mod, lat_row, gg, w_out_l, tm=tm_lat)
        ffn = functools.partial(_conv_ffn, g2=g2, wu=w_up[layer].astype(BF16), cw=conv_l,
                                wd=w_down[layer].astype(BF16))
        x = ffn(x, mod, lat_row, g_final=None if need_ctx else g_final[None], tm=_pick_tile(t, 512))
        if need_ctx:
            yac, ybc, ydc = _ctx_attention(sink, aqc, akc, avc, bqc, bkc, bvc, dqc, dkc, dvc)
            ctx = _out_proj(ctx, yac, ybc, hcf, hcb, coc, ydc, mod, ctx_row, gg, w_out_l, tm=tm_ctx)
            ctx = ffn(ctx, mod, ctx_row, g_final=None, tm=tm_ctx)
    return x
```

```python
import functools

import jax
import jax.numpy as jnp
from jax import lax
from jax.experimental import pallas as pl
from jax.experimental.pallas import tpu as pltpu

F32 = jnp.float32
BF16 = jnp.bfloat16

N_HEADS = 4
HEAD_DIM = 64
GROUP_W = N_HEADS * HEAD_DIM
KV_W = 2 * HEAD_DIM
GRID_W = 64
WINDOW = 128
NA_KH = 8
NA_KW = 16
ROPE_THETA = 10000.0
EPS = 1e-6
NEG_INF = -1e30
LANES = 128
MLSTM_L = 256
HALO = 8
MIB = 1024 * 1024
LOG2_E = 1.4426950408889634
MAX_UNSHIFTED_LOG2_SCORE = 60.0


def _params(sem, vmem_mib):
    return pltpu.CompilerParams(dimension_semantics=sem, vmem_limit_bytes=vmem_mib * MIB)


def _dot(a, b):
    return jnp.dot(a, b, preferred_element_type=F32)


def _dot_nt(a, b):
    return lax.dot_general(a, b, (((1,), (1,)), ((), ())), preferred_element_type=F32)


def _const_spec(shape):
    return pl.BlockSpec(shape, lambda *_: (0,) * len(shape), pipeline_mode=pl.Buffered(1))


def _mod_kernel(c_ref, w_ref, b_ref, o_ref):
    c = c_ref[...]
    a = (c * jax.nn.sigmoid(c)).astype(BF16)
    o_ref[0] = _dot(a, w_ref[0].astype(BF16)) + b_ref[0]


def _modulation(c_rows, w_mod, b_mod):
    depth, d, n = w_mod.shape
    tn = 1536
    return pl.pallas_call(
        _mod_kernel,
        grid=(depth, n // tn),
        in_specs=[pl.BlockSpec((8, d), lambda l, j: (0, 0)),
                  pl.BlockSpec((1, d, tn), lambda l, j: (l, 0, j)),
                  pl.BlockSpec((1, 1, tn), lambda l, j: (l, 0, j))],
        out_specs=pl.BlockSpec((1, 8, tn), lambda l, j: (l, 0, j)),
        out_shape=jax.ShapeDtypeStruct((depth, 8, n), F32),
        compiler_params=_params(("parallel", "parallel"), 40),
        name="modulation",
    )(c_rows, w_mod, b_mod.reshape(depth, 1, n))


def _rms_rows(x):
    return x * lax.rsqrt(jnp.mean(x * x, axis=-1, keepdims=True) + EPS)


def _head_rms(z, pool, gain):
    z2 = z * z
    hi = z2.astype(BF16)
    lo = (z2 - hi.astype(F32)).astype(BF16)
    ms = _dot(hi, pool) + _dot(lo, pool)
    return z * lax.rsqrt(ms + EPS) * gain


def _rope(z, cos, sin):
    lane = lax.broadcasted_iota(jnp.int32, z.shape, 1)
    first = (lane % 32) < 16
    partner = jnp.where(first, pltpu.roll(z, LANES - 16, 1), pltpu.roll(z, 16, 1))
    return z * cos + partner * sin


def _inproj_kernel(x_ref, mod_ref, g1_ref, w_ref, pool_ref, gq_ref, gk_ref, cos_ref, sin_ref, gb_ref,
                   aq_ref, ak_ref, av_ref, bq_ref, bk_ref, bv_ref, cq_ref, ck_ref, cv_ref, co_ref, cg_ref,
                   dq_ref, dk_ref, dv_ref, *, d_model, rope):
    d = d_model
    mod = mod_ref[0]
    xn = _rms_rows(x_ref[0]) * g1_ref[...]
    hb = (xn * (1.0 + mod[:, d:2 * d]) + mod[:, 0:d]).astype(BF16)
    projected = _dot(hb, w_ref[...])

    def proj(seg, width=GROUP_W):
        return projected[:, seg * GROUP_W:seg * GROUP_W + width]

    def rot(z):
        if not rope:
            return z
        cos, sin = cos_ref[...], sin_ref[...]
        return jnp.concatenate([_rope(z[:, i:i + LANES], cos, sin) for i in range(0, z.shape[1], LANES)], axis=1)

    scale = HEAD_DIM ** -0.5
    pool = pool_ref[...]
    aq_ref[0] = (rot(_head_rms(proj(0), pool, gq_ref[...])) * (scale * LOG2_E)).astype(BF16)
    akv = proj(1)
    ak_ref[0] = rot(_head_rms(akv[:, :KV_W], pool[:KV_W, :KV_W], gk_ref[...])).astype(BF16)
    ones = jnp.ones((akv.shape[0], LANES), F32)
    av_ref[0] = jnp.concatenate([akv[:, KV_W:], ones], axis=1).astype(BF16)
    bq_ref[0] = (rot(proj(2)) * (scale * LOG2_E)).astype(BF16)
    bkv = proj(3)
    bk_ref[0] = rot(bkv[:, :KV_W]).astype(BF16)
    bv_ref[0] = jnp.concatenate([bkv[:, KV_W:], ones], axis=1).astype(BF16)
    cq_ref[0] = (proj(4) * scale).astype(BF16)
    ck_ref[0] = proj(5).astype(BF16)
    cv_ref[0] = proj(6).astype(BF16)
    co_ref[0] = proj(7)
    dq_ref[0] = (proj(8) * scale).astype(BF16)
    dk_ref[0] = proj(9).astype(BF16)
    dv_ref[0] = proj(10).astype(BF16)
    gates = proj(11, LANES) + gb_ref[...]
    forget_lane = (lax.broadcasted_iota(jnp.int32, gates.shape, 1) % 8) >= 4
    cg_ref[0] = jnp.where(forget_lane, _log_sigmoid(gates), gates) * LOG2_E


def _in_proj(x, mod, mod_row, g1, w, pool, gq, gk, cos, sin, gbias, *, rope, tm):
    b, t, d = x.shape
    n_w = w.shape[1]
    widths = [(GROUP_W, BF16), (KV_W, BF16), (2 * LANES, BF16),
              (GROUP_W, BF16), (KV_W, BF16), (2 * LANES, BF16),
              (GROUP_W, BF16), (GROUP_W, BF16), (GROUP_W, BF16), (GROUP_W, F32), (LANES, F32),
              (GROUP_W, BF16), (GROUP_W, BF16), (GROUP_W, BF16)]
    row = lambda wd: pl.BlockSpec((1, wd), lambda bi, j: (0, 0))
    return pl.pallas_call(
        functools.partial(_inproj_kernel, d_model=d, rope=rope),
        grid=(b, t // tm),
        in_specs=[pl.BlockSpec((1, tm, d), lambda bi, j: (bi, j, 0)),
                  pl.BlockSpec((1, 1, 6 * d), lambda bi, j: (mod_row(bi), 0, 0)),
                  row(d),
                  _const_spec((d, n_w)),
                  _const_spec((GROUP_W, GROUP_W)),
                  row(GROUP_W), row(KV_W),
                  pl.BlockSpec((tm, LANES), lambda bi, j: (j, 0)),
                  pl.BlockSpec((tm, LANES), lambda bi, j: (j, 0)),
                  row(LANES)],
        out_specs=[pl.BlockSpec((1, tm, wd), lambda bi, j: (bi, j, 0)) for wd, _ in widths],
        out_shape=[jax.ShapeDtypeStruct((b, t, wd), dt) for wd, dt in widths],
        compiler_params=_params(("parallel", "parallel"), 48),
        name="in_proj",
    )(x, mod, g1, w, pool, gq, gk, cos, sin, gbias)


def _stack_gqa(q):
    qa, qb = q[:, :LANES], q[:, LANES:]
    left = lax.broadcasted_iota(jnp.int32, qa.shape, 1) < HEAD_DIM
    zero = jnp.zeros_like(qa)
    return jnp.concatenate([jnp.where(left, qa, zero), jnp.where(left, qb, zero),
                            jnp.where(left, zero, qa), jnp.where(left, zero, qb)], axis=0)


def _unstack_gqa(o, t):
    left = lax.broadcasted_iota(jnp.int32, (t, LANES), 1) < HEAD_DIM
    return jnp.concatenate([jnp.where(left, o[0:t], o[2 * t:3 * t]),
                            jnp.where(left, o[t:2 * t], o[3 * t:4 * t])], axis=1)


def _attn_a_kernel(safe_ref, q_ref, k_ref, v_ref, kc_ref, vc_ref, o_ref, qs_ref, m_ref, acc_ref, *, tq, tk, n_kb):
    qs_ref[...] = _stack_gqa(q_ref[0])
    heads = [slice(h * tq, (h + 1) * tq) for h in range(N_HEADS)]

    @pl.when(safe_ref[0] != 0)
    def _():
        for rows in heads:
            p = jnp.exp2(_dot_nt(qs_ref[rows], kc_ref[0]))
            acc_ref[rows] = _dot(p.astype(BF16), vc_ref[0])

        def body(kb, carry):
            start = pl.multiple_of(kb * tk, tk)
            for rows in heads:
                p = jnp.exp2(_dot_nt(qs_ref[rows], k_ref[0, pl.ds(start, tk), :]))
                acc_ref[rows] += _dot(p.astype(BF16), v_ref[0, pl.ds(start, tk), :])
            return carry

        lax.fori_loop(0, n_kb, body, 0)

    @pl.when(safe_ref[0] == 0)
    def _():
        for rows in heads:
            s = _dot_nt(qs_ref[rows], kc_ref[0])
            m0 = jnp.max(s, axis=-1, keepdims=True)
            m_ref[rows] = m0
            acc_ref[rows] = _dot(jnp.exp2(s - m0).astype(BF16), vc_ref[0])

        def body(kb, carry):
            start = pl.multiple_of(kb * tk, tk)
            for rows in heads:
                s = _dot_nt(qs_ref[rows], k_ref[0, pl.ds(start, tk), :])
                m_prev = m_ref[rows]
                m_new = jnp.maximum(m_prev, jnp.max(s, axis=-1, keepdims=True))
                p = jnp.exp2(s - m_new)
                acc_ref[rows] = jnp.exp2(m_prev - m_new) * acc_ref[rows] + _dot(
                    p.astype(BF16), v_ref[0, pl.ds(start, tk), :])
                m_ref[rows] = m_new
            return carry

        lax.fori_loop(0, n_kb, body, 0)

    acc = acc_ref[...]
    o_ref[0] = _unstack_gqa(acc[:, :LANES] / acc[:, LANES:], tq)


def _attn_global(safe, q, k, v, kc, vc, *, tq, tk):
    b, t, _ = q.shape
    n_ctx = kc.shape[1]
    per_b = lambda n, wd: pl.BlockSpec((1, n, wd), lambda bi, i: (bi, 0, 0))
    return pl.pallas_call(
        functools.partial(_attn_a_kernel, tq=tq, tk=tk, n_kb=t // tk),
        grid=(b, t // tq),
        in_specs=[pl.BlockSpec(memory_space=pltpu.SMEM),
                  pl.BlockSpec((1, tq, GROUP_W), lambda bi, i: (bi, i, 0)),
                  per_b(t, KV_W), per_b(t, 2 * LANES), per_b(n_ctx, KV_W), per_b(n_ctx, 2 * LANES)],
        out_specs=pl.BlockSpec((1, tq, GROUP_W), lambda bi, i: (bi, i, 0)),
        out_shape=jax.ShapeDtypeStruct((b, t, GROUP_W), F32),
        scratch_shapes=[pltpu.VMEM((4 * tq, LANES), BF16), pltpu.VMEM((4 * tq, 1), F32),
                        pltpu.VMEM((4 * tq, 2 * LANES), F32)],
        compiler_params=_params(("parallel", "parallel"), 48),
        name="attn_global",
    )(safe, q, k, v, kc, vc)


def _sink_column(sink_ref, t):
    row = lax.broadcasted_iota(jnp.int32, (4 * t, 1), 0)
    col = jnp.full((4 * t, 1), sink_ref[3], F32)
    for h in (2, 1, 0):
        col = jnp.where(row < (h + 1) * t, sink_ref[h], col)
    return col


def _attn_b_kernel(sink_ref, q_ref, k_ref, v_ref, kc_ref, vc_ref, o_ref, *, tq, t_total):
    span = tq + 2 * WINDOW
    q0 = pl.program_id(1) * tq
    ks = pl.multiple_of(jnp.clip(q0 - WINDOW, 0, t_total - span), WINDOW)
    qs = _stack_gqa(q_ref[0])
    kw = k_ref[0, pl.ds(ks, span), :]
    vw = v_ref[0, pl.ds(ks, span), :]
    q_pos = q0 + lax.broadcasted_iota(jnp.int32, (tq, span), 0)
    k_pos = ks + lax.broadcasted_iota(jnp.int32, (tq, span), 1)
    in_window = jnp.abs(k_pos - q_pos) <= WINDOW
    outs = []
    for h in range(N_HEADS):
        qh = qs[h * tq:(h + 1) * tq]
        s = jnp.where(in_window, _dot_nt(qh, kw), NEG_INF)
        sc = _dot_nt(qh, kc_ref[0])
        sink = sink_ref[h] * LOG2_E
        m = jnp.maximum(jnp.maximum(jnp.max(s, axis=-1, keepdims=True), jnp.max(sc, axis=-1, keepdims=True)), sink)
        acc = _dot(jnp.exp2(s - m).astype(BF16), vw) + _dot(jnp.exp2(sc - m).astype(BF16), vc_ref[0])
        outs.append(acc[:, :LANES] / (acc[:, LANES:] + jnp.exp2(sink - m)))
    o_ref[0] = _unstack_gqa(jnp.concatenate(outs, axis=0), tq)


def _attn_window(sink, q, k, v, kc, vc, *, tq):
    b, t, _ = q.shape
    n_ctx = kc.shape[1]
    assert t >= tq + 2 * WINDOW
    per_b = lambda n, wd: pl.BlockSpec((1, n, wd), lambda bi, i: (bi, 0, 0))
    return pl.pallas_call(
        functools.partial(_attn_b_kernel, tq=tq, t_total=t),
        grid=(b, t // tq),
        in_specs=[pl.BlockSpec(memory_space=pltpu.SMEM),
                  pl.BlockSpec((1, tq, GROUP_W), lambda bi, i: (bi, i, 0)),
                  per_b(t, KV_W), per_b(t, 2 * LANES), per_b(n_ctx, KV_W), per_b(n_ctx, 2 * LANES)],
        out_specs=pl.BlockSpec((1, tq, GROUP_W), lambda bi, i: (bi, i, 0)),
        out_shape=jax.ShapeDtypeStruct((b, t, GROUP_W), F32),
        compiler_params=_params(("parallel", "parallel"), 48),
        name="attn_window",
    )(sink, q, k, v, kc, vc)


def _head_masks(shape):
    lane = lax.broadcasted_iota(jnp.int32, shape, 1)
    return [(lane >= h * HEAD_DIM) & (lane < (h + 1) * HEAD_DIM) for h in range(N_HEADS)]


def _stack_mha(q):
    zero = jnp.zeros_like(q)
    return jnp.concatenate([jnp.where(mk, q, zero) for mk in _head_masks(q.shape)], axis=0)


def _unstack_mha(o, t):
    masks = _head_masks((t, GROUP_W))
    out = jnp.where(masks[0], o[0:t], 0.0)
    for h in range(1, N_HEADS):
        out = jnp.where(masks[h], o[h * t:(h + 1) * t], out)
    return out


def _attn_d_kernel(q_ref, k_ref, v_ref, kc_ref, vc_ref, bias_ref, o_ref, *, rows_per_step, rows):
    kc = kc_ref[0]
    vc = vc_ref[0]
    span = NA_KH * GRID_W

    def body(rr, carry):
        r = pl.program_id(1) * rows_per_step + rr
        rs = jnp.clip(r - NA_KH // 2, 0, rows - NA_KH)
        k0 = pl.multiple_of(rs * GRID_W, GRID_W)
        q0 = pl.multiple_of(rr * GRID_W, GRID_W)
        qs = _stack_mha(q_ref[0, pl.ds(q0, GRID_W), :])
        s = _dot_nt(qs, k_ref[0, pl.ds(k0, span), :]) + bias_ref[r - rs]
        sc = _dot_nt(qs, kc)
        m = jnp.maximum(jnp.max(s, axis=-1, keepdims=True), jnp.max(sc, axis=-1, keepdims=True))
        p = jnp.exp(s - m)
        pc = jnp.exp(sc - m)
        l = jnp.sum(p, axis=-1, keepdims=True) + jnp.sum(pc, axis=-1, keepdims=True)
        o = _dot(p.astype(BF16), v_ref[0, pl.ds(k0, span), :]) + _dot(pc.astype(BF16), vc)
        o_ref[0, pl.ds(q0, GRID_W), :] = _unstack_mha(o / l, GRID_W)
        return carry

    lax.fori_loop(0, rows_per_step, body, 0, unroll=True)


def _attn_neighbour(q, k, v, kc, vc, bias, *, rows_per_step):
    b, t, _ = q.shape
    n_ctx = kc.shape[1]
    rows = t // GRID_W
    assert rows >= NA_KH and rows % rows_per_step == 0
    tq = rows_per_step * GRID_W
    per_b = lambda n: pl.BlockSpec((1, n, GROUP_W), lambda bi, i: (bi, 0, 0))
    return pl.pallas_call(
        functools.partial(_attn_d_kernel, rows_per_step=rows_per_step, rows=rows),
        grid=(b, t // tq),
        in_specs=[pl.BlockSpec((1, tq, GROUP_W), lambda bi, i: (bi, i, 0)),
                  per_b(t), per_b(t), per_b(n_ctx), per_b(n_ctx),
                  _const_spec(bias.shape)],
        out_specs=pl.BlockSpec((1, tq, GROUP_W), lambda bi, i: (bi, i, 0)),
        out_shape=jax.ShapeDtypeStruct((b, t, GROUP_W), F32),
        compiler_params=_params(("parallel", "parallel"), 48),
        name="attn_neighbour",
    )(q, k, v, kc, vc, bias)


def _neighbour_bias(rpb):
    w = jnp.arange(GRID_W)
    cs = jnp.clip(w - NA_KW // 2, 0, GRID_W - NA_KW)
    col = jnp.arange(GRID_W)
    valid = (col[None, :] >= cs[:, None]) & (col[None, :] < cs[:, None] + NA_KW)
    dc = jnp.clip(col[None, :] - w[:, None] + (NA_KW - 1), 0, 2 * NA_KW - 2)
    dr = jnp.arange(NA_KH)[None, :] - jnp.arange(NA_KH)[:, None] + (NA_KH - 1)
    pick_r = (dr[:, :, None] == jnp.arange(2 * NA_KH - 1)).astype(F32)
    pick_c = (jnp.arange(2 * NA_KW - 1)[:, None, None] == dc[None]).astype(F32)
    tab = jnp.einsum('hrc,oir,cwk->ohwik', rpb.astype(F32), pick_r, pick_c, precision=lax.Precision.HIGHEST)
    tab = jnp.where(valid[None, None, :, None, :], tab, NEG_INF)
    return tab.reshape(NA_KH, N_HEADS * GRID_W, NA_KH * GRID_W)


def _ctx_attn_kernel(sink_ref, aq_ref, ak_ref, av_ref, bq_ref, bk_ref, bv_ref, dq_ref, dk_ref, dv_ref,
                     ya_ref, yb_ref, yd_ref, *, n):
    def attend(qs, k, v, sink=None):
        s = _dot_nt(qs, k)
        m = jnp.max(s, axis=-1, keepdims=True)
        if sink is not None:
            m = jnp.maximum(m, sink)
        p = jnp.exp(s - m)
        l = jnp.sum(p, axis=-1, keepdims=True)
        if sink is not None:
            l = l + jnp.exp(sink - m)
        return _dot(p.astype(BF16), v) / l

    s = _dot_nt(_stack_gqa(aq_ref[0]), ak_ref[0])
    acc = _dot(jnp.exp2(s - jnp.max(s, axis=-1, keepdims=True)).astype(BF16), av_ref[0])
    ya_ref[0] = _unstack_gqa(acc[:, :LANES] / acc[:, LANES:], n)
    s = _dot_nt(_stack_gqa(bq_ref[0]), bk_ref[0])
    sink = _sink_column(sink_ref, n) * LOG2_E
    m = jnp.maximum(jnp.max(s, axis=-1, keepdims=True), sink)
    acc = _dot(jnp.exp2(s - m).astype(BF16), bv_ref[0])
    yb_ref[0] = _unstack_gqa(acc[:, :LANES] / (acc[:, LANES:] + jnp.exp2(sink - m)), n)
    yd_ref[0] = _unstack_mha(attend(_stack_mha(dq_ref[0]), dk_ref[0], dv_ref[0]), n)


def _ctx_attention(sink, aq, ak, av, bq, bk, bv, dq, dk, dv):
    b, n, _ = aq.shape
    spec = lambda wd: pl.BlockSpec((1, n, wd), lambda bi: (bi, 0, 0))
    return pl.pallas_call(
        functools.partial(_ctx_attn_kernel, n=n),
        grid=(b,),
        in_specs=[pl.BlockSpec(memory_space=pltpu.SMEM),
                  spec(GROUP_W), spec(KV_W), spec(2 * LANES), spec(GROUP_W), spec(KV_W), spec(2 * LANES),
                  spec(GROUP_W), spec(GROUP_W), spec(GROUP_W)],
        out_specs=[spec(GROUP_W)] * 3,
        out_shape=[jax.ShapeDtypeStruct((b, n, GROUP_W), F32)] * 3,
        compiler_params=_params(("parallel",), 32),
        name="ctx_attention",
    )(sink, aq, ak, av, bq, bk, bv, dq, dk, dv)


def _log_sigmoid(x):
    return jnp.minimum(x, 0.0) - jnp.log1p(jnp.exp(-jnp.abs(x)))


def _split3(x):
    x1 = x.astype(BF16)
    r1 = x - x1.astype(F32)
    x2 = r1.astype(BF16)
    return x1, x2, (r1 - x2.astype(F32)).astype(BF16)


def _exact_dot_01(sel, x):
    return sum(_dot(sel, piece) for piece in _split3(x))


def _exact_dot_01_rhs(x, sel):
    return sum(_dot(piece, sel) for piece in _split3(x))


def _mlstm_chunk(q, k, v, gates, expand, tri, neg_mask, ones_cat, block, head_rows, s_ref, n_ref, m_ref, *,
                 reverse, L):
    ge = _exact_dot_01_rhs(gates, expand)
    li = ge[:, :GROUP_W]
    cum = _exact_dot_01(tri, ge[:, GROUP_W:])
    a_t = (li - cum).T
    m_prev = m_ref[0:1, :]
    head_b = [head_rows[h].astype(BF16) for h in range(N_HEADS)]
    mu = jnp.zeros((L, GROUP_W), F32)
    scores = []
    for h in range(N_HEADS):
        lane0 = h * HEAD_DIM
        a_m = a_t[lane0:lane0 + 1, :] + neg_mask
        mu_h = jnp.maximum(jnp.max(a_m, axis=-1, keepdims=True), m_prev[:, lane0:lane0 + 1])
        qk = _dot_nt(q * head_b[h], k)
        scores.append((qk * jnp.exp2(a_m - mu_h)).astype(BF16))
        mu = mu + mu_h * head_rows[h]
    s_cat = jnp.concatenate(scores, axis=1)
    v_cat = jnp.concatenate([v * hb for hb in head_b], axis=0)
    w_inter = jnp.exp2(m_prev - mu)
    num = _dot(s_cat, v_cat) + w_inter * _dot(q, s_ref[...].astype(BF16))
    den = _dot(s_cat, ones_cat) + w_inter * _dot(q, n_ref[...].astype(BF16))
    h_out = num / jnp.maximum(jnp.abs(den), jnp.exp2(-(cum + mu)))

    end_row = 0 if reverse else L - 1
    cum_end = cum[end_row:end_row + 1, :]
    log_end = cum_end - cum + li
    m_new = jnp.maximum(cum_end + m_prev, jnp.max(log_end, axis=0, keepdims=True))
    decay = jnp.exp2(cum_end + m_prev - m_new)
    w_end = jnp.exp2(log_end - m_new)
    k_t = k.astype(F32).T.astype(BF16)
    upd = _dot(k_t, jnp.concatenate([(v.astype(F32) * w_end).astype(BF16), w_end.astype(BF16)], axis=1))
    s_ref[...] = (s_ref[...] * decay + upd[:, :GROUP_W]) * block
    n_ref[...] = (n_ref[...] * decay + upd[:, GROUP_W:]) * block
    m_ref[...] = jnp.broadcast_to(m_new, m_ref.shape)
    return h_out


def _mlstm_kernel(qf_ref, kf_ref, vf_ref, gf_ref, qb_ref, kb_ref, vb_ref, gb_ref,
                  expand_ref, tri_ref, neg_ref, ones_ref, block_ref, heads_ref, s0_ref, n0_ref, m0_ref,
                  hf_ref, hb_ref, s_out, n_out, m_out, s_ref, n_ref, m_ref, *, chunk):
    @pl.when(pl.program_id(1) == 0)
    def _():
        s_ref[...] = s0_ref[0]
        n_ref[...] = n0_ref[0]
        m_ref[...] = m0_ref[0]

    sides = ((qf_ref, kf_ref, vf_ref, gf_ref, hf_ref), (qb_ref, kb_ref, vb_ref, gb_ref, hb_ref))
    for d, (q_ref, k_ref, v_ref, g_ref, h_ref) in enumerate(sides):
        h = _mlstm_chunk(q_ref[0], k_ref[0], v_ref[0], g_ref[0], expand_ref[d], tri_ref[d], neg_ref[d],
                         ones_ref[...], block_ref[...], heads_ref, s_ref.at[d], n_ref.at[d], m_ref.at[d],
                         reverse=bool(d), L=chunk)
        h_ref[0] = h.astype(h_ref.dtype)
    s_out[0] = s_ref[...]
    n_out[0] = n_ref[...]
    m_out[0] = m_ref[...]


def _mlstm_consts(L):
    lane = jnp.arange(LANES)[:, None]
    col = jnp.arange(2 * GROUP_W)[None, :]
    col_head = (col % GROUP_W) // HEAD_DIM
    expand = jnp.stack([(lane == 8 * d + 4 * (col // GROUP_W) + col_head) for d in range(2)]).astype(BF16)
    t_idx = jnp.arange(L)[:, None]
    s_idx = jnp.arange(L)[None, :]
    seen = jnp.stack([s_idx <= t_idx, s_idx >= t_idx])
    head = jnp.arange(GROUP_W) // HEAD_DIM
    ones_cat = (jnp.repeat(jnp.arange(N_HEADS), L)[:, None] == head[None, :]).astype(BF16)
    block = (head[:, None] == head[None, :]).astype(F32)
    head_rows = (jnp.arange(N_HEADS)[:, None, None] == head[None, None, :]).astype(F32)
    return expand, seen.astype(BF16), jnp.where(seen, 0.0, NEG_INF).astype(F32), ones_cat, block, head_rows


def _mlstm_scan(q, k, v, gates, state, consts):
    b, t, _ = q.shape
    L = MLSTM_L
    nc = t // L
    fwd = lambda wd: pl.BlockSpec((1, L, wd), lambda bi, j: (bi, j, 0))
    bwd = lambda wd: pl.BlockSpec((1, L, wd), lambda bi, j: (bi, nc - 1 - j, 0))
    st = lambda r: pl.BlockSpec((1, 2, r, GROUP_W), lambda bi, j: (bi, 0, 0, 0))
    tok_specs = [fwd(GROUP_W), fwd(GROUP_W), fwd(GROUP_W), fwd(LANES),
                 bwd(GROUP_W), bwd(GROUP_W), bwd(GROUP_W), bwd(LANES)]
    hf, hb, s_fin, n_fin, m_fin = pl.pallas_call(
        functools.partial(_mlstm_kernel, chunk=L),
        grid=(b, nc),
        in_specs=tok_specs + [_const_spec(c.shape) for c in consts] + [st(GROUP_W), st(GROUP_W), st(8)],
        out_specs=[fwd(GROUP_W), bwd(GROUP_W), st(GROUP_W), st(GROUP_W), st(8)],
        out_shape=[jax.ShapeDtypeStruct((b, t, GROUP_W), BF16), jax.ShapeDtypeStruct((b, t, GROUP_W), BF16),
                   jax.ShapeDtypeStruct((b, 2, GROUP_W, GROUP_W), F32),
                   jax.ShapeDtypeStruct((b, 2, GROUP_W, GROUP_W), F32),
                   jax.ShapeDtypeStruct((b, 2, 8, GROUP_W), F32)],
        scratch_shapes=[pltpu.VMEM((2, GROUP_W, GROUP_W), F32), pltpu.VMEM((2, GROUP_W, GROUP_W), F32),
                        pltpu.VMEM((2, 8, GROUP_W), F32)],
        compiler_params=_params(("parallel", "arbitrary"), 48),
        name="mlstm",
    )(q, k, v, gates, q, k, v, gates, *consts, *state)
    return hf, hb, (s_fin, n_fin, m_fin)


def _outproj_kernel(x_ref, ya_ref, yb_ref, hf_ref, hb_ref, og_ref, yd_ref, mod_ref, gg_ref, w_ref, o_ref, *, d_model):
    d = d_model
    yc = jax.nn.sigmoid(og_ref[0]) * (hf_ref[0].astype(F32) + hb_ref[0].astype(F32))
    parts = []
    for i, y in enumerate((ya_ref[0], yb_ref[0], yc, yd_ref[0])):
        parts.append((_rms_rows(y) * gg_ref[i:i + 1, :]).astype(BF16))
    res = _dot(jnp.concatenate(parts, axis=1), w_ref[...])
    o_ref[0] = x_ref[0] + mod_ref[0][:, 2 * d:3 * d] * res


def _out_proj(x, ya, yb, hf, hb, o_gate, yd, mod, mod_row, gg, w, *, tm):
    b, t, d = x.shape
    tokd = pl.BlockSpec((1, tm, d), lambda bi, j: (bi, j, 0))
    tokg = pl.BlockSpec((1, tm, GROUP_W), lambda bi, j: (bi, j, 0))
    return pl.pallas_call(
        functools.partial(_outproj_kernel, d_model=d),
        grid=(b, t // tm),
        in_specs=[tokd, tokg, tokg, tokg, tokg, tokg, tokg,
                  pl.BlockSpec((1, 1, 6 * d), lambda bi, j: (mod_row(bi), 0, 0)),
                  _const_spec(gg.shape), _const_spec(w.shape)],
        out_specs=tokd,
        out_shape=jax.ShapeDtypeStruct((b, t, d), F32),
        compiler_params=_params(("parallel", "parallel"), 48),
        name="out_proj",
    )(x, ya, yb, hf, hb, o_gate, yd, mod, gg, w)


def _ffn_kernel(*refs, d_model, d_ff, tm, final_norm):
    if final_norm:
        x_ref, xp_ref, xn_ref, mod_ref, g2_ref, wu_ref, cw_ref, wd_ref, gf_ref, o_ref = refs
    else:
        x_ref, xp_ref, xn_ref, mod_ref, g2_ref, wu_ref, cw_ref, wd_ref, o_ref = refs
    d = d_model
    j = pl.program_id(1)
    mod = mod_ref[0]
    g2 = g2_ref[...]

    def norm_mod(x):
        return _rms_rows(x) * g2 * (1.0 + mod[:, 4 * d:5 * d]) + mod[:, 3 * d:4 * d]

    keep_prev = (j > 0).astype(F32)
    keep_next = (j < pl.num_programs(1) - 1).astype(F32)
    x = x_ref[0]
    h_ext = jnp.concatenate([norm_mod(xp_ref[0]) * keep_prev, norm_mod(x), norm_mod(xn_ref[0]) * keep_next],
                            axis=0).astype(BF16)
    n_ext = tm + 2 * HALO

    u = _dot(h_ext, wu_ref[...])
    cw = cw_ref[...]
    u = (pltpu.roll(u, 1, 0) * cw[0:1] + u * cw[1:2] + pltpu.roll(u, n_ext - 1, 0) * cw[2:3] + cw[3:4])[HALO:HALO + tm]
    gate, val = u[:, :d_ff], u[:, d_ff:]
    act = (gate * jax.nn.sigmoid(gate) * val).astype(BF16)
    y = x + mod[:, 5 * d:6 * d] * _dot(act, wd_ref[...])
    if final_norm:
        y = _rms_rows(y) * gf_ref[...]
    o_ref[0] = y


def _conv_ffn(x, mod, mod_row, g2, wu, cw, wd, g_final, *, tm):
    b, t, d = x.shape
    d_ff = wd.shape[0]
    final_norm = g_final is not None
    hb = tm // HALO
    last = t // HALO - 1
    row = pl.BlockSpec((1, d), lambda bi, j: (0, 0))
    in_specs = [pl.BlockSpec((1, tm, d), lambda bi, j: (bi, j, 0)),
                pl.BlockSpec((1, HALO, d), lambda bi, j: (bi, jnp.maximum(j * hb - 1, 0), 0)),
                pl.BlockSpec((1, HALO, d), lambda bi, j: (bi, jnp.minimum((j + 1) * hb, last), 0)),
                pl.BlockSpec((1, 1, 6 * d), lambda bi, j: (mod_row(bi), 0, 0)),
                row,
                _const_spec(wu.shape), _const_spec(cw.shape), _const_spec(wd.shape)]
    args = [x, x, x, mod, g2, wu, cw, wd]
    if final_norm:
        in_specs.append(row)
        args.append(g_final)
    return pl.pallas_call(
        functools.partial(_ffn_kernel, d_model=d, d_ff=d_ff, tm=tm, final_norm=final_norm),
        grid=(b, t // tm),
        in_specs=in_specs,
        out_specs=pl.BlockSpec((1, tm, d), lambda bi, j: (bi, j, 0)),
        out_shape=jax.ShapeDtypeStruct((b, t, d), F32),
        compiler_params=_params(("parallel", "parallel"), 56),
        name="conv_ffn",
    )(*args)


_Q_HEAD_ORDER = (0, 2, 1, 3)


def _permute_heads(w, axis):
    parts = [lax.slice_in_dim(w, h * HEAD_DIM, (h + 1) * HEAD_DIM, axis=axis) for h in _Q_HEAD_ORDER]
    return jnp.concatenate(parts, axis=axis)


def _layout_w_in(w):
    g, kv = GROUP_W, KV_W
    o_b = g + 2 * kv
    o_c = 2 * o_b
    o_g = o_c + 4 * g
    o_d = o_g + 4 * N_HEADS
    sl = lambda a, n: w[:, a:a + n]
    pad = jnp.zeros((w.shape[0], LANES - 4 * N_HEADS), w.dtype)
    return jnp.concatenate([_permute_heads(sl(0, g), 1), sl(g, 2 * kv),
                            _permute_heads(sl(o_b, g), 1), sl(o_b + g, 2 * kv),
                            sl(o_c, 4 * g), sl(o_d, 3 * g), sl(o_g, 4 * N_HEADS), pad], axis=1).astype(BF16)


def _rope_tables(t):
    pos = jnp.arange(t)
    row = (pos // GRID_W).astype(F32)
    col = (pos % GRID_W).astype(F32)
    n_freq = HEAD_DIM // 4
    freqs = ROPE_THETA ** (-jnp.arange(n_freq, dtype=F32) / n_freq)
    ang_r, ang_c = row[:, None] * freqs, col[:, None] * freqs
    cos = jnp.concatenate([jnp.cos(ang_r), jnp.cos(ang_r), jnp.cos(ang_c), jnp.cos(ang_c)], axis=1)
    sin = jnp.concatenate([-jnp.sin(ang_r), jnp.sin(ang_r), -jnp.sin(ang_c), jnp.sin(ang_c)], axis=1)
    reps = LANES // HEAD_DIM
    return jnp.tile(cos, (1, reps)), jnp.tile(sin, (1, reps))


def _pick_tile(t, pref):
    while t % pref:
        pref //= 2
    return pref


def kernel(x, c, ctx, c_ctx, w_mod, b_mod, g_norm1, g_norm2, w_in, a_q_gain, a_k_gain, b_sink, c_gate_bias,
           d_rel_bias, g_group, w_out, w_up, conv_w, conv_b, w_down, g_final):
    batch, t, d = x.shape
    n_ctx = ctx.shape[1]
    depth = w_in.shape[0]
    d_ff = w_down.shape[1]
    assert batch < 8 and d_ff % LANES == 0 and t % MLSTM_L == 0 and n_ctx % MLSTM_L == 0

    c_rows = jnp.zeros((8, d), F32).at[:batch].set(c).at[batch].set(c_ctx)
    mod_all = _modulation(c_rows, w_mod, b_mod).reshape(depth, 8, 1, 6 * d)
    lat_row = lambda bi: bi
    ctx_row = lambda bi: batch

    cos, sin = _rope_tables(t)
    pool = jnp.where(jnp.arange(GROUP_W)[:, None] // HEAD_DIM == jnp.arange(GROUP_W)[None, :] // HEAD_DIM,
                     1.0 / HEAD_DIM, 0.0).astype(BF16)
    tm_lat = _pick_tile(t, 512)
    tm_ctx = _pick_tile(n_ctx, 256)
    zero_state = (jnp.zeros((batch, 2, GROUP_W, GROUP_W), F32), jnp.zeros((batch, 2, GROUP_W, GROUP_W), F32),
                  jnp.zeros((batch, 2, 8, GROUP_W), F32))
    mlstm_consts = _mlstm_consts(MLSTM_L)

    for layer in range(depth):
        need_ctx = layer < depth - 1
        mod = mod_all[layer]
        g1 = g_norm1[layer][None]
        g2 = g_norm2[layer][None]
        w_in_l = _layout_w_in(w_in[layer])
        gq = jnp.tile(a_q_gain[layer], N_HEADS)[None]
        gk = jnp.tile(a_k_gain[layer], 2)[None]
        gbias = jnp.pad(c_gate_bias[layer], (0, LANES - 4 * N_HEADS))[None]
        gg = g_group[layer].reshape(N_HEADS, GROUP_W)
        gg = jnp.concatenate([_permute_heads(gg[0:2], 1), gg[2:4]], axis=0)
        w_out_l = w_out[layer]
        w_out_l = jnp.concatenate([_permute_heads(w_out_l[0:GROUP_W], 0),
                                   _permute_heads(w_out_l[GROUP_W:2 * GROUP_W], 0),
                                   w_out_l[2 * GROUP_W:]], axis=0).astype(BF16)
        conv_l = jnp.concatenate([conv_w[layer], conv_b[layer][None],
                                  jnp.zeros((8 - 1 - conv_w.shape[1], 2 * d_ff), F32)], axis=0)
        sink = b_sink[layer]
        bias_tab = _neighbour_bias(d_rel_bias[layer])

        proj = functools.partial(_in_proj, g1=g1, w=w_in_l, pool=pool, gq=gq, gk=gk, gbias=gbias)
        (aqc, akc, avc, bqc, bkc, bvc, cqc, ckc, cvc, coc, cgc, dqc, dkc, dvc) = proj(
            ctx, mod, ctx_row, cos=cos, sin=sin, rope=False, tm=tm_ctx)
        (aq, ak, av, bq, bk, bv, cq, ck, cv, co, cg, dq, dk, dv) = proj(
            x, mod, lat_row, cos=cos, sin=sin, rope=True, tm=tm_lat)

        score_bound = 1.02 * LOG2_E * HEAD_DIM ** 0.5 * jnp.max(jnp.abs(a_q_gain[layer])) * jnp.max(jnp.abs(a_k_gain[layer]))
        safe = (score_bound <= MAX_UNSHIFTED_LOG2_SCORE).astype(jnp.int32).reshape(1)
        ya = _attn_global(safe, aq, ak, av, akc, avc, tq=_pick_tile(t, 1024), tk=_pick_tile(t, 1024))
        yb = _attn_window(sink, bq, bk, bv, bkc, bvc, tq=_pick_tile(t, 256))
        yd = _attn_neighbour(dq, dk, dv, dkc, dvc, bias_tab, rows_per_step=8)

        hcf, hcb, ctx_state = _mlstm_scan(cqc, ckc, cvc, cgc, zero_state, mlstm_consts)
        hf, hb, _ = _mlstm_scan(cq, ck, cv, cg, ctx_state, mlstm_consts)

        x = _out_proj(x, ya, yb, hf, hb, co, yd, mod, lat_row, gg, w_out_l, tm=tm_lat)
        ffn = functools.partial(_conv_ffn, g2=g2, wu=w_up[layer].astype(BF16), cw=conv_l,
                                wd=w_down[layer].astype(BF16))
        x = ffn(x, mod, lat_row, g_final=None if need_ctx else g_final[None], tm=_pick_tile(t, 512))
        if need_ctx:
            yac, ybc, ydc = _ctx_attention(sink, aqc, akc, avc, bqc, bkc, bvc, dqc, dkc, dvc)
            ctx = _out_proj(ctx, yac, ybc, hcf, hcb, coc, ydc, mod, ctx_row, gg, w_out_l, tm=tm_ctx)
            ctx = ffn(ctx, mod, ctx_row, g_final=None, tm=tm_ctx)
    return x
```

```python
import functools

import jax
import jax.numpy as jnp
from jax import lax
from jax.experimental import pallas as pl
from jax.experimental.pallas import tpu as pltpu

F32 = jnp.float32
BF16 = jnp.bfloat16

N_HEADS = 4
HEAD_DIM = 64
GROUP_W = N_HEADS * HEAD_DIM
KV_W = 2 * HEAD_DIM
GRID_W = 64
WINDOW = 128
NA_KH = 8
NA_KW = 16
ROPE_THETA = 10000.0
EPS = 1e-6
NEG_INF = -1e30
LANES = 128
MLSTM_L = 256
HALO = 8
MIB = 1024 * 1024
LOG2_E = 1.4426950408889634
MAX_UNSHIFTED_LOG2_SCORE = 60.0


def _params(sem, vmem_mib):
    return pltpu.CompilerParams(dimension_semantics=sem, vmem_limit_bytes=vmem_mib * MIB)


def _dot(a, b):
    return jnp.dot(a, b, preferred_element_type=F32)


def _dot_nt(a, b):
    return lax.dot_general(a, b, (((1,), (1,)), ((), ())), preferred_element_type=F32)


def _const_spec(shape):
    return pl.BlockSpec(shape, lambda *_: (0,) * len(shape), pipeline_mode=pl.Buffered(1))


def _layer_spec(stacked_shape, layer):
    rest = tuple(stacked_shape[1:])
    return pl.BlockSpec((None,) + rest, lambda *_: (layer,) + (0,) * len(rest), pipeline_mode=pl.Buffered(1))


def _mod_kernel(c_ref, w_ref, b_ref, o_ref):
    c = c_ref[...]
    a = (c * jax.nn.sigmoid(c)).astype(BF16)
    o_ref[0] = _dot(a, w_ref[0].astype(BF16)) + b_ref[0]


def _modulation(c_rows, w_mod, b_mod):
    depth, d, n = w_mod.shape
    tn = 1536
    return pl.pallas_call(
        _mod_kernel,
        grid=(depth, n // tn),
        in_specs=[pl.BlockSpec((8, d), lambda l, j: (0, 0)),
                  pl.BlockSpec((1, d, tn), lambda l, j: (l, 0, j)),
                  pl.BlockSpec((1, 1, tn), lambda l, j: (l, 0, j))],
        out_specs=pl.BlockSpec((1, 8, tn), lambda l, j: (l, 0, j)),
        out_shape=jax.ShapeDtypeStruct((depth, 8, n), F32),
        compiler_params=_params(("parallel", "parallel"), 40),
        name="modulation",
    )(c_rows, w_mod, b_mod.reshape(depth, 1, n))


def _rms_rows(x):
    return x * lax.rsqrt(jnp.mean(x * x, axis=-1, keepdims=True) + EPS)


def _head_rms(z, pool, gain):
    z2 = z * z
    hi = z2.astype(BF16)
    lo = (z2 - hi.astype(F32)).astype(BF16)
    ms = _dot(hi, pool) + _dot(lo, pool)
    return z * lax.rsqrt(ms + EPS) * gain


def _rope(z, cos, sin):
    lane = lax.broadcasted_iota(jnp.int32, z.shape, 1)
    first = (lane % 32) < 16
    partner = jnp.where(first, pltpu.roll(z, LANES - 16, 1), pltpu.roll(z, 16, 1))
    return z * cos + partner * sin


def _inproj_kernel(x_ref, mod_ref, g1_ref, w_ref, pool_ref, gq_ref, gk_ref, cos_ref, sin_ref, gb_ref,
                   aq_ref, ak_ref, av_ref, bq_ref, bk_ref, bv_ref, cq_ref, ck_ref, cv_ref, co_ref, cg_ref,
                   dq_ref, dk_ref, dv_ref, *, d_model, rope):
    d = d_model
    mod = mod_ref[0]
    xn = _rms_rows(x_ref[0]) * g1_ref[...]
    hb = (xn * (1.0 + mod[:, d:2 * d]) + mod[:, 0:d]).astype(BF16)
    projected = _dot(hb, w_ref[...])

    def proj(seg, width=GROUP_W):
        return projected[:, seg * GROUP_W:seg * GROUP_W + width]

    def rot(z):
        if not rope:
            return z
        cos, sin = cos_ref[...], sin_ref[...]
        return jnp.concatenate([_rope(z[:, i:i + LANES], cos, sin) for i in range(0, z.shape[1], LANES)], axis=1)

    scale = HEAD_DIM ** -0.5
    pool = pool_ref[...]
    aq_ref[0] = (rot(_head_rms(proj(0), pool, gq_ref[...])) * (scale * LOG2_E)).astype(BF16)
    akv = proj(1)
    ak_ref[0] = rot(_head_rms(akv[:, :KV_W], pool[:KV_W, :KV_W], gk_ref[...])).astype(BF16)
    ones = jnp.ones((akv.shape[0], LANES), F32)
    av_ref[0] = jnp.concatenate([akv[:, KV_W:], ones], axis=1).astype(BF16)
    bq_ref[0] = (rot(proj(2)) * (scale * LOG2_E)).astype(BF16)
    bkv = proj(3)
    bk_ref[0] = rot(bkv[:, :KV_W]).astype(BF16)
    bv_ref[0] = jnp.concatenate([bkv[:, KV_W:], ones], axis=1).astype(BF16)
    cq_ref[0] = (proj(4) * scale).astype(BF16)
    ck_ref[0] = proj(5).astype(BF16)
    cv_ref[0] = proj(6).astype(BF16)
    co_ref[0] = proj(7)
    dq_ref[0] = (proj(8) * scale).astype(BF16)
    dk_ref[0] = proj(9).astype(BF16)
    dv_ref[0] = proj(10).astype(BF16)
    gates = proj(11, LANES) + gb_ref[...]
    forget_lane = (lax.broadcasted_iota(jnp.int32, gates.shape, 1) % 8) >= 4
    cg_ref[0] = jnp.where(forget_lane, _log_sigmoid(gates), gates) * LOG2_E


def _in_proj(x, mod, mod_row, g1, w, pool, gq, gk, cos, sin, gbias, *, layer, rope, tm):
    b, t, d = x.shape
    widths = [(GROUP_W, BF16), (KV_W, BF16), (2 * LANES, BF16),
              (GROUP_W, BF16), (KV_W, BF16), (2 * LANES, BF16),
              (GROUP_W, BF16), (GROUP_W, BF16), (GROUP_W, BF16), (GROUP_W, F32), (LANES, F32),
              (GROUP_W, BF16), (GROUP_W, BF16), (GROUP_W, BF16)]
    row = lambda wd: pl.BlockSpec((1, wd), lambda bi, j: (0, 0))
    return pl.pallas_call(
        functools.partial(_inproj_kernel, d_model=d, rope=rope),
        grid=(b, t // tm),
        in_specs=[pl.BlockSpec((1, tm, d), lambda bi, j: (bi, j, 0)),
                  pl.BlockSpec((1, 1, 6 * d), lambda bi, j: (mod_row(bi), 0, 0)),
                  row(d),
                  _layer_spec(w.shape, layer),
                  _const_spec((GROUP_W, GROUP_W)),
                  row(GROUP_W), row(KV_W),
                  pl.BlockSpec((tm, LANES), lambda bi, j: (j, 0)),
                  pl.BlockSpec((tm, LANES), lambda bi, j: (j, 0)),
                  row(LANES)],
        out_specs=[pl.BlockSpec((1, tm, wd), lambda bi, j: (bi, j, 0)) for wd, _ in widths],
        out_shape=[jax.ShapeDtypeStruct((b, t, wd), dt) for wd, dt in widths],
        compiler_params=_params(("parallel", "parallel"), 48),
        name="in_proj",
    )(x, mod, g1, w, pool, gq, gk, cos, sin, gbias)


def _stack_gqa(q):
    qa, qb = q[:, :LANES], q[:, LANES:]
    left = lax.broadcasted_iota(jnp.int32, qa.shape, 1) < HEAD_DIM
    zero = jnp.zeros_like(qa)
    return jnp.concatenate([jnp.where(left, qa, zero), jnp.where(left, qb, zero),
                            jnp.where(left, zero, qa), jnp.where(left, zero, qb)], axis=0)


def _unstack_gqa(o, t):
    left = lax.broadcasted_iota(jnp.int32, (t, LANES), 1) < HEAD_DIM
    return jnp.concatenate([jnp.where(left, o[0:t], o[2 * t:3 * t]),
                            jnp.where(left, o[t:2 * t], o[3 * t:4 * t])], axis=1)


def _attn_a_kernel(safe_ref, q_ref, k_ref, v_ref, kc_ref, vc_ref, o_ref, qs_ref, m_ref, acc_ref, *, tq, tk, n_kb):
    qs_ref[...] = _stack_gqa(q_ref[0])
    heads = [slice(h * tq, (h + 1) * tq) for h in range(N_HEADS)]

    @pl.when(safe_ref[0] != 0)
    def _():
        for rows in heads:
            p = jnp.exp2(_dot_nt(qs_ref[rows], kc_ref[0]))
            acc_ref[rows] = _dot(p.astype(BF16), vc_ref[0])

        def body(kb, carry):
            start = pl.multiple_of(kb * tk, tk)
            for rows in heads:
                p = jnp.exp2(_dot_nt(qs_ref[rows], k_ref[0, pl.ds(start, tk), :]))
                acc_ref[rows] += _dot(p.astype(BF16), v_ref[0, pl.ds(start, tk), :])
            return carry

        lax.fori_loop(0, n_kb, body, 0)

    @pl.when(safe_ref[0] == 0)
    def _():
        for rows in heads:
            s = _dot_nt(qs_ref[rows], kc_ref[0])
            m0 = jnp.max(s, axis=-1, keepdims=True)
            m_ref[rows] = m0
            acc_ref[rows] = _dot(jnp.exp2(s - m0).astype(BF16), vc_ref[0])

        def body(kb, carry):
            start = pl.multiple_of(kb * tk, tk)
            for rows in heads:
                s = _dot_nt(qs_ref[rows], k_ref[0, pl.ds(start, tk), :])
                m_prev = m_ref[rows]
                m_new = jnp.maximum(m_prev, jnp.max(s, axis=-1, keepdims=True))
                p = jnp.exp2(s - m_new)
                acc_ref[rows] = jnp.exp2(m_prev - m_new) * acc_ref[rows] + _dot(
                    p.astype(BF16), v_ref[0, pl.ds(start, tk), :])
                m_ref[rows] = m_new
            return carry

        lax.fori_loop(0, n_kb, body, 0)

    acc = acc_ref[...]
    o_ref[0] = _unstack_gqa(acc[:, :LANES] / acc[:, LANES:], tq).astype(o_ref.dtype)


def _attn_global(safe, q, k, v, kc, vc, *, tq, tk):
    b, t, _ = q.shape
    n_ctx = kc.shape[1]
    per_b = lambda n, wd: pl.BlockSpec((1, n, wd), lambda bi, i: (bi, 0, 0))
    return pl.pallas_call(
        functools.partial(_attn_a_kernel, tq=tq, tk=tk, n_kb=t // tk),
        grid=(b, t // tq),
        in_specs=[pl.BlockSpec(memory_space=pltpu.SMEM),
                  pl.BlockSpec((1, tq, GROUP_W), lambda bi, i: (bi, i, 0)),
                  per_b(t, KV_W), per_b(t, 2 * LANES), per_b(n_ctx, KV_W), per_b(n_ctx, 2 * LANES)],
        out_specs=pl.BlockSpec((1, tq, GROUP_W), lambda bi, i: (bi, i, 0)),
        out_shape=jax.ShapeDtypeStruct((b, t, GROUP_W), BF16),
        scratch_shapes=[pltpu.VMEM((4 * tq, LANES), BF16), pltpu.VMEM((4 * tq, 1), F32),
                        pltpu.VMEM((4 * tq, 2 * LANES), F32)],
        compiler_params=_params(("parallel", "parallel"), 48),
        name="attn_global",
    )(safe, q, k, v, kc, vc)


def _sink_column(sink_ref, t):
    row = lax.broadcasted_iota(jnp.int32, (4 * t, 1), 0)
    col = jnp.full((4 * t, 1), sink_ref[3], F32)
    for h in (2, 1, 0):
        col = jnp.where(row < (h + 1) * t, sink_ref[h], col)
    return col


def _attn_b_kernel(sink_ref, q_ref, k_ref, v_ref, kc_ref, vc_ref, o_ref, *, tq, t_total):
    span = tq + 2 * WINDOW
    q0 = pl.program_id(1) * tq
    ks = pl.multiple_of(jnp.clip(q0 - WINDOW, 0, t_total - span), WINDOW)
    qs = _stack_gqa(q_ref[0])
    kw = k_ref[0, pl.ds(ks, span), :]
    vw = v_ref[0, pl.ds(ks, span), :]
    q_pos = q0 + lax.broadcasted_iota(jnp.int32, (tq, span), 0)
    k_pos = ks + lax.broadcasted_iota(jnp.int32, (tq, span), 1)
    in_window = jnp.abs(k_pos - q_pos) <= WINDOW
    outs = []
    for h in range(N_HEADS):
        qh = qs[h * tq:(h + 1) * tq]
        s = jnp.where(in_window, _dot_nt(qh, kw), NEG_INF)
        sc = _dot_nt(qh, kc_ref[0])
        sink = sink_ref[h] * LOG2_E
        m = jnp.maximum(jnp.maximum(jnp.max(s, axis=-1, keepdims=True), jnp.max(sc, axis=-1, keepdims=True)), sink)
        acc = _dot(jnp.exp2(s - m).astype(BF16), vw) + _dot(jnp.exp2(sc - m).astype(BF16), vc_ref[0])
        outs.append(acc[:, :LANES] / (acc[:, LANES:] + jnp.exp2(sink - m)))
    o_ref[0] = _unstack_gqa(jnp.concatenate(outs, axis=0), tq).astype(o_ref.dtype)


def _attn_window(sink, q, k, v, kc, vc, *, tq):
    b, t, _ = q.shape
    n_ctx = kc.shape[1]
    assert t >= tq + 2 * WINDOW
    per_b = lambda n, wd: pl.BlockSpec((1, n, wd), lambda bi, i: (bi, 0, 0))
    return pl.pallas_call(
        functools.partial(_attn_b_kernel, tq=tq, t_total=t),
        grid=(b, t // tq),
        in_specs=[pl.BlockSpec(memory_space=pltpu.SMEM),
                  pl.BlockSpec((1, tq, GROUP_W), lambda bi, i: (bi, i, 0)),
                  per_b(t, KV_W), per_b(t, 2 * LANES), per_b(n_ctx, KV_W), per_b(n_ctx, 2 * LANES)],
        out_specs=pl.BlockSpec((1, tq, GROUP_W), lambda bi, i: (bi, i, 0)),
        out_shape=jax.ShapeDtypeStruct((b, t, GROUP_W), BF16),
        compiler_params=_params(("parallel", "parallel"), 48),
        name="attn_window",
    )(sink, q, k, v, kc, vc)


def _head_masks(shape):
    lane = lax.broadcasted_iota(jnp.int32, shape, 1)
    return [(lane >= h * HEAD_DIM) & (lane < (h + 1) * HEAD_DIM) for h in range(N_HEADS)]


def _stack_mha(q):
    zero = jnp.zeros_like(q)
    return jnp.concatenate([jnp.where(mk, q, zero) for mk in _head_masks(q.shape)], axis=0)


def _unstack_mha(o, t):
    masks = _head_masks((t, GROUP_W))
    out = jnp.where(masks[0], o[0:t], 0.0)
    for h in range(1, N_HEADS):
        out = jnp.where(masks[h], o[h * t:(h + 1) * t], out)
    return out


def _attn_d_kernel(q_ref, k_ref, v_ref, kc_ref, vc_ref, bias_ref, o_ref, *, rows_per_step, rows):
    kc = kc_ref[0]
    vc = vc_ref[0]
    span = NA_KH * GRID_W

    def body(rr, carry):
        r = pl.program_id(1) * rows_per_step + rr
        rs = jnp.clip(r - NA_KH // 2, 0, rows - NA_KH)
        k0 = pl.multiple_of(rs * GRID_W, GRID_W)
        q0 = pl.multiple_of(rr * GRID_W, GRID_W)
        qs = _stack_mha(q_ref[0, pl.ds(q0, GRID_W), :])
        s = _dot_nt(qs, k_ref[0, pl.ds(k0, span), :]) + bias_ref[r - rs]
        sc = _dot_nt(qs, kc)
        m = jnp.maximum(jnp.max(s, axis=-1, keepdims=True), jnp.max(sc, axis=-1, keepdims=True))
        p = jnp.exp(s - m)
        pc = jnp.exp(sc - m)
        l = jnp.sum(p, axis=-1, keepdims=True) + jnp.sum(pc, axis=-1, keepdims=True)
        o = _dot(p.astype(BF16), v_ref[0, pl.ds(k0, span), :]) + _dot(pc.astype(BF16), vc)
        o_ref[0, pl.ds(q0, GRID_W), :] = _unstack_mha(o / l, GRID_W).astype(o_ref.dtype)
        return carry

    lax.fori_loop(0, rows_per_step, body, 0, unroll=True)


def _attn_neighbour(q, k, v, kc, vc, bias, *, rows_per_step):
    b, t, _ = q.shape
    n_ctx = kc.shape[1]
    rows = t // GRID_W
    assert rows >= NA_KH and rows % rows_per_step == 0
    tq = rows_per_step * GRID_W
    per_b = lambda n: pl.BlockSpec((1, n, GROUP_W), lambda bi, i: (bi, 0, 0))
    return pl.pallas_call(
        functools.partial(_attn_d_kernel, rows_per_step=rows_per_step, rows=rows),
        grid=(b, t // tq),
        in_specs=[pl.BlockSpec((1, tq, GROUP_W), lambda bi, i: (bi, i, 0)),
                  per_b(t), per_b(t), per_b(n_ctx), per_b(n_ctx),
                  _const_spec(bias.shape)],
        out_specs=pl.BlockSpec((1, tq, GROUP_W), lambda bi, i: (bi, i, 0)),
        out_shape=jax.ShapeDtypeStruct((b, t, GROUP_W), BF16),
        compiler_params=_params(("parallel", "parallel"), 48),
        name="attn_neighbour",
    )(q, k, v, kc, vc, bias)


def _neighbour_bias(rpb):
    w = jnp.arange(GRID_W)
    cs = jnp.clip(w - NA_KW // 2, 0, GRID_W - NA_KW)
    col = jnp.arange(GRID_W)
    valid = (col[None, :] >= cs[:, None]) & (col[None, :] < cs[:, None] + NA_KW)
    dc = jnp.clip(col[None, :] - w[:, None] + (NA_KW - 1), 0, 2 * NA_KW - 2)
    dr = jnp.arange(NA_KH)[None, :] - jnp.arange(NA_KH)[:, None] + (NA_KH - 1)
    pick_r = (dr[:, :, None] == jnp.arange(2 * NA_KH - 1)).astype(F32)
    pick_c = (jnp.arange(2 * NA_KW - 1)[:, None, None] == dc[None]).astype(F32)
    tab = jnp.einsum('hrc,oir,cwk->ohwik', rpb.astype(F32), pick_r, pick_c, precision=lax.Precision.HIGHEST)
    tab = jnp.where(valid[None, None, :, None, :], tab, NEG_INF)
    return tab.reshape(NA_KH, N_HEADS * GRID_W, NA_KH * GRID_W)


def _ctx_attn_kernel(sink_ref, aq_ref, ak_ref, av_ref, bq_ref, bk_ref, bv_ref, dq_ref, dk_ref, dv_ref,
                     ya_ref, yb_ref, yd_ref, *, n):
    s = _dot_nt(_stack_gqa(aq_ref[0]), ak_ref[0])
    acc = _dot(jnp.exp2(s - jnp.max(s, axis=-1, keepdims=True)).astype(BF16), av_ref[0])
    ya_ref[0] = _unstack_gqa(acc[:, :LANES] / acc[:, LANES:], n).astype(ya_ref.dtype)
    s = _dot_nt(_stack_gqa(bq_ref[0]), bk_ref[0])
    sink = _sink_column(sink_ref, n) * LOG2_E
    m = jnp.maximum(jnp.max(s, axis=-1, keepdims=True), sink)
    acc = _dot(jnp.exp2(s - m).astype(BF16), bv_ref[0])
    yb_ref[0] = _unstack_gqa(acc[:, :LANES] / (acc[:, LANES:] + jnp.exp2(sink - m)), n).astype(yb_ref.dtype)
    s = _dot_nt(_stack_mha(dq_ref[0]), dk_ref[0])
    p = jnp.exp(s - jnp.max(s, axis=-1, keepdims=True))
    o = _dot(p.astype(BF16), dv_ref[0]) / jnp.sum(p, axis=-1, keepdims=True)
    yd_ref[0] = _unstack_mha(o, n).astype(yd_ref.dtype)


def _ctx_attention(sink, aq, ak, av, bq, bk, bv, dq, dk, dv):
    b, n, _ = aq.shape
    spec = lambda wd: pl.BlockSpec((1, n, wd), lambda bi: (bi, 0, 0))
    return pl.pallas_call(
        functools.partial(_ctx_attn_kernel, n=n),
        grid=(b,),
        in_specs=[pl.BlockSpec(memory_space=pltpu.SMEM),
                  spec(GROUP_W), spec(KV_W), spec(2 * LANES), spec(GROUP_W), spec(KV_W), spec(2 * LANES),
                  spec(GROUP_W), spec(GROUP_W), spec(GROUP_W)],
        out_specs=[spec(GROUP_W)] * 3,
        out_shape=[jax.ShapeDtypeStruct((b, n, GROUP_W), BF16)] * 3,
        compiler_params=_params(("parallel",), 32),
        name="ctx_attention",
    )(sink, aq, ak, av, bq, bk, bv, dq, dk, dv)


def _log_sigmoid(x):
    return jnp.minimum(x, 0.0) - jnp.log1p(jnp.exp(-jnp.abs(x)))


def _split3(x):
    x1 = x.astype(BF16)
    r1 = x - x1.astype(F32)
    x2 = r1.astype(BF16)
    return x1, x2, (r1 - x2.astype(F32)).astype(BF16)


def _exact_dot_01(sel, x):
    return sum(_dot(sel, piece) for piece in _split3(x))


def _exact_dot_01_rhs(x, sel):
    return sum(_dot(piece, sel) for piece in _split3(x))


def _mlstm_chunk(q, k, v, gates, expand, tri, neg_mask, ones_cat, block, head_rows, s_ref, n_ref, m_ref, *,
                 reverse, L):
    ge = _exact_dot_01_rhs(gates, expand)
    li = ge[:, :GROUP_W]
    cum = _exact_dot_01(tri, ge[:, GROUP_W:])
    a_t = (li - cum).T
    m_prev = m_ref[0:1, :]
    head_b = [head_rows[h].astype(BF16) for h in range(N_HEADS)]
    mu = jnp.zeros((L, GROUP_W), F32)
    scores = []
    for h in range(N_HEADS):
        lane0 = h * HEAD_DIM
        a_m = a_t[lane0:lane0 + 1, :] + neg_mask
        mu_h = jnp.maximum(jnp.max(a_m, axis=-1, keepdims=True), m_prev[:, lane0:lane0 + 1])
        qk = _dot_nt(q * head_b[h], k)
        scores.append((qk * jnp.exp2(a_m - mu_h)).astype(BF16))
        mu = mu + mu_h * head_rows[h]
    s_cat = jnp.concatenate(scores, axis=1)
    v_cat = jnp.concatenate([v * hb for hb in head_b], axis=0)
    w_inter = jnp.exp2(m_prev - mu)
    num = _dot(s_cat, v_cat) + w_inter * _dot(q, s_ref[...].astype(BF16))
    den = _dot(s_cat, ones_cat) + w_inter * _dot(q, n_ref[...].astype(BF16))
    h_out = num / jnp.maximum(jnp.abs(den), jnp.exp2(-(cum + mu)))

    end_row = 0 if reverse else L - 1
    cum_end = cum[end_row:end_row + 1, :]
    log_end = cum_end - cum + li
    m_new = jnp.maximum(cum_end + m_prev, jnp.max(log_end, axis=0, keepdims=True))
    decay = jnp.exp2(cum_end + m_prev - m_new)
    w_end = jnp.exp2(log_end - m_new)
    k_t = k.astype(F32).T.astype(BF16)
    upd = _dot(k_t, jnp.concatenate([(v.astype(F32) * w_end).astype(BF16), w_end.astype(BF16)], axis=1))
    s_ref[...] = (s_ref[...] * decay + upd[:, :GROUP_W]) * block
    n_ref[...] = (n_ref[...] * decay + upd[:, GROUP_W:]) * block
    m_ref[...] = jnp.broadcast_to(m_new, m_ref.shape)
    return h_out


def _mlstm_kernel(qf_ref, kf_ref, vf_ref, gf_ref, qb_ref, kb_ref, vb_ref, gb_ref,
                  expand_ref, tri_ref, neg_ref, ones_ref, block_ref, heads_ref, s0_ref, n0_ref, m0_ref,
                  hf_ref, hb_ref, s_out, n_out, m_out, s_ref, n_ref, m_ref, *, chunk):
    @pl.when(pl.program_id(1) == 0)
    def _():
        s_ref[...] = s0_ref[0]
        n_ref[...] = n0_ref[0]
        m_ref[...] = m0_ref[0]

    sides = ((qf_ref, kf_ref, vf_ref, gf_ref, hf_ref), (qb_ref, kb_ref, vb_ref, gb_ref, hb_ref))
    for d, (q_ref, k_ref, v_ref, g_ref, h_ref) in enumerate(sides):
        h = _mlstm_chunk(q_ref[0], k_ref[0], v_ref[0], g_ref[0], expand_ref[d], tri_ref[d], neg_ref[d],
                         ones_ref[...], block_ref[...], heads_ref, s_ref.at[d], n_ref.at[d], m_ref.at[d],
                         reverse=bool(d), L=chunk)
        h_ref[0] = h.astype(h_ref.dtype)
    s_out[0] = s_ref[...]
    n_out[0] = n_ref[...]
    m_out[0] = m_ref[...]


def _mlstm_consts(L):
    lane = jnp.arange(LANES)[:, None]
    col = jnp.arange(2 * GROUP_W)[None, :]
    col_head = (col % GROUP_W) // HEAD_DIM
    expand = jnp.stack([(lane == 8 * d + 4 * (col // GROUP_W) + col_head) for d in range(2)]).astype(BF16)
    t_idx = jnp.arange(L)[:, None]
    s_idx = jnp.arange(L)[None, :]
    seen = jnp.stack([s_idx <= t_idx, s_idx >= t_idx])
    head = jnp.arange(GROUP_W) // HEAD_DIM
    ones_cat = (jnp.repeat(jnp.arange(N_HEADS), L)[:, None] == head[None, :]).astype(BF16)
    block = (head[:, None] == head[None, :]).astype(F32)
    head_rows = (jnp.arange(N_HEADS)[:, None, None] == head[None, None, :]).astype(F32)
    return expand, seen.astype(BF16), jnp.where(seen, 0.0, NEG_INF).astype(F32), ones_cat, block, head_rows


def _mlstm_scan(q, k, v, gates, state, consts):
    b, t, _ = q.shape
    L = MLSTM_L
    nc = t // L
    fwd = lambda wd: pl.BlockSpec((1, L, wd), lambda bi, j: (bi, j, 0))
    bwd = lambda wd: pl.BlockSpec((1, L, wd), lambda bi, j: (bi, nc - 1 - j, 0))
    st = lambda r: pl.BlockSpec((1, 2, r, GROUP_W), lambda bi, j: (bi, 0, 0, 0))
    tok_specs = [fwd(GROUP_W), fwd(GROUP_W), fwd(GROUP_W), fwd(LANES),
                 bwd(GROUP_W), bwd(GROUP_W), bwd(GROUP_W), bwd(LANES)]
    hf, hb, s_fin, n_fin, m_fin = pl.pallas_call(
        functools.partial(_mlstm_kernel, chunk=L),
        grid=(b, nc),
        in_specs=tok_specs + [_const_spec(c.shape) for c in consts] + [st(GROUP_W), st(GROUP_W), st(8)],
        out_specs=[fwd(GROUP_W), bwd(GROUP_W), st(GROUP_W), st(GROUP_W), st(8)],
        out_shape=[jax.ShapeDtypeStruct((b, t, GROUP_W), BF16), jax.ShapeDtypeStruct((b, t, GROUP_W), BF16),
                   jax.ShapeDtypeStruct((b, 2, GROUP_W, GROUP_W), F32),
                   jax.ShapeDtypeStruct((b, 2, GROUP_W, GROUP_W), F32),
                   jax.ShapeDtypeStruct((b, 2, 8, GROUP_W), F32)],
        scratch_shapes=[pltpu.VMEM((2, GROUP_W, GROUP_W), F32), pltpu.VMEM((2, GROUP_W, GROUP_W), F32),
                        pltpu.VMEM((2, 8, GROUP_W), F32)],
        compiler_params=_params(("parallel", "arbitrary"), 48),
        name="mlstm",
    )(q, k, v, gates, q, k, v, gates, *consts, *state)
    return hf, hb, (s_fin, n_fin, m_fin)


def _outproj_kernel(x_ref, ya_ref, yb_ref, hf_ref, hb_ref, og_ref, yd_ref, mod_ref, gg_ref, w_ref, o_ref, *, d_model):
    d = d_model
    yc = jax.nn.sigmoid(og_ref[0]) * (hf_ref[0].astype(F32) + hb_ref[0].astype(F32))
    parts = []
    for i, y in enumerate((ya_ref[0], yb_ref[0], yc, yd_ref[0])):
        parts.append((_rms_rows(y.astype(F32)) * gg_ref[i:i + 1, :]).astype(BF16))
    res = _dot(jnp.concatenate(parts, axis=1), w_ref[...])
    o_ref[0] = x_ref[0] + mod_ref[0][:, 2 * d:3 * d] * res


def _out_proj(x, ya, yb, hf, hb, o_gate, yd, mod, mod_row, gg, w, *, layer, tm):
    b, t, d = x.shape
    tokd = pl.BlockSpec((1, tm, d), lambda bi, j: (bi, j, 0))
    tokg = pl.BlockSpec((1, tm, GROUP_W), lambda bi, j: (bi, j, 0))
    return pl.pallas_call(
        functools.partial(_outproj_kernel, d_model=d),
        grid=(b, t // tm),
        in_specs=[tokd, tokg, tokg, tokg, tokg, tokg, tokg,
                  pl.BlockSpec((1, 1, 6 * d), lambda bi, j: (mod_row(bi), 0, 0)),
                  _const_spec(gg.shape), _layer_spec(w.shape, layer)],
        out_specs=tokd,
        out_shape=jax.ShapeDtypeStruct((b, t, d), F32),
        compiler_params=_params(("parallel", "parallel"), 48),
        name="out_proj",
    )(x, ya, yb, hf, hb, o_gate, yd, mod, gg, w)


def _ffn_kernel(*refs, d_model, d_ff, tm, final_norm):
    if final_norm:
        x_ref, xp_ref, xn_ref, mod_ref, g2_ref, wu_ref, cw_ref, wd_ref, gf_ref, o_ref = refs
    else:
        x_ref, xp_ref, xn_ref, mod_ref, g2_ref, wu_ref, cw_ref, wd_ref, o_ref = refs
    d = d_model
    j = pl.program_id(1)
    mod = mod_ref[0]
    g2 = g2_ref[...]

    def norm_mod(x):
        return _rms_rows(x) * g2 * (1.0 + mod[:, 4 * d:5 * d]) + mod[:, 3 * d:4 * d]

    keep_prev = (j > 0).astype(F32)
    keep_next = (j < pl.num_programs(1) - 1).astype(F32)
    x = x_ref[0]
    h_ext = jnp.concatenate([norm_mod(xp_ref[0]) * keep_prev, norm_mod(x), norm_mod(xn_ref[0]) * keep_next],
                            axis=0).astype(BF16)
    n_ext = tm + 2 * HALO

    u = _dot(h_ext, wu_ref[...])
    cw = cw_ref[...]
    u = (pltpu.roll(u, 1, 0) * cw[0:1] + u * cw[1:2] + pltpu.roll(u, n_ext - 1, 0) * cw[2:3] + cw[3:4])[HALO:HALO + tm]
    gate, val = u[:, :d_ff], u[:, d_ff:]
    act = (gate * jax.nn.sigmoid(gate) * val).astype(BF16)
    y = x + mod[:, 5 * d:6 * d] * _dot(act, wd_ref[...])
    if final_norm:
        y = _rms_rows(y) * gf_ref[...]
    o_ref[0] = y


def _conv_ffn(x, mod, mod_row, g2, wu, cw, wd, g_final, *, layer, tm):
    b, t, d = x.shape
    d_ff = wd.shape[1]
    final_norm = g_final is not None
    hb = tm // HALO
    last = t // HALO - 1
    row = pl.BlockSpec((1, d), lambda bi, j: (0, 0))
    in_specs = [pl.BlockSpec((1, tm, d), lambda bi, j: (bi, j, 0)),
                pl.BlockSpec((1, HALO, d), lambda bi, j: (bi, jnp.maximum(j * hb - 1, 0), 0)),
                pl.BlockSpec((1, HALO, d), lambda bi, j: (bi, jnp.minimum((j + 1) * hb, last), 0)),
                pl.BlockSpec((1, 1, 6 * d), lambda bi, j: (mod_row(bi), 0, 0)),
                row,
                _layer_spec(wu.shape, layer), _const_spec(cw.shape), _layer_spec(wd.shape, layer)]
    args = [x, x, x, mod, g2, wu, cw, wd]
    if final_norm:
        in_specs.append(row)
        args.append(g_final)
    return pl.pallas_call(
        functools.partial(_ffn_kernel, d_model=d, d_ff=d_ff, tm=tm, final_norm=final_norm),
        grid=(b, t // tm),
        in_specs=in_specs,
        out_specs=pl.BlockSpec((1, tm, d), lambda bi, j: (bi, j, 0)),
        out_shape=jax.ShapeDtypeStruct((b, t, d), F32),
        compiler_params=_params(("parallel", "parallel"), 56),
        name="conv_ffn",
    )(*args)


_Q_HEAD_ORDER = (0, 2, 1, 3)


def _permute_heads(w, axis):
    parts = [lax.slice_in_dim(w, h * HEAD_DIM, (h + 1) * HEAD_DIM, axis=axis) for h in _Q_HEAD_ORDER]
    return jnp.concatenate(parts, axis=axis)


def _layout_w_in(w):
    w = w.astype(BF16)
    g, kv = GROUP_W, KV_W
    o_b = g + 2 * kv
    o_c = 2 * o_b
    o_g = o_c + 4 * g
    o_d = o_g + 4 * N_HEADS
    sl = lambda a, n: w[:, :, a:a + n]
    pad = jnp.zeros(w.shape[:2] + (LANES - 4 * N_HEADS,), BF16)
    return jnp.concatenate([_permute_heads(sl(0, g), 2), sl(g, 2 * kv),
                            _permute_heads(sl(o_b, g), 2), sl(o_b + g, 2 * kv),
                            sl(o_c, 4 * g), sl(o_d, 3 * g), sl(o_g, 4 * N_HEADS), pad], axis=2)


def _layout_w_out(w):
    w = w.astype(BF16)
    return jnp.concatenate([_permute_heads(w[:, 0:GROUP_W], 1), _permute_heads(w[:, GROUP_W:2 * GROUP_W], 1),
                            w[:, 2 * GROUP_W:]], axis=1)


def _rope_tables(t):
    pos = jnp.arange(t)
    row = (pos // GRID_W).astype(F32)
    col = (pos % GRID_W).astype(F32)
    n_freq = HEAD_DIM // 4
    freqs = ROPE_THETA ** (-jnp.arange(n_freq, dtype=F32) / n_freq)
    ang_r, ang_c = row[:, None] * freqs, col[:, None] * freqs
    cos = jnp.concatenate([jnp.cos(ang_r), jnp.cos(ang_r), jnp.cos(ang_c), jnp.cos(ang_c)], axis=1)
    sin = jnp.concatenate([-jnp.sin(ang_r), jnp.sin(ang_r), -jnp.sin(ang_c), jnp.sin(ang_c)], axis=1)
    reps = LANES // HEAD_DIM
    return jnp.tile(cos, (1, reps)), jnp.tile(sin, (1, reps))


def _pick_tile(t, pref):
    while t % pref:
        pref //= 2
    return pref


def kernel(x, c, ctx, c_ctx, w_mod, b_mod, g_norm1, g_norm2, w_in, a_q_gain, a_k_gain, b_sink, c_gate_bias,
           d_rel_bias, g_group, w_out, w_up, conv_w, conv_b, w_down, g_final):
    batch, t, d = x.shape
    n_ctx = ctx.shape[1]
    depth = w_in.shape[0]
    d_ff = w_down.shape[1]
    assert batch < 8 and d_ff % LANES == 0 and t % MLSTM_L == 0 and n_ctx % MLSTM_L == 0

    c_rows = jnp.zeros((8, d), F32).at[:batch].set(c).at[batch].set(c_ctx)
    mod_all = _modulation(c_rows, w_mod, b_mod).reshape(depth, 8, 1, 6 * d)
    lat_row = lambda bi: bi
    ctx_row = lambda bi: batch

    cos, sin = _rope_tables(t)
    pool = jnp.where(jnp.arange(GROUP_W)[:, None] // HEAD_DIM == jnp.arange(GROUP_W)[None, :] // HEAD_DIM,
                     1.0 / HEAD_DIM, 0.0).astype(BF16)
    tm_lat = _pick_tile(t, 512)
    tm_ctx = _pick_tile(n_ctx, 256)
    zero_state = (jnp.zeros((batch, 2, GROUP_W, GROUP_W), F32), jnp.zeros((batch, 2, GROUP_W, GROUP_W), F32),
                  jnp.zeros((batch, 2, 8, GROUP_W), F32))
    mlstm_consts = _mlstm_consts(MLSTM_L)
    w_in_b = _layout_w_in(w_in)
    w_out_b = _layout_w_out(w_out)
    w_up_b = w_up.astype(BF16)
    w_down_b = w_down.astype(BF16)

    for layer in range(depth):
        need_ctx = layer < depth - 1
        mod = mod_all[layer]
        g1 = g_norm1[layer][None]
        g2 = g_norm2[layer][None]
        gq = jnp.tile(a_q_gain[layer], N_HEADS)[None]
        gk = jnp.tile(a_k_gain[layer], 2)[None]
        gbias = jnp.pad(c_gate_bias[layer], (0, LANES - 4 * N_HEADS))[None]
        gg = g_group[layer].reshape(N_HEADS, GROUP_W)
        gg = jnp.concatenate([_permute_heads(gg[0:2], 1), gg[2:4]], axis=0)
        conv_l = jnp.concatenate([conv_w[layer], conv_b[layer][None],
                                  jnp.zeros((8 - 1 - conv_w.shape[1], 2 * d_ff), F32)], axis=0)
        sink = b_sink[layer]
        bias_tab = _neighbour_bias(d_rel_bias[layer])

        proj = functools.partial(_in_proj, g1=g1, w=w_in_b, pool=pool, gq=gq, gk=gk, gbias=gbias, layer=layer)
        (aqc, akc, avc, bqc, bkc, bvc, cqc, ckc, cvc, coc, cgc, dqc, dkc, dvc) = proj(
            ctx, mod, ctx_row, cos=cos, sin=sin, rope=False, tm=tm_ctx)
        (aq, ak, av, bq, bk, bv, cq, ck, cv, co, cg, dq, dk, dv) = proj(
            x, mod, lat_row, cos=cos, sin=sin, rope=True, tm=tm_lat)

        score_bound = 1.02 * LOG2_E * HEAD_DIM ** 0.5 * jnp.max(jnp.abs(a_q_gain[layer])) * jnp.max(jnp.abs(a_k_gain[layer]))
        safe = (score_bound <= MAX_UNSHIFTED_LOG2_SCORE).astype(jnp.int32).reshape(1)
        ya = _attn_global(safe, aq, ak, av, akc, avc, tq=_pick_tile(t, 1024), tk=_pick_tile(t, 1024))
        yb = _attn_window(sink, bq, bk, bv, bkc, bvc, tq=_pick_tile(t, 256))
        yd = _attn_neighbour(dq, dk, dv, dkc, dvc, bias_tab, rows_per_step=8)

        hcf, hcb, ctx_state = _mlstm_scan(cqc, ckc, cvc, cgc, zero_state, mlstm_consts)
        hf, hb, _ = _mlstm_scan(cq, ck, cv, cg, ctx_state, mlstm_consts)

        x = _out_proj(x, ya, yb, hf, hb, co, yd, mod, lat_row, gg, w_out_b, layer=layer, tm=tm_lat)
        ffn = functools.partial(_conv_ffn, g2=g2, wu=w_up_b, cw=conv_l, wd=w_down_b, layer=layer)
        x = ffn(x, mod, lat_row, g_final=None if need_ctx else g_final[None], tm=_pick_tile(t, 512))
        if need_ctx:
            yac, ybc, ydc = _ctx_attention(sink, aqc, akc, avc, bqc, bkc, bvc, dqc, dkc, dvc)
            ctx = _out_proj(ctx, yac, ybc, hcf, hcb, coc, ydc, mod, ctx_row, gg, w_out_b, layer=layer, tm=tm_ctx)
            ctx = ffn(ctx, mod, ctx_row, g_final=None, tm=tm_ctx)
    return x
```

```python
import functools

import jax
import jax.numpy as jnp
from jax import lax
from jax.experimental import pallas as pl
from jax.experimental.pallas import tpu as pltpu

F32 = jnp.float32
BF16 = jnp.bfloat16

N_HEADS = 4
HEAD_DIM = 64
GROUP_W = N_HEADS * HEAD_DIM
KV_W = 2 * HEAD_DIM
GRID_W = 64
WINDOW = 128
NA_KH = 8
NA_KW = 16
ROPE_THETA = 10000.0
EPS = 1e-6
NEG_INF = -1e30
LANES = 128
MLSTM_L = 256
HALO = 8
MIB = 1024 * 1024
LOG2_E = 1.4426950408889634
MAX_UNSHIFTED_LOG2_SCORE = 60.0


TOKEN_TILE = 512
CTX_TILE = 256
ATTN_Q_TILE = 1024
ATTN_K_TILE = 2048
WINDOW_Q_TILE = 256
NEIGHBOUR_ROWS = 8
VMEM_LIMIT_MIB = {"modulation": 40, "in_proj": 48, "attn_global": 48, "attn_window": 48, "attn_neighbour": 48,
                  "ctx_attention": 32, "mlstm": 48, "out_proj": 48, "conv_ffn": 56}


def _params(sem, name):
    return pltpu.CompilerParams(dimension_semantics=sem, vmem_limit_bytes=VMEM_LIMIT_MIB[name] * MIB)


def _dot(a, b):
    return jnp.dot(a, b, preferred_element_type=F32)


def _dot_nt(a, b):
    return lax.dot_general(a, b, (((1,), (1,)), ((), ())), preferred_element_type=F32)


def _const_spec(shape):
    return pl.BlockSpec(shape, lambda *_: (0,) * len(shape), pipeline_mode=pl.Buffered(1))


def _layer_spec(stacked_shape, layer):
    rest = tuple(stacked_shape[1:])
    return pl.BlockSpec((None,) + rest, lambda *_: (layer,) + (0,) * len(rest), pipeline_mode=pl.Buffered(1))


def _mod_kernel(c_ref, w_ref, b_ref, o_ref):
    c = c_ref[...]
    a = (c * jax.nn.sigmoid(c)).astype(BF16)
    o_ref[0] = _dot(a, w_ref[0].astype(BF16)) + b_ref[0]


def _modulation(c_rows, w_mod, b_mod):
    depth, d, n = w_mod.shape
    tn = 1536
    return pl.pallas_call(
        _mod_kernel,
        grid=(depth, n // tn),
        in_specs=[pl.BlockSpec((8, d), lambda l, j: (0, 0)),
                  pl.BlockSpec((1, d, tn), lambda l, j: (l, 0, j)),
                  pl.BlockSpec((1, 1, tn), lambda l, j: (l, 0, j))],
        out_specs=pl.BlockSpec((1, 8, tn), lambda l, j: (l, 0, j)),
        out_shape=jax.ShapeDtypeStruct((depth, 8, n), F32),
        compiler_params=_params(("parallel", "parallel"), "modulation"),
        name="modulation",
    )(c_rows, w_mod, b_mod.reshape(depth, 1, n))


def _rms_rows(x):
    return x * lax.rsqrt(jnp.mean(x * x, axis=-1, keepdims=True) + EPS)


def _head_rms(z, pool, gain):
    z2 = z * z
    hi = z2.astype(BF16)
    lo = (z2 - hi.astype(F32)).astype(BF16)
    ms = _dot(hi, pool) + _dot(lo, pool)
    return z * lax.rsqrt(ms + EPS) * gain


def _rope(z, cos, sin):
    lane = lax.broadcasted_iota(jnp.int32, z.shape, 1)
    first = (lane % 32) < 16
    partner = jnp.where(first, pltpu.roll(z, LANES - 16, 1), pltpu.roll(z, 16, 1))
    return z * cos + partner * sin


def _inproj_kernel(x_ref, mod_ref, g1_ref, w_ref, pool_ref, gq_ref, gk_ref, cos_ref, sin_ref, gb_ref,
                   aq_ref, ak_ref, av_ref, bq_ref, bk_ref, bv_ref, cq_ref, ck_ref, cv_ref, co_ref, cg_ref,
                   dq_ref, dk_ref, dv_ref, *, d_model, rope):
    d = d_model
    mod = mod_ref[0]
    xn = _rms_rows(x_ref[0]) * g1_ref[...]
    hb = (xn * (1.0 + mod[:, d:2 * d]) + mod[:, 0:d]).astype(BF16)
    projected = _dot(hb, w_ref[...])

    def proj(seg, width=GROUP_W):
        return projected[:, seg * GROUP_W:seg * GROUP_W + width]

    def rot(z):
        if not rope:
            return z
        cos, sin = cos_ref[...], sin_ref[...]
        return jnp.concatenate([_rope(z[:, i:i + LANES], cos, sin) for i in range(0, z.shape[1], LANES)], axis=1)

    scale = HEAD_DIM ** -0.5
    pool = pool_ref[...]
    aq_ref[0] = (rot(_head_rms(proj(0), pool, gq_ref[...])) * (scale * LOG2_E)).astype(BF16)
    akv = proj(1)
    ak_ref[0] = rot(_head_rms(akv[:, :KV_W], pool[:KV_W, :KV_W], gk_ref[...])).astype(BF16)
    ones = jnp.ones((akv.shape[0], LANES), F32)
    av_ref[0] = jnp.concatenate([akv[:, KV_W:], ones], axis=1).astype(BF16)
    bq_ref[0] = (rot(proj(2)) * (scale * LOG2_E)).astype(BF16)
    bkv = proj(3)
    bk_ref[0] = rot(bkv[:, :KV_W]).astype(BF16)
    bv_ref[0] = jnp.concatenate([bkv[:, KV_W:], ones], axis=1).astype(BF16)
    cq_ref[0] = (proj(4) * scale).astype(BF16)
    ck_ref[0] = proj(5).astype(BF16)
    cv_ref[0] = proj(6).astype(BF16)
    co_ref[0] = proj(7)
    dq_ref[0] = (proj(8) * scale).astype(BF16)
    dk_ref[0] = proj(9).astype(BF16)
    dv_ref[0] = proj(10).astype(BF16)
    gates = proj(11, LANES) + gb_ref[...]
    forget_lane = (lax.broadcasted_iota(jnp.int32, gates.shape, 1) % 8) >= 4
    cg_ref[0] = jnp.where(forget_lane, _log_sigmoid(gates), gates) * LOG2_E


def _in_proj(x, mod, mod_row, g1, w, pool, gq, gk, cos, sin, gbias, *, layer, rope, tm):
    b, t, d = x.shape
    widths = [(GROUP_W, BF16), (KV_W, BF16), (2 * LANES, BF16),
              (GROUP_W, BF16), (KV_W, BF16), (2 * LANES, BF16),
              (GROUP_W, BF16), (GROUP_W, BF16), (GROUP_W, BF16), (GROUP_W, F32), (LANES, F32),
              (GROUP_W, BF16), (GROUP_W, BF16), (GROUP_W, BF16)]
    row = lambda wd: pl.BlockSpec((1, wd), lambda bi, j: (0, 0))
    return pl.pallas_call(
        functools.partial(_inproj_kernel, d_model=d, rope=rope),
        grid=(b, t // tm),
        in_specs=[pl.BlockSpec((1, tm, d), lambda bi, j: (bi, j, 0)),
                  pl.BlockSpec((1, 1, 6 * d), lambda bi, j: (mod_row(bi), 0, 0)),
                  row(d),
                  _layer_spec(w.shape, layer),
                  _const_spec((GROUP_W, GROUP_W)),
                  row(GROUP_W), row(KV_W),
                  pl.BlockSpec((tm, LANES), lambda bi, j: (j, 0)),
                  pl.BlockSpec((tm, LANES), lambda bi, j: (j, 0)),
                  row(LANES)],
        out_specs=[pl.BlockSpec((1, tm, wd), lambda bi, j: (bi, j, 0)) for wd, _ in widths],
        out_shape=[jax.ShapeDtypeStruct((b, t, wd), dt) for wd, dt in widths],
        compiler_params=_params(("parallel", "parallel"), "in_proj"),
        name="in_proj",
    )(x, mod, g1, w, pool, gq, gk, cos, sin, gbias)


def _stack_gqa(q):
    qa, qb = q[:, :LANES], q[:, LANES:]
    left = lax.broadcasted_iota(jnp.int32, qa.shape, 1) < HEAD_DIM
    zero = jnp.zeros_like(qa)
    return jnp.concatenate([jnp.where(left, qa, zero), jnp.where(left, qb, zero),
                            jnp.where(left, zero, qa), jnp.where(left, zero, qb)], axis=0)


def _unstack_gqa(o, t):
    left = lax.broadcasted_iota(jnp.int32, (t, LANES), 1) < HEAD_DIM
    return jnp.concatenate([jnp.where(left, o[0:t], o[2 * t:3 * t]),
                            jnp.where(left, o[t:2 * t], o[3 * t:4 * t])], axis=1)


def _attn_a_kernel(safe_ref, q_ref, k_ref, v_ref, kc_ref, vc_ref, o_ref, qs_ref, m_ref, acc_ref, *, tq, tk, n_kb):
    qs_ref[...] = _stack_gqa(q_ref[0])
    heads = [slice(h * tq, (h + 1) * tq) for h in range(N_HEADS)]

    @pl.when(safe_ref[0] != 0)
    def _():
        for rows in heads:
            p = jnp.exp2(_dot_nt(qs_ref[rows], kc_ref[0]))
            acc_ref[rows] = _dot(p.astype(BF16), vc_ref[0])

        def body(kb, carry):
            start = pl.multiple_of(kb * tk, tk)
            for rows in heads:
                p = jnp.exp2(_dot_nt(qs_ref[rows], k_ref[0, pl.ds(start, tk), :]))
                acc_ref[rows] += _dot(p.astype(BF16), v_ref[0, pl.ds(start, tk), :])
            return carry

        lax.fori_loop(0, n_kb, body, 0)

    @pl.when(safe_ref[0] == 0)
    def _():
        for rows in heads:
            s = _dot_nt(qs_ref[rows], kc_ref[0])
            m0 = jnp.max(s, axis=-1, keepdims=True)
            m_ref[rows] = m0
            acc_ref[rows] = _dot(jnp.exp2(s - m0).astype(BF16), vc_ref[0])

        def body(kb, carry):
            start = pl.multiple_of(kb * tk, tk)
            for rows in heads:
                s = _dot_nt(qs_ref[rows], k_ref[0, pl.ds(start, tk), :])
                m_prev = m_ref[rows]
                m_new = jnp.maximum(m_prev, jnp.max(s, axis=-1, keepdims=True))
                p = jnp.exp2(s - m_new)
                acc_ref[rows] = jnp.exp2(m_prev - m_new) * acc_ref[rows] + _dot(
                    p.astype(BF16), v_ref[0, pl.ds(start, tk), :])
                m_ref[rows] = m_new
            return carry

        lax.fori_loop(0, n_kb, body, 0)

    acc = acc_ref[...]
    o_ref[0] = _unstack_gqa(acc[:, :LANES] / acc[:, LANES:], tq).astype(o_ref.dtype)


def _attn_global(safe, q, k, v, kc, vc, *, tq, tk):
    b, t, _ = q.shape
    n_ctx = kc.shape[1]
    per_b = lambda n, wd: pl.BlockSpec((1, n, wd), lambda bi, i: (bi, 0, 0))
    return pl.pallas_call(
        functools.partial(_attn_a_kernel, tq=tq, tk=tk, n_kb=t // tk),
        grid=(b, t // tq),
        in_specs=[pl.BlockSpec(memory_space=pltpu.SMEM),
                  pl.BlockSpec((1, tq, GROUP_W), lambda bi, i: (bi, i, 0)),
                  per_b(t, KV_W), per_b(t, 2 * LANES), per_b(n_ctx, KV_W), per_b(n_ctx, 2 * LANES)],
        out_specs=pl.BlockSpec((1, tq, GROUP_W), lambda bi, i: (bi, i, 0)),
        out_shape=jax.ShapeDtypeStruct((b, t, GROUP_W), BF16),
        scratch_shapes=[pltpu.VMEM((4 * tq, LANES), BF16), pltpu.VMEM((4 * tq, 1), F32),
                        pltpu.VMEM((4 * tq, 2 * LANES), F32)],
        compiler_params=_params(("parallel", "parallel"), "attn_global"),
        name="attn_global",
    )(safe, q, k, v, kc, vc)


def _sink_column(sink_ref, t):
    row = lax.broadcasted_iota(jnp.int32, (4 * t, 1), 0)
    col = jnp.full((4 * t, 1), sink_ref[3], F32)
    for h in (2, 1, 0):
        col = jnp.where(row < (h + 1) * t, sink_ref[h], col)
    return col


def _attn_b_kernel(sink_ref, q_ref, k_ref, v_ref, kc_ref, vc_ref, o_ref, *, tq, t_total):
    span = tq + 2 * WINDOW
    q0 = pl.program_id(1) * tq
    ks = pl.multiple_of(jnp.clip(q0 - WINDOW, 0, t_total - span), WINDOW)
    qs = _stack_gqa(q_ref[0])
    kw = k_ref[0, pl.ds(ks, span), :]
    vw = v_ref[0, pl.ds(ks, span), :]
    q_pos = q0 + lax.broadcasted_iota(jnp.int32, (tq, span), 0)
    k_pos = ks + lax.broadcasted_iota(jnp.int32, (tq, span), 1)
    in_window = jnp.abs(k_pos - q_pos) <= WINDOW
    outs = []
    for h in range(N_HEADS):
        qh = qs[h * tq:(h + 1) * tq]
        s = jnp.where(in_window, _dot_nt(qh, kw), NEG_INF)
        sc = _dot_nt(qh, kc_ref[0])
        sink = sink_ref[h] * LOG2_E
        m = jnp.maximum(jnp.maximum(jnp.max(s, axis=-1, keepdims=True), jnp.max(sc, axis=-1, keepdims=True)), sink)
        acc = _dot(jnp.exp2(s - m).astype(BF16), vw) + _dot(jnp.exp2(sc - m).astype(BF16), vc_ref[0])
        outs.append(acc[:, :LANES] / (acc[:, LANES:] + jnp.exp2(sink - m)))
    o_ref[0] = _unstack_gqa(jnp.concatenate(outs, axis=0), tq).astype(o_ref.dtype)


def _attn_window(sink, q, k, v, kc, vc, *, tq):
    b, t, _ = q.shape
    n_ctx = kc.shape[1]
    assert t >= tq + 2 * WINDOW
    per_b = lambda n, wd: pl.BlockSpec((1, n, wd), lambda bi, i: (bi, 0, 0))
    return pl.pallas_call(
        functools.partial(_attn_b_kernel, tq=tq, t_total=t),
        grid=(b, t // tq),
        in_specs=[pl.BlockSpec(memory_space=pltpu.SMEM),
                  pl.BlockSpec((1, tq, GROUP_W), lambda bi, i: (bi, i, 0)),
                  per_b(t, KV_W), per_b(t, 2 * LANES), per_b(n_ctx, KV_W), per_b(n_ctx, 2 * LANES)],
        out_specs=pl.BlockSpec((1, tq, GROUP_W), lambda bi, i: (bi, i, 0)),
        out_shape=jax.ShapeDtypeStruct((b, t, GROUP_W), BF16),
        compiler_params=_params(("parallel", "parallel"), "attn_window"),
        name="attn_window",
    )(sink, q, k, v, kc, vc)


def _head_masks(shape):
    lane = lax.broadcasted_iota(jnp.int32, shape, 1)
    return [(lane >= h * HEAD_DIM) & (lane < (h + 1) * HEAD_DIM) for h in range(N_HEADS)]


def _stack_mha(q):
    zero = jnp.zeros_like(q)
    return jnp.concatenate([jnp.where(mk, q, zero) for mk in _head_masks(q.shape)], axis=0)


def _unstack_mha(o, t):
    masks = _head_masks((t, GROUP_W))
    out = jnp.where(masks[0], o[0:t], 0.0)
    for h in range(1, N_HEADS):
        out = jnp.where(masks[h], o[h * t:(h + 1) * t], out)
    return out


def _attn_d_kernel(q_ref, k_ref, v_ref, kc_ref, vc_ref, bias_ref, o_ref, *, rows_per_step, rows):
    kc = kc_ref[0]
    vc = vc_ref[0]
    span = NA_KH * GRID_W

    def body(rr, carry):
        r = pl.program_id(1) * rows_per_step + rr
        rs = jnp.clip(r - NA_KH // 2, 0, rows - NA_KH)
        k0 = pl.multiple_of(rs * GRID_W, GRID_W)
        q0 = pl.multiple_of(rr * GRID_W, GRID_W)
        qs = _stack_mha(q_ref[0, pl.ds(q0, GRID_W), :])
        s = _dot_nt(qs, k_ref[0, pl.ds(k0, span), :]) + bias_ref[r - rs]
        sc = _dot_nt(qs, kc)
        m = jnp.maximum(jnp.max(s, axis=-1, keepdims=True), jnp.max(sc, axis=-1, keepdims=True))
        p = jnp.exp(s - m)
        pc = jnp.exp(sc - m)
        l = jnp.sum(p, axis=-1, keepdims=True) + jnp.sum(pc, axis=-1, keepdims=True)
        o = _dot(p.astype(BF16), v_ref[0, pl.ds(k0, span), :]) + _dot(pc.astype(BF16), vc)
        o_ref[0, pl.ds(q0, GRID_W), :] = _unstack_mha(o / l, GRID_W).astype(o_ref.dtype)
        return carry

    lax.fori_loop(0, rows_per_step, body, 0, unroll=True)


def _attn_neighbour(q, k, v, kc, vc, bias, *, rows_per_step):
    b, t, _ = q.shape
    n_ctx = kc.shape[1]
    rows = t // GRID_W
    assert rows >= NA_KH and rows % rows_per_step == 0
    tq = rows_per_step * GRID_W
    per_b = lambda n: pl.BlockSpec((1, n, GROUP_W), lambda bi, i: (bi, 0, 0))
    return pl.pallas_call(
        functools.partial(_attn_d_kernel, rows_per_step=rows_per_step, rows=rows),
        grid=(b, t // tq),
        in_specs=[pl.BlockSpec((1, tq, GROUP_W), lambda bi, i: (bi, i, 0)),
                  per_b(t), per_b(t), per_b(n_ctx), per_b(n_ctx),
                  _const_spec(bias.shape)],
        out_specs=pl.BlockSpec((1, tq, GROUP_W), lambda bi, i: (bi, i, 0)),
        out_shape=jax.ShapeDtypeStruct((b, t, GROUP_W), BF16),
        compiler_params=_params(("parallel", "parallel"), "attn_neighbour"),
        name="attn_neighbour",
    )(q, k, v, kc, vc, bias)


def _neighbour_bias(rpb):
    w = jnp.arange(GRID_W)
    cs = jnp.clip(w - NA_KW // 2, 0, GRID_W - NA_KW)
    col = jnp.arange(GRID_W)
    valid = (col[None, :] >= cs[:, None]) & (col[None, :] < cs[:, None] + NA_KW)
    dc = jnp.clip(col[None, :] - w[:, None] + (NA_KW - 1), 0, 2 * NA_KW - 2)
    dr = jnp.arange(NA_KH)[None, :] - jnp.arange(NA_KH)[:, None] + (NA_KH - 1)
    pick_r = (dr[:, :, None] == jnp.arange(2 * NA_KH - 1)).astype(F32)
    pick_c = (jnp.arange(2 * NA_KW - 1)[:, None, None] == dc[None]).astype(F32)
    tab = jnp.einsum('hrc,oir,cwk->ohwik', rpb.astype(F32), pick_r, pick_c, precision=lax.Precision.HIGHEST)
    tab = jnp.where(valid[None, None, :, None, :], tab, NEG_INF)
    return tab.reshape(NA_KH, N_HEADS * GRID_W, NA_KH * GRID_W)


def _ctx_attn_kernel(sink_ref, aq_ref, ak_ref, av_ref, bq_ref, bk_ref, bv_ref, dq_ref, dk_ref, dv_ref,
                     ya_ref, yb_ref, yd_ref, *, n):
    s = _dot_nt(_stack_gqa(aq_ref[0]), ak_ref[0])
    acc = _dot(jnp.exp2(s - jnp.max(s, axis=-1, keepdims=True)).astype(BF16), av_ref[0])
    ya_ref[0] = _unstack_gqa(acc[:, :LANES] / acc[:, LANES:], n).astype(ya_ref.dtype)
    s = _dot_nt(_stack_gqa(bq_ref[0]), bk_ref[0])
    sink = _sink_column(sink_ref, n) * LOG2_E
    m = jnp.maximum(jnp.max(s, axis=-1, keepdims=True), sink)
    acc = _dot(jnp.exp2(s - m).astype(BF16), bv_ref[0])
    yb_ref[0] = _unstack_gqa(acc[:, :LANES] / (acc[:, LANES:] + jnp.exp2(sink - m)), n).astype(yb_ref.dtype)
    s = _dot_nt(_stack_mha(dq_ref[0]), dk_ref[0])
    p = jnp.exp(s - jnp.max(s, axis=-1, keepdims=True))
    o = _dot(p.astype(BF16), dv_ref[0]) / jnp.sum(p, axis=-1, keepdims=True)
    yd_ref[0] = _unstack_mha(o, n).astype(yd_ref.dtype)


def _ctx_attention(sink, aq, ak, av, bq, bk, bv, dq, dk, dv):
    b, n, _ = aq.shape
    spec = lambda wd: pl.BlockSpec((1, n, wd), lambda bi: (bi, 0, 0))
    return pl.pallas_call(
        functools.partial(_ctx_attn_kernel, n=n),
        grid=(b,),
        in_specs=[pl.BlockSpec(memory_space=pltpu.SMEM),
                  spec(GROUP_W), spec(KV_W), spec(2 * LANES), spec(GROUP_W), spec(KV_W), spec(2 * LANES),
                  spec(GROUP_W), spec(GROUP_W), spec(GROUP_W)],
        out_specs=[spec(GROUP_W)] * 3,
        out_shape=[jax.ShapeDtypeStruct((b, n, GROUP_W), BF16)] * 3,
        compiler_params=_params(("parallel",), "ctx_attention"),
        name="ctx_attention",
    )(sink, aq, ak, av, bq, bk, bv, dq, dk, dv)


def _log_sigmoid(x):
    return jnp.minimum(x, 0.0) - jnp.log1p(jnp.exp(-jnp.abs(x)))


def _split3(x):
    x1 = x.astype(BF16)
    r1 = x - x1.astype(F32)
    x2 = r1.astype(BF16)
    return x1, x2, (r1 - x2.astype(F32)).astype(BF16)


def _exact_dot_01(sel, x):
    return sum(_dot(sel, piece) for piece in _split3(x))


def _exact_dot_01_rhs(x, sel):
    return sum(_dot(piece, sel) for piece in _split3(x))


def _mlstm_chunk(q, k, v, gates, expand, tri, neg_mask, ones_cat, block, head_rows, s_ref, n_ref, m_ref, *,
                 reverse, L):
    ge = _exact_dot_01_rhs(gates, expand)
    li = ge[:, :GROUP_W]
    cum = _exact_dot_01(tri, ge[:, GROUP_W:])
    a_t = (li - cum).T
    m_prev = m_ref[0:1, :]
    head_b = [head_rows[h].astype(BF16) for h in range(N_HEADS)]
    mu = jnp.zeros((L, GROUP_W), F32)
    scores = []
    for h in range(N_HEADS):
        lane0 = h * HEAD_DIM
        a_m = a_t[lane0:lane0 + 1, :] + neg_mask
        mu_h = jnp.maximum(jnp.max(a_m, axis=-1, keepdims=True), m_prev[:, lane0:lane0 + 1])
        qk = _dot_nt(q * head_b[h], k)
        scores.append((qk * jnp.exp2(a_m - mu_h)).astype(BF16))
        mu = mu + mu_h * head_rows[h]
    s_cat = jnp.concatenate(scores, axis=1)
    v_cat = jnp.concatenate([v * hb for hb in head_b], axis=0)
    w_inter = jnp.exp2(m_prev - mu)
    num = _dot(s_cat, v_cat) + w_inter * _dot(q, s_ref[...].astype(BF16))
    den = _dot(s_cat, ones_cat) + w_inter * _dot(q, n_ref[...].astype(BF16))
    h_out = num / jnp.maximum(jnp.abs(den), jnp.exp2(-(cum + mu)))

    end_row = 0 if reverse else L - 1
    cum_end = cum[end_row:end_row + 1, :]
    log_end = cum_end - cum + li
    m_new = jnp.maximum(cum_end + m_prev, jnp.max(log_end, axis=0, keepdims=True))
    decay = jnp.exp2(cum_end + m_prev - m_new)
    w_end = jnp.exp2(log_end - m_new)
    k_t = k.astype(F32).T.astype(BF16)
    upd = _dot(k_t, jnp.concatenate([(v.astype(F32) * w_end).astype(BF16), w_end.astype(BF16)], axis=1))
    s_ref[...] = (s_ref[...] * decay + upd[:, :GROUP_W]) * block
    n_ref[...] = (n_ref[...] * decay + upd[:, GROUP_W:]) * block
    m_ref[...] = jnp.broadcast_to(m_new, m_ref.shape)
    return h_out


def _mlstm_kernel(qf_ref, kf_ref, vf_ref, gf_ref, qb_ref, kb_ref, vb_ref, gb_ref,
                  expand_ref, tri_ref, neg_ref, ones_ref, block_ref, heads_ref, s0_ref, n0_ref, m0_ref,
                  hf_ref, hb_ref, s_out, n_out, m_out, s_ref, n_ref, m_ref, *, chunk):
    @pl.when(pl.program_id(1) == 0)
    def _():
        s_ref[...] = s0_ref[0]
        n_ref[...] = n0_ref[0]
        m_ref[...] = m0_ref[0]

    sides = ((qf_ref, kf_ref, vf_ref, gf_ref, hf_ref), (qb_ref, kb_ref, vb_ref, gb_ref, hb_ref))
    for d, (q_ref, k_ref, v_ref, g_ref, h_ref) in enumerate(sides):
        h = _mlstm_chunk(q_ref[0], k_ref[0], v_ref[0], g_ref[0], expand_ref[d], tri_ref[d], neg_ref[d],
                         ones_ref[...], block_ref[...], heads_ref, s_ref.at[d], n_ref.at[d], m_ref.at[d],
                         reverse=bool(d), L=chunk)
        h_ref[0] = h.astype(h_ref.dtype)
    s_out[0] = s_ref[...]
    n_out[0] = n_ref[...]
    m_out[0] = m_ref[...]


def _mlstm_consts(L):
    lane = jnp.arange(LANES)[:, None]
    col = jnp.arange(2 * GROUP_W)[None, :]
    col_head = (col % GROUP_W) // HEAD_DIM
    expand = jnp.stack([(lane == 8 * d + 4 * (col // GROUP_W) + col_head) for d in range(2)]).astype(BF16)
    t_idx = jnp.arange(L)[:, None]
    s_idx = jnp.arange(L)[None, :]
    seen = jnp.stack([s_idx <= t_idx, s_idx >= t_idx])
    head = jnp.arange(GROUP_W) // HEAD_DIM
    ones_cat = (jnp.repeat(jnp.arange(N_HEADS), L)[:, None] == head[None, :]).astype(BF16)
    block = (head[:, None] == head[None, :]).astype(F32)
    head_rows = (jnp.arange(N_HEADS)[:, None, None] == head[None, None, :]).astype(F32)
    return expand, seen.astype(BF16), jnp.where(seen, 0.0, NEG_INF).astype(F32), ones_cat, block, head_rows


def _mlstm_scan(q, k, v, gates, state, consts):
    b, t, _ = q.shape
    L = MLSTM_L
    nc = t // L
    fwd = lambda wd: pl.BlockSpec((1, L, wd), lambda bi, j: (bi, j, 0))
    bwd = lambda wd: pl.BlockSpec((1, L, wd), lambda bi, j: (bi, nc - 1 - j, 0))
    st = lambda r: pl.BlockSpec((1, 2, r, GROUP_W), lambda bi, j: (bi, 0, 0, 0))
    tok_specs = [fwd(GROUP_W), fwd(GROUP_W), fwd(GROUP_W), fwd(LANES),
                 bwd(GROUP_W), bwd(GROUP_W), bwd(GROUP_W), bwd(LANES)]
    hf, hb, s_fin, n_fin, m_fin = pl.pallas_call(
        functools.partial(_mlstm_kernel, chunk=L),
        grid=(b, nc),
        in_specs=tok_specs + [_const_spec(c.shape) for c in consts] + [st(GROUP_W), st(GROUP_W), st(8)],
        out_specs=[fwd(GROUP_W), bwd(GROUP_W), st(GROUP_W), st(GROUP_W), st(8)],
        out_shape=[jax.ShapeDtypeStruct((b, t, GROUP_W), BF16), jax.ShapeDtypeStruct((b, t, GROUP_W), BF16),
                   jax.ShapeDtypeStruct((b, 2, GROUP_W, GROUP_W), F32),
                   jax.ShapeDtypeStruct((b, 2, GROUP_W, GROUP_W), F32),
                   jax.ShapeDtypeStruct((b, 2, 8, GROUP_W), F32)],
        scratch_shapes=[pltpu.VMEM((2, GROUP_W, GROUP_W), F32), pltpu.VMEM((2, GROUP_W, GROUP_W), F32),
                        pltpu.VMEM((2, 8, GROUP_W), F32)],
        compiler_params=_params(("parallel", "arbitrary"), "mlstm"),
        name="mlstm",
    )(q, k, v, gates, q, k, v, gates, *consts, *state)
    return hf, hb, (s_fin, n_fin, m_fin)


def _outproj_kernel(x_ref, ya_ref, yb_ref, hf_ref, hb_ref, og_ref, yd_ref, mod_ref, gg_ref, w_ref, o_ref, *, d_model):
    d = d_model
    yc = jax.nn.sigmoid(og_ref[0]) * (hf_ref[0].astype(F32) + hb_ref[0].astype(F32))
    parts = []
    for i, y in enumerate((ya_ref[0], yb_ref[0], yc, yd_ref[0])):
        parts.append((_rms_rows(y.astype(F32)) * gg_ref[i:i + 1, :]).astype(BF16))
    res = _dot(jnp.concatenate(parts, axis=1), w_ref[...])
    o_ref[0] = x_ref[0] + mod_ref[0][:, 2 * d:3 * d] * res


def _out_proj(x, ya, yb, hf, hb, o_gate, yd, mod, mod_row, gg, w, *, layer, tm):
    b, t, d = x.shape
    tokd = pl.BlockSpec((1, tm, d), lambda bi, j: (bi, j, 0))
    tokg = pl.BlockSpec((1, tm, GROUP_W), lambda bi, j: (bi, j, 0))
    return pl.pallas_call(
        functools.partial(_outproj_kernel, d_model=d),
        grid=(b, t // tm),
        in_specs=[tokd, tokg, tokg, tokg, tokg, tokg, tokg,
                  pl.BlockSpec((1, 1, 6 * d), lambda bi, j: (mod_row(bi), 0, 0)),
                  _const_spec(gg.shape), _layer_spec(w.shape, layer)],
        out_specs=tokd,
        out_shape=jax.ShapeDtypeStruct((b, t, d), F32),
        compiler_params=_params(("parallel", "parallel"), "out_proj"),
        name="out_proj",
    )(x, ya, yb, hf, hb, o_gate, yd, mod, gg, w)


def _ffn_kernel(*refs, d_model, d_ff, tm, final_norm):
    if final_norm:
        x_ref, xp_ref, xn_ref, mod_ref, g2_ref, wu_ref, cw_ref, wd_ref, gf_ref, o_ref = refs
    else:
        x_ref, xp_ref, xn_ref, mod_ref, g2_ref, wu_ref, cw_ref, wd_ref, o_ref = refs
    d = d_model
    j = pl.program_id(1)
    mod = mod_ref[0]
    g2 = g2_ref[...]

    def norm_mod(x):
        return _rms_rows(x) * g2 * (1.0 + mod[:, 4 * d:5 * d]) + mod[:, 3 * d:4 * d]

    keep_prev = (j > 0).astype(F32)
    keep_next = (j < pl.num_programs(1) - 1).astype(F32)
    x = x_ref[0]
    h_ext = jnp.concatenate([norm_mod(xp_ref[0]) * keep_prev, norm_mod(x), norm_mod(xn_ref[0]) * keep_next],
                            axis=0).astype(BF16)
    n_ext = tm + 2 * HALO

    u = _dot(h_ext, wu_ref[...])
    cw = cw_ref[...]
    u = (pltpu.roll(u, 1, 0) * cw[0:1] + u * cw[1:2] + pltpu.roll(u, n_ext - 1, 0) * cw[2:3] + cw[3:4])[HALO:HALO + tm]
    gate, val = u[:, :d_ff], u[:, d_ff:]
    act = (gate * jax.nn.sigmoid(gate) * val).astype(BF16)
    y = x + mod[:, 5 * d:6 * d] * _dot(act, wd_ref[...])
    if final_norm:
        y = _rms_rows(y) * gf_ref[...]
    o_ref[0] = y


def _conv_ffn(x, mod, mod_row, g2, wu, cw, wd, g_final, *, layer, tm):
    b, t, d = x.shape
    d_ff = wd.shape[1]
    final_norm = g_final is not None
    hb = tm // HALO
    last = t // HALO - 1
    row = pl.BlockSpec((1, d), lambda bi, j: (0, 0))
    in_specs = [pl.BlockSpec((1, tm, d), lambda bi, j: (bi, j, 0)),
                pl.BlockSpec((1, HALO, d), lambda bi, j: (bi, jnp.maximum(j * hb - 1, 0), 0)),
                pl.BlockSpec((1, HALO, d), lambda bi, j: (bi, jnp.minimum((j + 1) * hb, last), 0)),
                pl.BlockSpec((1, 1, 6 * d), lambda bi, j: (mod_row(bi), 0, 0)),
                row,
                _layer_spec(wu.shape, layer), _const_spec(cw.shape), _layer_spec(wd.shape, layer)]
    args = [x, x, x, mod, g2, wu, cw, wd]
    if final_norm:
        in_specs.append(row)
        args.append(g_final)
    return pl.pallas_call(
        functools.partial(_ffn_kernel, d_model=d, d_ff=d_ff, tm=tm, final_norm=final_norm),
        grid=(b, t // tm),
        in_specs=in_specs,
        out_specs=pl.BlockSpec((1, tm, d), lambda bi, j: (bi, j, 0)),
        out_shape=jax.ShapeDtypeStruct((b, t, d), F32),
        compiler_params=_params(("parallel", "parallel"), "conv_ffn"),
        name="conv_ffn",
    )(*args)


_Q_HEAD_ORDER = (0, 2, 1, 3)


def _permute_heads(w, axis):
    parts = [lax.slice_in_dim(w, h * HEAD_DIM, (h + 1) * HEAD_DIM, axis=axis) for h in _Q_HEAD_ORDER]
    return jnp.concatenate(parts, axis=axis)


def _layout_w_in(w):
    w = w.astype(BF16)
    g, kv = GROUP_W, KV_W
    o_b = g + 2 * kv
    o_c = 2 * o_b
    o_g = o_c + 4 * g
    o_d = o_g + 4 * N_HEADS
    sl = lambda a, n: w[:, :, a:a + n]
    pad = jnp.zeros(w.shape[:2] + (LANES - 4 * N_HEADS,), BF16)
    return jnp.concatenate([_permute_heads(sl(0, g), 2), sl(g, 2 * kv),
                            _permute_heads(sl(o_b, g), 2), sl(o_b + g, 2 * kv),
                            sl(o_c, 4 * g), sl(o_d, 3 * g), sl(o_g, 4 * N_HEADS), pad], axis=2)


def _layout_w_out(w):
    w = w.astype(BF16)
    return jnp.concatenate([_permute_heads(w[:, 0:GROUP_W], 1), _permute_heads(w[:, GROUP_W:2 * GROUP_W], 1),
                            w[:, 2 * GROUP_W:]], axis=1)


def _rope_tables(t):
    pos = jnp.arange(t)
    row = (pos // GRID_W).astype(F32)
    col = (pos % GRID_W).astype(F32)
    n_freq = HEAD_DIM // 4
    freqs = ROPE_THETA ** (-jnp.arange(n_freq, dtype=F32) / n_freq)
    ang_r, ang_c = row[:, None] * freqs, col[:, None] * freqs
    cos = jnp.concatenate([jnp.cos(ang_r), jnp.cos(ang_r), jnp.cos(ang_c), jnp.cos(ang_c)], axis=1)
    sin = jnp.concatenate([-jnp.sin(ang_r), jnp.sin(ang_r), -jnp.sin(ang_c), jnp.sin(ang_c)], axis=1)
    reps = LANES // HEAD_DIM
    return jnp.tile(cos, (1, reps)), jnp.tile(sin, (1, reps))


def _pick_tile(t, pref):
    while t % pref:
        pref //= 2
    return pref


def kernel(x, c, ctx, c_ctx, w_mod, b_mod, g_norm1, g_norm2, w_in, a_q_gain, a_k_gain, b_sink, c_gate_bias,
           d_rel_bias, g_group, w_out, w_up, conv_w, conv_b, w_down, g_final):
    batch, t, d = x.shape
    n_ctx = ctx.shape[1]
    depth = w_in.shape[0]
    d_ff = w_down.shape[1]
    assert batch < 8 and d_ff % LANES == 0 and t % MLSTM_L == 0 and n_ctx % MLSTM_L == 0

    c_rows = jnp.zeros((8, d), F32).at[:batch].set(c).at[batch].set(c_ctx)
    mod_all = _modulation(c_rows, w_mod, b_mod).reshape(depth, 8, 1, 6 * d)
    lat_row = lambda bi: bi
    ctx_row = lambda bi: batch

    cos, sin = _rope_tables(t)
    pool = jnp.where(jnp.arange(GROUP_W)[:, None] // HEAD_DIM == jnp.arange(GROUP_W)[None, :] // HEAD_DIM,
                     1.0 / HEAD_DIM, 0.0).astype(BF16)
    tm_lat = _pick_tile(t, TOKEN_TILE)
    tm_ctx = _pick_tile(n_ctx, CTX_TILE)
    zero_state = (jnp.zeros((batch, 2, GROUP_W, GROUP_W), F32), jnp.zeros((batch, 2, GROUP_W, GROUP_W), F32),
                  jnp.zeros((batch, 2, 8, GROUP_W), F32))
    mlstm_consts = _mlstm_consts(MLSTM_L)
    w_in_b = _layout_w_in(w_in)
    w_out_b = _layout_w_out(w_out)
    w_up_b = w_up.astype(BF16)
    w_down_b = w_down.astype(BF16)

    for layer in range(depth):
        need_ctx = layer < depth - 1
        mod = mod_all[layer]
        g1 = g_norm1[layer][None]
        g2 = g_norm2[layer][None]
        gq = jnp.tile(a_q_gain[layer], N_HEADS)[None]
        gk = jnp.tile(a_k_gain[layer], 2)[None]
        gbias = jnp.pad(c_gate_bias[layer], (0, LANES - 4 * N_HEADS))[None]
        gg = g_group[layer].reshape(N_HEADS, GROUP_W)
        gg = jnp.concatenate([_permute_heads(gg[0:2], 1), gg[2:4]], axis=0)
        conv_l = jnp.concatenate([conv_w[layer], conv_b[layer][None],
                                  jnp.zeros((8 - 1 - conv_w.shape[1], 2 * d_ff), F32)], axis=0)
        sink = b_sink[layer]
        bias_tab = _neighbour_bias(d_rel_bias[layer])

        proj = functools.partial(_in_proj, g1=g1, w=w_in_b, pool=pool, gq=gq, gk=gk, gbias=gbias, layer=layer)
        (aqc, akc, avc, bqc, bkc, bvc, cqc, ckc, cvc, coc, cgc, dqc, dkc, dvc) = proj(
            ctx, mod, ctx_row, cos=cos, sin=sin, rope=False, tm=tm_ctx)
        (aq, ak, av, bq, bk, bv, cq, ck, cv, co, cg, dq, dk, dv) = proj(
            x, mod, lat_row, cos=cos, sin=sin, rope=True, tm=tm_lat)

        score_bound = 1.02 * LOG2_E * HEAD_DIM ** 0.5 * jnp.max(jnp.abs(a_q_gain[layer])) * jnp.max(jnp.abs(a_k_gain[layer]))
        safe = (score_bound <= MAX_UNSHIFTED_LOG2_SCORE).astype(jnp.int32).reshape(1)
        ya = _attn_global(safe, aq, ak, av, akc, avc, tq=_pick_tile(t, ATTN_Q_TILE), tk=_pick_tile(t, ATTN_K_TILE))
        yb = _attn_window(sink, bq, bk, bv, bkc, bvc, tq=_pick_tile(t, WINDOW_Q_TILE))
        yd = _attn_neighbour(dq, dk, dv, dkc, dvc, bias_tab, rows_per_step=NEIGHBOUR_ROWS)

        hcf, hcb, ctx_state = _mlstm_scan(cqc, ckc, cvc, cgc, zero_state, mlstm_consts)
        hf, hb, _ = _mlstm_scan(cq, ck, cv, cg, ctx_state, mlstm_consts)

        x = _out_proj(x, ya, yb, hf, hb, co, yd, mod, lat_row, gg, w_out_b, layer=layer, tm=tm_lat)
        ffn = functools.partial(_conv_ffn, g2=g2, wu=w_up_b, cw=conv_l, wd=w_down_b, layer=layer)
        x = ffn(x, mod, lat_row, g_final=None if need_ctx else g_final[None], tm=tm_lat)
        if need_ctx:
            yac, ybc, ydc = _ctx_attention(sink, aqc, akc, avc, bqc, bkc, bvc, dqc, dkc, dvc)
            ctx = _out_proj(ctx, yac, ybc, hcf, hcb, coc, ydc, mod, ctx_row, gg, w_out_b, layer=layer, tm=tm_ctx)
            ctx = ffn(ctx, mod, ctx_row, g_final=None, tm=tm_ctx)
    return x
```

```python
import functools

import jax
import jax.numpy as jnp
from jax import lax
from jax.experimental import pallas as pl
from jax.experimental.pallas import tpu as pltpu

F32 = jnp.float32
BF16 = jnp.bfloat16

N_HEADS = 4
HEAD_DIM = 64
GROUP_W = N_HEADS * HEAD_DIM
KV_W = 2 * HEAD_DIM
GRID_W = 64
WINDOW = 128
NA_KH = 8
NA_KW = 16
ROPE_THETA = 10000.0
EPS = 1e-6
NEG_INF = -1e30
LANES = 128
MLSTM_L = 256
HALO = 8
MIB = 1024 * 1024
LOG2_E = 1.4426950408889634
MAX_UNSHIFTED_LOG2_SCORE = 60.0


TOKEN_TILE = 512
CTX_TILE = 256
ATTN_Q_TILE = 1024
ATTN_K_TILE = 2048
WINDOW_Q_TILE = 256
NEIGHBOUR_ROWS = 8
VMEM_LIMIT_MIB = {"modulation": 40, "in_proj": 48, "attn_global": 48, "attn_window": 48, "attn_neighbour": 48,
                  "ctx_attention": 32, "mlstm": 48, "out_proj": 48, "conv_ffn": 56}


def _params(sem, name):
    return pltpu.CompilerParams(dimension_semantics=sem, vmem_limit_bytes=VMEM_LIMIT_MIB[name] * MIB)


def _dot(a, b):
    return jnp.dot(a, b, preferred_element_type=F32)


def _dot_nt(a, b):
    return lax.dot_general(a, b, (((1,), (1,)), ((), ())), preferred_element_type=F32)


def _const_spec(shape):
    return pl.BlockSpec(shape, lambda *_: (0,) * len(shape), pipeline_mode=pl.Buffered(1))


def _layer_spec(stacked_shape, layer):
    rest = tuple(stacked_shape[1:])
    return pl.BlockSpec((None,) + rest, lambda *_: (layer,) + (0,) * len(rest), pipeline_mode=pl.Buffered(1))


def _mod_kernel(c_ref, w_ref, b_ref, o_ref):
    c = c_ref[...]
    a = (c * jax.nn.sigmoid(c)).astype(BF16)
    o_ref[0] = _dot(a, w_ref[0].astype(BF16)) + b_ref[0]


def _modulation(c_rows, w_mod, b_mod):
    depth, d, n = w_mod.shape
    tn = 1536
    return pl.pallas_call(
        _mod_kernel,
        grid=(depth, n // tn),
        in_specs=[pl.BlockSpec((8, d), lambda l, j: (0, 0)),
                  pl.BlockSpec((1, d, tn), lambda l, j: (l, 0, j)),
                  pl.BlockSpec((1, 1, tn), lambda l, j: (l, 0, j))],
        out_specs=pl.BlockSpec((1, 8, tn), lambda l, j: (l, 0, j)),
        out_shape=jax.ShapeDtypeStruct((depth, 8, n), F32),
        compiler_params=_params(("parallel", "parallel"), "modulation"),
        name="modulation",
    )(c_rows, w_mod, b_mod.reshape(depth, 1, n))


def _rms_rows(x):
    return x * lax.rsqrt(jnp.mean(x * x, axis=-1, keepdims=True) + EPS)


def _head_rms(z, pool, gain):
    z2 = z * z
    hi = z2.astype(BF16)
    lo = (z2 - hi.astype(F32)).astype(BF16)
    ms = _dot(hi, pool) + _dot(lo, pool)
    return z * lax.rsqrt(ms + EPS) * gain


def _rope(z, cos, sin):
    lane = lax.broadcasted_iota(jnp.int32, z.shape, 1)
    first = (lane % 32) < 16
    partner = jnp.where(first, pltpu.roll(z, LANES - 16, 1), pltpu.roll(z, 16, 1))
    return z * cos + partner * sin


def _inproj_kernel(x_ref, mod_ref, g1_ref, w_ref, pool_ref, gq_ref, gk_ref, cos_ref, sin_ref, gb_ref,
                   aq_ref, ak_ref, av_ref, bq_ref, bk_ref, bv_ref, cq_ref, ck_ref, cv_ref, co_ref, cg_ref,
                   dq_ref, dk_ref, dv_ref, *, d_model, rope):
    d = d_model
    mod = mod_ref[0]
    xn = _rms_rows(x_ref[0]) * g1_ref[...]
    hb = (xn * (1.0 + mod[:, d:2 * d]) + mod[:, 0:d]).astype(BF16)
    projected = _dot(hb, w_ref[...])

    def proj(seg, width=GROUP_W):
        return projected[:, seg * GROUP_W:seg * GROUP_W + width]

    def rot(z):
        if not rope:
            return z
        cos, sin = cos_ref[...], sin_ref[...]
        return jnp.concatenate([_rope(z[:, i:i + LANES], cos, sin) for i in range(0, z.shape[1], LANES)], axis=1)

    scale = HEAD_DIM ** -0.5
    pool = pool_ref[...]
    aq_ref[0] = (rot(_head_rms(proj(0), pool, gq_ref[...])) * (scale * LOG2_E)).astype(BF16)
    akv = proj(1)
    ak_ref[0] = rot(_head_rms(akv[:, :KV_W], pool[:KV_W, :KV_W], gk_ref[...])).astype(BF16)
    ones = jnp.ones((akv.shape[0], LANES), F32)
    av_ref[0] = jnp.concatenate([akv[:, KV_W:], ones], axis=1).astype(BF16)
    bq_ref[0] = (rot(proj(2)) * (scale * LOG2_E)).astype(BF16)
    bkv = proj(3)
    bk_ref[0] = rot(bkv[:, :KV_W]).astype(BF16)
    bv_ref[0] = jnp.concatenate([bkv[:, KV_W:], ones], axis=1).astype(BF16)
    cq_ref[0] = (proj(4) * scale).astype(BF16)
    ck_ref[0] = proj(5).astype(BF16)
    cv_ref[0] = proj(6).astype(BF16)
    co_ref[0] = proj(7)
    dq_ref[0] = (proj(8) * scale).astype(BF16)
    dk_ref[0] = proj(9).astype(BF16)
    dv_ref[0] = proj(10).astype(BF16)
    gates = proj(11, LANES) + gb_ref[...]
    forget_lane = (lax.broadcasted_iota(jnp.int32, gates.shape, 1) % 8) >= 4
    cg_ref[0] = jnp.where(forget_lane, _log_sigmoid(gates), gates) * LOG2_E


def _in_proj(x, mod, mod_row, g1, w, pool, gq, gk, cos, sin, gbias, *, layer, rope, tm):
    b, t, d = x.shape
    widths = [(GROUP_W, BF16), (KV_W, BF16), (2 * LANES, BF16),
              (GROUP_W, BF16), (KV_W, BF16), (2 * LANES, BF16),
              (GROUP_W, BF16), (GROUP_W, BF16), (GROUP_W, BF16), (GROUP_W, F32), (LANES, F32),
              (GROUP_W, BF16), (GROUP_W, BF16), (GROUP_W, BF16)]
    row = lambda wd: pl.BlockSpec((1, wd), lambda bi, j: (0, 0))
    return pl.pallas_call(
        functools.partial(_inproj_kernel, d_model=d, rope=rope),
        grid=(b, t // tm),
        in_specs=[pl.BlockSpec((1, tm, d), lambda bi, j: (bi, j, 0)),
                  pl.BlockSpec((1, 1, 6 * d), lambda bi, j: (mod_row(bi), 0, 0)),
                  row(d),
                  _layer_spec(w.shape, layer),
                  _const_spec((GROUP_W, GROUP_W)),
                  row(GROUP_W), row(KV_W),
                  pl.BlockSpec((tm, LANES), lambda bi, j: (j, 0)),
                  pl.BlockSpec((tm, LANES), lambda bi, j: (j, 0)),
                  row(LANES)],
        out_specs=[pl.BlockSpec((1, tm, wd), lambda bi, j: (bi, j, 0)) for wd, _ in widths],
        out_shape=[jax.ShapeDtypeStruct((b, t, wd), dt) for wd, dt in widths],
        compiler_params=_params(("parallel", "parallel"), "in_proj"),
        name="in_proj",
    )(x, mod, g1, w, pool, gq, gk, cos, sin, gbias)


def _stack_gqa(q):
    qa, qb = q[:, :LANES], q[:, LANES:]
    left = lax.broadcasted_iota(jnp.int32, qa.shape, 1) < HEAD_DIM
    zero = jnp.zeros_like(qa)
    return jnp.concatenate([jnp.where(left, qa, zero), jnp.where(left, qb, zero),
                            jnp.where(left, zero, qa), jnp.where(left, zero, qb)], axis=0)


def _unstack_gqa(o, t):
    left = lax.broadcasted_iota(jnp.int32, (t, LANES), 1) < HEAD_DIM
    return jnp.concatenate([jnp.where(left, o[0:t], o[2 * t:3 * t]),
                            jnp.where(left, o[t:2 * t], o[3 * t:4 * t])], axis=1)


def _attn_a_kernel(safe_ref, q_ref, k_ref, v_ref, kc_ref, vc_ref, o_ref, qs_ref, m_ref, acc_ref, *, tq, tk, n_kb):
    qs_ref[...] = _stack_gqa(q_ref[0])
    heads = [slice(h * tq, (h + 1) * tq) for h in range(N_HEADS)]

    @pl.when(safe_ref[0] != 0)
    def _():
        for rows in heads:
            p = jnp.exp2(_dot_nt(qs_ref[rows], kc_ref[0]))
            acc_ref[rows] = _dot(p.astype(BF16), vc_ref[0])

        def body(kb, carry):
            start = pl.multiple_of(kb * tk, tk)
            for rows in heads:
                p = jnp.exp2(_dot_nt(qs_ref[rows], k_ref[0, pl.ds(start, tk), :]))
                acc_ref[rows] += _dot(p.astype(BF16), v_ref[0, pl.ds(start, tk), :])
            return carry

        lax.fori_loop(0, n_kb, body, 0)

    @pl.when(safe_ref[0] == 0)
    def _():
        for rows in heads:
            s = _dot_nt(qs_ref[rows], kc_ref[0])
            m0 = jnp.max(s, axis=-1, keepdims=True)
            m_ref[rows] = m0
            acc_ref[rows] = _dot(jnp.exp2(s - m0).astype(BF16), vc_ref[0])

        def body(kb, carry):
            start = pl.multiple_of(kb * tk, tk)
            for rows in heads:
                s = _dot_nt(qs_ref[rows], k_ref[0, pl.ds(start, tk), :])
                m_prev = m_ref[rows]
                m_new = jnp.maximum(m_prev, jnp.max(s, axis=-1, keepdims=True))
                p = jnp.exp2(s - m_new)
                acc_ref[rows] = jnp.exp2(m_prev - m_new) * acc_ref[rows] + _dot(
                    p.astype(BF16), v_ref[0, pl.ds(start, tk), :])
                m_ref[rows] = m_new
            return carry

        lax.fori_loop(0, n_kb, body, 0)

    acc = acc_ref[...]
    o_ref[0] = _unstack_gqa(acc[:, :LANES] / acc[:, LANES:], tq).astype(o_ref.dtype)


def _attn_global(safe, q, k, v, kc, vc, *, tq, tk):
    b, t, _ = q.shape
    n_ctx = kc.shape[1]
    per_b = lambda n, wd: pl.BlockSpec((1, n, wd), lambda bi, i: (bi, 0, 0))
    return pl.pallas_call(
        functools.partial(_attn_a_kernel, tq=tq, tk=tk, n_kb=t // tk),
        grid=(b, t // tq),
        in_specs=[pl.BlockSpec(memory_space=pltpu.SMEM),
                  pl.BlockSpec((1, tq, GROUP_W), lambda bi, i: (bi, i, 0)),
                  per_b(t, KV_W), per_b(t, 2 * LANES), per_b(n_ctx, KV_W), per_b(n_ctx, 2 * LANES)],
        out_specs=pl.BlockSpec((1, tq, GROUP_W), lambda bi, i: (bi, i, 0)),
        out_shape=jax.ShapeDtypeStruct((b, t, GROUP_W), BF16),
        scratch_shapes=[pltpu.VMEM((4 * tq, LANES), BF16), pltpu.VMEM((4 * tq, 1), F32),
                        pltpu.VMEM((4 * tq, 2 * LANES), F32)],
        compiler_params=_params(("parallel", "parallel"), "attn_global"),
        name="attn_global",
    )(safe, q, k, v, kc, vc)


def _sink_column(sink_ref, t):
    row = lax.broadcasted_iota(jnp.int32, (4 * t, 1), 0)
    col = jnp.full((4 * t, 1), sink_ref[3], F32)
    for h in (2, 1, 0):
        col = jnp.where(row < (h + 1) * t, sink_ref[h], col)
    return col


def _attn_b_kernel(sink_ref, q_ref, k_ref, v_ref, kc_ref, vc_ref, o_ref, *, tq, t_total):
    span = tq + 2 * WINDOW
    q0 = pl.program_id(1) * tq
    ks = pl.multiple_of(jnp.clip(q0 - WINDOW, 0, t_total - span), WINDOW)
    qs = _stack_gqa(q_ref[0])
    kw = k_ref[0, pl.ds(ks, span), :]
    vw = v_ref[0, pl.ds(ks, span), :]
    q_pos = q0 + lax.broadcasted_iota(jnp.int32, (tq, span), 0)
    k_pos = ks + lax.broadcasted_iota(jnp.int32, (tq, span), 1)
    in_window = jnp.abs(k_pos - q_pos) <= WINDOW
    outs = []
    for h in range(N_HEADS):
        qh = qs[h * tq:(h + 1) * tq]
        s = jnp.where(in_window, _dot_nt(qh, kw), NEG_INF)
        sc = _dot_nt(qh, kc_ref[0])
        sink = sink_ref[h] * LOG2_E
        m = jnp.maximum(jnp.maximum(jnp.max(s, axis=-1, keepdims=True), jnp.max(sc, axis=-1, keepdims=True)), sink)
        acc = _dot(jnp.exp2(s - m).astype(BF16), vw) + _dot(jnp.exp2(sc - m).astype(BF16), vc_ref[0])
        outs.append(acc[:, :LANES] / (acc[:, LANES:] + jnp.exp2(sink - m)))
    o_ref[0] = _unstack_gqa(jnp.concatenate(outs, axis=0), tq).astype(o_ref.dtype)


def _attn_window(sink, q, k, v, kc, vc, *, tq):
    b, t, _ = q.shape
    n_ctx = kc.shape[1]
    assert t >= tq + 2 * WINDOW
    per_b = lambda n, wd: pl.BlockSpec((1, n, wd), lambda bi, i: (bi, 0, 0))
    return pl.pallas_call(
        functools.partial(_attn_b_kernel, tq=tq, t_total=t),
        grid=(b, t // tq),
        in_specs=[pl.BlockSpec(memory_space=pltpu.SMEM),
                  pl.BlockSpec((1, tq, GROUP_W), lambda bi, i: (bi, i, 0)),
                  per_b(t, KV_W), per_b(t, 2 * LANES), per_b(n_ctx, KV_W), per_b(n_ctx, 2 * LANES)],
        out_specs=pl.BlockSpec((1, tq, GROUP_W), lambda bi, i: (bi, i, 0)),
        out_shape=jax.ShapeDtypeStruct((b, t, GROUP_W), BF16),
        compiler_params=_params(("parallel", "parallel"), "attn_window"),
        name="attn_window",
    )(sink, q, k, v, kc, vc)


def _head_masks(shape):
    lane = lax.broadcasted_iota(jnp.int32, shape, 1)
    return [(lane >= h * HEAD_DIM) & (lane < (h + 1) * HEAD_DIM) for h in range(N_HEADS)]


def _stack_mha(q):
    zero = jnp.zeros_like(q)
    return jnp.concatenate([jnp.where(mk, q, zero) for mk in _head_masks(q.shape)], axis=0)


def _unstack_mha(o, t):
    masks = _head_masks((t, GROUP_W))
    out = jnp.where(masks[0], o[0:t], 0.0)
    for h in range(1, N_HEADS):
        out = jnp.where(masks[h], o[h * t:(h + 1) * t], out)
    return out


def _attn_d_kernel(q_ref, k_ref, v_ref, kc_ref, vc_ref, bias_ref, o_ref, *, rows_per_step, rows):
    kc = kc_ref[0]
    vc = vc_ref[0]
    span = NA_KH * GRID_W

    def body(rr, carry):
        r = pl.program_id(1) * rows_per_step + rr
        rs = jnp.clip(r - NA_KH // 2, 0, rows - NA_KH)
        k0 = pl.multiple_of(rs * GRID_W, GRID_W)
        q0 = pl.multiple_of(rr * GRID_W, GRID_W)
        qs = _stack_mha(q_ref[0, pl.ds(q0, GRID_W), :])
        s = _dot_nt(qs, k_ref[0, pl.ds(k0, span), :]) + bias_ref[r - rs]
        sc = _dot_nt(qs, kc)
        m = jnp.maximum(jnp.max(s, axis=-1, keepdims=True), jnp.max(sc, axis=-1, keepdims=True))
        p = jnp.exp(s - m)
        pc = jnp.exp(sc - m)
        l = jnp.sum(p, axis=-1, keepdims=True) + jnp.sum(pc, axis=-1, keepdims=True)
        o = _dot(p.astype(BF16), v_ref[0, pl.ds(k0, span), :]) + _dot(pc.astype(BF16), vc)
        o_ref[0, pl.ds(q0, GRID_W), :] = _unstack_mha(o / l, GRID_W).astype(o_ref.dtype)
        return carry

    lax.fori_loop(0, rows_per_step, body, 0, unroll=True)


def _attn_neighbour(q, k, v, kc, vc, bias, *, rows_per_step):
    b, t, _ = q.shape
    n_ctx = kc.shape[1]
    rows = t // GRID_W
    assert rows >= NA_KH and rows % rows_per_step == 0
    tq = rows_per_step * GRID_W
    per_b = lambda n: pl.BlockSpec((1, n, GROUP_W), lambda bi, i: (bi, 0, 0))
    return pl.pallas_call(
        functools.partial(_attn_d_kernel, rows_per_step=rows_per_step, rows=rows),
        grid=(b, t // tq),
        in_specs=[pl.BlockSpec((1, tq, GROUP_W), lambda bi, i: (bi, i, 0)),
                  per_b(t), per_b(t), per_b(n_ctx), per_b(n_ctx),
                  _const_spec(bias.shape)],
        out_specs=pl.BlockSpec((1, tq, GROUP_W), lambda bi, i: (bi, i, 0)),
        out_shape=jax.ShapeDtypeStruct((b, t, GROUP_W), BF16),
        compiler_params=_params(("parallel", "parallel"), "attn_neighbour"),
        name="attn_neighbour",
    )(q, k, v, kc, vc, bias)


def _neighbour_bias(rpb):
    w = jnp.arange(GRID_W)
    cs = jnp.clip(w - NA_KW // 2, 0, GRID_W - NA_KW)
    col = jnp.arange(GRID_W)
    valid = (col[None, :] >= cs[:, None]) & (col[None, :] < cs[:, None] + NA_KW)
    dc = jnp.clip(col[None, :] - w[:, None] + (NA_KW - 1), 0, 2 * NA_KW - 2)
    pick_c = (jnp.arange(2 * NA_KW - 1)[:, None, None] == dc[None]).astype(F32)
    by_col = jnp.einsum('hrc,cwk->hwrk', rpb.astype(F32), pick_c, precision=lax.Precision.HIGHEST)
    by_col = jnp.where(valid[None, :, None, :], by_col, NEG_INF)
    tab = jnp.stack([by_col[:, :, NA_KH - 1 - o:2 * NA_KH - 1 - o, :] for o in range(NA_KH)])
    return tab.reshape(NA_KH, N_HEADS * GRID_W, NA_KH * GRID_W)


def _ctx_attn_kernel(sink_ref, aq_ref, ak_ref, av_ref, bq_ref, bk_ref, bv_ref, dq_ref, dk_ref, dv_ref,
                     ya_ref, yb_ref, yd_ref, *, n):
    s = _dot_nt(_stack_gqa(aq_ref[0]), ak_ref[0])
    acc = _dot(jnp.exp2(s - jnp.max(s, axis=-1, keepdims=True)).astype(BF16), av_ref[0])
    ya_ref[0] = _unstack_gqa(acc[:, :LANES] / acc[:, LANES:], n).astype(ya_ref.dtype)
    s = _dot_nt(_stack_gqa(bq_ref[0]), bk_ref[0])
    sink = _sink_column(sink_ref, n) * LOG2_E
    m = jnp.maximum(jnp.max(s, axis=-1, keepdims=True), sink)
    acc = _dot(jnp.exp2(s - m).astype(BF16), bv_ref[0])
    yb_ref[0] = _unstack_gqa(acc[:, :LANES] / (acc[:, LANES:] + jnp.exp2(sink - m)), n).astype(yb_ref.dtype)
    s = _dot_nt(_stack_mha(dq_ref[0]), dk_ref[0])
    p = jnp.exp(s - jnp.max(s, axis=-1, keepdims=True))
    o = _dot(p.astype(BF16), dv_ref[0]) / jnp.sum(p, axis=-1, keepdims=True)
    yd_ref[0] = _unstack_mha(o, n).astype(yd_ref.dtype)


def _ctx_attention(sink, aq, ak, av, bq, bk, bv, dq, dk, dv):
    b, n, _ = aq.shape
    spec = lambda wd: pl.BlockSpec((1, n, wd), lambda bi: (bi, 0, 0))
    return pl.pallas_call(
        functools.partial(_ctx_attn_kernel, n=n),
        grid=(b,),
        in_specs=[pl.BlockSpec(memory_space=pltpu.SMEM),
                  spec(GROUP_W), spec(KV_W), spec(2 * LANES), spec(GROUP_W), spec(KV_W), spec(2 * LANES),
                  spec(GROUP_W), spec(GROUP_W), spec(GROUP_W)],
        out_specs=[spec(GROUP_W)] * 3,
        out_shape=[jax.ShapeDtypeStruct((b, n, GROUP_W), BF16)] * 3,
        compiler_params=_params(("parallel",), "ctx_attention"),
        name="ctx_attention",
    )(sink, aq, ak, av, bq, bk, bv, dq, dk, dv)


def _log_sigmoid(x):
    return jnp.minimum(x, 0.0) - jnp.log1p(jnp.exp(-jnp.abs(x)))


def _split3(x):
    x1 = x.astype(BF16)
    r1 = x - x1.astype(F32)
    x2 = r1.astype(BF16)
    return x1, x2, (r1 - x2.astype(F32)).astype(BF16)


def _exact_dot_01(sel, x):
    return sum(_dot(sel, piece) for piece in _split3(x))


def _exact_dot_01_rhs(x, sel):
    return sum(_dot(piece, sel) for piece in _split3(x))


def _mlstm_chunk(q, k, v, gates, expand, tri, neg_mask, ones_cat, block, head_rows, s_ref, n_ref, m_ref, *,
                 reverse, L):
    ge = _exact_dot_01_rhs(gates, expand)
    li = ge[:, :GROUP_W]
    cum = _exact_dot_01(tri, ge[:, GROUP_W:])
    a_t = (li - cum).T
    m_prev = m_ref[0:1, :]
    head_b = [head_rows[h].astype(BF16) for h in range(N_HEADS)]
    mu = jnp.zeros((L, GROUP_W), F32)
    scores = []
    for h in range(N_HEADS):
        lane0 = h * HEAD_DIM
        a_m = a_t[lane0:lane0 + 1, :] + neg_mask
        mu_h = jnp.maximum(jnp.max(a_m, axis=-1, keepdims=True), m_prev[:, lane0:lane0 + 1])
        qk = _dot_nt(q * head_b[h], k)
        scores.append((qk * jnp.exp2(a_m - mu_h)).astype(BF16))
        mu = mu + mu_h * head_rows[h]
    s_cat = jnp.concatenate(scores, axis=1)
    v_cat = jnp.concatenate([v * hb for hb in head_b], axis=0)
    w_inter = jnp.exp2(m_prev - mu)
    num = _dot(s_cat, v_cat) + w_inter * _dot(q, s_ref[...].astype(BF16))
    den = _dot(s_cat, ones_cat) + w_inter * _dot(q, n_ref[...].astype(BF16))
    h_out = num / jnp.maximum(jnp.abs(den), jnp.exp2(-(cum + mu)))

    end_row = 0 if reverse else L - 1
    cum_end = cum[end_row:end_row + 1, :]
    log_end = cum_end - cum + li
    m_new = jnp.maximum(cum_end + m_prev, jnp.max(log_end, axis=0, keepdims=True))
    decay = jnp.exp2(cum_end + m_prev - m_new)
    w_end = jnp.exp2(log_end - m_new)
    k_t = k.astype(F32).T.astype(BF16)
    upd = _dot(k_t, jnp.concatenate([(v.astype(F32) * w_end).astype(BF16), w_end.astype(BF16)], axis=1))
    s_ref[...] = (s_ref[...] * decay + upd[:, :GROUP_W]) * block
    n_ref[...] = (n_ref[...] * decay + upd[:, GROUP_W:]) * block
    m_ref[...] = jnp.broadcast_to(m_new, m_ref.shape)
    return h_out


def _mlstm_kernel(qf_ref, kf_ref, vf_ref, gf_ref, qb_ref, kb_ref, vb_ref, gb_ref,
                  expand_ref, tri_ref, neg_ref, ones_ref, block_ref, heads_ref, s0_ref, n0_ref, m0_ref,
                  hf_ref, hb_ref, s_out, n_out, m_out, s_ref, n_ref, m_ref, *, chunk):
    @pl.when(pl.program_id(1) == 0)
    def _():
        s_ref[...] = s0_ref[0]
        n_ref[...] = n0_ref[0]
        m_ref[...] = m0_ref[0]

    sides = ((qf_ref, kf_ref, vf_ref, gf_ref, hf_ref), (qb_ref, kb_ref, vb_ref, gb_ref, hb_ref))
    for d, (q_ref, k_ref, v_ref, g_ref, h_ref) in enumerate(sides):
        h = _mlstm_chunk(q_ref[0], k_ref[0], v_ref[0], g_ref[0], expand_ref[d], tri_ref[d], neg_ref[d],
                         ones_ref[...], block_ref[...], heads_ref, s_ref.at[d], n_ref.at[d], m_ref.at[d],
                         reverse=bool(d), L=chunk)
        h_ref[0] = h.astype(h_ref.dtype)
    s_out[0] = s_ref[...]
    n_out[0] = n_ref[...]
    m_out[0] = m_ref[...]


def _mlstm_consts(L):
    lane = jnp.arange(LANES)[:, None]
    col = jnp.arange(2 * GROUP_W)[None, :]
    col_head = (col % GROUP_W) // HEAD_DIM
    expand = jnp.stack([(lane == 8 * d + 4 * (col // GROUP_W) + col_head) for d in range(2)]).astype(BF16)
    t_idx = jnp.arange(L)[:, None]
    s_idx = jnp.arange(L)[None, :]
    seen = jnp.stack([s_idx <= t_idx, s_idx >= t_idx])
    head = jnp.arange(GROUP_W) // HEAD_DIM
    ones_cat = (jnp.repeat(jnp.arange(N_HEADS), L)[:, None] == head[None, :]).astype(BF16)
    block = (head[:, None] == head[None, :]).astype(F32)
    head_rows = (jnp.arange(N_HEADS)[:, None, None] == head[None, None, :]).astype(F32)
    return expand, seen.astype(BF16), jnp.where(seen, 0.0, NEG_INF).astype(F32), ones_cat, block, head_rows


def _mlstm_scan(q, k, v, gates, state, consts):
    b, t, _ = q.shape
    L = MLSTM_L
    nc = t // L
    fwd = lambda wd: pl.BlockSpec((1, L, wd), lambda bi, j: (bi, j, 0))
    bwd = lambda wd: pl.BlockSpec((1, L, wd), lambda bi, j: (bi, nc - 1 - j, 0))
    st = lambda r: pl.BlockSpec((1, 2, r, GROUP_W), lambda bi, j: (bi, 0, 0, 0))
    tok_specs = [fwd(GROUP_W), fwd(GROUP_W), fwd(GROUP_W), fwd(LANES),
                 bwd(GROUP_W), bwd(GROUP_W), bwd(GROUP_W), bwd(LANES)]
    hf, hb, s_fin, n_fin, m_fin = pl.pallas_call(
        functools.partial(_mlstm_kernel, chunk=L),
        grid=(b, nc),
        in_specs=tok_specs + [_const_spec(c.shape) for c in consts] + [st(GROUP_W), st(GROUP_W), st(8)],
        out_specs=[fwd(GROUP_W), bwd(GROUP_W), st(GROUP_W), st(GROUP_W), st(8)],
        out_shape=[jax.ShapeDtypeStruct((b, t, GROUP_W), BF16), jax.ShapeDtypeStruct((b, t, GROUP_W), BF16),
                   jax.ShapeDtypeStruct((b, 2, GROUP_W, GROUP_W), F32),
                   jax.ShapeDtypeStruct((b, 2, GROUP_W, GROUP_W), F32),
                   jax.ShapeDtypeStruct((b, 2, 8, GROUP_W), F32)],
        scratch_shapes=[pltpu.VMEM((2, GROUP_W, GROUP_W), F32), pltpu.VMEM((2, GROUP_W, GROUP_W), F32),
                        pltpu.VMEM((2, 8, GROUP_W), F32)],
        compiler_params=_params(("parallel", "arbitrary"), "mlstm"),
        name="mlstm",
    )(q, k, v, gates, q, k, v, gates, *consts, *state)
    return hf, hb, (s_fin, n_fin, m_fin)


def _outproj_kernel(x_ref, ya_ref, yb_ref, hf_ref, hb_ref, og_ref, yd_ref, mod_ref, gg_ref, w_ref, o_ref, *, d_model):
    d = d_model
    yc = jax.nn.sigmoid(og_ref[0]) * (hf_ref[0].astype(F32) + hb_ref[0].astype(F32))
    parts = []
    for i, y in enumerate((ya_ref[0], yb_ref[0], yc, yd_ref[0])):
        parts.append((_rms_rows(y.astype(F32)) * gg_ref[i:i + 1, :]).astype(BF16))
    res = _dot(jnp.concatenate(parts, axis=1), w_ref[...])
    o_ref[0] = x_ref[0] + mod_ref[0][:, 2 * d:3 * d] * res


def _out_proj(x, ya, yb, hf, hb, o_gate, yd, mod, mod_row, gg, w, *, layer, tm):
    b, t, d = x.shape
    tokd = pl.BlockSpec((1, tm, d), lambda bi, j: (bi, j, 0))
    tokg = pl.BlockSpec((1, tm, GROUP_W), lambda bi, j: (bi, j, 0))
    return pl.pallas_call(
        functools.partial(_outproj_kernel, d_model=d),
        grid=(b, t // tm),
        in_specs=[tokd, tokg, tokg, tokg, tokg, tokg, tokg,
                  pl.BlockSpec((1, 1, 6 * d), lambda bi, j: (mod_row(bi), 0, 0)),
                  _const_spec(gg.shape), _layer_spec(w.shape, layer)],
        out_specs=tokd,
        out_shape=jax.ShapeDtypeStruct((b, t, d), F32),
        compiler_params=_params(("parallel", "parallel"), "out_proj"),
        name="out_proj",
    )(x, ya, yb, hf, hb, o_gate, yd, mod, gg, w)


def _ffn_kernel(*refs, d_model, d_ff, tm, final_norm):
    if final_norm:
        x_ref, xp_ref, xn_ref, mod_ref, g2_ref, wu_ref, cw_ref, wd_ref, gf_ref, o_ref = refs
    else:
        x_ref, xp_ref, xn_ref, mod_ref, g2_ref, wu_ref, cw_ref, wd_ref, o_ref = refs
    d = d_model
    j = pl.program_id(1)
    mod = mod_ref[0]
    g2 = g2_ref[...]

    def norm_mod(x):
        return _rms_rows(x) * g2 * (1.0 + mod[:, 4 * d:5 * d]) + mod[:, 3 * d:4 * d]

    keep_prev = (j > 0).astype(F32)
    keep_next = (j < pl.num_programs(1) - 1).astype(F32)
    x = x_ref[0]
    h_ext = jnp.concatenate([norm_mod(xp_ref[0]) * keep_prev, norm_mod(x), norm_mod(xn_ref[0]) * keep_next],
                            axis=0).astype(BF16)
    n_ext = tm + 2 * HALO

    u = _dot(h_ext, wu_ref[...])
    cw = cw_ref[...]
    u = (pltpu.roll(u, 1, 0) * cw[0:1] + u * cw[1:2] + pltpu.roll(u, n_ext - 1, 0) * cw[2:3] + cw[3:4])[HALO:HALO + tm]
    gate, val = u[:, :d_ff], u[:, d_ff:]
    act = (gate * jax.nn.sigmoid(gate) * val).astype(BF16)
    y = x + mod[:, 5 * d:6 * d] * _dot(act, wd_ref[...])
    if final_norm:
        y = _rms_rows(y) * gf_ref[...]
    o_ref[0] = y


def _conv_ffn(x, mod, mod_row, g2, wu, cw, wd, g_final, *, layer, tm):
    b, t, d = x.shape
    d_ff = wd.shape[1]
    final_norm = g_final is not None
    hb = tm // HALO
    last = t // HALO - 1
    row = pl.BlockSpec((1, d), lambda bi, j: (0, 0))
    in_specs = [pl.BlockSpec((1, tm, d), lambda bi, j: (bi, j, 0)),
                pl.BlockSpec((1, HALO, d), lambda bi, j: (bi, jnp.maximum(j * hb - 1, 0), 0)),
                pl.BlockSpec((1, HALO, d), lambda bi, j: (bi, jnp.minimum((j + 1) * hb, last), 0)),
                pl.BlockSpec((1, 1, 6 * d), lambda bi, j: (mod_row(bi), 0, 0)),
                row,
                _layer_spec(wu.shape, layer), _const_spec(cw.shape), _layer_spec(wd.shape, layer)]
    args = [x, x, x, mod, g2, wu, cw, wd]
    if final_norm:
        in_specs.append(row)
        args.append(g_final)
    return pl.pallas_call(
        functools.partial(_ffn_kernel, d_model=d, d_ff=d_ff, tm=tm, final_norm=final_norm),
        grid=(b, t // tm),
        in_specs=in_specs,
        out_specs=pl.BlockSpec((1, tm, d), lambda bi, j: (bi, j, 0)),
        out_shape=jax.ShapeDtypeStruct((b, t, d), F32),
        compiler_params=_params(("parallel", "parallel"), "conv_ffn"),
        name="conv_ffn",
    )(*args)


_Q_HEAD_ORDER = (0, 2, 1, 3)


def _permute_heads(w, axis):
    parts = [lax.slice_in_dim(w, h * HEAD_DIM, (h + 1) * HEAD_DIM, axis=axis) for h in _Q_HEAD_ORDER]
    return jnp.concatenate(parts, axis=axis)


def _layout_w_in(w):
    w = w.astype(BF16)
    g, kv = GROUP_W, KV_W
    o_b = g + 2 * kv
    o_c = 2 * o_b
    o_g = o_c + 4 * g
    o_d = o_g + 4 * N_HEADS
    sl = lambda a, n: w[:, :, a:a + n]
    pad = jnp.zeros(w.shape[:2] + (LANES - 4 * N_HEADS,), BF16)
    return jnp.concatenate([_permute_heads(sl(0, g), 2), sl(g, 2 * kv),
                            _permute_heads(sl(o_b, g), 2), sl(o_b + g, 2 * kv),
                            sl(o_c, 4 * g), sl(o_d, 3 * g), sl(o_g, 4 * N_HEADS), pad], axis=2)


def _layout_w_out(w):
    w = w.astype(BF16)
    return jnp.concatenate([_permute_heads(w[:, 0:GROUP_W], 1), _permute_heads(w[:, GROUP_W:2 * GROUP_W], 1),
                            w[:, 2 * GROUP_W:]], axis=1)


def _rope_tables(t):
    pos = jnp.arange(t)
    row = (pos // GRID_W).astype(F32)
    col = (pos % GRID_W).astype(F32)
    n_freq = HEAD_DIM // 4
    freqs = ROPE_THETA ** (-jnp.arange(n_freq, dtype=F32) / n_freq)
    ang_r, ang_c = row[:, None] * freqs, col[:, None] * freqs
    cos = jnp.concatenate([jnp.cos(ang_r), jnp.cos(ang_r), jnp.cos(ang_c), jnp.cos(ang_c)], axis=1)
    sin = jnp.concatenate([-jnp.sin(ang_r), jnp.sin(ang_r), -jnp.sin(ang_c), jnp.sin(ang_c)], axis=1)
    reps = LANES // HEAD_DIM
    return jnp.tile(cos, (1, reps)), jnp.tile(sin, (1, reps))


def _pick_tile(t, pref):
    while t % pref:
        pref //= 2
    return pref


def kernel(x, c, ctx, c_ctx, w_mod, b_mod, g_norm1, g_norm2, w_in, a_q_gain, a_k_gain, b_sink, c_gate_bias,
           d_rel_bias, g_group, w_out, w_up, conv_w, conv_b, w_down, g_final):
    batch, t, d = x.shape
    n_ctx = ctx.shape[1]
    depth = w_in.shape[0]
    d_ff = w_down.shape[1]
    assert batch < 8 and d_ff % LANES == 0 and t % MLSTM_L == 0 and n_ctx % MLSTM_L == 0

    c_rows = jnp.zeros((8, d), F32).at[:batch].set(c).at[batch].set(c_ctx)
    mod_all = _modulation(c_rows, w_mod, b_mod).reshape(depth, 8, 1, 6 * d)
    lat_row = lambda bi: bi
    ctx_row = lambda bi: batch

    cos, sin = _rope_tables(t)
    pool = jnp.where(jnp.arange(GROUP_W)[:, None] // HEAD_DIM == jnp.arange(GROUP_W)[None, :] // HEAD_DIM,
                     1.0 / HEAD_DIM, 0.0).astype(BF16)
    tm_lat = _pick_tile(t, TOKEN_TILE)
    tm_ctx = _pick_tile(n_ctx, CTX_TILE)
    zero_state = (jnp.zeros((batch, 2, GROUP_W, GROUP_W), F32), jnp.zeros((batch, 2, GROUP_W, GROUP_W), F32),
                  jnp.zeros((batch, 2, 8, GROUP_W), F32))
    mlstm_consts = _mlstm_consts(MLSTM_L)
    w_in_b = _layout_w_in(w_in)
    w_out_b = _layout_w_out(w_out)
    w_up_b = w_up.astype(BF16)
    w_down_b = w_down.astype(BF16)

    for layer in range(depth):
        need_ctx = layer < depth - 1
        mod = mod_all[layer]
        g1 = g_norm1[layer][None]
        g2 = g_norm2[layer][None]
        gq = jnp.tile(a_q_gain[layer], N_HEADS)[None]
        gk = jnp.tile(a_k_gain[layer], 2)[None]
        gbias = jnp.pad(c_gate_bias[layer], (0, LANES - 4 * N_HEADS))[None]
        gg = g_group[layer].reshape(N_HEADS, GROUP_W)
        gg = jnp.concatenate([_permute_heads(gg[0:2], 1), gg[2:4]], axis=0)
        conv_l = jnp.concatenate([conv_w[layer], conv_b[layer][None],
                                  jnp.zeros((8 - 1 - conv_w.shape[1], 2 * d_ff), F32)], axis=0)
        sink = b_sink[layer]
        bias_tab = _neighbour_bias(d_rel_bias[layer])

        proj = functools.partial(_in_proj, g1=g1, w=w_in_b, pool=pool, gq=gq, gk=gk, gbias=gbias, layer=layer)
        (aqc, akc, avc, bqc, bkc, bvc, cqc, ckc, cvc, coc, cgc, dqc, dkc, dvc) = proj(
            ctx, mod, ctx_row, cos=cos, sin=sin, rope=False, tm=tm_ctx)
        (aq, ak, av, bq, bk, bv, cq, ck, cv, co, cg, dq, dk, dv) = proj(
            x, mod, lat_row, cos=cos, sin=sin, rope=True, tm=_pick_tile(t, 2 * TOKEN_TILE))

        score_bound = 1.02 * LOG2_E * HEAD_DIM ** 0.5 * jnp.max(jnp.abs(a_q_gain[layer])) * jnp.max(jnp.abs(a_k_gain[layer]))
        safe = (score_bound <= MAX_UNSHIFTED_LOG2_SCORE).astype(jnp.int32).reshape(1)
        ya = _attn_global(safe, aq, ak, av, akc, avc, tq=_pick_tile(t, ATTN_Q_TILE), tk=_pick_tile(t, ATTN_K_TILE))
        yb = _attn_window(sink, bq, bk, bv, bkc, bvc, tq=_pick_tile(t, WINDOW_Q_TILE))
        yd = _attn_neighbour(dq, dk, dv, dkc, dvc, bias_tab, rows_per_step=NEIGHBOUR_ROWS)

        hcf, hcb, ctx_state = _mlstm_scan(cqc, ckc, cvc, cgc, zero_state, mlstm_consts)
        hf, hb, _ = _mlstm_scan(cq, ck, cv, cg, ctx_state, mlstm_consts)

        x = _out_proj(x, ya, yb, hf, hb, co, yd, mod, lat_row, gg, w_out_b, layer=layer,
                      tm=_pick_tile(t, 2 * TOKEN_TILE))
        ffn = functools.partial(_conv_ffn, g2=g2, wu=w_up_b, cw=conv_l, wd=w_down_b, layer=layer)
        x = ffn(x, mod, lat_row, g_final=None if need_ctx else g_final[None], tm=tm_lat)
        if need_ctx:
            yac, ybc, ydc = _ctx_attention(sink, aqc, akc, avc, bqc, bkc, bvc, dqc, dkc, dvc)
            ctx = _out_proj(ctx, yac, ybc, hcf, hcb, coc, ydc, mod, ctx_row, gg, w_out_b, layer=layer, tm=tm_ctx)
            ctx = ffn(ctx, mod, ctx_row, g_final=None, tm=tm_ctx)
    return x
```

```python
import functools

import jax
import jax.numpy as jnp
from jax import lax
from jax.experimental import pallas as pl
from jax.experimental.pallas import tpu as pltpu

F32 = jnp.float32
BF16 = jnp.bfloat16

N_HEADS = 4
HEAD_DIM = 64
GROUP_W = N_HEADS * HEAD_DIM
KV_W = 2 * HEAD_DIM
GRID_W = 64
WINDOW = 128
NA_KH = 8
NA_KW = 16
ROPE_THETA = 10000.0
EPS = 1e-6
NEG_INF = -1e30
LANES = 128
MLSTM_L = 256
HALO = 8
MIB = 1024 * 1024
LOG2_E = 1.4426950408889634
MAX_UNSHIFTED_LOG2_SCORE = 60.0


TOKEN_TILE = 512
CTX_TILE = 256
ATTN_Q_TILE = 1024
ATTN_K_TILE = 2048
WINDOW_Q_TILE = 256
NEIGHBOUR_ROWS = 8
VMEM_LIMIT_MIB = {"modulation": 40, "in_proj": 48, "attn_global": 48, "attn_window": 48, "attn_neighbour": 48,
                  "ctx_attention": 32, "mlstm": 48, "out_proj": 48, "conv_ffn": 56}


def _params(sem, name):
    return pltpu.CompilerParams(dimension_semantics=sem, vmem_limit_bytes=VMEM_LIMIT_MIB[name] * MIB)


def _dot(a, b):
    return jnp.dot(a, b, preferred_element_type=F32)


def _dot_nt(a, b):
    return lax.dot_general(a, b, (((1,), (1,)), ((), ())), preferred_element_type=F32)


def _const_spec(shape):
    return pl.BlockSpec(shape, lambda *_: (0,) * len(shape), pipeline_mode=pl.Buffered(1))


def _layer_spec(stacked_shape, layer):
    rest = tuple(stacked_shape[1:])
    return pl.BlockSpec((None,) + rest, lambda *_: (layer,) + (0,) * len(rest), pipeline_mode=pl.Buffered(1))


def _mod_kernel(c_ref, w_ref, b_ref, o_ref):
    c = c_ref[...]
    a = (c * jax.nn.sigmoid(c)).astype(BF16)
    o_ref[0] = _dot(a, w_ref[0].astype(BF16)) + b_ref[0]


def _modulation(c_rows, w_mod, b_mod):
    depth, d, n = w_mod.shape
    tn = 1536
    return pl.pallas_call(
        _mod_kernel,
        grid=(depth, n // tn),
        in_specs=[pl.BlockSpec((8, d), lambda l, j: (0, 0)),
                  pl.BlockSpec((1, d, tn), lambda l, j: (l, 0, j)),
                  pl.BlockSpec((1, 1, tn), lambda l, j: (l, 0, j))],
        out_specs=pl.BlockSpec((1, 8, tn), lambda l, j: (l, 0, j)),
        out_shape=jax.ShapeDtypeStruct((depth, 8, n), F32),
        compiler_params=_params(("parallel", "parallel"), "modulation"),
        name="modulation",
    )(c_rows, w_mod, b_mod.reshape(depth, 1, n))


def _rms_rows(x):
    return x * lax.rsqrt(jnp.mean(x * x, axis=-1, keepdims=True) + EPS)


def _head_rms(z, pool, gain):
    z2 = z * z
    hi = z2.astype(BF16)
    lo = (z2 - hi.astype(F32)).astype(BF16)
    ms = _dot(hi, pool) + _dot(lo, pool)
    return z * lax.rsqrt(ms + EPS) * gain


def _rope(z, cos, sin):
    lane = lax.broadcasted_iota(jnp.int32, z.shape, 1)
    first = (lane % 32) < 16
    partner = jnp.where(first, pltpu.roll(z, LANES - 16, 1), pltpu.roll(z, 16, 1))
    return z * cos + partner * sin


def _inproj_kernel(x_ref, mod_ref, g1_ref, w_ref, pool_ref, gq_ref, gk_ref, cos_ref, sin_ref, gb_ref,
                   aq_ref, ak_ref, av_ref, bq_ref, bk_ref, bv_ref, cq_ref, ck_ref, cv_ref, co_ref, cg_ref,
                   dq_ref, dk_ref, dv_ref, *, d_model, rope):
    d = d_model
    mod = mod_ref[0]
    xn = _rms_rows(x_ref[0]) * g1_ref[...]
    hb = (xn * (1.0 + mod[:, d:2 * d]) + mod[:, 0:d]).astype(BF16)
    projected = _dot(hb, w_ref[...])

    def proj(seg, width=GROUP_W):
        return projected[:, seg * GROUP_W:seg * GROUP_W + width]

    def rot(z):
        if not rope:
            return z
        cos, sin = cos_ref[...], sin_ref[...]
        return jnp.concatenate([_rope(z[:, i:i + LANES], cos, sin) for i in range(0, z.shape[1], LANES)], axis=1)

    scale = HEAD_DIM ** -0.5
    pool = pool_ref[...]
    aq_ref[0] = (rot(_head_rms(proj(0), pool, gq_ref[...])) * (scale * LOG2_E)).astype(BF16)
    akv = proj(1)
    ak_ref[0] = rot(_head_rms(akv[:, :KV_W], pool[:KV_W, :KV_W], gk_ref[...])).astype(BF16)
    ones = jnp.ones((akv.shape[0], LANES), F32)
    av_ref[0] = jnp.concatenate([akv[:, KV_W:], ones], axis=1).astype(BF16)
    bq_ref[0] = (rot(proj(2)) * (scale * LOG2_E)).astype(BF16)
    bkv = proj(3)
    bk_ref[0] = rot(bkv[:, :KV_W]).astype(BF16)
    bv_ref[0] = jnp.concatenate([bkv[:, KV_W:], ones], axis=1).astype(BF16)
    cq_ref[0] = (proj(4) * scale).astype(BF16)
    ck_ref[0] = proj(5).astype(BF16)
    cv_ref[0] = proj(6).astype(BF16)
    co_ref[0] = proj(7)
    dq_ref[0] = (proj(8) * scale).astype(BF16)
    dk_ref[0] = proj(9).astype(BF16)
    dv_ref[0] = proj(10).astype(BF16)
    gates = proj(11, LANES) + gb_ref[...]
    forget_lane = (lax.broadcasted_iota(jnp.int32, gates.shape, 1) % 8) >= 4
    cg_ref[0] = jnp.where(forget_lane, _log_sigmoid(gates), gates) * LOG2_E


def _in_proj(x, mod, mod_row, g1, w, pool, gq, gk, cos, sin, gbias, *, layer, rope, tm):
    b, t, d = x.shape
    widths = [(GROUP_W, BF16), (KV_W, BF16), (2 * LANES, BF16),
              (GROUP_W, BF16), (KV_W, BF16), (2 * LANES, BF16),
              (GROUP_W, BF16), (GROUP_W, BF16), (GROUP_W, BF16), (GROUP_W, F32), (LANES, F32),
              (GROUP_W, BF16), (GROUP_W, BF16), (GROUP_W, BF16)]
    row = lambda wd: pl.BlockSpec((1, wd), lambda bi, j: (0, 0))
    return pl.pallas_call(
        functools.partial(_inproj_kernel, d_model=d, rope=rope),
        grid=(b, t // tm),
        in_specs=[pl.BlockSpec((1, tm, d), lambda bi, j: (bi, j, 0)),
                  pl.BlockSpec((1, 1, 6 * d), lambda bi, j: (mod_row(bi), 0, 0)),
                  row(d),
                  _layer_spec(w.shape, layer),
                  _const_spec((GROUP_W, GROUP_W)),
                  row(GROUP_W), row(KV_W),
                  pl.BlockSpec((tm, LANES), lambda bi, j: (j, 0)),
                  pl.BlockSpec((tm, LANES), lambda bi, j: (j, 0)),
                  row(LANES)],
        out_specs=[pl.BlockSpec((1, tm, wd), lambda bi, j: (bi, j, 0)) for wd, _ in widths],
        out_shape=[jax.ShapeDtypeStruct((b, t, wd), dt) for wd, dt in widths],
        compiler_params=_params(("parallel", "parallel"), "in_proj"),
        name="in_proj",
    )(x, mod, g1, w, pool, gq, gk, cos, sin, gbias)


def _stack_gqa(q):
    qa, qb = q[:, :LANES], q[:, LANES:]
    left = lax.broadcasted_iota(jnp.int32, qa.shape, 1) < HEAD_DIM
    zero = jnp.zeros_like(qa)
    return jnp.concatenate([jnp.where(left, qa, zero), jnp.where(left, qb, zero),
                            jnp.where(left, zero, qa), jnp.where(left, zero, qb)], axis=0)


def _unstack_gqa(o, t):
    left = lax.broadcasted_iota(jnp.int32, (t, LANES), 1) < HEAD_DIM
    return jnp.concatenate([jnp.where(left, o[0:t], o[2 * t:3 * t]),
                            jnp.where(left, o[t:2 * t], o[3 * t:4 * t])], axis=1)


def _attn_a_kernel(safe_ref, q_ref, k_ref, v_ref, kc_ref, vc_ref, o_ref, qs_ref, m_ref, acc_ref, *, tq, tk, n_kb):
    qs_ref[...] = _stack_gqa(q_ref[0])
    heads = [slice(h * tq, (h + 1) * tq) for h in range(N_HEADS)]

    @pl.when(safe_ref[0] != 0)
    def _():
        for rows in heads:
            p = jnp.exp2(_dot_nt(qs_ref[rows], kc_ref[0]))
            acc_ref[rows] = _dot(p.astype(BF16), vc_ref[0])

        def body(kb, carry):
            start = pl.multiple_of(kb * tk, tk)
            for rows in heads:
                p = jnp.exp2(_dot_nt(qs_ref[rows], k_ref[0, pl.ds(start, tk), :]))
                acc_ref[rows] += _dot(p.astype(BF16), v_ref[0, pl.ds(start, tk), :])
            return carry

        lax.fori_loop(0, n_kb, body, 0)

    @pl.when(safe_ref[0] == 0)
    def _():
        for rows in heads:
            s = _dot_nt(qs_ref[rows], kc_ref[0])
            m0 = jnp.max(s, axis=-1, keepdims=True)
            m_ref[rows] = m0
            acc_ref[rows] = _dot(jnp.exp2(s - m0).astype(BF16), vc_ref[0])

        def body(kb, carry):
            start = pl.multiple_of(kb * tk, tk)
            for rows in heads:
                s = _dot_nt(qs_ref[rows], k_ref[0, pl.ds(start, tk), :])
                m_prev = m_ref[rows]
                m_new = jnp.maximum(m_prev, jnp.max(s, axis=-1, keepdims=True))
                p = jnp.exp2(s - m_new)
                acc_ref[rows] = jnp.exp2(m_prev - m_new) * acc_ref[rows] + _dot(
                    p.astype(BF16), v_ref[0, pl.ds(start, tk), :])
                m_ref[rows] = m_new
            return carry

        lax.fori_loop(0, n_kb, body, 0)

    acc = acc_ref[...]
    o_ref[0] = _unstack_gqa(acc[:, :LANES] / acc[:, LANES:], tq).astype(o_ref.dtype)


def _attn_global(safe, q, k, v, kc, vc, *, tq, tk):
    b, t, _ = q.shape
    n_ctx = kc.shape[1]
    per_b = lambda n, wd: pl.BlockSpec((1, n, wd), lambda bi, i: (bi, 0, 0))
    return pl.pallas_call(
        functools.partial(_attn_a_kernel, tq=tq, tk=tk, n_kb=t // tk),
        grid=(b, t // tq),
        in_specs=[pl.BlockSpec(memory_space=pltpu.SMEM),
                  pl.BlockSpec((1, tq, GROUP_W), lambda bi, i: (bi, i, 0)),
                  per_b(t, KV_W), per_b(t, 2 * LANES), per_b(n_ctx, KV_W), per_b(n_ctx, 2 * LANES)],
        out_specs=pl.BlockSpec((1, tq, GROUP_W), lambda bi, i: (bi, i, 0)),
        out_shape=jax.ShapeDtypeStruct((b, t, GROUP_W), BF16),
        scratch_shapes=[pltpu.VMEM((4 * tq, LANES), BF16), pltpu.VMEM((4 * tq, 1), F32),
                        pltpu.VMEM((4 * tq, 2 * LANES), F32)],
        compiler_params=_params(("parallel", "parallel"), "attn_global"),
        name="attn_global",
    )(safe, q, k, v, kc, vc)


def _sink_column(sink_ref, t):
    row = lax.broadcasted_iota(jnp.int32, (4 * t, 1), 0)
    col = jnp.full((4 * t, 1), sink_ref[3], F32)
    for h in (2, 1, 0):
        col = jnp.where(row < (h + 1) * t, sink_ref[h], col)
    return col


def _attn_b_kernel(sink_ref, q_ref, k_ref, v_ref, kc_ref, vc_ref, o_ref, *, tq, t_total):
    span = tq + 2 * WINDOW
    q0 = pl.program_id(1) * tq
    ks = pl.multiple_of(jnp.clip(q0 - WINDOW, 0, t_total - span), WINDOW)
    qs = _stack_gqa(q_ref[0])
    kw = k_ref[0, pl.ds(ks, span), :]
    vw = v_ref[0, pl.ds(ks, span), :]
    q_pos = q0 + lax.broadcasted_iota(jnp.int32, (tq, span), 0)
    k_pos = ks + lax.broadcasted_iota(jnp.int32, (tq, span), 1)
    in_window = jnp.abs(k_pos - q_pos) <= WINDOW
    outs = []
    for h in range(N_HEADS):
        qh = qs[h * tq:(h + 1) * tq]
        s = jnp.where(in_window, _dot_nt(qh, kw), NEG_INF)
        sc = _dot_nt(qh, kc_ref[0])
        sink = sink_ref[h] * LOG2_E
        m = jnp.maximum(jnp.maximum(jnp.max(s, axis=-1, keepdims=True), jnp.max(sc, axis=-1, keepdims=True)), sink)
        acc = _dot(jnp.exp2(s - m).astype(BF16), vw) + _dot(jnp.exp2(sc - m).astype(BF16), vc_ref[0])
        outs.append(acc[:, :LANES] / (acc[:, LANES:] + jnp.exp2(sink - m)))
    o_ref[0] = _unstack_gqa(jnp.concatenate(outs, axis=0), tq).astype(o_ref.dtype)


def _attn_window(sink, q, k, v, kc, vc, *, tq):
    b, t, _ = q.shape
    n_ctx = kc.shape[1]
    assert t >= tq + 2 * WINDOW
    per_b = lambda n, wd: pl.BlockSpec((1, n, wd), lambda bi, i: (bi, 0, 0))
    return pl.pallas_call(
        functools.partial(_attn_b_kernel, tq=tq, t_total=t),
        grid=(b, t // tq),
        in_specs=[pl.BlockSpec(memory_space=pltpu.SMEM),
                  pl.BlockSpec((1, tq, GROUP_W), lambda bi, i: (bi, i, 0)),
                  per_b(t, KV_W), per_b(t, 2 * LANES), per_b(n_ctx, KV_W), per_b(n_ctx, 2 * LANES)],
        out_specs=pl.BlockSpec((1, tq, GROUP_W), lambda bi, i: (bi, i, 0)),
        out_shape=jax.ShapeDtypeStruct((b, t, GROUP_W), BF16),
        compiler_params=_params(("parallel", "parallel"), "attn_window"),
        name="attn_window",
    )(sink, q, k, v, kc, vc)


def _head_masks(shape):
    lane = lax.broadcasted_iota(jnp.int32, shape, 1)
    return [(lane >= h * HEAD_DIM) & (lane < (h + 1) * HEAD_DIM) for h in range(N_HEADS)]


def _stack_mha(q):
    zero = jnp.zeros_like(q)
    return jnp.concatenate([jnp.where(mk, q, zero) for mk in _head_masks(q.shape)], axis=0)


def _unstack_mha(o, t):
    masks = _head_masks((t, GROUP_W))
    out = jnp.where(masks[0], o[0:t], 0.0)
    for h in range(1, N_HEADS):
        out = jnp.where(masks[h], o[h * t:(h + 1) * t], out)
    return out


def _attn_d_kernel(q_ref, k_ref, v_ref, kc_ref, vc_ref, bias_ref, o_ref, *, rows_per_step, rows):
    kc = kc_ref[0]
    vc = vc_ref[0]
    span = NA_KH * GRID_W

    def body(rr, carry):
        r = pl.program_id(1) * rows_per_step + rr
        rs = jnp.clip(r - NA_KH // 2, 0, rows - NA_KH)
        k0 = pl.multiple_of(rs * GRID_W, GRID_W)
        q0 = pl.multiple_of(rr * GRID_W, GRID_W)
        qs = _stack_mha(q_ref[0, pl.ds(q0, GRID_W), :])
        s = _dot_nt(qs, k_ref[0, pl.ds(k0, span), :]) + bias_ref[r - rs]
        sc = _dot_nt(qs, kc)
        m = jnp.maximum(jnp.max(s, axis=-1, keepdims=True), jnp.max(sc, axis=-1, keepdims=True))
        p = jnp.exp(s - m)
        pc = jnp.exp(sc - m)
        l = jnp.sum(p, axis=-1, keepdims=True) + jnp.sum(pc, axis=-1, keepdims=True)
        o = _dot(p.astype(BF16), v_ref[0, pl.ds(k0, span), :]) + _dot(pc.astype(BF16), vc)
        o_ref[0, pl.ds(q0, GRID_W), :] = _unstack_mha(o / l, GRID_W).astype(o_ref.dtype)
        return carry

    lax.fori_loop(0, rows_per_step, body, 0, unroll=True)


def _attn_neighbour(q, k, v, kc, vc, bias, *, rows_per_step):
    b, t, _ = q.shape
    n_ctx = kc.shape[1]
    rows = t // GRID_W
    assert rows >= NA_KH and rows % rows_per_step == 0
    tq = rows_per_step * GRID_W
    per_b = lambda n: pl.BlockSpec((1, n, GROUP_W), lambda bi, i: (bi, 0, 0))
    return pl.pallas_call(
        functools.partial(_attn_d_kernel, rows_per_step=rows_per_step, rows=rows),
        grid=(b, t // tq),
        in_specs=[pl.BlockSpec((1, tq, GROUP_W), lambda bi, i: (bi, i, 0)),
                  per_b(t), per_b(t), per_b(n_ctx), per_b(n_ctx),
                  _const_spec(bias.shape)],
        out_specs=pl.BlockSpec((1, tq, GROUP_W), lambda bi, i: (bi, i, 0)),
        out_shape=jax.ShapeDtypeStruct((b, t, GROUP_W), BF16),
        compiler_params=_params(("parallel", "parallel"), "attn_neighbour"),
        name="attn_neighbour",
    )(q, k, v, kc, vc, bias)


def _neighbour_bias(rpb):
    w = jnp.arange(GRID_W)
    cs = jnp.clip(w - NA_KW // 2, 0, GRID_W - NA_KW)
    col = jnp.arange(GRID_W)
    valid = (col[None, :] >= cs[:, None]) & (col[None, :] < cs[:, None] + NA_KW)
    dc = jnp.clip(col[None, :] - w[:, None] + (NA_KW - 1), 0, 2 * NA_KW - 2)
    pick_c = (jnp.arange(2 * NA_KW - 1)[:, None, None] == dc[None]).astype(F32)
    by_col = jnp.einsum('hrc,cwk->hwrk', rpb.astype(F32), pick_c, precision=lax.Precision.HIGHEST)
    by_col = jnp.where(valid[None, :, None, :], by_col, NEG_INF)
    flat = by_col.reshape(N_HEADS * GRID_W, (2 * NA_KH - 1) * GRID_W)
    return jnp.stack([flat[:, (NA_KH - 1 - o) * GRID_W:(2 * NA_KH - 1 - o) * GRID_W] for o in range(NA_KH)])


def _ctx_attn_kernel(sink_ref, aq_ref, ak_ref, av_ref, bq_ref, bk_ref, bv_ref, dq_ref, dk_ref, dv_ref,
                     ya_ref, yb_ref, yd_ref, *, n):
    s = _dot_nt(_stack_gqa(aq_ref[0]), ak_ref[0])
    acc = _dot(jnp.exp2(s - jnp.max(s, axis=-1, keepdims=True)).astype(BF16), av_ref[0])
    ya_ref[0] = _unstack_gqa(acc[:, :LANES] / acc[:, LANES:], n).astype(ya_ref.dtype)
    s = _dot_nt(_stack_gqa(bq_ref[0]), bk_ref[0])
    sink = _sink_column(sink_ref, n) * LOG2_E
    m = jnp.maximum(jnp.max(s, axis=-1, keepdims=True), sink)
    acc = _dot(jnp.exp2(s - m).astype(BF16), bv_ref[0])
    yb_ref[0] = _unstack_gqa(acc[:, :LANES] / (acc[:, LANES:] + jnp.exp2(sink - m)), n).astype(yb_ref.dtype)
    s = _dot_nt(_stack_mha(dq_ref[0]), dk_ref[0])
    p = jnp.exp(s - jnp.max(s, axis=-1, keepdims=True))
    o = _dot(p.astype(BF16), dv_ref[0]) / jnp.sum(p, axis=-1, keepdims=True)
    yd_ref[0] = _unstack_mha(o, n).astype(yd_ref.dtype)


def _ctx_attention(sink, aq, ak, av, bq, bk, bv, dq, dk, dv):
    b, n, _ = aq.shape
    spec = lambda wd: pl.BlockSpec((1, n, wd), lambda bi: (bi, 0, 0))
    return pl.pallas_call(
        functools.partial(_ctx_attn_kernel, n=n),
        grid=(b,),
        in_specs=[pl.BlockSpec(memory_space=pltpu.SMEM),
                  spec(GROUP_W), spec(KV_W), spec(2 * LANES), spec(GROUP_W), spec(KV_W), spec(2 * LANES),
                  spec(GROUP_W), spec(GROUP_W), spec(GROUP_W)],
        out_specs=[spec(GROUP_W)] * 3,
        out_shape=[jax.ShapeDtypeStruct((b, n, GROUP_W), BF16)] * 3,
        compiler_params=_params(("parallel",), "ctx_attention"),
        name="ctx_attention",
    )(sink, aq, ak, av, bq, bk, bv, dq, dk, dv)


def _log_sigmoid(x):
    return jnp.minimum(x, 0.0) - jnp.log1p(jnp.exp(-jnp.abs(x)))


def _split3(x):
    x1 = x.astype(BF16)
    r1 = x - x1.astype(F32)
    x2 = r1.astype(BF16)
    return x1, x2, (r1 - x2.astype(F32)).astype(BF16)


def _exact_dot_01(sel, x):
    return sum(_dot(sel, piece) for piece in _split3(x))


def _exact_dot_01_rhs(x, sel):
    return sum(_dot(piece, sel) for piece in _split3(x))


def _mlstm_chunk(q, k, v, gates, expand, tri, neg_mask, ones_cat, block, head_rows, s_ref, n_ref, m_ref, *,
                 reverse, L):
    ge = _exact_dot_01_rhs(gates, expand)
    li = ge[:, :GROUP_W]
    cum = _exact_dot_01(tri, ge[:, GROUP_W:])
    a_t = (li - cum).T
    m_prev = m_ref[0:1, :]
    head_b = [head_rows[h].astype(BF16) for h in range(N_HEADS)]
    mu = jnp.zeros((L, GROUP_W), F32)
    scores = []
    for h in range(N_HEADS):
        lane0 = h * HEAD_DIM
        a_m = a_t[lane0:lane0 + 1, :] + neg_mask
        mu_h = jnp.maximum(jnp.max(a_m, axis=-1, keepdims=True), m_prev[:, lane0:lane0 + 1])
        qk = _dot_nt(q * head_b[h], k)
        scores.append((qk * jnp.exp2(a_m - mu_h)).astype(BF16))
        mu = mu + mu_h * head_rows[h]
    s_cat = jnp.concatenate(scores, axis=1)
    v_cat = jnp.concatenate([v * hb for hb in head_b], axis=0)
    w_inter = jnp.exp2(m_prev - mu)
    num = _dot(s_cat, v_cat) + w_inter * _dot(q, s_ref[...].astype(BF16))
    den = _dot(s_cat, ones_cat) + w_inter * _dot(q, n_ref[...].astype(BF16))
    h_out = num / jnp.maximum(jnp.abs(den), jnp.exp2(-(cum + mu)))

    end_row = 0 if reverse else L - 1
    cum_end = cum[end_row:end_row + 1, :]
    log_end = cum_end - cum + li
    m_new = jnp.maximum(cum_end + m_prev, jnp.max(log_end, axis=0, keepdims=True))
    decay = jnp.exp2(cum_end + m_prev - m_new)
    w_end = jnp.exp2(log_end - m_new)
    k_t = k.astype(F32).T.astype(BF16)
    upd = _dot(k_t, jnp.concatenate([(v.astype(F32) * w_end).astype(BF16), w_end.astype(BF16)], axis=1))
    s_ref[...] = (s_ref[...] * decay + upd[:, :GROUP_W]) * block
    n_ref[...] = (n_ref[...] * decay + upd[:, GROUP_W:]) * block
    m_ref[...] = jnp.broadcast_to(m_new, m_ref.shape)
    return h_out


def _mlstm_kernel(qf_ref, kf_ref, vf_ref, gf_ref, qb_ref, kb_ref, vb_ref, gb_ref,
                  expand_ref, tri_ref, neg_ref, ones_ref, block_ref, heads_ref, s0_ref, n0_ref, m0_ref,
                  hf_ref, hb_ref, s_out, n_out, m_out, s_ref, n_ref, m_ref, *, chunk):
    @pl.when(pl.program_id(1) == 0)
    def _():
        s_ref[...] = s0_ref[0]
        n_ref[...] = n0_ref[0]
        m_ref[...] = m0_ref[0]

    sides = ((qf_ref, kf_ref, vf_ref, gf_ref, hf_ref), (qb_ref, kb_ref, vb_ref, gb_ref, hb_ref))
    for d, (q_ref, k_ref, v_ref, g_ref, h_ref) in enumerate(sides):
        h = _mlstm_chunk(q_ref[0], k_ref[0], v_ref[0], g_ref[0], expand_ref[d], tri_ref[d], neg_ref[d],
                         ones_ref[...], block_ref[...], heads_ref, s_ref.at[d], n_ref.at[d], m_ref.at[d],
                         reverse=bool(d), L=chunk)
        h_ref[0] = h.astype(h_ref.dtype)
    s_out[0] = s_ref[...]
    n_out[0] = n_ref[...]
    m_out[0] = m_ref[...]


def _mlstm_consts(L):
    lane = jnp.arange(LANES)[:, None]
    col = jnp.arange(2 * GROUP_W)[None, :]
    col_head = (col % GROUP_W) // HEAD_DIM
    expand = jnp.stack([(lane == 8 * d + 4 * (col // GROUP_W) + col_head) for d in range(2)]).astype(BF16)
    t_idx = jnp.arange(L)[:, None]
    s_idx = jnp.arange(L)[None, :]
    seen = jnp.stack([s_idx <= t_idx, s_idx >= t_idx])
    head = jnp.arange(GROUP_W) // HEAD_DIM
    ones_cat = (jnp.repeat(jnp.arange(N_HEADS), L)[:, None] == head[None, :]).astype(BF16)
    block = (head[:, None] == head[None, :]).astype(F32)
    head_rows = (jnp.arange(N_HEADS)[:, None, None] == head[None, None, :]).astype(F32)
    return expand, seen.astype(BF16), jnp.where(seen, 0.0, NEG_INF).astype(F32), ones_cat, block, head_rows


def _mlstm_scan(q, k, v, gates, state, consts):
    b, t, _ = q.shape
    L = MLSTM_L
    nc = t // L
    fwd = lambda wd: pl.BlockSpec((1, L, wd), lambda bi, j: (bi, j, 0))
    bwd = lambda wd: pl.BlockSpec((1, L, wd), lambda bi, j: (bi, nc - 1 - j, 0))
    st = lambda r: pl.BlockSpec((1, 2, r, GROUP_W), lambda bi, j: (bi, 0, 0, 0))
    tok_specs = [fwd(GROUP_W), fwd(GROUP_W), fwd(GROUP_W), fwd(LANES),
                 bwd(GROUP_W), bwd(GROUP_W), bwd(GROUP_W), bwd(LANES)]
    hf, hb, s_fin, n_fin, m_fin = pl.pallas_call(
        functools.partial(_mlstm_kernel, chunk=L),
        grid=(b, nc),
        in_specs=tok_specs + [_const_spec(c.shape) for c in consts] + [st(GROUP_W), st(GROUP_W), st(8)],
        out_specs=[fwd(GROUP_W), bwd(GROUP_W), st(GROUP_W), st(GROUP_W), st(8)],
        out_shape=[jax.ShapeDtypeStruct((b, t, GROUP_W), BF16), jax.ShapeDtypeStruct((b, t, GROUP_W), BF16),
                   jax.ShapeDtypeStruct((b, 2, GROUP_W, GROUP_W), F32),
                   jax.ShapeDtypeStruct((b, 2, GROUP_W, GROUP_W), F32),
                   jax.ShapeDtypeStruct((b, 2, 8, GROUP_W), F32)],
        scratch_shapes=[pltpu.VMEM((2, GROUP_W, GROUP_W), F32), pltpu.VMEM((2, GROUP_W, GROUP_W), F32),
                        pltpu.VMEM((2, 8, GROUP_W), F32)],
        compiler_params=_params(("parallel", "arbitrary"), "mlstm"),
        name="mlstm",
    )(q, k, v, gates, q, k, v, gates, *consts, *state)
    return hf, hb, (s_fin, n_fin, m_fin)


def _outproj_kernel(x_ref, ya_ref, yb_ref, hf_ref, hb_ref, og_ref, yd_ref, mod_ref, gg_ref, w_ref, o_ref, *, d_model):
    d = d_model
    yc = jax.nn.sigmoid(og_ref[0]) * (hf_ref[0].astype(F32) + hb_ref[0].astype(F32))
    parts = []
    for i, y in enumerate((ya_ref[0], yb_ref[0], yc, yd_ref[0])):
        parts.append((_rms_rows(y.astype(F32)) * gg_ref[i:i + 1, :]).astype(BF16))
    res = _dot(jnp.concatenate(parts, axis=1), w_ref[...])
    o_ref[0] = x_ref[0] + mod_ref[0][:, 2 * d:3 * d] * res


def _out_proj(x, ya, yb, hf, hb, o_gate, yd, mod, mod_row, gg, w, *, layer, tm):
    b, t, d = x.shape
    tokd = pl.BlockSpec((1, tm, d), lambda bi, j: (bi, j, 0))
    tokg = pl.BlockSpec((1, tm, GROUP_W), lambda bi, j: (bi, j, 0))
    return pl.pallas_call(
        functools.partial(_outproj_kernel, d_model=d),
        grid=(b, t // tm),
        in_specs=[tokd, tokg, tokg, tokg, tokg, tokg, tokg,
                  pl.BlockSpec((1, 1, 6 * d), lambda bi, j: (mod_row(bi), 0, 0)),
                  _const_spec(gg.shape), _layer_spec(w.shape, layer)],
        out_specs=tokd,
        out_shape=jax.ShapeDtypeStruct((b, t, d), F32),
        compiler_params=_params(("parallel", "parallel"), "out_proj"),
        name="out_proj",
    )(x, ya, yb, hf, hb, o_gate, yd, mod, gg, w)


def _ffn_kernel(*refs, d_model, d_ff, tm, final_norm):
    if final_norm:
        x_ref, xp_ref, xn_ref, mod_ref, g2_ref, wu_ref, cw_ref, wd_ref, gf_ref, o_ref = refs
    else:
        x_ref, xp_ref, xn_ref, mod_ref, g2_ref, wu_ref, cw_ref, wd_ref, o_ref = refs
    d = d_model
    j = pl.program_id(1)
    mod = mod_ref[0]
    g2 = g2_ref[...]

    def norm_mod(x):
        return _rms_rows(x) * g2 * (1.0 + mod[:, 4 * d:5 * d]) + mod[:, 3 * d:4 * d]

    keep_prev = (j > 0).astype(F32)
    keep_next = (j < pl.num_programs(1) - 1).astype(F32)
    x = x_ref[0]
    h_ext = jnp.concatenate([norm_mod(xp_ref[0]) * keep_prev, norm_mod(x), norm_mod(xn_ref[0]) * keep_next],
                            axis=0).astype(BF16)
    n_ext = tm + 2 * HALO

    u = _dot(h_ext, wu_ref[...])
    cw = cw_ref[...]
    u = (pltpu.roll(u, 1, 0) * cw[0:1] + u * cw[1:2] + pltpu.roll(u, n_ext - 1, 0) * cw[2:3] + cw[3:4])[HALO:HALO + tm]
    gate, val = u[:, :d_ff], u[:, d_ff:]
    act = (gate * jax.nn.sigmoid(gate) * val).astype(BF16)
    y = x + mod[:, 5 * d:6 * d] * _dot(act, wd_ref[...])
    if final_norm:
        y = _rms_rows(y) * gf_ref[...]
    o_ref[0] = y


def _conv_ffn(x, mod, mod_row, g2, wu, cw, wd, g_final, *, layer, tm):
    b, t, d = x.shape
    d_ff = wd.shape[1]
    final_norm = g_final is not None
    hb = tm // HALO
    last = t // HALO - 1
    row = pl.BlockSpec((1, d), lambda bi, j: (0, 0))
    in_specs = [pl.BlockSpec((1, tm, d), lambda bi, j: (bi, j, 0)),
                pl.BlockSpec((1, HALO, d), lambda bi, j: (bi, jnp.maximum(j * hb - 1, 0), 0)),
                pl.BlockSpec((1, HALO, d), lambda bi, j: (bi, jnp.minimum((j + 1) * hb, last), 0)),
                pl.BlockSpec((1, 1, 6 * d), lambda bi, j: (mod_row(bi), 0, 0)),
                row,
                _layer_spec(wu.shape, layer), _const_spec(cw.shape), _layer_spec(wd.shape, layer)]
    args = [x, x, x, mod, g2, wu, cw, wd]
    if final_norm:
        in_specs.append(row)
        args.append(g_final)
    return pl.pallas_call(
        functools.partial(_ffn_kernel, d_model=d, d_ff=d_ff, tm=tm, final_norm=final_norm),
        grid=(b, t // tm),
        in_specs=in_specs,
        out_specs=pl.BlockSpec((1, tm, d), lambda bi, j: (bi, j, 0)),
        out_shape=jax.ShapeDtypeStruct((b, t, d), F32),
        compiler_params=_params(("parallel", "parallel"), "conv_ffn"),
        name="conv_ffn",
    )(*args)


_Q_HEAD_ORDER = (0, 2, 1, 3)


def _permute_heads(w, axis):
    parts = [lax.slice_in_dim(w, h * HEAD_DIM, (h + 1) * HEAD_DIM, axis=axis) for h in _Q_HEAD_ORDER]
    return jnp.concatenate(parts, axis=axis)


def _layout_w_in(w):
    w = w.astype(BF16)
    g, kv = GROUP_W, KV_W
    o_b = g + 2 * kv
    o_c = 2 * o_b
    o_g = o_c + 4 * g
    o_d = o_g + 4 * N_HEADS
    sl = lambda a, n: w[:, :, a:a + n]
    pad = jnp.zeros(w.shape[:2] + (LANES - 4 * N_HEADS,), BF16)
    return jnp.concatenate([_permute_heads(sl(0, g), 2), sl(g, 2 * kv),
                            _permute_heads(sl(o_b, g), 2), sl(o_b + g, 2 * kv),
                            sl(o_c, 4 * g), sl(o_d, 3 * g), sl(o_g, 4 * N_HEADS), pad], axis=2)


def _layout_w_out(w):
    w = w.astype(BF16)
    return jnp.concatenate([_permute_heads(w[:, 0:GROUP_W], 1), _permute_heads(w[:, GROUP_W:2 * GROUP_W], 1),
                            w[:, 2 * GROUP_W:]], axis=1)


def _rope_tables(t):
    pos = jnp.arange(t)
    row = (pos // GRID_W).astype(F32)
    col = (pos % GRID_W).astype(F32)
    n_freq = HEAD_DIM // 4
    freqs = ROPE_THETA ** (-jnp.arange(n_freq, dtype=F32) / n_freq)
    ang_r, ang_c = row[:, None] * freqs, col[:, None] * freqs
    cos = jnp.concatenate([jnp.cos(ang_r), jnp.cos(ang_r), jnp.cos(ang_c), jnp.cos(ang_c)], axis=1)
    sin = jnp.concatenate([-jnp.sin(ang_r), jnp.sin(ang_r), -jnp.sin(ang_c), jnp.sin(ang_c)], axis=1)
    reps = LANES // HEAD_DIM
    return jnp.tile(cos, (1, reps)), jnp.tile(sin, (1, reps))


def _pick_tile(t, pref):
    while t % pref:
        pref //= 2
    return pref


def kernel(x, c, ctx, c_ctx, w_mod, b_mod, g_norm1, g_norm2, w_in, a_q_gain, a_k_gain, b_sink, c_gate_bias,
           d_rel_bias, g_group, w_out, w_up, conv_w, conv_b, w_down, g_final):
    batch, t, d = x.shape
    n_ctx = ctx.shape[1]
    depth = w_in.shape[0]
    d_ff = w_down.shape[1]
    assert batch < 8 and d_ff % LANES == 0 and t % MLSTM_L == 0 and n_ctx % MLSTM_L == 0

    c_rows = jnp.zeros((8, d), F32).at[:batch].set(c).at[batch].set(c_ctx)
    mod_all = _modulation(c_rows, w_mod, b_mod).reshape(depth, 8, 1, 6 * d)
    lat_row = lambda bi: bi
    ctx_row = lambda bi: batch

    cos, sin = _rope_tables(t)
    pool = jnp.where(jnp.arange(GROUP_W)[:, None] // HEAD_DIM == jnp.arange(GROUP_W)[None, :] // HEAD_DIM,
                     1.0 / HEAD_DIM, 0.0).astype(BF16)
    tm_lat = _pick_tile(t, TOKEN_TILE)
    tm_ctx = _pick_tile(n_ctx, CTX_TILE)
    zero_state = (jnp.zeros((batch, 2, GROUP_W, GROUP_W), F32), jnp.zeros((batch, 2, GROUP_W, GROUP_W), F32),
                  jnp.zeros((batch, 2, 8, GROUP_W), F32))
    mlstm_consts = _mlstm_consts(MLSTM_L)
    w_in_b = _layout_w_in(w_in)
    w_out_b = _layout_w_out(w_out)
    w_up_b = w_up.astype(BF16)
    w_down_b = w_down.astype(BF16)

    for layer in range(depth):
        need_ctx = layer < depth - 1
        mod = mod_all[layer]
        g1 = g_norm1[layer][None]
        g2 = g_norm2[layer][None]
        gq = jnp.tile(a_q_gain[layer], N_HEADS)[None]
        gk = jnp.tile(a_k_gain[layer], 2)[None]
        gbias = jnp.pad(c_gate_bias[layer], (0, LANES - 4 * N_HEADS))[None]
        gg = g_group[layer].reshape(N_HEADS, GROUP_W)
        gg = jnp.concatenate([_permute_heads(gg[0:2], 1), gg[2:4]], axis=0)
        conv_l = jnp.concatenate([conv_w[layer], conv_b[layer][None],
                                  jnp.zeros((8 - 1 - conv_w.shape[1], 2 * d_ff), F32)], axis=0)
        sink = b_sink[layer]
        bias_tab = _neighbour_bias(d_rel_bias[layer])

        proj = functools.partial(_in_proj, g1=g1, w=w_in_b, pool=pool, gq=gq, gk=gk, gbias=gbias, layer=layer)
        (aqc, akc, avc, bqc, bkc, bvc, cqc, ckc, cvc, coc, cgc, dqc, dkc, dvc) = proj(
            ctx, mod, ctx_row, cos=cos, sin=sin, rope=False, tm=tm_ctx)
        (aq, ak, av, bq, bk, bv, cq, ck, cv, co, cg, dq, dk, dv) = proj(
            x, mod, lat_row, cos=cos, sin=sin, rope=True, tm=_pick_tile(t, 2 * TOKEN_TILE))

        score_bound = 1.02 * LOG2_E * HEAD_DIM ** 0.5 * jnp.max(jnp.abs(a_q_gain[layer])) * jnp.max(jnp.abs(a_k_gain[layer]))
        safe = (score_bound <= MAX_UNSHIFTED_LOG2_SCORE).astype(jnp.int32).reshape(1)
        ya = _attn_global(safe, aq, ak, av, akc, avc, tq=_pick_tile(t, ATTN_Q_TILE), tk=_pick_tile(t, ATTN_K_TILE))
        yb = _attn_window(sink, bq, bk, bv, bkc, bvc, tq=_pick_tile(t, WINDOW_Q_TILE))
        yd = _attn_neighbour(dq, dk, dv, dkc, dvc, bias_tab, rows_per_step=NEIGHBOUR_ROWS)

        hcf, hcb, ctx_state = _mlstm_scan(cqc, ckc, cvc, cgc, zero_state, mlstm_consts)
        hf, hb, _ = _mlstm_scan(cq, ck, cv, cg, ctx_state, mlstm_consts)

        x = _out_proj(x, ya, yb, hf, hb, co, yd, mod, lat_row, gg, w_out_b, layer=layer,
                      tm=_pick_tile(t, 2 * TOKEN_TILE))
        ffn = functools.partial(_conv_ffn, g2=g2, wu=w_up_b, cw=conv_l, wd=w_down_b, layer=layer)
        x = ffn(x, mod, lat_row, g_final=None if need_ctx else g_final[None], tm=tm_lat)
        if need_ctx:
            yac, ybc, ydc = _ctx_attention(sink, aqc, akc, avc, bqc, bkc, bvc, dqc, dkc, dvc)
            ctx = _out_proj(ctx, yac, ybc, hcf, hcb, coc, ydc, mod, ctx_row, gg, w_out_b, layer=layer, tm=tm_ctx)
            ctx = ffn(ctx, mod, ctx_row, g_final=None, tm=tm_ctx)
    return x
```

```python
import functools

import jax
import jax.numpy as jnp
from jax import lax
from jax.experimental import pallas as pl
from jax.experimental.pallas import tpu as pltpu

F32 = jnp.float32
BF16 = jnp.bfloat16

N_HEADS = 4
HEAD_DIM = 64
GROUP_W = N_HEADS * HEAD_DIM
KV_W = 2 * HEAD_DIM
GRID_W = 64
WINDOW = 128
NA_KH = 8
NA_KW = 16
ROPE_THETA = 10000.0
EPS = 1e-6
NEG_INF = -1e30
LANES = 128
MLSTM_L = 256
HALO = 8
MIB = 1024 * 1024
LOG2_E = 1.4426950408889634
MAX_UNSHIFTED_LOG2_SCORE = 60.0


TOKEN_TILE = 512
CTX_TILE = 256
ATTN_Q_TILE = 1024
ATTN_K_TILE = 2048
WINDOW_Q_TILE = 256
NEIGHBOUR_ROWS = 8
VMEM_LIMIT_MIB = {"modulation": 40, "in_proj": 48, "attn_global": 48, "attn_window": 48, "attn_neighbour": 48,
                  "ctx_attention": 32, "mlstm": 48, "out_proj": 48, "conv_ffn": 56}


def _params(sem, name):
    return pltpu.CompilerParams(dimension_semantics=sem, vmem_limit_bytes=VMEM_LIMIT_MIB[name] * MIB)


def _dot(a, b):
    return jnp.dot(a, b, preferred_element_type=F32)


def _dot_nt(a, b):
    return lax.dot_general(a, b, (((1,), (1,)), ((), ())), preferred_element_type=F32)


def _const_spec(shape):
    return pl.BlockSpec(shape, lambda *_: (0,) * len(shape), pipeline_mode=pl.Buffered(1))


def _layer_spec(stacked_shape, layer):
    rest = tuple(stacked_shape[1:])
    return pl.BlockSpec((None,) + rest, lambda *_: (layer,) + (0,) * len(rest), pipeline_mode=pl.Buffered(1))


def _mod_kernel(c_ref, w_ref, b_ref, o_ref):
    c = c_ref[...]
    a = (c * jax.nn.sigmoid(c)).astype(BF16)
    o_ref[0] = _dot(a, w_ref[0].astype(BF16)) + b_ref[0]


def _modulation(c_rows, w_mod, b_mod):
    depth, d, n = w_mod.shape
    tn = 1536
    return pl.pallas_call(
        _mod_kernel,
        grid=(depth, n // tn),
        in_specs=[pl.BlockSpec((8, d), lambda l, j: (0, 0)),
                  pl.BlockSpec((1, d, tn), lambda l, j: (l, 0, j)),
                  pl.BlockSpec((1, 1, tn), lambda l, j: (l, 0, j))],
        out_specs=pl.BlockSpec((1, 8, tn), lambda l, j: (l, 0, j)),
        out_shape=jax.ShapeDtypeStruct((depth, 8, n), F32),
        compiler_params=_params(("parallel", "parallel"), "modulation"),
        name="modulation",
    )(c_rows, w_mod, b_mod.reshape(depth, 1, n))


def _rms_rows(x):
    return x * lax.rsqrt(jnp.mean(x * x, axis=-1, keepdims=True) + EPS)


def _head_rms(z, pool, gain):
    z2 = z * z
    hi = z2.astype(BF16)
    lo = (z2 - hi.astype(F32)).astype(BF16)
    ms = _dot(hi, pool) + _dot(lo, pool)
    return z * lax.rsqrt(ms + EPS) * gain


def _rope(z, cos, sin):
    lane = lax.broadcasted_iota(jnp.int32, z.shape, 1)
    first = (lane % 32) < 16
    partner = jnp.where(first, pltpu.roll(z, LANES - 16, 1), pltpu.roll(z, 16, 1))
    return z * cos + partner * sin


def _inproj_kernel(x_ref, mod_ref, g1_ref, w_ref, pool_ref, gq_ref, gk_ref, cos_ref, sin_ref, gb_ref,
                   aq_ref, ak_ref, av_ref, bq_ref, bk_ref, bv_ref, cq_ref, ck_ref, cv_ref, co_ref, cg_ref,
                   dq_ref, dk_ref, dv_ref, *, d_model, rope):
    d = d_model
    mod = mod_ref[0]
    xn = _rms_rows(x_ref[0]) * g1_ref[...]
    hb = (xn * (1.0 + mod[:, d:2 * d]) + mod[:, 0:d]).astype(BF16)
    projected = _dot(hb, w_ref[...])

    o_b = GROUP_W + 2 * KV_W
    o_c = 2 * o_b
    o_g = o_c + 4 * GROUP_W
    o_d = o_g + 4 * N_HEADS

    def cols(start, width=GROUP_W):
        return projected[:, start:start + width]

    def rot(z):
        if not rope:
            return z
        cos, sin = cos_ref[...], sin_ref[...]
        return jnp.concatenate([_rope(z[:, i:i + LANES], cos, sin) for i in range(0, z.shape[1], LANES)], axis=1)

    def gqa_order(z):
        a, b = z[:, :LANES], z[:, LANES:]
        left = lax.broadcasted_iota(jnp.int32, a.shape, 1) < HEAD_DIM
        return jnp.concatenate([jnp.where(left, a, pltpu.roll(b, HEAD_DIM, 1)),
                                jnp.where(left, pltpu.roll(a, HEAD_DIM, 1), b)], axis=1)

    scale = HEAD_DIM ** -0.5
    pool = pool_ref[...]
    aq_ref[0] = (gqa_order(rot(_head_rms(cols(0), pool, gq_ref[...]))) * (scale * LOG2_E)).astype(BF16)
    akv = cols(GROUP_W)
    ak_ref[0] = rot(_head_rms(akv[:, :KV_W], pool[:KV_W, :KV_W], gk_ref[...])).astype(BF16)
    ones = jnp.ones((akv.shape[0], LANES), F32)
    av_ref[0] = jnp.concatenate([akv[:, KV_W:], ones], axis=1).astype(BF16)
    bq_ref[0] = (gqa_order(rot(cols(o_b))) * (scale * LOG2_E)).astype(BF16)
    bkv = cols(o_b + GROUP_W)
    bk_ref[0] = rot(bkv[:, :KV_W]).astype(BF16)
    bv_ref[0] = jnp.concatenate([bkv[:, KV_W:], ones], axis=1).astype(BF16)
    cq_ref[0] = (cols(o_c) * scale).astype(BF16)
    ck_ref[0] = cols(o_c + GROUP_W).astype(BF16)
    cv_ref[0] = cols(o_c + 2 * GROUP_W).astype(BF16)
    co_ref[0] = cols(o_c + 3 * GROUP_W)
    dq_ref[0] = (cols(o_d) * scale).astype(BF16)
    dk_ref[0] = cols(o_d + GROUP_W).astype(BF16)
    dv_ref[0] = cols(o_d + 2 * GROUP_W).astype(BF16)
    gates = cols(o_g, LANES) + gb_ref[...]
    lane = lax.broadcasted_iota(jnp.int32, gates.shape, 1)
    gates = jnp.where((lane % 8) >= 4, _log_sigmoid(gates), gates) * LOG2_E
    cg_ref[0] = jnp.where(lane < 4 * N_HEADS, gates, 0.0)


def _in_proj(x, mod, mod_row, g1, w, pool, gq, gk, cos, sin, gbias, *, layer, rope, tm):
    b, t, d = x.shape
    widths = [(GROUP_W, BF16), (KV_W, BF16), (2 * LANES, BF16),
              (GROUP_W, BF16), (KV_W, BF16), (2 * LANES, BF16),
              (GROUP_W, BF16), (GROUP_W, BF16), (GROUP_W, BF16), (GROUP_W, F32), (LANES, F32),
              (GROUP_W, BF16), (GROUP_W, BF16), (GROUP_W, BF16)]
    row = lambda wd: pl.BlockSpec((1, wd), lambda bi, j: (0, 0))
    return pl.pallas_call(
        functools.partial(_inproj_kernel, d_model=d, rope=rope),
        grid=(b, t // tm),
        in_specs=[pl.BlockSpec((1, tm, d), lambda bi, j: (bi, j, 0)),
                  pl.BlockSpec((1, 1, 6 * d), lambda bi, j: (mod_row(bi), 0, 0)),
                  row(d),
                  _layer_spec(w.shape, layer),
                  _const_spec((GROUP_W, GROUP_W)),
                  row(GROUP_W), row(KV_W),
                  pl.BlockSpec((tm, LANES), lambda bi, j: (j, 0)),
                  pl.BlockSpec((tm, LANES), lambda bi, j: (j, 0)),
                  row(LANES)],
        out_specs=[pl.BlockSpec((1, tm, wd), lambda bi, j: (bi, j, 0)) for wd, _ in widths],
        out_shape=[jax.ShapeDtypeStruct((b, t, wd), dt) for wd, dt in widths],
        compiler_params=_params(("parallel", "parallel"), "in_proj"),
        name="in_proj",
    )(x, mod, g1, w, pool, gq, gk, cos, sin, gbias)


def _stack_gqa(q):
    qa, qb = q[:, :LANES], q[:, LANES:]
    left = lax.broadcasted_iota(jnp.int32, qa.shape, 1) < HEAD_DIM
    zero = jnp.zeros_like(qa)
    return jnp.concatenate([jnp.where(left, qa, zero), jnp.where(left, qb, zero),
                            jnp.where(left, zero, qa), jnp.where(left, zero, qb)], axis=0)


def _unstack_gqa(o, t):
    left = lax.broadcasted_iota(jnp.int32, (t, LANES), 1) < HEAD_DIM
    return jnp.concatenate([jnp.where(left, o[0:t], o[2 * t:3 * t]),
                            jnp.where(left, o[t:2 * t], o[3 * t:4 * t])], axis=1)


def _attn_a_kernel(safe_ref, q_ref, k_ref, v_ref, kc_ref, vc_ref, o_ref, qs_ref, m_ref, acc_ref, *, tq, tk, n_kb):
    qs_ref[...] = _stack_gqa(q_ref[0])
    heads = [slice(h * tq, (h + 1) * tq) for h in range(N_HEADS)]

    @pl.when(safe_ref[0] != 0)
    def _():
        for rows in heads:
            p = jnp.exp2(_dot_nt(qs_ref[rows], kc_ref[0]))
            acc_ref[rows] = _dot(p.astype(BF16), vc_ref[0])

        def body(kb, carry):
            start = pl.multiple_of(kb * tk, tk)
            for rows in heads:
                p = jnp.exp2(_dot_nt(qs_ref[rows], k_ref[0, pl.ds(start, tk), :]))
                acc_ref[rows] += _dot(p.astype(BF16), v_ref[0, pl.ds(start, tk), :])
            return carry

        lax.fori_loop(0, n_kb, body, 0)

    @pl.when(safe_ref[0] == 0)
    def _():
        for rows in heads:
            s = _dot_nt(qs_ref[rows], kc_ref[0])
            m0 = jnp.max(s, axis=-1, keepdims=True)
            m_ref[rows] = m0
            acc_ref[rows] = _dot(jnp.exp2(s - m0).astype(BF16), vc_ref[0])

        def body(kb, carry):
            start = pl.multiple_of(kb * tk, tk)
            for rows in heads:
                s = _dot_nt(qs_ref[rows], k_ref[0, pl.ds(start, tk), :])
                m_prev = m_ref[rows]
                m_new = jnp.maximum(m_prev, jnp.max(s, axis=-1, keepdims=True))
                p = jnp.exp2(s - m_new)
                acc_ref[rows] = jnp.exp2(m_prev - m_new) * acc_ref[rows] + _dot(
                    p.astype(BF16), v_ref[0, pl.ds(start, tk), :])
                m_ref[rows] = m_new
            return carry

        lax.fori_loop(0, n_kb, body, 0)

    acc = acc_ref[...]
    o_ref[0] = _unstack_gqa(acc[:, :LANES] / acc[:, LANES:], tq).astype(o_ref.dtype)


def _attn_global(safe, q, k, v, kc, vc, *, tq, tk):
    b, t, _ = q.shape
    n_ctx = kc.shape[1]
    per_b = lambda n, wd: pl.BlockSpec((1, n, wd), lambda bi, i: (bi, 0, 0))
    return pl.pallas_call(
        functools.partial(_attn_a_kernel, tq=tq, tk=tk, n_kb=t // tk),
        grid=(b, t // tq),
        in_specs=[pl.BlockSpec(memory_space=pltpu.SMEM),
                  pl.BlockSpec((1, tq, GROUP_W), lambda bi, i: (bi, i, 0)),
                  per_b(t, KV_W), per_b(t, 2 * LANES), per_b(n_ctx, KV_W), per_b(n_ctx, 2 * LANES)],
        out_specs=pl.BlockSpec((1, tq, GROUP_W), lambda bi, i: (bi, i, 0)),
        out_shape=jax.ShapeDtypeStruct((b, t, GROUP_W), BF16),
        scratch_shapes=[pltpu.VMEM((4 * tq, LANES), BF16), pltpu.VMEM((4 * tq, 1), F32),
                        pltpu.VMEM((4 * tq, 2 * LANES), F32)],
        compiler_params=_params(("parallel", "parallel"), "attn_global"),
        name="attn_global",
    )(safe, q, k, v, kc, vc)


def _sink_column(sink_ref, t):
    row = lax.broadcasted_iota(jnp.int32, (4 * t, 1), 0)
    col = jnp.full((4 * t, 1), sink_ref[3], F32)
    for h in (2, 1, 0):
        col = jnp.where(row < (h + 1) * t, sink_ref[h], col)
    return col


def _attn_b_kernel(sink_ref, q_ref, k_ref, v_ref, kc_ref, vc_ref, o_ref, *, tq, t_total):
    span = tq + 2 * WINDOW
    q0 = pl.program_id(1) * tq
    ks = pl.multiple_of(jnp.clip(q0 - WINDOW, 0, t_total - span), WINDOW)
    qs = _stack_gqa(q_ref[0])
    kw = k_ref[0, pl.ds(ks, span), :]
    vw = v_ref[0, pl.ds(ks, span), :]
    q_pos = q0 + lax.broadcasted_iota(jnp.int32, (tq, span), 0)
    k_pos = ks + lax.broadcasted_iota(jnp.int32, (tq, span), 1)
    in_window = jnp.abs(k_pos - q_pos) <= WINDOW
    outs = []
    for h in range(N_HEADS):
        qh = qs[h * tq:(h + 1) * tq]
        s = jnp.where(in_window, _dot_nt(qh, kw), NEG_INF)
        sc = _dot_nt(qh, kc_ref[0])
        sink = sink_ref[h] * LOG2_E
        m = jnp.maximum(jnp.maximum(jnp.max(s, axis=-1, keepdims=True), jnp.max(sc, axis=-1, keepdims=True)), sink)
        acc = _dot(jnp.exp2(s - m).astype(BF16), vw) + _dot(jnp.exp2(sc - m).astype(BF16), vc_ref[0])
        outs.append(acc[:, :LANES] / (acc[:, LANES:] + jnp.exp2(sink - m)))
    o_ref[0] = _unstack_gqa(jnp.concatenate(outs, axis=0), tq).astype(o_ref.dtype)


def _attn_window(sink, q, k, v, kc, vc, *, tq):
    b, t, _ = q.shape
    n_ctx = kc.shape[1]
    assert t >= tq + 2 * WINDOW
    per_b = lambda n, wd: pl.BlockSpec((1, n, wd), lambda bi, i: (bi, 0, 0))
    return pl.pallas_call(
        functools.partial(_attn_b_kernel, tq=tq, t_total=t),
        grid=(b, t // tq),
        in_specs=[pl.BlockSpec(memory_space=pltpu.SMEM),
                  pl.BlockSpec((1, tq, GROUP_W), lambda bi, i: (bi, i, 0)),
                  per_b(t, KV_W), per_b(t, 2 * LANES), per_b(n_ctx, KV_W), per_b(n_ctx, 2 * LANES)],
        out_specs=pl.BlockSpec((1, tq, GROUP_W), lambda bi, i: (bi, i, 0)),
        out_shape=jax.ShapeDtypeStruct((b, t, GROUP_W), BF16),
        compiler_params=_params(("parallel", "parallel"), "attn_window"),
        name="attn_window",
    )(sink, q, k, v, kc, vc)


def _head_masks(shape):
    lane = lax.broadcasted_iota(jnp.int32, shape, 1)
    return [(lane >= h * HEAD_DIM) & (lane < (h + 1) * HEAD_DIM) for h in range(N_HEADS)]


def _stack_mha(q):
    zero = jnp.zeros_like(q)
    return jnp.concatenate([jnp.where(mk, q, zero) for mk in _head_masks(q.shape)], axis=0)


def _unstack_mha(o, t):
    masks = _head_masks((t, GROUP_W))
    out = jnp.where(masks[0], o[0:t], 0.0)
    for h in range(1, N_HEADS):
        out = jnp.where(masks[h], o[h * t:(h + 1) * t], out)
    return out


def _attn_d_kernel(q_ref, k_ref, v_ref, kc_ref, vc_ref, bias_ref, o_ref, *, rows_per_step, rows):
    kc = kc_ref[0]
    vc = vc_ref[0]
    span = NA_KH * GRID_W

    def body(rr, carry):
        r = pl.program_id(1) * rows_per_step + rr
        rs = jnp.clip(r - NA_KH // 2, 0, rows - NA_KH)
        k0 = pl.multiple_of(rs * GRID_W, GRID_W)
        q0 = pl.multiple_of(rr * GRID_W, GRID_W)
        qs = _stack_mha(q_ref[0, pl.ds(q0, GRID_W), :])
        s = _dot_nt(qs, k_ref[0, pl.ds(k0, span), :]) + bias_ref[r - rs]
        sc = _dot_nt(qs, kc)
        m = jnp.maximum(jnp.max(s, axis=-1, keepdims=True), jnp.max(sc, axis=-1, keepdims=True))
        p = jnp.exp(s - m)
        pc = jnp.exp(sc - m)
        l = jnp.sum(p, axis=-1, keepdims=True) + jnp.sum(pc, axis=-1, keepdims=True)
        o = _dot(p.astype(BF16), v_ref[0, pl.ds(k0, span), :]) + _dot(pc.astype(BF16), vc)
        o_ref[0, pl.ds(q0, GRID_W), :] = _unstack_mha(o / l, GRID_W).astype(o_ref.dtype)
        return carry

    lax.fori_loop(0, rows_per_step, body, 0, unroll=True)


def _attn_neighbour(q, k, v, kc, vc, bias, *, rows_per_step):
    b, t, _ = q.shape
    n_ctx = kc.shape[1]
    rows = t // GRID_W
    assert rows >= NA_KH and rows % rows_per_step == 0
    tq = rows_per_step * GRID_W
    per_b = lambda n: pl.BlockSpec((1, n, GROUP_W), lambda bi, i: (bi, 0, 0))
    return pl.pallas_call(
        functools.partial(_attn_d_kernel, rows_per_step=rows_per_step, rows=rows),
        grid=(b, t // tq),
        in_specs=[pl.BlockSpec((1, tq, GROUP_W), lambda bi, i: (bi, i, 0)),
                  per_b(t), per_b(t), per_b(n_ctx), per_b(n_ctx),
                  _const_spec(bias.shape)],
        out_specs=pl.BlockSpec((1, tq, GROUP_W), lambda bi, i: (bi, i, 0)),
        out_shape=jax.ShapeDtypeStruct((b, t, GROUP_W), BF16),
        compiler_params=_params(("parallel", "parallel"), "attn_neighbour"),
        name="attn_neighbour",
    )(q, k, v, kc, vc, bias)


def _neighbour_bias(rpb):
    w = jnp.arange(GRID_W)
    cs = jnp.clip(w - NA_KW // 2, 0, GRID_W - NA_KW)
    col = jnp.arange(GRID_W)
    valid = (col[None, :] >= cs[:, None]) & (col[None, :] < cs[:, None] + NA_KW)
    dc = jnp.clip(col[None, :] - w[:, None] + (NA_KW - 1), 0, 2 * NA_KW - 2)
    pick_c = (jnp.arange(2 * NA_KW - 1)[:, None, None] == dc[None]).astype(F32)
    by_col = jnp.einsum('hrc,cwk->hwrk', rpb.astype(F32), pick_c, precision=lax.Precision.HIGHEST)
    by_col = jnp.where(valid[None, :, None, :], by_col, NEG_INF)
    flat = by_col.reshape(N_HEADS * GRID_W, (2 * NA_KH - 1) * GRID_W)
    return jnp.stack([flat[:, (NA_KH - 1 - o) * GRID_W:(2 * NA_KH - 1 - o) * GRID_W] for o in range(NA_KH)])


def _ctx_attn_kernel(sink_ref, aq_ref, ak_ref, av_ref, bq_ref, bk_ref, bv_ref, dq_ref, dk_ref, dv_ref,
                     ya_ref, yb_ref, yd_ref, *, n):
    s = _dot_nt(_stack_gqa(aq_ref[0]), ak_ref[0])
    acc = _dot(jnp.exp2(s - jnp.max(s, axis=-1, keepdims=True)).astype(BF16), av_ref[0])
    ya_ref[0] = _unstack_gqa(acc[:, :LANES] / acc[:, LANES:], n).astype(ya_ref.dtype)
    s = _dot_nt(_stack_gqa(bq_ref[0]), bk_ref[0])
    sink = _sink_column(sink_ref, n) * LOG2_E
    m = jnp.maximum(jnp.max(s, axis=-1, keepdims=True), sink)
    acc = _dot(jnp.exp2(s - m).astype(BF16), bv_ref[0])
    yb_ref[0] = _unstack_gqa(acc[:, :LANES] / (acc[:, LANES:] + jnp.exp2(sink - m)), n).astype(yb_ref.dtype)
    s = _dot_nt(_stack_mha(dq_ref[0]), dk_ref[0])
    p = jnp.exp(s - jnp.max(s, axis=-1, keepdims=True))
    o = _dot(p.astype(BF16), dv_ref[0]) / jnp.sum(p, axis=-1, keepdims=True)
    yd_ref[0] = _unstack_mha(o, n).astype(yd_ref.dtype)


def _ctx_attention(sink, aq, ak, av, bq, bk, bv, dq, dk, dv):
    b, n, _ = aq.shape
    spec = lambda wd: pl.BlockSpec((1, n, wd), lambda bi: (bi, 0, 0))
    return pl.pallas_call(
        functools.partial(_ctx_attn_kernel, n=n),
        grid=(b,),
        in_specs=[pl.BlockSpec(memory_space=pltpu.SMEM),
                  spec(GROUP_W), spec(KV_W), spec(2 * LANES), spec(GROUP_W), spec(KV_W), spec(2 * LANES),
                  spec(GROUP_W), spec(GROUP_W), spec(GROUP_W)],
        out_specs=[spec(GROUP_W)] * 3,
        out_shape=[jax.ShapeDtypeStruct((b, n, GROUP_W), BF16)] * 3,
        compiler_params=_params(("parallel",), "ctx_attention"),
        name="ctx_attention",
    )(sink, aq, ak, av, bq, bk, bv, dq, dk, dv)


def _log_sigmoid(x):
    return jnp.minimum(x, 0.0) - jnp.log1p(jnp.exp(-jnp.abs(x)))


def _split3(x):
    x1 = x.astype(BF16)
    r1 = x - x1.astype(F32)
    x2 = r1.astype(BF16)
    return x1, x2, (r1 - x2.astype(F32)).astype(BF16)


def _exact_dot_01(sel, x):
    return sum(_dot(sel, piece) for piece in _split3(x))


def _exact_dot_01_rhs(x, sel):
    return sum(_dot(piece, sel) for piece in _split3(x))


def _mlstm_chunk(q, k, v, gates, expand, tri, neg_mask, ones_cat, block, head_rows, s_ref, n_ref, m_ref, *,
                 reverse, L):
    ge = _exact_dot_01_rhs(gates, expand)
    li = ge[:, :GROUP_W]
    cum = _exact_dot_01(tri, ge[:, GROUP_W:])
    a_t = (li - cum).T
    m_prev = m_ref[0:1, :]
    head_b = [head_rows[h].astype(BF16) for h in range(N_HEADS)]
    mu = jnp.zeros((L, GROUP_W), F32)
    scores = []
    for h in range(N_HEADS):
        lane0 = h * HEAD_DIM
        a_m = a_t[lane0:lane0 + 1, :] + neg_mask
        mu_h = jnp.maximum(jnp.max(a_m, axis=-1, keepdims=True), m_prev[:, lane0:lane0 + 1])
        qk = _dot_nt(q * head_b[h], k)
        scores.append((qk * jnp.exp2(a_m - mu_h)).astype(BF16))
        mu = mu + mu_h * head_rows[h]
    s_cat = jnp.concatenate(scores, axis=1)
    v_cat = jnp.concatenate([v * hb for hb in head_b], axis=0)
    w_inter = jnp.exp2(m_prev - mu)
    num = _dot(s_cat, v_cat) + w_inter * _dot(q, s_ref[...].astype(BF16))
    den = _dot(s_cat, ones_cat) + w_inter * _dot(q, n_ref[...].astype(BF16))
    h_out = num / jnp.maximum(jnp.abs(den), jnp.exp2(-(cum + mu)))

    end_row = 0 if reverse else L - 1
    cum_end = cum[end_row:end_row + 1, :]
    log_end = cum_end - cum + li
    m_new = jnp.maximum(cum_end + m_prev, jnp.max(log_end, axis=0, keepdims=True))
    decay = jnp.exp2(cum_end + m_prev - m_new)
    w_end = jnp.exp2(log_end - m_new)
    k_t = k.astype(F32).T.astype(BF16)
    upd = _dot(k_t, jnp.concatenate([(v.astype(F32) * w_end).astype(BF16), w_end.astype(BF16)], axis=1))
    s_ref[...] = (s_ref[...] * decay + upd[:, :GROUP_W]) * block
    n_ref[...] = (n_ref[...] * decay + upd[:, GROUP_W:]) * block
    m_ref[...] = jnp.broadcast_to(m_new, m_ref.shape)
    return h_out


def _mlstm_kernel(qf_ref, kf_ref, vf_ref, gf_ref, qb_ref, kb_ref, vb_ref, gb_ref,
                  expand_ref, tri_ref, neg_ref, ones_ref, block_ref, heads_ref, s0_ref, n0_ref, m0_ref,
                  hf_ref, hb_ref, s_out, n_out, m_out, s_ref, n_ref, m_ref, *, chunk):
    @pl.when(pl.program_id(1) == 0)
    def _():
        s_ref[...] = s0_ref[0]
        n_ref[...] = n0_ref[0]
        m_ref[...] = m0_ref[0]

    sides = ((qf_ref, kf_ref, vf_ref, gf_ref, hf_ref), (qb_ref, kb_ref, vb_ref, gb_ref, hb_ref))
    for d, (q_ref, k_ref, v_ref, g_ref, h_ref) in enumerate(sides):
        h = _mlstm_chunk(q_ref[0], k_ref[0], v_ref[0], g_ref[0], expand_ref[d], tri_ref[d], neg_ref[d],
                         ones_ref[...], block_ref[...], heads_ref, s_ref.at[d], n_ref.at[d], m_ref.at[d],
                         reverse=bool(d), L=chunk)
        h_ref[0] = h.astype(h_ref.dtype)
    s_out[0] = s_ref[...]
    n_out[0] = n_ref[...]
    m_out[0] = m_ref[...]


def _mlstm_consts(L):
    lane = jnp.arange(LANES)[:, None]
    col = jnp.arange(2 * GROUP_W)[None, :]
    col_head = (col % GROUP_W) // HEAD_DIM
    expand = jnp.stack([(lane == 8 * d + 4 * (col // GROUP_W) + col_head) for d in range(2)]).astype(BF16)
    t_idx = jnp.arange(L)[:, None]
    s_idx = jnp.arange(L)[None, :]
    seen = jnp.stack([s_idx <= t_idx, s_idx >= t_idx])
    head = jnp.arange(GROUP_W) // HEAD_DIM
    ones_cat = (jnp.repeat(jnp.arange(N_HEADS), L)[:, None] == head[None, :]).astype(BF16)
    block = (head[:, None] == head[None, :]).astype(F32)
    head_rows = (jnp.arange(N_HEADS)[:, None, None] == head[None, None, :]).astype(F32)
    return expand, seen.astype(BF16), jnp.where(seen, 0.0, NEG_INF).astype(F32), ones_cat, block, head_rows


def _mlstm_scan(q, k, v, gates, state, consts):
    b, t, _ = q.shape
    L = MLSTM_L
    nc = t // L
    fwd = lambda wd: pl.BlockSpec((1, L, wd), lambda bi, j: (bi, j, 0))
    bwd = lambda wd: pl.BlockSpec((1, L, wd), lambda bi, j: (bi, nc - 1 - j, 0))
    st = lambda r: pl.BlockSpec((1, 2, r, GROUP_W), lambda bi, j: (bi, 0, 0, 0))
    tok_specs = [fwd(GROUP_W), fwd(GROUP_W), fwd(GROUP_W), fwd(LANES),
                 bwd(GROUP_W), bwd(GROUP_W), bwd(GROUP_W), bwd(LANES)]
    hf, hb, s_fin, n_fin, m_fin = pl.pallas_call(
        functools.partial(_mlstm_kernel, chunk=L),
        grid=(b, nc),
        in_specs=tok_specs + [_const_spec(c.shape) for c in consts] + [st(GROUP_W), st(GROUP_W), st(8)],
        out_specs=[fwd(GROUP_W), bwd(GROUP_W), st(GROUP_W), st(GROUP_W), st(8)],
        out_shape=[jax.ShapeDtypeStruct((b, t, GROUP_W), BF16), jax.ShapeDtypeStruct((b, t, GROUP_W), BF16),
                   jax.ShapeDtypeStruct((b, 2, GROUP_W, GROUP_W), F32),
                   jax.ShapeDtypeStruct((b, 2, GROUP_W, GROUP_W), F32),
                   jax.ShapeDtypeStruct((b, 2, 8, GROUP_W), F32)],
        scratch_shapes=[pltpu.VMEM((2, GROUP_W, GROUP_W), F32), pltpu.VMEM((2, GROUP_W, GROUP_W), F32),
                        pltpu.VMEM((2, 8, GROUP_W), F32)],
        compiler_params=_params(("parallel", "arbitrary"), "mlstm"),
        name="mlstm",
    )(q, k, v, gates, q, k, v, gates, *consts, *state)
    return hf, hb, (s_fin, n_fin, m_fin)


def _outproj_kernel(x_ref, ya_ref, yb_ref, hf_ref, hb_ref, og_ref, yd_ref, mod_ref, gg_ref, w_ref, o_ref, *, d_model):
    d = d_model
    yc = jax.nn.sigmoid(og_ref[0]) * (hf_ref[0].astype(F32) + hb_ref[0].astype(F32))
    parts = []
    for i, y in enumerate((ya_ref[0], yb_ref[0], yc, yd_ref[0])):
        parts.append((_rms_rows(y.astype(F32)) * gg_ref[i:i + 1, :]).astype(BF16))
    res = _dot(jnp.concatenate(parts, axis=1), w_ref[...])
    o_ref[0] = x_ref[0] + mod_ref[0][:, 2 * d:3 * d] * res


def _out_proj(x, ya, yb, hf, hb, o_gate, yd, mod, mod_row, gg, w, *, layer, tm):
    b, t, d = x.shape
    tokd = pl.BlockSpec((1, tm, d), lambda bi, j: (bi, j, 0))
    tokg = pl.BlockSpec((1, tm, GROUP_W), lambda bi, j: (bi, j, 0))
    return pl.pallas_call(
        functools.partial(_outproj_kernel, d_model=d),
        grid=(b, t // tm),
        in_specs=[tokd, tokg, tokg, tokg, tokg, tokg, tokg,
                  pl.BlockSpec((1, 1, 6 * d), lambda bi, j: (mod_row(bi), 0, 0)),
                  _const_spec(gg.shape), _layer_spec(w.shape, layer)],
        out_specs=tokd,
        out_shape=jax.ShapeDtypeStruct((b, t, d), F32),
        compiler_params=_params(("parallel", "parallel"), "out_proj"),
        name="out_proj",
    )(x, ya, yb, hf, hb, o_gate, yd, mod, gg, w)


def _ffn_kernel(*refs, d_model, d_ff, tm, final_norm):
    if final_norm:
        x_ref, xp_ref, xn_ref, mod_ref, g2_ref, wu_ref, cw_ref, wd_ref, gf_ref, o_ref = refs
    else:
        x_ref, xp_ref, xn_ref, mod_ref, g2_ref, wu_ref, cw_ref, wd_ref, o_ref = refs
    d = d_model
    j = pl.program_id(1)
    mod = mod_ref[0]
    g2 = g2_ref[...]

    def norm_mod(x):
        return _rms_rows(x) * g2 * (1.0 + mod[:, 4 * d:5 * d]) + mod[:, 3 * d:4 * d]

    keep_prev = (j > 0).astype(F32)
    keep_next = (j < pl.num_programs(1) - 1).astype(F32)
    x = x_ref[0]
    h_ext = jnp.concatenate([norm_mod(xp_ref[0]) * keep_prev, norm_mod(x), norm_mod(xn_ref[0]) * keep_next],
                            axis=0).astype(BF16)
    n_ext = tm + 2 * HALO

    u = _dot(h_ext, wu_ref[...])
    cw = cw_ref[...]
    u = (pltpu.roll(u, 1, 0) * cw[0:1] + u * cw[1:2] + pltpu.roll(u, n_ext - 1, 0) * cw[2:3] + cw[3:4])[HALO:HALO + tm]
    gate, val = u[:, :d_ff], u[:, d_ff:]
    act = (gate * jax.nn.sigmoid(gate) * val).astype(BF16)
    y = x + mod[:, 5 * d:6 * d] * _dot(act, wd_ref[...])
    if final_norm:
        y = _rms_rows(y) * gf_ref[...]
    o_ref[0] = y


def _conv_ffn(x, mod, mod_row, g2, wu, cw, wd, g_final, *, layer, tm):
    b, t, d = x.shape
    d_ff = wd.shape[1]
    final_norm = g_final is not None
    hb = tm // HALO
    last = t // HALO - 1
    row = pl.BlockSpec((1, d), lambda bi, j: (0, 0))
    in_specs = [pl.BlockSpec((1, tm, d), lambda bi, j: (bi, j, 0)),
                pl.BlockSpec((1, HALO, d), lambda bi, j: (bi, jnp.maximum(j * hb - 1, 0), 0)),
                pl.BlockSpec((1, HALO, d), lambda bi, j: (bi, jnp.minimum((j + 1) * hb, last), 0)),
                pl.BlockSpec((1, 1, 6 * d), lambda bi, j: (mod_row(bi), 0, 0)),
                row,
                _layer_spec(wu.shape, layer), _const_spec(cw.shape), _layer_spec(wd.shape, layer)]
    args = [x, x, x, mod, g2, wu, cw, wd]
    if final_norm:
        in_specs.append(row)
        args.append(g_final)
    return pl.pallas_call(
        functools.partial(_ffn_kernel, d_model=d, d_ff=d_ff, tm=tm, final_norm=final_norm),
        grid=(b, t // tm),
        in_specs=in_specs,
        out_specs=pl.BlockSpec((1, tm, d), lambda bi, j: (bi, j, 0)),
        out_shape=jax.ShapeDtypeStruct((b, t, d), F32),
        compiler_params=_params(("parallel", "parallel"), "conv_ffn"),
        name="conv_ffn",
    )(*args)


_Q_HEAD_ORDER = (0, 2, 1, 3)


def _permute_heads(w, axis):
    parts = [lax.slice_in_dim(w, h * HEAD_DIM, (h + 1) * HEAD_DIM, axis=axis) for h in _Q_HEAD_ORDER]
    return jnp.concatenate(parts, axis=axis)


def _layout_w_in(w):
    return jnp.pad(w.astype(BF16), ((0, 0), (0, 0), (0, -w.shape[2] % LANES)))


def _layout_w_out(w):
    w = w.astype(BF16)
    return jnp.concatenate([_permute_heads(w[:, 0:GROUP_W], 1), _permute_heads(w[:, GROUP_W:2 * GROUP_W], 1),
                            w[:, 2 * GROUP_W:]], axis=1)


def _rope_tables(t):
    pos = jnp.arange(t)
    row = (pos // GRID_W).astype(F32)
    col = (pos % GRID_W).astype(F32)
    n_freq = HEAD_DIM // 4
    freqs = ROPE_THETA ** (-jnp.arange(n_freq, dtype=F32) / n_freq)
    ang_r, ang_c = row[:, None] * freqs, col[:, None] * freqs
    cos = jnp.concatenate([jnp.cos(ang_r), jnp.cos(ang_r), jnp.cos(ang_c), jnp.cos(ang_c)], axis=1)
    sin = jnp.concatenate([-jnp.sin(ang_r), jnp.sin(ang_r), -jnp.sin(ang_c), jnp.sin(ang_c)], axis=1)
    reps = LANES // HEAD_DIM
    return jnp.tile(cos, (1, reps)), jnp.tile(sin, (1, reps))


def _pick_tile(t, pref):
    while t % pref:
        pref //= 2
    return pref


def kernel(x, c, ctx, c_ctx, w_mod, b_mod, g_norm1, g_norm2, w_in, a_q_gain, a_k_gain, b_sink, c_gate_bias,
           d_rel_bias, g_group, w_out, w_up, conv_w, conv_b, w_down, g_final):
    batch, t, d = x.shape
    n_ctx = ctx.shape[1]
    depth = w_in.shape[0]
    d_ff = w_down.shape[1]
    assert batch < 8 and d_ff % LANES == 0 and t % MLSTM_L == 0 and n_ctx % MLSTM_L == 0

    c_rows = jnp.zeros((8, d), F32).at[:batch].set(c).at[batch].set(c_ctx)
    mod_all = _modulation(c_rows, w_mod, b_mod).reshape(depth, 8, 1, 6 * d)
    lat_row = lambda bi: bi
    ctx_row = lambda bi: batch

    cos, sin = _rope_tables(t)
    pool = jnp.where(jnp.arange(GROUP_W)[:, None] // HEAD_DIM == jnp.arange(GROUP_W)[None, :] // HEAD_DIM,
                     1.0 / HEAD_DIM, 0.0).astype(BF16)
    tm_lat = _pick_tile(t, TOKEN_TILE)
    tm_ctx = _pick_tile(n_ctx, CTX_TILE)
    zero_state = (jnp.zeros((batch, 2, GROUP_W, GROUP_W), F32), jnp.zeros((batch, 2, GROUP_W, GROUP_W), F32),
                  jnp.zeros((batch, 2, 8, GROUP_W), F32))
    mlstm_consts = _mlstm_consts(MLSTM_L)
    w_in_b = _layout_w_in(w_in)
    w_out_b = _layout_w_out(w_out)
    w_up_b = w_up.astype(BF16)
    w_down_b = w_down.astype(BF16)

    for layer in range(depth):
        need_ctx = layer < depth - 1
        mod = mod_all[layer]
        g1 = g_norm1[layer][None]
        g2 = g_norm2[layer][None]
        gq = jnp.tile(a_q_gain[layer], N_HEADS)[None]
        gk = jnp.tile(a_k_gain[layer], 2)[None]
        gbias = jnp.pad(c_gate_bias[layer], (0, LANES - 4 * N_HEADS))[None]
        gg = g_group[layer].reshape(N_HEADS, GROUP_W)
        gg = jnp.concatenate([_permute_heads(gg[0:2], 1), gg[2:4]], axis=0)
        conv_l = jnp.concatenate([conv_w[layer], conv_b[layer][None],
                                  jnp.zeros((8 - 1 - conv_w.shape[1], 2 * d_ff), F32)], axis=0)
        sink = b_sink[layer]
        bias_tab = _neighbour_bias(d_rel_bias[layer])

        proj = functools.partial(_in_proj, g1=g1, w=w_in_b, pool=pool, gq=gq, gk=gk, gbias=gbias, layer=layer)
        (aqc, akc, avc, bqc, bkc, bvc, cqc, ckc, cvc, coc, cgc, dqc, dkc, dvc) = proj(
            ctx, mod, ctx_row, cos=cos, sin=sin, rope=False, tm=tm_ctx)
        (aq, ak, av, bq, bk, bv, cq, ck, cv, co, cg, dq, dk, dv) = proj(
            x, mod, lat_row, cos=cos, sin=sin, rope=True, tm=_pick_tile(t, 2 * TOKEN_TILE))

        score_bound = 1.02 * LOG2_E * HEAD_DIM ** 0.5 * jnp.max(jnp.abs(a_q_gain[layer])) * jnp.max(jnp.abs(a_k_gain[layer]))
        safe = (score_bound <= MAX_UNSHIFTED_LOG2_SCORE).astype(jnp.int32).reshape(1)
        ya = _attn_global(safe, aq, ak, av, akc, avc, tq=_pick_tile(t, ATTN_Q_TILE), tk=_pick_tile(t, ATTN_K_TILE))
        yb = _attn_window(sink, bq, bk, bv, bkc, bvc, tq=_pick_tile(t, WINDOW_Q_TILE))
        yd = _attn_neighbour(dq, dk, dv, dkc, dvc, bias_tab, rows_per_step=NEIGHBOUR_ROWS)

        hcf, hcb, ctx_state = _mlstm_scan(cqc, ckc, cvc, cgc, zero_state, mlstm_consts)
        hf, hb, _ = _mlstm_scan(cq, ck, cv, cg, ctx_state, mlstm_consts)

        x = _out_proj(x, ya, yb, hf, hb, co, yd, mod, lat_row, gg, w_out_b, layer=layer,
                      tm=_pick_tile(t, 2 * TOKEN_TILE))
        ffn = functools.partial(_conv_ffn, g2=g2, wu=w_up_b, cw=conv_l, wd=w_down_b, layer=layer)
        x = ffn(x, mod, lat_row, g_final=None if need_ctx else g_final[None], tm=tm_lat)
        if need_ctx:
            yac, ybc, ydc = _ctx_attention(sink, aqc, akc, avc, bqc, bkc, bvc, dqc, dkc, dvc)
            ctx = _out_proj(ctx, yac, ybc, hcf, hcb, coc, ydc, mod, ctx_row, gg, w_out_b, layer=layer, tm=tm_ctx)
            ctx = ffn(ctx, mod, ctx_row, g_final=None, tm=tm_ctx)
    return x
```

```python
import functools

import jax
import jax.numpy as jnp
from jax import lax
from jax.experimental import pallas as pl
from jax.experimental.pallas import tpu as pltpu

F32 = jnp.float32
BF16 = jnp.bfloat16

N_HEADS = 4
HEAD_DIM = 64
GROUP_W = N_HEADS * HEAD_DIM
KV_W = 2 * HEAD_DIM
GRID_W = 64
WINDOW = 128
NA_KH = 8
NA_KW = 16
ROPE_THETA = 10000.0
EPS = 1e-6
NEG_INF = -1e30
LANES = 128
MLSTM_L = 256
MLSTM_CHUNKS_PER_STEP = 2
HALO = 8
MIB = 1024 * 1024
LOG2_E = 1.4426950408889634
MAX_UNSHIFTED_LOG2_SCORE = 60.0


TOKEN_TILE = 512
CTX_TILE = 256
ATTN_Q_TILE = 1024
ATTN_K_TILE = 2048
WINDOW_Q_TILE = 256
WINDOW_TILES_PER_STEP = 8
NEIGHBOUR_ROWS = 16
VMEM_LIMIT_MIB = {"modulation": 40, "in_proj": 48, "attn_global": 48, "attn_window": 48, "attn_neighbour": 48,
                  "ctx_attention": 32, "mlstm": 48, "out_proj": 48, "conv_ffn": 56}


def _params(sem, name):
    return pltpu.CompilerParams(dimension_semantics=sem, vmem_limit_bytes=VMEM_LIMIT_MIB[name] * MIB)


def _dot(a, b):
    return jnp.dot(a, b, preferred_element_type=F32)


def _dot_nt(a, b):
    return lax.dot_general(a, b, (((1,), (1,)), ((), ())), preferred_element_type=F32)


def _const_spec(shape):
    return pl.BlockSpec(shape, lambda *_: (0,) * len(shape), pipeline_mode=pl.Buffered(1))


def _layer_spec(stacked_shape, layer):
    rest = tuple(stacked_shape[1:])
    return pl.BlockSpec((None,) + rest, lambda *_: (layer,) + (0,) * len(rest), pipeline_mode=pl.Buffered(1))


def _mod_kernel(c_ref, w_ref, b_ref, o_ref):
    c = c_ref[...]
    a = (c * jax.nn.sigmoid(c)).astype(BF16)
    o_ref[0] = _dot(a, w_ref[0].astype(BF16)) + b_ref[0]


def _modulation(c_rows, w_mod, b_mod):
    depth, d, n = w_mod.shape
    tn = 1536
    return pl.pallas_call(
        _mod_kernel,
        grid=(depth, n // tn),
        in_specs=[pl.BlockSpec((8, d), lambda l, j: (0, 0)),
                  pl.BlockSpec((1, d, tn), lambda l, j: (l, 0, j)),
                  pl.BlockSpec((1, 1, tn), lambda l, j: (l, 0, j))],
        out_specs=pl.BlockSpec((1, 8, tn), lambda l, j: (l, 0, j)),
        out_shape=jax.ShapeDtypeStruct((depth, 8, n), F32),
        compiler_params=_params(("parallel", "parallel"), "modulation"),
        name="modulation",
    )(c_rows, w_mod, b_mod.reshape(depth, 1, n))


def _rms_rows(x):
    return x * lax.rsqrt(jnp.mean(x * x, axis=-1, keepdims=True) + EPS)


def _head_rms(z, pool, gain):
    z2 = z * z
    hi = z2.astype(BF16)
    lo = (z2 - hi.astype(F32)).astype(BF16)
    ms = _dot(hi, pool) + _dot(lo, pool)
    return z * lax.rsqrt(ms + EPS) * gain


def _rope(z, cos, sin):
    lane = lax.broadcasted_iota(jnp.int32, z.shape, 1)
    first = (lane % 32) < 16
    partner = jnp.where(first, pltpu.roll(z, LANES - 16, 1), pltpu.roll(z, 16, 1))
    return z * cos + partner * sin


def _inproj_kernel(x_ref, mod_ref, g1_ref, w_ref, pool_ref, gq_ref, gk_ref, cos_ref, sin_ref, gb_ref,
                   aq_ref, ak_ref, av_ref, bq_ref, bk_ref, bv_ref, cq_ref, ck_ref, cv_ref, co_ref, cg_ref,
                   dq_ref, dk_ref, dv_ref, *, d_model, rope):
    d = d_model
    mod = mod_ref[0]
    xn = _rms_rows(x_ref[0]) * g1_ref[...]
    hb = (xn * (1.0 + mod[:, d:2 * d]) + mod[:, 0:d]).astype(BF16)
    projected = _dot(hb, w_ref[...])

    o_b = GROUP_W + 2 * KV_W
    o_c = 2 * o_b
    o_g = o_c + 4 * GROUP_W
    o_d = o_g + 4 * N_HEADS

    def cols(start, width=GROUP_W):
        return projected[:, start:start + width]

    def rot(z):
        if not rope:
            return z
        cos, sin = cos_ref[...], sin_ref[...]
        return jnp.concatenate([_rope(z[:, i:i + LANES], cos, sin) for i in range(0, z.shape[1], LANES)], axis=1)

    def gqa_order(z):
        a, b = z[:, :LANES], z[:, LANES:]
        left = lax.broadcasted_iota(jnp.int32, a.shape, 1) < HEAD_DIM
        return jnp.concatenate([jnp.where(left, a, pltpu.roll(b, HEAD_DIM, 1)),
                                jnp.where(left, pltpu.roll(a, HEAD_DIM, 1), b)], axis=1)

    scale = HEAD_DIM ** -0.5
    pool = pool_ref[...]
    aq_ref[0] = (gqa_order(rot(_head_rms(cols(0), pool, gq_ref[...]))) * (scale * LOG2_E)).astype(BF16)
    akv = cols(GROUP_W)
    ak_ref[0] = rot(_head_rms(akv[:, :KV_W], pool[:KV_W, :KV_W], gk_ref[...])).astype(BF16)
    ones = jnp.ones((akv.shape[0], LANES), F32)
    av_ref[0] = jnp.concatenate([akv[:, KV_W:], ones], axis=1).astype(BF16)
    bq_ref[0] = (gqa_order(rot(cols(o_b))) * (scale * LOG2_E)).astype(BF16)
    bkv = cols(o_b + GROUP_W)
    bk_ref[0] = rot(bkv[:, :KV_W]).astype(BF16)
    bv_ref[0] = jnp.concatenate([bkv[:, KV_W:], ones], axis=1).astype(BF16)
    cq_ref[0] = (cols(o_c) * scale).astype(BF16)
    ck_ref[0] = cols(o_c + GROUP_W).astype(BF16)
    cv_ref[0] = cols(o_c + 2 * GROUP_W).astype(BF16)
    co_ref[0] = cols(o_c + 3 * GROUP_W)
    dq_ref[0] = (cols(o_d) * scale).astype(BF16)
    dk_ref[0] = cols(o_d + GROUP_W).astype(BF16)
    dv_ref[0] = cols(o_d + 2 * GROUP_W).astype(BF16)
    gates = cols(o_g, LANES) + gb_ref[...]
    lane = lax.broadcasted_iota(jnp.int32, gates.shape, 1)
    gates = jnp.where((lane % 8) >= 4, _log_sigmoid(gates), gates) * LOG2_E
    cg_ref[0] = jnp.where(lane < 4 * N_HEADS, gates, 0.0)


def _in_proj(x, mod, mod_row, g1, w, pool, gq, gk, cos, sin, gbias, *, layer, rope, tm):
    b, t, d = x.shape
    widths = [(GROUP_W, BF16), (KV_W, BF16), (2 * LANES, BF16),
              (GROUP_W, BF16), (KV_W, BF16), (2 * LANES, BF16),
              (GROUP_W, BF16), (GROUP_W, BF16), (GROUP_W, BF16), (GROUP_W, F32), (LANES, F32),
              (GROUP_W, BF16), (GROUP_W, BF16), (GROUP_W, BF16)]
    row = lambda wd: pl.BlockSpec((1, wd), lambda bi, j: (0, 0))
    return pl.pallas_call(
        functools.partial(_inproj_kernel, d_model=d, rope=rope),
        grid=(b, t // tm),
        in_specs=[pl.BlockSpec((1, tm, d), lambda bi, j: (bi, j, 0)),
                  pl.BlockSpec((1, 1, 6 * d), lambda bi, j: (mod_row(bi), 0, 0)),
                  row(d),
                  _layer_spec(w.shape, layer),
                  _const_spec((GROUP_W, GROUP_W)),
                  row(GROUP_W), row(KV_W),
                  pl.BlockSpec((tm, LANES), lambda bi, j: (j, 0)),
                  pl.BlockSpec((tm, LANES), lambda bi, j: (j, 0)),
                  row(LANES)],
        out_specs=[pl.BlockSpec((1, tm, wd), lambda bi, j: (bi, j, 0)) for wd, _ in widths],
        out_shape=[jax.ShapeDtypeStruct((b, t, wd), dt) for wd, dt in widths],
        compiler_params=_params(("parallel", "parallel"), "in_proj"),
        name="in_proj",
    )(x, mod, g1, w, pool, gq, gk, cos, sin, gbias)


def _stack_gqa(q):
    qa, qb = q[:, :LANES], q[:, LANES:]
    left = lax.broadcasted_iota(jnp.int32, qa.shape, 1) < HEAD_DIM
    zero = jnp.zeros_like(qa)
    return jnp.concatenate([jnp.where(left, qa, zero), jnp.where(left, qb, zero),
                            jnp.where(left, zero, qa), jnp.where(left, zero, qb)], axis=0)


def _unstack_gqa(o, t):
    left = lax.broadcasted_iota(jnp.int32, (t, LANES), 1) < HEAD_DIM
    return jnp.concatenate([jnp.where(left, o[0:t], o[2 * t:3 * t]),
                            jnp.where(left, o[t:2 * t], o[3 * t:4 * t])], axis=1)


def _attn_a_kernel(safe_ref, q_ref, k_ref, v_ref, kc_ref, vc_ref, o_ref, qs_ref, m_ref, acc_ref, *, tq, tk, n_kb):
    qs_ref[...] = _stack_gqa(q_ref[0])
    heads = [slice(h * tq, (h + 1) * tq) for h in range(N_HEADS)]

    @pl.when(safe_ref[0] != 0)
    def _():
        for rows in heads:
            p = jnp.exp2(_dot_nt(qs_ref[rows], kc_ref[0]))
            acc_ref[rows] = _dot(p.astype(BF16), vc_ref[0])

        def body(kb, carry):
            start = pl.multiple_of(kb * tk, tk)
            for rows in heads:
                p = jnp.exp2(_dot_nt(qs_ref[rows], k_ref[0, pl.ds(start, tk), :]))
                acc_ref[rows] += _dot(p.astype(BF16), v_ref[0, pl.ds(start, tk), :])
            return carry

        lax.fori_loop(0, n_kb, body, 0)

    @pl.when(safe_ref[0] == 0)
    def _():
        for rows in heads:
            s = _dot_nt(qs_ref[rows], kc_ref[0])
            m0 = jnp.max(s, axis=-1, keepdims=True)
            m_ref[rows] = m0
            acc_ref[rows] = _dot(jnp.exp2(s - m0).astype(BF16), vc_ref[0])

        def body(kb, carry):
            start = pl.multiple_of(kb * tk, tk)
            for rows in heads:
                s = _dot_nt(qs_ref[rows], k_ref[0, pl.ds(start, tk), :])
                m_prev = m_ref[rows]
                m_new = jnp.maximum(m_prev, jnp.max(s, axis=-1, keepdims=True))
                p = jnp.exp2(s - m_new)
                acc_ref[rows] = jnp.exp2(m_prev - m_new) * acc_ref[rows] + _dot(
                    p.astype(BF16), v_ref[0, pl.ds(start, tk), :])
                m_ref[rows] = m_new
            return carry

        lax.fori_loop(0, n_kb, body, 0)

    acc = acc_ref[...]
    o_ref[0] = _unstack_gqa(acc[:, :LANES] / acc[:, LANES:], tq).astype(o_ref.dtype)


def _attn_global(safe, q, k, v, kc, vc, *, tq, tk):
    b, t, _ = q.shape
    n_ctx = kc.shape[1]
    per_b = lambda n, wd: pl.BlockSpec((1, n, wd), lambda bi, i: (bi, 0, 0))
    return pl.pallas_call(
        functools.partial(_attn_a_kernel, tq=tq, tk=tk, n_kb=t // tk),
        grid=(b, t // tq),
        in_specs=[pl.BlockSpec(memory_space=pltpu.SMEM),
                  pl.BlockSpec((1, tq, GROUP_W), lambda bi, i: (bi, i, 0)),
                  per_b(t, KV_W), per_b(t, 2 * LANES), per_b(n_ctx, KV_W), per_b(n_ctx, 2 * LANES)],
        out_specs=pl.BlockSpec((1, tq, GROUP_W), lambda bi, i: (bi, i, 0)),
        out_shape=jax.ShapeDtypeStruct((b, t, GROUP_W), BF16),
        scratch_shapes=[pltpu.VMEM((4 * tq, LANES), BF16), pltpu.VMEM((4 * tq, 1), F32),
                        pltpu.VMEM((4 * tq, 2 * LANES), F32)],
        compiler_params=_params(("parallel", "parallel"), "attn_global"),
        name="attn_global",
    )(safe, q, k, v, kc, vc)


def _sink_column(sink_ref, t):
    row = lax.broadcasted_iota(jnp.int32, (4 * t, 1), 0)
    col = jnp.full((4 * t, 1), sink_ref[3], F32)
    for h in (2, 1, 0):
        col = jnp.where(row < (h + 1) * t, sink_ref[h], col)
    return col


def _attn_b_kernel(sink_ref, q_ref, k_ref, v_ref, kc_ref, vc_ref, o_ref, *, tq, tiles_per_step, t_total):
    span = tq + 2 * WINDOW
    for sub in range(tiles_per_step):
        q0 = (pl.program_id(1) * tiles_per_step + sub) * tq
        ks = pl.multiple_of(jnp.clip(q0 - WINDOW, 0, t_total - span), WINDOW)
        qs = _stack_gqa(q_ref[0, sub * tq:(sub + 1) * tq, :])
        kw = k_ref[0, pl.ds(ks, span), :]
        vw = v_ref[0, pl.ds(ks, span), :]
        q_pos = q0 + lax.broadcasted_iota(jnp.int32, (tq, span), 0)
        k_pos = ks + lax.broadcasted_iota(jnp.int32, (tq, span), 1)
        in_window = jnp.abs(k_pos - q_pos) <= WINDOW
        outs = []
        for h in range(N_HEADS):
            qh = qs[h * tq:(h + 1) * tq]
            s = jnp.where(in_window, _dot_nt(qh, kw), NEG_INF)
            sc = _dot_nt(qh, kc_ref[0])
            sink = sink_ref[h] * LOG2_E
            m = jnp.maximum(jnp.maximum(jnp.max(s, axis=-1, keepdims=True), jnp.max(sc, axis=-1, keepdims=True)),
                            sink)
            acc = _dot(jnp.exp2(s - m).astype(BF16), vw) + _dot(jnp.exp2(sc - m).astype(BF16), vc_ref[0])
            outs.append(acc[:, :LANES] / (acc[:, LANES:] + jnp.exp2(sink - m)))
        o_ref[0, sub * tq:(sub + 1) * tq, :] = _unstack_gqa(jnp.concatenate(outs, axis=0), tq).astype(o_ref.dtype)


def _attn_window(sink, q, k, v, kc, vc, *, tq):
    b, t, _ = q.shape
    n_ctx = kc.shape[1]
    assert t >= tq + 2 * WINDOW
    tiles = WINDOW_TILES_PER_STEP if (t // tq) % WINDOW_TILES_PER_STEP == 0 else 1
    per_b = lambda n, wd: pl.BlockSpec((1, n, wd), lambda bi, i: (bi, 0, 0))
    return pl.pallas_call(
        functools.partial(_attn_b_kernel, tq=tq, tiles_per_step=tiles, t_total=t),
        grid=(b, t // (tq * tiles)),
        in_specs=[pl.BlockSpec(memory_space=pltpu.SMEM),
                  pl.BlockSpec((1, tq * tiles, GROUP_W), lambda bi, i: (bi, i, 0)),
                  per_b(t, KV_W), per_b(t, 2 * LANES), per_b(n_ctx, KV_W), per_b(n_ctx, 2 * LANES)],
        out_specs=pl.BlockSpec((1, tq * tiles, GROUP_W), lambda bi, i: (bi, i, 0)),
        out_shape=jax.ShapeDtypeStruct((b, t, GROUP_W), BF16),
        compiler_params=_params(("parallel", "parallel"), "attn_window"),
        name="attn_window",
    )(sink, q, k, v, kc, vc)


def _head_masks(shape):
    lane = lax.broadcasted_iota(jnp.int32, shape, 1)
    return [(lane >= h * HEAD_DIM) & (lane < (h + 1) * HEAD_DIM) for h in range(N_HEADS)]


def _stack_mha(q):
    zero = jnp.zeros_like(q)
    return jnp.concatenate([jnp.where(mk, q, zero) for mk in _head_masks(q.shape)], axis=0)


def _unstack_mha(o, t):
    masks = _head_masks((t, GROUP_W))
    out = jnp.where(masks[0], o[0:t], 0.0)
    for h in range(1, N_HEADS):
        out = jnp.where(masks[h], o[h * t:(h + 1) * t], out)
    return out


def _attn_d_kernel(q_ref, k_ref, v_ref, kc_ref, vc_ref, bias_ref, o_ref, *, rows_per_step, rows):
    kc = kc_ref[0]
    vc = vc_ref[0]
    span = NA_KH * GRID_W

    def body(rr, carry):
        r = pl.program_id(1) * rows_per_step + rr
        rs = jnp.clip(r - NA_KH // 2, 0, rows - NA_KH)
        k0 = pl.multiple_of(rs * GRID_W, GRID_W)
        q0 = pl.multiple_of(rr * GRID_W, GRID_W)
        qs = _stack_mha(q_ref[0, pl.ds(q0, GRID_W), :])
        s = _dot_nt(qs, k_ref[0, pl.ds(k0, span), :]) + bias_ref[r - rs]
        sc = _dot_nt(qs, kc)
        m = jnp.maximum(jnp.max(s, axis=-1, keepdims=True), jnp.max(sc, axis=-1, keepdims=True))
        p = jnp.exp(s - m)
        pc = jnp.exp(sc - m)
        l = jnp.sum(p, axis=-1, keepdims=True) + jnp.sum(pc, axis=-1, keepdims=True)
        o = _dot(p.astype(BF16), v_ref[0, pl.ds(k0, span), :]) + _dot(pc.astype(BF16), vc)
        o_ref[0, pl.ds(q0, GRID_W), :] = _unstack_mha(o / l, GRID_W).astype(o_ref.dtype)
        return carry

    lax.fori_loop(0, rows_per_step, body, 0, unroll=True)


def _attn_neighbour(q, k, v, kc, vc, bias, *, rows_per_step):
    b, t, _ = q.shape
    n_ctx = kc.shape[1]
    rows = t // GRID_W
    assert rows >= NA_KH and rows % rows_per_step == 0
    tq = rows_per_step * GRID_W
    per_b = lambda n: pl.BlockSpec((1, n, GROUP_W), lambda bi, i: (bi, 0, 0))
    return pl.pallas_call(
        functools.partial(_attn_d_kernel, rows_per_step=rows_per_step, rows=rows),
        grid=(b, t // tq),
        in_specs=[pl.BlockSpec((1, tq, GROUP_W), lambda bi, i: (bi, i, 0)),
                  per_b(t), per_b(t), per_b(n_ctx), per_b(n_ctx),
                  _const_spec(bias.shape)],
        out_specs=pl.BlockSpec((1, tq, GROUP_W), lambda bi, i: (bi, i, 0)),
        out_shape=jax.ShapeDtypeStruct((b, t, GROUP_W), BF16),
        compiler_params=_params(("parallel", "parallel"), "attn_neighbour"),
        name="attn_neighbour",
    )(q, k, v, kc, vc, bias)


def _neighbour_bias(rpb):
    w = jnp.arange(GRID_W)
    cs = jnp.clip(w - NA_KW // 2, 0, GRID_W - NA_KW)
    col = jnp.arange(GRID_W)
    valid = (col[None, :] >= cs[:, None]) & (col[None, :] < cs[:, None] + NA_KW)
    dc = jnp.clip(col[None, :] - w[:, None] + (NA_KW - 1), 0, 2 * NA_KW - 2)
    pick_c = (jnp.arange(2 * NA_KW - 1)[:, None, None] == dc[None]).astype(F32)
    by_col = jnp.einsum('hrc,cwk->hwrk', rpb.astype(F32), pick_c, precision=lax.Precision.HIGHEST)
    by_col = jnp.where(valid[None, :, None, :], by_col, NEG_INF)
    flat = by_col.reshape(N_HEADS * GRID_W, (2 * NA_KH - 1) * GRID_W)
    return jnp.stack([flat[:, (NA_KH - 1 - o) * GRID_W:(2 * NA_KH - 1 - o) * GRID_W] for o in range(NA_KH)])


def _ctx_attn_kernel(sink_ref, aq_ref, ak_ref, av_ref, bq_ref, bk_ref, bv_ref, dq_ref, dk_ref, dv_ref,
                     ya_ref, yb_ref, yd_ref, *, n):
    s = _dot_nt(_stack_gqa(aq_ref[0]), ak_ref[0])
    acc = _dot(jnp.exp2(s - jnp.max(s, axis=-1, keepdims=True)).astype(BF16), av_ref[0])
    ya_ref[0] = _unstack_gqa(acc[:, :LANES] / acc[:, LANES:], n).astype(ya_ref.dtype)
    s = _dot_nt(_stack_gqa(bq_ref[0]), bk_ref[0])
    sink = _sink_column(sink_ref, n) * LOG2_E
    m = jnp.maximum(jnp.max(s, axis=-1, keepdims=True), sink)
    acc = _dot(jnp.exp2(s - m).astype(BF16), bv_ref[0])
    yb_ref[0] = _unstack_gqa(acc[:, :LANES] / (acc[:, LANES:] + jnp.exp2(sink - m)), n).astype(yb_ref.dtype)
    s = _dot_nt(_stack_mha(dq_ref[0]), dk_ref[0])
    p = jnp.exp(s - jnp.max(s, axis=-1, keepdims=True))
    o = _dot(p.astype(BF16), dv_ref[0]) / jnp.sum(p, axis=-1, keepdims=True)
    yd_ref[0] = _unstack_mha(o, n).astype(yd_ref.dtype)


def _ctx_attention(sink, aq, ak, av, bq, bk, bv, dq, dk, dv):
    b, n, _ = aq.shape
    spec = lambda wd: pl.BlockSpec((1, n, wd), lambda bi: (bi, 0, 0))
    return pl.pallas_call(
        functools.partial(_ctx_attn_kernel, n=n),
        grid=(b,),
        in_specs=[pl.BlockSpec(memory_space=pltpu.SMEM),
                  spec(GROUP_W), spec(KV_W), spec(2 * LANES), spec(GROUP_W), spec(KV_W), spec(2 * LANES),
                  spec(GROUP_W), spec(GROUP_W), spec(GROUP_W)],
        out_specs=[spec(GROUP_W)] * 3,
        out_shape=[jax.ShapeDtypeStruct((b, n, GROUP_W), BF16)] * 3,
        compiler_params=_params(("parallel",), "ctx_attention"),
        name="ctx_attention",
    )(sink, aq, ak, av, bq, bk, bv, dq, dk, dv)


def _log_sigmoid(x):
    return jnp.minimum(x, 0.0) - jnp.log1p(jnp.exp(-jnp.abs(x)))


def _split3(x):
    x1 = x.astype(BF16)
    r1 = x - x1.astype(F32)
    x2 = r1.astype(BF16)
    return x1, x2, (r1 - x2.astype(F32)).astype(BF16)


def _exact_dot_01(sel, x):
    return sum(_dot(sel, piece) for piece in _split3(x))


def _exact_dot_01_rhs(x, sel):
    return sum(_dot(piece, sel) for piece in _split3(x))


def _mlstm_chunk(q, k, v, gates, expand, tri, neg_mask, ones_cat, block, head_rows, s_ref, n_ref, m_ref, *,
                 reverse, L):
    ge = _exact_dot_01_rhs(gates, expand)
    li = ge[:, :GROUP_W]
    cum = _exact_dot_01(tri, ge[:, GROUP_W:])
    a_t = (li - cum).T
    m_prev = m_ref[0:1, :]
    head_b = [head_rows[h].astype(BF16) for h in range(N_HEADS)]
    mu = jnp.zeros((L, GROUP_W), F32)
    scores = []
    for h in range(N_HEADS):
        lane0 = h * HEAD_DIM
        a_m = a_t[lane0:lane0 + 1, :] + neg_mask
        mu_h = jnp.maximum(jnp.max(a_m, axis=-1, keepdims=True), m_prev[:, lane0:lane0 + 1])
        qk = _dot_nt(q * head_b[h], k)
        scores.append((qk * jnp.exp2(a_m - mu_h)).astype(BF16))
        mu = mu + mu_h * head_rows[h]
    s_cat = jnp.concatenate(scores, axis=1)
    v_cat = jnp.concatenate([v * hb for hb in head_b], axis=0)
    w_inter = jnp.exp2(m_prev - mu)
    num = _dot(s_cat, v_cat) + w_inter * _dot(q, s_ref[...].astype(BF16))
    den = _dot(s_cat, ones_cat) + w_inter * _dot(q, n_ref[...].astype(BF16))
    h_out = num / jnp.maximum(jnp.abs(den), jnp.exp2(-(cum + mu)))

    end_row = 0 if reverse else L - 1
    cum_end = cum[end_row:end_row + 1, :]
    log_end = cum_end - cum + li
    m_new = jnp.maximum(cum_end + m_prev, jnp.max(log_end, axis=0, keepdims=True))
    decay = jnp.exp2(cum_end + m_prev - m_new)
    w_end = jnp.exp2(log_end - m_new)
    k_t = k.astype(F32).T.astype(BF16)
    upd = _dot(k_t, jnp.concatenate([(v.astype(F32) * w_end).astype(BF16), w_end.astype(BF16)], axis=1))
    s_ref[...] = (s_ref[...] * decay + upd[:, :GROUP_W]) * block
    n_ref[...] = (n_ref[...] * decay + upd[:, GROUP_W:]) * block
    m_ref[...] = jnp.broadcast_to(m_new, m_ref.shape)
    return h_out


def _mlstm_kernel(qf_ref, kf_ref, vf_ref, gf_ref, qb_ref, kb_ref, vb_ref, gb_ref,
                  expand_ref, tri_ref, neg_ref, ones_ref, block_ref, heads_ref, s0_ref, n0_ref, m0_ref,
                  hf_ref, hb_ref, s_out, n_out, m_out, s_ref, n_ref, m_ref, *, chunk, chunks_per_step):
    @pl.when(pl.program_id(1) == 0)
    def _():
        s_ref[...] = s0_ref[0]
        n_ref[...] = n0_ref[0]
        m_ref[...] = m0_ref[0]

    sides = ((qf_ref, kf_ref, vf_ref, gf_ref, hf_ref), (qb_ref, kb_ref, vb_ref, gb_ref, hb_ref))
    for c in range(chunks_per_step):
        for d, (q_ref, k_ref, v_ref, g_ref, h_ref) in enumerate(sides):
            rows = pl.ds((chunks_per_step - 1 - c if d else c) * chunk, chunk)
            h = _mlstm_chunk(q_ref[0, rows, :], k_ref[0, rows, :], v_ref[0, rows, :], g_ref[0, rows, :],
                             expand_ref[d], tri_ref[d], neg_ref[d], ones_ref[...], block_ref[...], heads_ref,
                             s_ref.at[d], n_ref.at[d], m_ref.at[d], reverse=bool(d), L=chunk)
            h_ref[0, rows, :] = h.astype(h_ref.dtype)
    s_out[0] = s_ref[...]
    n_out[0] = n_ref[...]
    m_out[0] = m_ref[...]


def _mlstm_consts(L):
    lane = jnp.arange(LANES)[:, None]
    col = jnp.arange(2 * GROUP_W)[None, :]
    col_head = (col % GROUP_W) // HEAD_DIM
    expand = jnp.stack([(lane == 8 * d + 4 * (col // GROUP_W) + col_head) for d in range(2)]).astype(BF16)
    t_idx = jnp.arange(L)[:, None]
    s_idx = jnp.arange(L)[None, :]
    seen = jnp.stack([s_idx <= t_idx, s_idx >= t_idx])
    head = jnp.arange(GROUP_W) // HEAD_DIM
    ones_cat = (jnp.repeat(jnp.arange(N_HEADS), L)[:, None] == head[None, :]).astype(BF16)
    block = (head[:, None] == head[None, :]).astype(F32)
    head_rows = (jnp.arange(N_HEADS)[:, None, None] == head[None, None, :]).astype(F32)
    return expand, seen.astype(BF16), jnp.where(seen, 0.0, NEG_INF).astype(F32), ones_cat, block, head_rows


def _mlstm_scan(q, k, v, gates, state, consts):
    b, t, _ = q.shape
    L = MLSTM_L
    per_step = MLSTM_CHUNKS_PER_STEP if (t // L) % MLSTM_CHUNKS_PER_STEP == 0 else 1
    nc = t // (L * per_step)
    fwd = lambda wd: pl.BlockSpec((1, L * per_step, wd), lambda bi, j: (bi, j, 0))
    bwd = lambda wd: pl.BlockSpec((1, L * per_step, wd), lambda bi, j: (bi, nc - 1 - j, 0))
    st = lambda r: pl.BlockSpec((1, 2, r, GROUP_W), lambda bi, j: (bi, 0, 0, 0))
    tok_specs = [fwd(GROUP_W), fwd(GROUP_W), fwd(GROUP_W), fwd(LANES),
                 bwd(GROUP_W), bwd(GROUP_W), bwd(GROUP_W), bwd(LANES)]
    hf, hb, s_fin, n_fin, m_fin = pl.pallas_call(
        functools.partial(_mlstm_kernel, chunk=L, chunks_per_step=per_step),
        grid=(b, nc),
        in_specs=tok_specs + [_const_spec(c.shape) for c in consts] + [st(GROUP_W), st(GROUP_W), st(8)],
        out_specs=[fwd(GROUP_W), bwd(GROUP_W), st(GROUP_W), st(GROUP_W), st(8)],
        out_shape=[jax.ShapeDtypeStruct((b, t, GROUP_W), BF16), jax.ShapeDtypeStruct((b, t, GROUP_W), BF16),
                   jax.ShapeDtypeStruct((b, 2, GROUP_W, GROUP_W), F32),
                   jax.ShapeDtypeStruct((b, 2, GROUP_W, GROUP_W), F32),
                   jax.ShapeDtypeStruct((b, 2, 8, GROUP_W), F32)],
        scratch_shapes=[pltpu.VMEM((2, GROUP_W, GROUP_W), F32), pltpu.VMEM((2, GROUP_W, GROUP_W), F32),
                        pltpu.VMEM((2, 8, GROUP_W), F32)],
        compiler_params=_params(("parallel", "arbitrary"), "mlstm"),
        name="mlstm",
    )(q, k, v, gates, q, k, v, gates, *consts, *state)
    return hf, hb, (s_fin, n_fin, m_fin)


def _outproj_kernel(x_ref, ya_ref, yb_ref, hf_ref, hb_ref, og_ref, yd_ref, mod_ref, gg_ref, w_ref, o_ref, *, d_model):
    d = d_model
    yc = jax.nn.sigmoid(og_ref[0]) * (hf_ref[0].astype(F32) + hb_ref[0].astype(F32))
    parts = []
    for i, y in enumerate((ya_ref[0], yb_ref[0], yc, yd_ref[0])):
        parts.append((_rms_rows(y.astype(F32)) * gg_ref[i:i + 1, :]).astype(BF16))
    res = _dot(jnp.concatenate(parts, axis=1), w_ref[...])
    o_ref[0] = x_ref[0] + mod_ref[0][:, 2 * d:3 * d] * res


def _out_proj(x, ya, yb, hf, hb, o_gate, yd, mod, mod_row, gg, w, *, layer, tm):
    b, t, d = x.shape
    tokd = pl.BlockSpec((1, tm, d), lambda bi, j: (bi, j, 0))
    tokg = pl.BlockSpec((1, tm, GROUP_W), lambda bi, j: (bi, j, 0))
    return pl.pallas_call(
        functools.partial(_outproj_kernel, d_model=d),
        grid=(b, t // tm),
        in_specs=[tokd, tokg, tokg, tokg, tokg, tokg, tokg,
                  pl.BlockSpec((1, 1, 6 * d), lambda bi, j: (mod_row(bi), 0, 0)),
                  _const_spec(gg.shape), _layer_spec(w.shape, layer)],
        out_specs=tokd,
        out_shape=jax.ShapeDtypeStruct((b, t, d), F32),
        compiler_params=_params(("parallel", "parallel"), "out_proj"),
        name="out_proj",
    )(x, ya, yb, hf, hb, o_gate, yd, mod, gg, w)


def _ffn_kernel(*refs, d_model, d_ff, tm, final_norm):
    if final_norm:
        x_ref, xp_ref, xn_ref, mod_ref, g2_ref, wu_ref, cw_ref, wd_ref, gf_ref, o_ref = refs
    else:
        x_ref, xp_ref, xn_ref, mod_ref, g2_ref, wu_ref, cw_ref, wd_ref, o_ref = refs
    d = d_model
    j = pl.program_id(1)
    mod = mod_ref[0]
    g2 = g2_ref[...]

    def norm_mod(x):
        return _rms_rows(x) * g2 * (1.0 + mod[:, 4 * d:5 * d]) + mod[:, 3 * d:4 * d]

    keep_prev = (j > 0).astype(F32)
    keep_next = (j < pl.num_programs(1) - 1).astype(F32)
    x = x_ref[0]
    h_ext = jnp.concatenate([norm_mod(xp_ref[0]) * keep_prev, norm_mod(x), norm_mod(xn_ref[0]) * keep_next],
                            axis=0).astype(BF16)
    n_ext = tm + 2 * HALO

    u = _dot(h_ext, wu_ref[...])
    cw = cw_ref[...]
    u = (pltpu.roll(u, 1, 0) * cw[0:1] + u * cw[1:2] + pltpu.roll(u, n_ext - 1, 0) * cw[2:3] + cw[3:4])[HALO:HALO + tm]
    gate, val = u[:, :d_ff], u[:, d_ff:]
    act = (gate * jax.nn.sigmoid(gate) * val).astype(BF16)
    y = x + mod[:, 5 * d:6 * d] * _dot(act, wd_ref[...])
    if final_norm:
        y = _rms_rows(y) * gf_ref[...]
    o_ref[0] = y


def _conv_ffn(x, mod, mod_row, g2, wu, cw, wd, g_final, *, layer, tm):
    b, t, d = x.shape
    d_ff = wd.shape[1]
    final_norm = g_final is not None
    hb = tm // HALO
    last = t // HALO - 1
    row = pl.BlockSpec((1, d), lambda bi, j: (0, 0))
    in_specs = [pl.BlockSpec((1, tm, d), lambda bi, j: (bi, j, 0)),
                pl.BlockSpec((1, HALO, d), lambda bi, j: (bi, jnp.maximum(j * hb - 1, 0), 0)),
                pl.BlockSpec((1, HALO, d), lambda bi, j: (bi, jnp.minimum((j + 1) * hb, last), 0)),
                pl.BlockSpec((1, 1, 6 * d), lambda bi, j: (mod_row(bi), 0, 0)),
                row,
                _layer_spec(wu.shape, layer), _const_spec(cw.shape), _layer_spec(wd.shape, layer)]
    args = [x, x, x, mod, g2, wu, cw, wd]
    if final_norm:
        in_specs.append(row)
        args.append(g_final)
    return pl.pallas_call(
        functools.partial(_ffn_kernel, d_model=d, d_ff=d_ff, tm=tm, final_norm=final_norm),
        grid=(b, t // tm),
        in_specs=in_specs,
        out_specs=pl.BlockSpec((1, tm, d), lambda bi, j: (bi, j, 0)),
        out_shape=jax.ShapeDtypeStruct((b, t, d), F32),
        compiler_params=_params(("parallel", "parallel"), "conv_ffn"),
        name="conv_ffn",
    )(*args)


_Q_HEAD_ORDER = (0, 2, 1, 3)


def _permute_heads(w, axis):
    parts = [lax.slice_in_dim(w, h * HEAD_DIM, (h + 1) * HEAD_DIM, axis=axis) for h in _Q_HEAD_ORDER]
    return jnp.concatenate(parts, axis=axis)


def _layout_w_in(w):
    return w.astype(BF16)


def _layout_w_out(w):
    w = w.astype(BF16)
    return jnp.concatenate([_permute_heads(w[:, 0:GROUP_W], 1), _permute_heads(w[:, GROUP_W:2 * GROUP_W], 1),
                            w[:, 2 * GROUP_W:]], axis=1)


def _rope_tables(t):
    pos = jnp.arange(t)
    row = (pos // GRID_W).astype(F32)
    col = (pos % GRID_W).astype(F32)
    n_freq = HEAD_DIM // 4
    freqs = ROPE_THETA ** (-jnp.arange(n_freq, dtype=F32) / n_freq)
    ang_r, ang_c = row[:, None] * freqs, col[:, None] * freqs
    cos = jnp.concatenate([jnp.cos(ang_r), jnp.cos(ang_r), jnp.cos(ang_c), jnp.cos(ang_c)], axis=1)
    sin = jnp.concatenate([-jnp.sin(ang_r), jnp.sin(ang_r), -jnp.sin(ang_c), jnp.sin(ang_c)], axis=1)
    reps = LANES // HEAD_DIM
    return jnp.tile(cos, (1, reps)), jnp.tile(sin, (1, reps))


def _pick_tile(t, pref):
    while t % pref:
        pref //= 2
    return pref


def kernel(x, c, ctx, c_ctx, w_mod, b_mod, g_norm1, g_norm2, w_in, a_q_gain, a_k_gain, b_sink, c_gate_bias,
           d_rel_bias, g_group, w_out, w_up, conv_w, conv_b, w_down, g_final):
    batch, t, d = x.shape
    n_ctx = ctx.shape[1]
    depth = w_in.shape[0]
    d_ff = w_down.shape[1]
    assert batch < 8 and d_ff % LANES == 0 and t % MLSTM_L == 0 and n_ctx % MLSTM_L == 0

    c_rows = jnp.zeros((8, d), F32).at[:batch].set(c).at[batch].set(c_ctx)
    mod_all = _modulation(c_rows, w_mod, b_mod).reshape(depth, 8, 1, 6 * d)
    lat_row = lambda bi: bi
    ctx_row = lambda bi: batch

    cos, sin = _rope_tables(t)
    pool = jnp.where(jnp.arange(GROUP_W)[:, None] // HEAD_DIM == jnp.arange(GROUP_W)[None, :] // HEAD_DIM,
                     1.0 / HEAD_DIM, 0.0).astype(BF16)
    tm_lat = _pick_tile(t, TOKEN_TILE)
    tm_ctx = _pick_tile(n_ctx, CTX_TILE)
    zero_state = (jnp.zeros((batch, 2, GROUP_W, GROUP_W), F32), jnp.zeros((batch, 2, GROUP_W, GROUP_W), F32),
                  jnp.zeros((batch, 2, 8, GROUP_W), F32))
    mlstm_consts = _mlstm_consts(MLSTM_L)
    w_in_b = _layout_w_in(w_in)
    w_out_b = _layout_w_out(w_out)
    w_up_b = w_up.astype(BF16)
    w_down_b = w_down.astype(BF16)

    for layer in range(depth):
        need_ctx = layer < depth - 1
        mod = mod_all[layer]
        g1 = g_norm1[layer][None]
        g2 = g_norm2[layer][None]
        gq = jnp.tile(a_q_gain[layer], N_HEADS)[None]
        gk = jnp.tile(a_k_gain[layer], 2)[None]
        gbias = jnp.pad(c_gate_bias[layer], (0, LANES - 4 * N_HEADS))[None]
        gg = g_group[layer].reshape(N_HEADS, GROUP_W)
        gg = jnp.concatenate([_permute_heads(gg[0:2], 1), gg[2:4]], axis=0)
        conv_l = jnp.concatenate([conv_w[layer], conv_b[layer][None],
                                  jnp.zeros((8 - 1 - conv_w.shape[1], 2 * d_ff), F32)], axis=0)
        sink = b_sink[layer]
        bias_tab = _neighbour_bias(d_rel_bias[layer])

        proj = functools.partial(_in_proj, g1=g1, w=w_in_b, pool=pool, gq=gq, gk=gk, gbias=gbias, layer=layer)
        (aqc, akc, avc, bqc, bkc, bvc, cqc, ckc, cvc, coc, cgc, dqc, dkc, dvc) = proj(
            ctx, mod, ctx_row, cos=cos, sin=sin, rope=False, tm=tm_ctx)
        (aq, ak, av, bq, bk, bv, cq, ck, cv, co, cg, dq, dk, dv) = proj(
            x, mod, lat_row, cos=cos, sin=sin, rope=True, tm=_pick_tile(t, 2 * TOKEN_TILE))

        score_bound = 1.02 * LOG2_E * HEAD_DIM ** 0.5 * jnp.max(jnp.abs(a_q_gain[layer])) * jnp.max(jnp.abs(a_k_gain[layer]))
        safe = (score_bound <= MAX_UNSHIFTED_LOG2_SCORE).astype(jnp.int32).reshape(1)
        ya = _attn_global(safe, aq, ak, av, akc, avc, tq=_pick_tile(t, ATTN_Q_TILE), tk=_pick_tile(t, ATTN_K_TILE))
        yb = _attn_window(sink, bq, bk, bv, bkc, bvc, tq=_pick_tile(t, WINDOW_Q_TILE))
        yd = _attn_neighbour(dq, dk, dv, dkc, dvc, bias_tab, rows_per_step=NEIGHBOUR_ROWS)

        hcf, hcb, ctx_state = _mlstm_scan(cqc, ckc, cvc, cgc, zero_state, mlstm_consts)
        hf, hb, _ = _mlstm_scan(cq, ck, cv, cg, ctx_state, mlstm_consts)

        x = _out_proj(x, ya, yb, hf, hb, co, yd, mod, lat_row, gg, w_out_b, layer=layer,
                      tm=_pick_tile(t, 2 * TOKEN_TILE))
        ffn = functools.partial(_conv_ffn, g2=g2, wu=w_up_b, cw=conv_l, wd=w_down_b, layer=layer)
        x = ffn(x, mod, lat_row, g_final=None if need_ctx else g_final[None], tm=tm_lat)
        if need_ctx:
            yac, ybc, ydc = _ctx_attention(sink, aqc, akc, avc, bqc, bkc, bvc, dqc, dkc, dvc)
            ctx = _out_proj(ctx, yac, ybc, hcf, hcb, coc, ydc, mod, ctx_row, gg, w_out_b, layer=layer, tm=tm_ctx)
            ctx = ffn(ctx, mod, ctx_row, g_final=None, tm=tm_ctx)
    return x
```

```python
import functools

import jax
import jax.numpy as jnp
from jax import lax
from jax.experimental import pallas as pl
from jax.experimental.pallas import tpu as pltpu

F32 = jnp.float32
BF16 = jnp.bfloat16

N_HEADS = 4
HEAD_DIM = 64
GROUP_W = N_HEADS * HEAD_DIM
KV_W = 2 * HEAD_DIM
GRID_W = 64
WINDOW = 128
NA_KH = 8
NA_KW = 16
ROPE_THETA = 10000.0
EPS = 1e-6
NEG_INF = -1e30
LANES = 128
MLSTM_L = 256
MLSTM_CHUNKS_PER_STEP = 2
HALO = 8
MIB = 1024 * 1024
LOG2_E = 1.4426950408889634
MAX_UNSHIFTED_LOG2_SCORE = 60.0


TOKEN_TILE = 512
CTX_TILE = 256
ATTN_Q_TILE = 1024
ATTN_K_TILE = 2048
WINDOW_Q_TILE = 256
WINDOW_TILES_PER_STEP = 8
NEIGHBOUR_ROWS = 16
VMEM_LIMIT_MIB = {"modulation": 40, "in_proj": 48, "attn_global": 48, "attn_window": 48, "attn_neighbour": 48,
                  "ctx_attention": 32, "mlstm": 48, "out_proj": 48, "conv_ffn": 56}


def _params(sem, name):
    return pltpu.CompilerParams(dimension_semantics=sem, vmem_limit_bytes=VMEM_LIMIT_MIB[name] * MIB)


def _dot(a, b):
    return jnp.dot(a, b, preferred_element_type=F32)


def _dot_nt(a, b):
    return lax.dot_general(a, b, (((1,), (1,)), ((), ())), preferred_element_type=F32)


def _const_spec(shape):
    return pl.BlockSpec(shape, lambda *_: (0,) * len(shape), pipeline_mode=pl.Buffered(1))


def _layer_spec(stacked_shape, layer):
    rest = tuple(stacked_shape[1:])
    return pl.BlockSpec((None,) + rest, lambda *_: (layer,) + (0,) * len(rest), pipeline_mode=pl.Buffered(1))


def _mod_kernel(c_ref, w_ref, b_ref, o_ref):
    c = c_ref[...]
    a = (c * jax.nn.sigmoid(c)).astype(BF16)
    o_ref[0] = _dot(a, w_ref[0].astype(BF16)) + b_ref[0]


def _modulation(c_rows, w_mod, b_mod):
    depth, d, n = w_mod.shape
    tn = 1536
    return pl.pallas_call(
        _mod_kernel,
        grid=(depth, n // tn),
        in_specs=[pl.BlockSpec((8, d), lambda l, j: (0, 0)),
                  pl.BlockSpec((1, d, tn), lambda l, j: (l, 0, j)),
                  pl.BlockSpec((1, 1, tn), lambda l, j: (l, 0, j))],
        out_specs=pl.BlockSpec((1, 8, tn), lambda l, j: (l, 0, j)),
        out_shape=jax.ShapeDtypeStruct((depth, 8, n), F32),
        compiler_params=_params(("parallel", "parallel"), "modulation"),
        name="modulation",
    )(c_rows, w_mod, b_mod.reshape(depth, 1, n))


def _rms_rows(x):
    return x * lax.rsqrt(jnp.mean(x * x, axis=-1, keepdims=True) + EPS)


def _head_rms(z, pool, gain):
    z2 = z * z
    hi = z2.astype(BF16)
    lo = (z2 - hi.astype(F32)).astype(BF16)
    ms = _dot(hi, pool) + _dot(lo, pool)
    return z * lax.rsqrt(ms + EPS) * gain


def _rope(z, cos, sin):
    lane = lax.broadcasted_iota(jnp.int32, z.shape, 1)
    first = (lane % 32) < 16
    partner = jnp.where(first, pltpu.roll(z, LANES - 16, 1), pltpu.roll(z, 16, 1))
    return z * cos + partner * sin


def _inproj_kernel(*refs, d_model, rope, sub_rows):
    x_ref, mod_ref, g1_ref, w_ref, pool_ref, gq_ref, gk_ref, cos_ref, sin_ref, gb_ref = refs[:10]
    for sub in range(x_ref.shape[1] // sub_rows):
        rows = pl.ds(sub * sub_rows, sub_rows)
        _inproj_tile(x_ref.at[:, rows, :], mod_ref, g1_ref, w_ref, pool_ref, gq_ref, gk_ref,
                     cos_ref.at[rows, :], sin_ref.at[rows, :], gb_ref, *[o.at[:, rows, :] for o in refs[10:]],
                     d_model=d_model, rope=rope)


def _inproj_tile(x_ref, mod_ref, g1_ref, w_ref, pool_ref, gq_ref, gk_ref, cos_ref, sin_ref, gb_ref,
                 aq_ref, ak_ref, av_ref, bq_ref, bk_ref, bv_ref, cq_ref, ck_ref, cv_ref, co_ref, cg_ref,
                 dq_ref, dk_ref, dv_ref, *, d_model, rope):
    d = d_model
    mod = mod_ref[0]
    xn = _rms_rows(x_ref[0]) * g1_ref[...]
    hb = (xn * (1.0 + mod[:, d:2 * d]) + mod[:, 0:d]).astype(BF16)
    projected = _dot(hb, w_ref[...])

    o_b = GROUP_W + 2 * KV_W
    o_c = 2 * o_b
    o_g = o_c + 4 * GROUP_W
    o_d = o_g + 4 * N_HEADS

    def cols(start, width=GROUP_W):
        return projected[:, start:start + width]

    def rot(z):
        if not rope:
            return z
        cos, sin = cos_ref[...], sin_ref[...]
        return jnp.concatenate([_rope(z[:, i:i + LANES], cos, sin) for i in range(0, z.shape[1], LANES)], axis=1)

    def gqa_order(z):
        a, b = z[:, :LANES], z[:, LANES:]
        left = lax.broadcasted_iota(jnp.int32, a.shape, 1) < HEAD_DIM
        return jnp.concatenate([jnp.where(left, a, pltpu.roll(b, HEAD_DIM, 1)),
                                jnp.where(left, pltpu.roll(a, HEAD_DIM, 1), b)], axis=1)

    scale = HEAD_DIM ** -0.5
    pool = pool_ref[...]
    aq_ref[0] = (gqa_order(rot(_head_rms(cols(0), pool, gq_ref[...]))) * (scale * LOG2_E)).astype(BF16)
    akv = cols(GROUP_W)
    ak_ref[0] = rot(_head_rms(akv[:, :KV_W], pool[:KV_W, :KV_W], gk_ref[...])).astype(BF16)
    ones = jnp.ones((akv.shape[0], LANES), F32)
    av_ref[0] = jnp.concatenate([akv[:, KV_W:], ones], axis=1).astype(BF16)
    bq_ref[0] = (gqa_order(rot(cols(o_b))) * (scale * LOG2_E)).astype(BF16)
    bkv = cols(o_b + GROUP_W)
    bk_ref[0] = rot(bkv[:, :KV_W]).astype(BF16)
    bv_ref[0] = jnp.concatenate([bkv[:, KV_W:], ones], axis=1).astype(BF16)
    cq_ref[0] = (cols(o_c) * scale).astype(BF16)
    ck_ref[0] = cols(o_c + GROUP_W).astype(BF16)
    cv_ref[0] = cols(o_c + 2 * GROUP_W).astype(BF16)
    co_ref[0] = cols(o_c + 3 * GROUP_W)
    dq_ref[0] = (cols(o_d) * scale).astype(BF16)
    dk_ref[0] = cols(o_d + GROUP_W).astype(BF16)
    dv_ref[0] = cols(o_d + 2 * GROUP_W).astype(BF16)
    gates = cols(o_g, LANES) + gb_ref[...]
    lane = lax.broadcasted_iota(jnp.int32, gates.shape, 1)
    gates = jnp.where((lane % 8) >= 4, _log_sigmoid(gates), gates) * LOG2_E
    cg_ref[0] = jnp.where(lane < 4 * N_HEADS, gates, 0.0)


def _in_proj(x, mod, mod_row, g1, w, pool, gq, gk, cos, sin, gbias, *, layer, rope, tm):
    b, t, d = x.shape
    widths = [(GROUP_W, BF16), (KV_W, BF16), (2 * LANES, BF16),
              (GROUP_W, BF16), (KV_W, BF16), (2 * LANES, BF16),
              (GROUP_W, BF16), (GROUP_W, BF16), (GROUP_W, BF16), (GROUP_W, F32), (LANES, F32),
              (GROUP_W, BF16), (GROUP_W, BF16), (GROUP_W, BF16)]
    row = lambda wd: pl.BlockSpec((1, wd), lambda bi, j: (0, 0))
    return pl.pallas_call(
        functools.partial(_inproj_kernel, d_model=d, rope=rope, sub_rows=min(tm, TOKEN_TILE)),
        grid=(b, t // tm),
        in_specs=[pl.BlockSpec((1, tm, d), lambda bi, j: (bi, j, 0)),
                  pl.BlockSpec((1, 1, 6 * d), lambda bi, j: (mod_row(bi), 0, 0)),
                  row(d),
                  _layer_spec(w.shape, layer),
                  _const_spec((GROUP_W, GROUP_W)),
                  row(GROUP_W), row(KV_W),
                  pl.BlockSpec((tm, LANES), lambda bi, j: (j, 0)),
                  pl.BlockSpec((tm, LANES), lambda bi, j: (j, 0)),
                  row(LANES)],
        out_specs=[pl.BlockSpec((1, tm, wd), lambda bi, j: (bi, j, 0)) for wd, _ in widths],
        out_shape=[jax.ShapeDtypeStruct((b, t, wd), dt) for wd, dt in widths],
        compiler_params=_params(("parallel", "parallel"), "in_proj"),
        name="in_proj",
    )(x, mod, g1, w, pool, gq, gk, cos, sin, gbias)


def _stack_gqa(q):
    qa, qb = q[:, :LANES], q[:, LANES:]
    left = lax.broadcasted_iota(jnp.int32, qa.shape, 1) < HEAD_DIM
    zero = jnp.zeros_like(qa)
    return jnp.concatenate([jnp.where(left, qa, zero), jnp.where(left, qb, zero),
                            jnp.where(left, zero, qa), jnp.where(left, zero, qb)], axis=0)


def _unstack_gqa(o, t):
    left = lax.broadcasted_iota(jnp.int32, (t, LANES), 1) < HEAD_DIM
    return jnp.concatenate([jnp.where(left, o[0:t], o[2 * t:3 * t]),
                            jnp.where(left, o[t:2 * t], o[3 * t:4 * t])], axis=1)


def _attn_a_kernel(safe_ref, q_ref, k_ref, v_ref, kc_ref, vc_ref, o_ref, qs_ref, m_ref, acc_ref, *, tq, tk, n_kb):
    qs_ref[...] = _stack_gqa(q_ref[0])
    heads = [slice(h * tq, (h + 1) * tq) for h in range(N_HEADS)]

    @pl.when(safe_ref[0] != 0)
    def _():
        for rows in heads:
            p = jnp.exp2(_dot_nt(qs_ref[rows], kc_ref[0]))
            acc_ref[rows] = _dot(p.astype(BF16), vc_ref[0])

        def body(kb, carry):
            start = pl.multiple_of(kb * tk, tk)
            for rows in heads:
                p = jnp.exp2(_dot_nt(qs_ref[rows], k_ref[0, pl.ds(start, tk), :]))
                acc_ref[rows] += _dot(p.astype(BF16), v_ref[0, pl.ds(start, tk), :])
            return carry

        lax.fori_loop(0, n_kb, body, 0)

    @pl.when(safe_ref[0] == 0)
    def _():
        for rows in heads:
            s = _dot_nt(qs_ref[rows], kc_ref[0])
            m0 = jnp.max(s, axis=-1, keepdims=True)
            m_ref[rows] = m0
            acc_ref[rows] = _dot(jnp.exp2(s - m0).astype(BF16), vc_ref[0])

        def body(kb, carry):
            start = pl.multiple_of(kb * tk, tk)
            for rows in heads:
                s = _dot_nt(qs_ref[rows], k_ref[0, pl.ds(start, tk), :])
                m_prev = m_ref[rows]
                m_new = jnp.maximum(m_prev, jnp.max(s, axis=-1, keepdims=True))
                p = jnp.exp2(s - m_new)
                acc_ref[rows] = jnp.exp2(m_prev - m_new) * acc_ref[rows] + _dot(
                    p.astype(BF16), v_ref[0, pl.ds(start, tk), :])
                m_ref[rows] = m_new
            return carry

        lax.fori_loop(0, n_kb, body, 0)

    acc = acc_ref[...]
    o_ref[0] = _unstack_gqa(acc[:, :LANES] / acc[:, LANES:], tq).astype(o_ref.dtype)


def _attn_global(safe, q, k, v, kc, vc, *, tq, tk):
    b, t, _ = q.shape
    n_ctx = kc.shape[1]
    per_b = lambda n, wd: pl.BlockSpec((1, n, wd), lambda bi, i: (bi, 0, 0))
    return pl.pallas_call(
        functools.partial(_attn_a_kernel, tq=tq, tk=tk, n_kb=t // tk),
        grid=(b, t // tq),
        in_specs=[pl.BlockSpec(memory_space=pltpu.SMEM),
                  pl.BlockSpec((1, tq, GROUP_W), lambda bi, i: (bi, i, 0)),
                  per_b(t, KV_W), per_b(t, 2 * LANES), per_b(n_ctx, KV_W), per_b(n_ctx, 2 * LANES)],
        out_specs=pl.BlockSpec((1, tq, GROUP_W), lambda bi, i: (bi, i, 0)),
        out_shape=jax.ShapeDtypeStruct((b, t, GROUP_W), BF16),
        scratch_shapes=[pltpu.VMEM((4 * tq, LANES), BF16), pltpu.VMEM((4 * tq, 1), F32),
                        pltpu.VMEM((4 * tq, 2 * LANES), F32)],
        compiler_params=_params(("parallel", "parallel"), "attn_global"),
        name="attn_global",
    )(safe, q, k, v, kc, vc)


def _sink_column(sink_ref, t):
    row = lax.broadcasted_iota(jnp.int32, (4 * t, 1), 0)
    col = jnp.full((4 * t, 1), sink_ref[3], F32)
    for h in (2, 1, 0):
        col = jnp.where(row < (h + 1) * t, sink_ref[h], col)
    return col


def _attn_b_kernel(sink_ref, q_ref, k_ref, v_ref, kc_ref, vc_ref, o_ref, *, tq, tiles_per_step, t_total):
    span = tq + 2 * WINDOW
    for sub in range(tiles_per_step):
        q0 = (pl.program_id(1) * tiles_per_step + sub) * tq
        ks = pl.multiple_of(jnp.clip(q0 - WINDOW, 0, t_total - span), WINDOW)
        qs = _stack_gqa(q_ref[0, sub * tq:(sub + 1) * tq, :])
        kw = k_ref[0, pl.ds(ks, span), :]
        vw = v_ref[0, pl.ds(ks, span), :]
        q_pos = q0 + lax.broadcasted_iota(jnp.int32, (tq, span), 0)
        k_pos = ks + lax.broadcasted_iota(jnp.int32, (tq, span), 1)
        in_window = jnp.abs(k_pos - q_pos) <= WINDOW
        outs = []
        for h in range(N_HEADS):
            qh = qs[h * tq:(h + 1) * tq]
            s = jnp.where(in_window, _dot_nt(qh, kw), NEG_INF)
            sc = _dot_nt(qh, kc_ref[0])
            sink = sink_ref[h] * LOG2_E
            m = jnp.maximum(jnp.maximum(jnp.max(s, axis=-1, keepdims=True), jnp.max(sc, axis=-1, keepdims=True)),
                            sink)
            acc = _dot(jnp.exp2(s - m).astype(BF16), vw) + _dot(jnp.exp2(sc - m).astype(BF16), vc_ref[0])
            outs.append(acc[:, :LANES] / (acc[:, LANES:] + jnp.exp2(sink - m)))
        o_ref[0, sub * tq:(sub + 1) * tq, :] = _unstack_gqa(jnp.concatenate(outs, axis=0), tq).astype(o_ref.dtype)


def _attn_window(sink, q, k, v, kc, vc, *, tq):
    b, t, _ = q.shape
    n_ctx = kc.shape[1]
    assert t >= tq + 2 * WINDOW
    tiles = WINDOW_TILES_PER_STEP if (t // tq) % WINDOW_TILES_PER_STEP == 0 else 1
    per_b = lambda n, wd: pl.BlockSpec((1, n, wd), lambda bi, i: (bi, 0, 0))
    return pl.pallas_call(
        functools.partial(_attn_b_kernel, tq=tq, tiles_per_step=tiles, t_total=t),
        grid=(b, t // (tq * tiles)),
        in_specs=[pl.BlockSpec(memory_space=pltpu.SMEM),
                  pl.BlockSpec((1, tq * tiles, GROUP_W), lambda bi, i: (bi, i, 0)),
                  per_b(t, KV_W), per_b(t, 2 * LANES), per_b(n_ctx, KV_W), per_b(n_ctx, 2 * LANES)],
        out_specs=pl.BlockSpec((1, tq * tiles, GROUP_W), lambda bi, i: (bi, i, 0)),
        out_shape=jax.ShapeDtypeStruct((b, t, GROUP_W), BF16),
        compiler_params=_params(("parallel", "parallel"), "attn_window"),
        name="attn_window",
    )(sink, q, k, v, kc, vc)


def _head_masks(shape):
    lane = lax.broadcasted_iota(jnp.int32, shape, 1)
    return [(lane >= h * HEAD_DIM) & (lane < (h + 1) * HEAD_DIM) for h in range(N_HEADS)]


def _stack_mha(q):
    zero = jnp.zeros_like(q)
    return jnp.concatenate([jnp.where(mk, q, zero) for mk in _head_masks(q.shape)], axis=0)


def _unstack_mha(o, t):
    masks = _head_masks((t, GROUP_W))
    out = jnp.where(masks[0], o[0:t], 0.0)
    for h in range(1, N_HEADS):
        out = jnp.where(masks[h], o[h * t:(h + 1) * t], out)
    return out


def _attn_d_kernel(q_ref, k_ref, v_ref, kc_ref, vc_ref, bias_ref, o_ref, *, rows_per_step, rows):
    kc = kc_ref[0]
    vc = vc_ref[0]
    span = NA_KH * GRID_W

    def body(rr, carry):
        r = pl.program_id(1) * rows_per_step + rr
        rs = jnp.clip(r - NA_KH // 2, 0, rows - NA_KH)
        k0 = pl.multiple_of(rs * GRID_W, GRID_W)
        q0 = pl.multiple_of(rr * GRID_W, GRID_W)
        qs = _stack_mha(q_ref[0, pl.ds(q0, GRID_W), :])
        s = _dot_nt(qs, k_ref[0, pl.ds(k0, span), :]) + bias_ref[r - rs]
        sc = _dot_nt(qs, kc)
        m = jnp.maximum(jnp.max(s, axis=-1, keepdims=True), jnp.max(sc, axis=-1, keepdims=True))
        p = jnp.exp(s - m)
        pc = jnp.exp(sc - m)
        l = jnp.sum(p, axis=-1, keepdims=True) + jnp.sum(pc, axis=-1, keepdims=True)
        o = _dot(p.astype(BF16), v_ref[0, pl.ds(k0, span), :]) + _dot(pc.astype(BF16), vc)
        o_ref[0, pl.ds(q0, GRID_W), :] = _unstack_mha(o / l, GRID_W).astype(o_ref.dtype)
        return carry

    lax.fori_loop(0, rows_per_step, body, 0, unroll=True)


def _attn_neighbour(q, k, v, kc, vc, bias, *, rows_per_step):
    b, t, _ = q.shape
    n_ctx = kc.shape[1]
    rows = t // GRID_W
    assert rows >= NA_KH and rows % rows_per_step == 0
    tq = rows_per_step * GRID_W
    per_b = lambda n: pl.BlockSpec((1, n, GROUP_W), lambda bi, i: (bi, 0, 0))
    return pl.pallas_call(
        functools.partial(_attn_d_kernel, rows_per_step=rows_per_step, rows=rows),
        grid=(b, t // tq),
        in_specs=[pl.BlockSpec((1, tq, GROUP_W), lambda bi, i: (bi, i, 0)),
                  per_b(t), per_b(t), per_b(n_ctx), per_b(n_ctx),
                  _const_spec(bias.shape)],
        out_specs=pl.BlockSpec((1, tq, GROUP_W), lambda bi, i: (bi, i, 0)),
        out_shape=jax.ShapeDtypeStruct((b, t, GROUP_W), BF16),
        compiler_params=_params(("parallel", "parallel"), "attn_neighbour"),
        name="attn_neighbour",
    )(q, k, v, kc, vc, bias)


def _neighbour_bias(rpb):
    w = jnp.arange(GRID_W)
    cs = jnp.clip(w - NA_KW // 2, 0, GRID_W - NA_KW)
    col = jnp.arange(GRID_W)
    valid = (col[None, :] >= cs[:, None]) & (col[None, :] < cs[:, None] + NA_KW)
    dc = jnp.clip(col[None, :] - w[:, None] + (NA_KW - 1), 0, 2 * NA_KW - 2)
    pick_c = (jnp.arange(2 * NA_KW - 1)[:, None, None] == dc[None]).astype(F32)
    by_col = jnp.einsum('hrc,cwk->hwrk', rpb.astype(F32), pick_c, precision=lax.Precision.HIGHEST)
    by_col = jnp.where(valid[None, :, None, :], by_col, NEG_INF)
    flat = by_col.reshape(N_HEADS * GRID_W, (2 * NA_KH - 1) * GRID_W)
    return jnp.stack([flat[:, (NA_KH - 1 - o) * GRID_W:(2 * NA_KH - 1 - o) * GRID_W] for o in range(NA_KH)])


def _ctx_attn_kernel(sink_ref, aq_ref, ak_ref, av_ref, bq_ref, bk_ref, bv_ref, dq_ref, dk_ref, dv_ref,
                     ya_ref, yb_ref, yd_ref, *, n):
    s = _dot_nt(_stack_gqa(aq_ref[0]), ak_ref[0])
    acc = _dot(jnp.exp2(s - jnp.max(s, axis=-1, keepdims=True)).astype(BF16), av_ref[0])
    ya_ref[0] = _unstack_gqa(acc[:, :LANES] / acc[:, LANES:], n).astype(ya_ref.dtype)
    s = _dot_nt(_stack_gqa(bq_ref[0]), bk_ref[0])
    sink = _sink_column(sink_ref, n) * LOG2_E
    m = jnp.maximum(jnp.max(s, axis=-1, keepdims=True), sink)
    acc = _dot(jnp.exp2(s - m).astype(BF16), bv_ref[0])
    yb_ref[0] = _unstack_gqa(acc[:, :LANES] / (acc[:, LANES:] + jnp.exp2(sink - m)), n).astype(yb_ref.dtype)
    s = _dot_nt(_stack_mha(dq_ref[0]), dk_ref[0])
    p = jnp.exp(s - jnp.max(s, axis=-1, keepdims=True))
    o = _dot(p.astype(BF16), dv_ref[0]) / jnp.sum(p, axis=-1, keepdims=True)
    yd_ref[0] = _unstack_mha(o, n).astype(yd_ref.dtype)


def _ctx_attention(sink, aq, ak, av, bq, bk, bv, dq, dk, dv):
    b, n, _ = aq.shape
    spec = lambda wd: pl.BlockSpec((1, n, wd), lambda bi: (bi, 0, 0))
    return pl.pallas_call(
        functools.partial(_ctx_attn_kernel, n=n),
        grid=(b,),
        in_specs=[pl.BlockSpec(memory_space=pltpu.SMEM),
                  spec(GROUP_W), spec(KV_W), spec(2 * LANES), spec(GROUP_W), spec(KV_W), spec(2 * LANES),
                  spec(GROUP_W), spec(GROUP_W), spec(GROUP_W)],
        out_specs=[spec(GROUP_W)] * 3,
        out_shape=[jax.ShapeDtypeStruct((b, n, GROUP_W), BF16)] * 3,
        compiler_params=_params(("parallel",), "ctx_attention"),
        name="ctx_attention",
    )(sink, aq, ak, av, bq, bk, bv, dq, dk, dv)


def _log_sigmoid(x):
    return jnp.minimum(x, 0.0) - jnp.log1p(jnp.exp(-jnp.abs(x)))


def _split3(x):
    x1 = x.astype(BF16)
    r1 = x - x1.astype(F32)
    x2 = r1.astype(BF16)
    return x1, x2, (r1 - x2.astype(F32)).astype(BF16)


def _exact_dot_01(sel, x):
    return sum(_dot(sel, piece) for piece in _split3(x))


def _exact_dot_01_rhs(x, sel):
    return sum(_dot(piece, sel) for piece in _split3(x))


def _mlstm_chunk(q, k, v, gates, expand, tri, neg_mask, ones_cat, block, head_rows, s_ref, n_ref, m_ref, *,
                 reverse, L):
    ge = _exact_dot_01_rhs(gates, expand)
    li = ge[:, :GROUP_W]
    cum = _exact_dot_01(tri, ge[:, GROUP_W:])
    a_t = (li - cum).T
    m_prev = m_ref[0:1, :]
    head_b = [head_rows[h].astype(BF16) for h in range(N_HEADS)]
    k_t = k.astype(F32).T.astype(BF16)
    mu = jnp.zeros((L, GROUP_W), F32)
    scores = []
    for h in range(N_HEADS):
        lane0 = h * HEAD_DIM
        a_m = a_t[lane0:lane0 + 1, :] + neg_mask
        mu_h = jnp.maximum(jnp.max(a_m, axis=-1, keepdims=True), m_prev[:, lane0:lane0 + 1])
        qk = _dot(q * head_b[h], k_t)
        scores.append((qk * jnp.exp2(a_m - mu_h)).astype(BF16))
        mu = mu + mu_h * head_rows[h]
    s_cat = jnp.concatenate(scores, axis=1)
    v_cat = jnp.concatenate([v * hb for hb in head_b], axis=0)
    w_inter = jnp.exp2(m_prev - mu)
    num = _dot(s_cat, v_cat) + w_inter * _dot(q, s_ref[...].astype(BF16))
    den = _dot(s_cat, ones_cat) + w_inter * _dot(q, n_ref[...].astype(BF16))
    h_out = num / jnp.maximum(jnp.abs(den), jnp.exp2(-(cum + mu)))

    end_row = 0 if reverse else L - 1
    cum_end = cum[end_row:end_row + 1, :]
    log_end = cum_end - cum + li
    m_new = jnp.maximum(cum_end + m_prev, jnp.max(log_end, axis=0, keepdims=True))
    decay = jnp.exp2(cum_end + m_prev - m_new)
    w_end = jnp.exp2(log_end - m_new)
    upd =_dot(k_t, jnp.concatenate([(v.astype(F32) * w_end).astype(BF16), w_end.astype(BF16)], axis=1))
    s_ref[...] = (s_ref[...] * decay + upd[:, :GROUP_W]) * block
    n_ref[...] = (n_ref[...] * decay + upd[:, GROUP_W:]) * block
    m_ref[...] = jnp.broadcast_to(m_new, m_ref.shape)
    return h_out


def _mlstm_kernel(qf_ref, kf_ref, vf_ref, gf_ref, qb_ref, kb_ref, vb_ref, gb_ref,
                  expand_ref, tri_ref, neg_ref, ones_ref, block_ref, heads_ref, s0_ref, n0_ref, m0_ref,
                  hf_ref, hb_ref, s_out, n_out, m_out, s_ref, n_ref, m_ref, *, chunk, chunks_per_step):
    @pl.when(pl.program_id(1) == 0)
    def _():
        s_ref[...] = s0_ref[0]
        n_ref[...] = n0_ref[0]
        m_ref[...] = m0_ref[0]

    sides = ((qf_ref, kf_ref, vf_ref, gf_ref, hf_ref), (qb_ref, kb_ref, vb_ref, gb_ref, hb_ref))
    for c in range(chunks_per_step):
        for d, (q_ref, k_ref, v_ref, g_ref, h_ref) in enumerate(sides):
            rows = pl.ds((chunks_per_step - 1 - c if d else c) * chunk, chunk)
            h = _mlstm_chunk(q_ref[0, rows, :], k_ref[0, rows, :], v_ref[0, rows, :], g_ref[0, rows, :],
                             expand_ref[d], tri_ref[d], neg_ref[d], ones_ref[...], block_ref[...], heads_ref,
                             s_ref.at[d], n_ref.at[d], m_ref.at[d], reverse=bool(d), L=chunk)
            h_ref[0, rows, :] = h.astype(h_ref.dtype)
    s_out[0] = s_ref[...]
    n_out[0] = n_ref[...]
    m_out[0] = m_ref[...]


def _mlstm_consts(L):
    lane = jnp.arange(LANES)[:, None]
    col = jnp.arange(2 * GROUP_W)[None, :]
    col_head = (col % GROUP_W) // HEAD_DIM
    expand = jnp.stack([(lane == 8 * d + 4 * (col // GROUP_W) + col_head) for d in range(2)]).astype(BF16)
    t_idx = jnp.arange(L)[:, None]
    s_idx = jnp.arange(L)[None, :]
    seen = jnp.stack([s_idx <= t_idx, s_idx >= t_idx])
    head = jnp.arange(GROUP_W) // HEAD_DIM
    ones_cat = (jnp.repeat(jnp.arange(N_HEADS), L)[:, None] == head[None, :]).astype(BF16)
    block = (head[:, None] == head[None, :]).astype(F32)
    head_rows = (jnp.arange(N_HEADS)[:, None, None] == head[None, None, :]).astype(F32)
    return expand, seen.astype(BF16), jnp.where(seen, 0.0, NEG_INF).astype(F32), ones_cat, block, head_rows


def _mlstm_scan(q, k, v, gates, state, consts):
    b, t, _ = q.shape
    L = MLSTM_L
    per_step = MLSTM_CHUNKS_PER_STEP if (t // L) % MLSTM_CHUNKS_PER_STEP == 0 else 1
    nc = t // (L * per_step)
    fwd = lambda wd: pl.BlockSpec((1, L * per_step, wd), lambda bi, j: (bi, j, 0))
    bwd = lambda wd: pl.BlockSpec((1, L * per_step, wd), lambda bi, j: (bi, nc - 1 - j, 0))
    st = lambda r: pl.BlockSpec((1, 2, r, GROUP_W), lambda bi, j: (bi, 0, 0, 0))
    tok_specs = [fwd(GROUP_W), fwd(GROUP_W), fwd(GROUP_W), fwd(LANES),
                 bwd(GROUP_W), bwd(GROUP_W), bwd(GROUP_W), bwd(LANES)]
    hf, hb, s_fin, n_fin, m_fin = pl.pallas_call(
        functools.partial(_mlstm_kernel, chunk=L, chunks_per_step=per_step),
        grid=(b, nc),
        in_specs=tok_specs + [_const_spec(c.shape) for c in consts] + [st(GROUP_W), st(GROUP_W), st(8)],
        out_specs=[fwd(GROUP_W), bwd(GROUP_W), st(GROUP_W), st(GROUP_W), st(8)],
        out_shape=[jax.ShapeDtypeStruct((b, t, GROUP_W), BF16), jax.ShapeDtypeStruct((b, t, GROUP_W), BF16),
                   jax.ShapeDtypeStruct((b, 2, GROUP_W, GROUP_W), F32),
                   jax.ShapeDtypeStruct((b, 2, GROUP_W, GROUP_W), F32),
                   jax.ShapeDtypeStruct((b, 2, 8, GROUP_W), F32)],
        scratch_shapes=[pltpu.VMEM((2, GROUP_W, GROUP_W), F32), pltpu.VMEM((2, GROUP_W, GROUP_W), F32),
                        pltpu.VMEM((2, 8, GROUP_W), F32)],
        compiler_params=_params(("parallel", "arbitrary"), "mlstm"),
        name="mlstm",
    )(q, k, v, gates, q, k, v, gates, *consts, *state)
    return hf, hb, (s_fin, n_fin, m_fin)


def _outproj_kernel(x_ref, ya_ref, yb_ref, hf_ref, hb_ref, og_ref, yd_ref, mod_ref, gg_ref, w_ref, o_ref, *, d_model):
    d = d_model
    yc = jax.nn.sigmoid(og_ref[0]) * (hf_ref[0].astype(F32) + hb_ref[0].astype(F32))
    parts = []
    for i, y in enumerate((ya_ref[0], yb_ref[0], yc, yd_ref[0])):
        parts.append((_rms_rows(y.astype(F32)) * gg_ref[i:i + 1, :]).astype(BF16))
    res = _dot(jnp.concatenate(parts, axis=1), w_ref[...])
    o_ref[0] = x_ref[0] + mod_ref[0][:, 2 * d:3 * d] * res


def _out_proj(x, ya, yb, hf, hb, o_gate, yd, mod, mod_row, gg, w, *, layer, tm):
    b, t, d = x.shape
    tokd = pl.BlockSpec((1, tm, d), lambda bi, j: (bi, j, 0))
    tokg = pl.BlockSpec((1, tm, GROUP_W), lambda bi, j: (bi, j, 0))
    return pl.pallas_call(
        functools.partial(_outproj_kernel, d_model=d),
        grid=(b, t // tm),
        in_specs=[tokd, tokg, tokg, tokg, tokg, tokg, tokg,
                  pl.BlockSpec((1, 1, 6 * d), lambda bi, j: (mod_row(bi), 0, 0)),
                  _const_spec(gg.shape), _layer_spec(w.shape, layer)],
        out_specs=tokd,
        out_shape=jax.ShapeDtypeStruct((b, t, d), F32),
        compiler_params=_params(("parallel", "parallel"), "out_proj"),
        name="out_proj",
    )(x, ya, yb, hf, hb, o_gate, yd, mod, gg, w)


def _ffn_kernel(*refs, d_model, d_ff, tm, final_norm):
    if final_norm:
        x_ref, xp_ref, xn_ref, mod_ref, g2_ref, wu_ref, cw_ref, wd_ref, gf_ref, o_ref = refs
    else:
        x_ref, xp_ref, xn_ref, mod_ref, g2_ref, wu_ref, cw_ref, wd_ref, o_ref = refs
    d = d_model
    j = pl.program_id(1)
    mod = mod_ref[0]
    g2 = g2_ref[...]

    def norm_mod(x):
        return _rms_rows(x) * g2 * (1.0 + mod[:, 4 * d:5 * d]) + mod[:, 3 * d:4 * d]

    keep_prev = (j > 0).astype(F32)
    keep_next = (j < pl.num_programs(1) - 1).astype(F32)
    x = x_ref[0]
    h_ext = jnp.concatenate([norm_mod(xp_ref[0]) * keep_prev, norm_mod(x), norm_mod(xn_ref[0]) * keep_next],
                            axis=0).astype(BF16)
    n_ext = tm + 2 * HALO

    u = _dot(h_ext, wu_ref[...])
    cw = cw_ref[...]
    u = (pltpu.roll(u, 1, 0) * cw[0:1] + u * cw[1:2] + pltpu.roll(u, n_ext - 1, 0) * cw[2:3] + cw[3:4])[HALO:HALO + tm]
    gate, val = u[:, :d_ff], u[:, d_ff:]
    act = (gate * jax.nn.sigmoid(gate) * val).astype(BF16)
    y = x + mod[:, 5 * d:6 * d] * _dot(act, wd_ref[...])
    if final_norm:
        y = _rms_rows(y) * gf_ref[...]
    o_ref[0] = y


def _conv_ffn(x, mod, mod_row, g2, wu, cw, wd, g_final, *, layer, tm):
    b, t, d = x.shape
    d_ff = wd.shape[1]
    final_norm = g_final is not None
    hb = tm // HALO
    last = t // HALO - 1
    row = pl.BlockSpec((1, d), lambda bi, j: (0, 0))
    in_specs = [pl.BlockSpec((1, tm, d), lambda bi, j: (bi, j, 0)),
                pl.BlockSpec((1, HALO, d), lambda bi, j: (bi, jnp.maximum(j * hb - 1, 0), 0)),
                pl.BlockSpec((1, HALO, d), lambda bi, j: (bi, jnp.minimum((j + 1) * hb, last), 0)),
                pl.BlockSpec((1, 1, 6 * d), lambda bi, j: (mod_row(bi), 0, 0)),
                row,
                _layer_spec(wu.shape, layer), _const_spec(cw.shape), _layer_spec(wd.shape, layer)]
    args = [x, x, x, mod, g2, wu, cw, wd]
    if final_norm:
        in_specs.append(row)
        args.append(g_final)
    return pl.pallas_call(
        functools.partial(_ffn_kernel, d_model=d, d_ff=d_ff, tm=tm, final_norm=final_norm),
        grid=(b, t // tm),
        in_specs=in_specs,
        out_specs=pl.BlockSpec((1, tm, d), lambda bi, j: (bi, j, 0)),
        out_shape=jax.ShapeDtypeStruct((b, t, d), F32),
        compiler_params=_params(("parallel", "parallel"), "conv_ffn"),
        name="conv_ffn",
    )(*args)


_Q_HEAD_ORDER = (0, 2, 1, 3)


def _permute_heads(w, axis):
    parts = [lax.slice_in_dim(w, h * HEAD_DIM, (h + 1) * HEAD_DIM, axis=axis) for h in _Q_HEAD_ORDER]
    return jnp.concatenate(parts, axis=axis)


def _layout_w_in(w):
    return w.astype(BF16)


def _layout_w_out(w):
    w = w.astype(BF16)
    return jnp.concatenate([_permute_heads(w[:, 0:GROUP_W], 1), _permute_heads(w[:, GROUP_W:2 * GROUP_W], 1),
                            w[:, 2 * GROUP_W:]], axis=1)


def _rope_tables(t):
    pos = jnp.arange(t)
    row = (pos // GRID_W).astype(F32)
    col = (pos % GRID_W).astype(F32)
    n_freq = HEAD_DIM // 4
    freqs = ROPE_THETA ** (-jnp.arange(n_freq, dtype=F32) / n_freq)
    ang_r, ang_c = row[:, None] * freqs, col[:, None] * freqs
    cos = jnp.concatenate([jnp.cos(ang_r), jnp.cos(ang_r), jnp.cos(ang_c), jnp.cos(ang_c)], axis=1)
    sin = jnp.concatenate([-jnp.sin(ang_r), jnp.sin(ang_r), -jnp.sin(ang_c), jnp.sin(ang_c)], axis=1)
    reps = LANES // HEAD_DIM
    return jnp.tile(cos, (1, reps)), jnp.tile(sin, (1, reps))


def _pick_tile(t, pref):
    while t % pref:
        pref //= 2
    return pref


def kernel(x, c, ctx, c_ctx, w_mod, b_mod, g_norm1, g_norm2, w_in, a_q_gain, a_k_gain, b_sink, c_gate_bias,
           d_rel_bias, g_group, w_out, w_up, conv_w, conv_b, w_down, g_final):
    batch, t, d = x.shape
    n_ctx = ctx.shape[1]
    depth = w_in.shape[0]
    d_ff = w_down.shape[1]
    assert batch < 8 and d_ff % LANES == 0 and t % MLSTM_L == 0 and n_ctx % MLSTM_L == 0

    c_rows = jnp.zeros((8, d), F32).at[:batch].set(c).at[batch].set(c_ctx)
    mod_all = _modulation(c_rows, w_mod, b_mod).reshape(depth, 8, 1, 6 * d)
    lat_row = lambda bi: bi
    ctx_row = lambda bi: batch

    cos, sin = _rope_tables(t)
    pool = jnp.where(jnp.arange(GROUP_W)[:, None] // HEAD_DIM == jnp.arange(GROUP_W)[None, :] // HEAD_DIM,
                     1.0 / HEAD_DIM, 0.0).astype(BF16)
    tm_lat = _pick_tile(t, TOKEN_TILE)
    tm_ctx = _pick_tile(n_ctx, CTX_TILE)
    zero_state = (jnp.zeros((batch, 2, GROUP_W, GROUP_W), F32), jnp.zeros((batch, 2, GROUP_W, GROUP_W), F32),
                  jnp.zeros((batch, 2, 8, GROUP_W), F32))
    mlstm_consts = _mlstm_consts(MLSTM_L)
    w_in_b = _layout_w_in(w_in)
    w_out_b = _layout_w_out(w_out)
    w_up_b = w_up.astype(BF16)
    w_down_b = w_down.astype(BF16)

    for layer in range(depth):
        need_ctx = layer < depth - 1
        mod = mod_all[layer]
        g1 = g_norm1[layer][None]
        g2 = g_norm2[layer][None]
        gq = jnp.tile(a_q_gain[layer], N_HEADS)[None]
        gk = jnp.tile(a_k_gain[layer], 2)[None]
        gbias = jnp.pad(c_gate_bias[layer], (0, LANES - 4 * N_HEADS))[None]
        gg = g_group[layer].reshape(N_HEADS, GROUP_W)
        gg = jnp.concatenate([_permute_heads(gg[0:2], 1), gg[2:4]], axis=0)
        conv_l = jnp.concatenate([conv_w[layer], conv_b[layer][None],
                                  jnp.zeros((8 - 1 - conv_w.shape[1], 2 * d_ff), F32)], axis=0)
        sink = b_sink[layer]
        bias_tab = _neighbour_bias(d_rel_bias[layer])

        proj = functools.partial(_in_proj, g1=g1, w=w_in_b, pool=pool, gq=gq, gk=gk, gbias=gbias, layer=layer)
        (aqc, akc, avc, bqc, bkc, bvc, cqc, ckc, cvc, coc, cgc, dqc, dkc, dvc) = proj(
            ctx, mod, ctx_row, cos=cos, sin=sin, rope=False, tm=tm_ctx)
        (aq, ak, av, bq, bk, bv, cq, ck, cv, co, cg, dq, dk, dv) = proj(
            x, mod, lat_row, cos=cos, sin=sin, rope=True, tm=_pick_tile(t, 2 * TOKEN_TILE))

        score_bound = 1.02 * LOG2_E * HEAD_DIM ** 0.5 * jnp.max(jnp.abs(a_q_gain[layer])) * jnp.max(jnp.abs(a_k_gain[layer]))
        safe = (score_bound <= MAX_UNSHIFTED_LOG2_SCORE).astype(jnp.int32).reshape(1)
        ya = _attn_global(safe, aq, ak, av, akc, avc, tq=_pick_tile(t, ATTN_Q_TILE), tk=_pick_tile(t, ATTN_K_TILE))
        yb = _attn_window(sink, bq, bk, bv, bkc, bvc, tq=_pick_tile(t, WINDOW_Q_TILE))
        yd = _attn_neighbour(dq, dk, dv, dkc, dvc, bias_tab, rows_per_step=NEIGHBOUR_ROWS)

        hcf, hcb, ctx_state = _mlstm_scan(cqc, ckc, cvc, cgc, zero_state, mlstm_consts)
        hf, hb, _ = _mlstm_scan(cq, ck, cv, cg, ctx_state, mlstm_consts)

        x = _out_proj(x, ya, yb, hf, hb, co, yd, mod, lat_row, gg, w_out_b, layer=layer,
                      tm=_pick_tile(t, 2 * TOKEN_TILE))
        ffn = functools.partial(_conv_ffn, g2=g2, wu=w_up_b, cw=conv_l, wd=w_down_b, layer=layer)
        x = ffn(x, mod, lat_row, g_final=None if need_ctx else g_final[None], tm=tm_lat)
        if need_ctx:
            yac, ybc, ydc = _ctx_attention(sink, aqc, akc, avc, bqc, bkc, bvc, dqc, dkc, dvc)
            ctx = _out_proj(ctx, yac, ybc, hcf, hcb, coc, ydc, mod, ctx_row, gg, w_out_b, layer=layer, tm=tm_ctx)
            ctx = ffn(ctx, mod, ctx_row, g_final=None, tm=tm_ctx)
    return x
```

```python
import functools

import jax
import jax.numpy as jnp
from jax import lax
from jax.experimental import pallas as pl
from jax.experimental.pallas import tpu as pltpu

F32 = jnp.float32
BF16 = jnp.bfloat16

N_HEADS = 4
HEAD_DIM = 64
GROUP_W = N_HEADS * HEAD_DIM
KV_W = 2 * HEAD_DIM
GRID_W = 64
WINDOW = 128
NA_KH = 8
NA_KW = 16
ROPE_THETA = 10000.0
EPS = 1e-6
NEG_INF = -1e30
LANES = 128
MLSTM_L = 256
MLSTM_CHUNKS_PER_STEP = 2
HALO = 8
MIB = 1024 * 1024
LOG2_E = 1.4426950408889634
MAX_UNSHIFTED_LOG2_SCORE = 60.0


TOKEN_TILE = 512
CTX_TILE = 256
ATTN_Q_TILE = 1024
ATTN_K_TILE = 2048
WINDOW_Q_TILE = 256
WINDOW_TILES_PER_STEP = 16
NEIGHBOUR_ROWS = 32
VMEM_LIMIT_MIB = {"modulation": 40, "in_proj": 48, "attn_global": 48, "attn_window": 48, "attn_neighbour": 48,
                  "ctx_attention": 32, "mlstm": 48, "out_proj": 48, "conv_ffn": 56}


def _params(sem, name):
    return pltpu.CompilerParams(dimension_semantics=sem, vmem_limit_bytes=VMEM_LIMIT_MIB[name] * MIB)


def _dot(a, b):
    return jnp.dot(a, b, preferred_element_type=F32)


def _dot_nt(a, b):
    return lax.dot_general(a, b, (((1,), (1,)), ((), ())), preferred_element_type=F32)


def _const_spec(shape):
    return pl.BlockSpec(shape, lambda *_: (0,) * len(shape), pipeline_mode=pl.Buffered(1))


def _layer_spec(stacked_shape, layer):
    rest = tuple(stacked_shape[1:])
    return pl.BlockSpec((None,) + rest, lambda *_: (layer,) + (0,) * len(rest), pipeline_mode=pl.Buffered(1))


def _mod_kernel(c_ref, w_ref, b_ref, o_ref):
    c = c_ref[...]
    a = (c * jax.nn.sigmoid(c)).astype(BF16)
    o_ref[0] = _dot(a, w_ref[0].astype(BF16)) + b_ref[0]


def _modulation(c_rows, w_mod, b_mod):
    depth, d, n = w_mod.shape
    tn = 1536
    return pl.pallas_call(
        _mod_kernel,
        grid=(depth, n // tn),
        in_specs=[pl.BlockSpec((8, d), lambda l, j: (0, 0)),
                  pl.BlockSpec((1, d, tn), lambda l, j: (l, 0, j)),
                  pl.BlockSpec((1, 1, tn), lambda l, j: (l, 0, j))],
        out_specs=pl.BlockSpec((1, 8, tn), lambda l, j: (l, 0, j)),
        out_shape=jax.ShapeDtypeStruct((depth, 8, n), F32),
        compiler_params=_params(("parallel", "parallel"), "modulation"),
        name="modulation",
    )(c_rows, w_mod, b_mod.reshape(depth, 1, n))


def _rms_rows(x):
    return x * lax.rsqrt(jnp.mean(x * x, axis=-1, keepdims=True) + EPS)


def _head_rms(z, pool, gain):
    z2 = z * z
    hi = z2.astype(BF16)
    lo = (z2 - hi.astype(F32)).astype(BF16)
    ms = _dot(hi, pool) + _dot(lo, pool)
    return z * lax.rsqrt(ms + EPS) * gain


def _rope(z, cos, sin):
    lane = lax.broadcasted_iota(jnp.int32, z.shape, 1)
    first = (lane % 32) < 16
    partner = jnp.where(first, pltpu.roll(z, LANES - 16, 1), pltpu.roll(z, 16, 1))
    return z * cos + partner * sin


def _inproj_kernel(*refs, d_model, rope, sub_rows):
    x_ref, mod_ref, g1_ref, w_ref, pool_ref, gq_ref, gk_ref, cos_ref, sin_ref, gb_ref = refs[:10]
    for sub in range(x_ref.shape[1] // sub_rows):
        rows = pl.ds(sub * sub_rows, sub_rows)
        _inproj_tile(x_ref.at[:, rows, :], mod_ref, g1_ref, w_ref, pool_ref, gq_ref, gk_ref,
                     cos_ref.at[rows, :], sin_ref.at[rows, :], gb_ref, *[o.at[:, rows, :] for o in refs[10:]],
                     d_model=d_model, rope=rope)


def _inproj_tile(x_ref, mod_ref, g1_ref, w_ref, pool_ref, gq_ref, gk_ref, cos_ref, sin_ref, gb_ref,
                 aq_ref, ak_ref, av_ref, bq_ref, bk_ref, bv_ref, cq_ref, ck_ref, cv_ref, co_ref, cg_ref,
                 dq_ref, dk_ref, dv_ref, *, d_model, rope):
    d = d_model
    mod = mod_ref[0]
    xn = _rms_rows(x_ref[0]) * g1_ref[...]
    hb = (xn * (1.0 + mod[:, d:2 * d]) + mod[:, 0:d]).astype(BF16)
    projected = _dot(hb, w_ref[...])

    o_b = GROUP_W + 2 * KV_W
    o_c = 2 * o_b
    o_g = o_c + 4 * GROUP_W
    o_d = o_g + 4 * N_HEADS

    def cols(start, width=GROUP_W):
        return projected[:, start:start + width]

    def rot(z):
        if not rope:
            return z
        cos, sin = cos_ref[...], sin_ref[...]
        return jnp.concatenate([_rope(z[:, i:i + LANES], cos, sin) for i in range(0, z.shape[1], LANES)], axis=1)

    def gqa_order(z):
        a, b = z[:, :LANES], z[:, LANES:]
        left = lax.broadcasted_iota(jnp.int32, a.shape, 1) < HEAD_DIM
        return jnp.concatenate([jnp.where(left, a, pltpu.roll(b, HEAD_DIM, 1)),
                                jnp.where(left, pltpu.roll(a, HEAD_DIM, 1), b)], axis=1)

    scale = HEAD_DIM ** -0.5
    pool = pool_ref[...]
    aq_ref[0] = (gqa_order(rot(_head_rms(cols(0), pool, gq_ref[...]))) * (scale * LOG2_E)).astype(BF16)
    akv = cols(GROUP_W)
    ak_ref[0] = rot(_head_rms(akv[:, :KV_W], pool[:KV_W, :KV_W], gk_ref[...])).astype(BF16)
    ones = jnp.ones((akv.shape[0], LANES), F32)
    av_ref[0] = jnp.concatenate([akv[:, KV_W:], ones], axis=1).astype(BF16)
    bq_ref[0] = (gqa_order(rot(cols(o_b))) * (scale * LOG2_E)).astype(BF16)
    bkv = cols(o_b + GROUP_W)
    bk_ref[0] = rot(bkv[:, :KV_W]).astype(BF16)
    bv_ref[0] = jnp.concatenate([bkv[:, KV_W:], ones], axis=1).astype(BF16)
    cq_ref[0] = (cols(o_c) * scale).astype(BF16)
    ck_ref[0] = cols(o_c + GROUP_W).astype(BF16)
    cv_ref[0] = cols(o_c + 2 * GROUP_W).astype(BF16)
    co_ref[0] = cols(o_c + 3 * GROUP_W)
    dq_ref[0] = (cols(o_d) * scale).astype(BF16)
    dk_ref[0] = cols(o_d + GROUP_W).astype(BF16)
    dv_ref[0] = cols(o_d + 2 * GROUP_W).astype(BF16)
    gates = cols(o_g, LANES) + gb_ref[...]
    lane = lax.broadcasted_iota(jnp.int32, gates.shape, 1)
    gates = jnp.where((lane % 8) >= 4, _log_sigmoid(gates), gates) * LOG2_E
    cg_ref[0] = jnp.where(lane < 4 * N_HEADS, gates, 0.0)


def _in_proj(x, mod, mod_row, g1, w, pool, gq, gk, cos, sin, gbias, *, layer, rope, tm):
    b, t, d = x.shape
    widths = [(GROUP_W, BF16), (KV_W, BF16), (2 * LANES, BF16),
              (GROUP_W, BF16), (KV_W, BF16), (2 * LANES, BF16),
              (GROUP_W, BF16), (GROUP_W, BF16), (GROUP_W, BF16), (GROUP_W, F32), (LANES, F32),
              (GROUP_W, BF16), (GROUP_W, BF16), (GROUP_W, BF16)]
    row = lambda wd: pl.BlockSpec((1, wd), lambda bi, j: (0, 0))
    return pl.pallas_call(
        functools.partial(_inproj_kernel, d_model=d, rope=rope, sub_rows=min(tm, TOKEN_TILE)),
        grid=(b, t // tm),
        in_specs=[pl.BlockSpec((1, tm, d), lambda bi, j: (bi, j, 0)),
                  pl.BlockSpec((1, 1, 6 * d), lambda bi, j: (mod_row(bi), 0, 0)),
                  row(d),
                  _layer_spec(w.shape, layer),
                  _const_spec((GROUP_W, GROUP_W)),
                  row(GROUP_W), row(KV_W),
                  pl.BlockSpec((tm, LANES), lambda bi, j: (j, 0)),
                  pl.BlockSpec((tm, LANES), lambda bi, j: (j, 0)),
                  row(LANES)],
        out_specs=[pl.BlockSpec((1, tm, wd), lambda bi, j: (bi, j, 0)) for wd, _ in widths],
        out_shape=[jax.ShapeDtypeStruct((b, t, wd), dt) for wd, dt in widths],
        compiler_params=_params(("parallel", "parallel"), "in_proj"),
        name="in_proj",
    )(x, mod, g1, w, pool, gq, gk, cos, sin, gbias)


def _stack_gqa(q):
    qa, qb = q[:, :LANES], q[:, LANES:]
    left = lax.broadcasted_iota(jnp.int32, qa.shape, 1) < HEAD_DIM
    zero = jnp.zeros_like(qa)
    return jnp.concatenate([jnp.where(left, qa, zero), jnp.where(left, qb, zero),
                            jnp.where(left, zero, qa), jnp.where(left, zero, qb)], axis=0)


def _unstack_gqa(o, t):
    left = lax.broadcasted_iota(jnp.int32, (t, LANES), 1) < HEAD_DIM
    return jnp.concatenate([jnp.where(left, o[0:t], o[2 * t:3 * t]),
                            jnp.where(left, o[t:2 * t], o[3 * t:4 * t])], axis=1)


def _attn_a_kernel(safe_ref, q_ref, k_ref, v_ref, kc_ref, vc_ref, o_ref, qs_ref, m_ref, acc_ref, *, tq, tk, n_kb):
    qs_ref[...] = _stack_gqa(q_ref[0])
    heads = [slice(h * tq, (h + 1) * tq) for h in range(N_HEADS)]

    @pl.when(safe_ref[0] != 0)
    def _():
        for rows in heads:
            p = jnp.exp2(_dot_nt(qs_ref[rows], kc_ref[0]))
            acc_ref[rows] = _dot(p.astype(BF16), vc_ref[0])

        def body(kb, carry):
            start = pl.multiple_of(kb * tk, tk)
            for rows in heads:
                p = jnp.exp2(_dot_nt(qs_ref[rows], k_ref[0, pl.ds(start, tk), :]))
                acc_ref[rows] += _dot(p.astype(BF16), v_ref[0, pl.ds(start, tk), :])
            return carry

        lax.fori_loop(0, n_kb, body, 0)

    @pl.when(safe_ref[0] == 0)
    def _():
        for rows in heads:
            s = _dot_nt(qs_ref[rows], kc_ref[0])
            m0 = jnp.max(s, axis=-1, keepdims=True)
            m_ref[rows] = m0
            acc_ref[rows] = _dot(jnp.exp2(s - m0).astype(BF16), vc_ref[0])

        def body(kb, carry):
            start = pl.multiple_of(kb * tk, tk)
            for rows in heads:
                s = _dot_nt(qs_ref[rows], k_ref[0, pl.ds(start, tk), :])
                m_prev = m_ref[rows]
                m_new = jnp.maximum(m_prev, jnp.max(s, axis=-1, keepdims=True))
                p = jnp.exp2(s - m_new)
                acc_ref[rows] = jnp.exp2(m_prev - m_new) * acc_ref[rows] + _dot(
                    p.astype(BF16), v_ref[0, pl.ds(start, tk), :])
                m_ref[rows] = m_new
            return carry

        lax.fori_loop(0, n_kb, body, 0)

    acc = acc_ref[...]
    o_ref[0] = _unstack_gqa(acc[:, :LANES] / acc[:, LANES:], tq).astype(o_ref.dtype)


def _attn_global(safe, q, k, v, kc, vc, *, tq, tk):
    b, t, _ = q.shape
    n_ctx = kc.shape[1]
    per_b = lambda n, wd: pl.BlockSpec((1, n, wd), lambda bi, i: (bi, 0, 0))
    return pl.pallas_call(
        functools.partial(_attn_a_kernel, tq=tq, tk=tk, n_kb=t // tk),
        grid=(b, t // tq),
        in_specs=[pl.BlockSpec(memory_space=pltpu.SMEM),
                  pl.BlockSpec((1, tq, GROUP_W), lambda bi, i: (bi, i, 0)),
                  per_b(t, KV_W), per_b(t, 2 * LANES), per_b(n_ctx, KV_W), per_b(n_ctx, 2 * LANES)],
        out_specs=pl.BlockSpec((1, tq, GROUP_W), lambda bi, i: (bi, i, 0)),
        out_shape=jax.ShapeDtypeStruct((b, t, GROUP_W), BF16),
        scratch_shapes=[pltpu.VMEM((4 * tq, LANES), BF16), pltpu.VMEM((4 * tq, 1), F32),
                        pltpu.VMEM((4 * tq, 2 * LANES), F32)],
        compiler_params=_params(("parallel", "parallel"), "attn_global"),
        name="attn_global",
    )(safe, q, k, v, kc, vc)


def _sink_column(sink_ref, t):
    row = lax.broadcasted_iota(jnp.int32, (4 * t, 1), 0)
    col = jnp.full((4 * t, 1), sink_ref[3], F32)
    for h in (2, 1, 0):
        col = jnp.where(row < (h + 1) * t, sink_ref[h], col)
    return col


def _attn_b_kernel(sink_ref, q_ref, k_ref, v_ref, kc_ref, vc_ref, o_ref, *, tq, tiles_per_step, t_total):
    span = tq + 2 * WINDOW
    for sub in range(tiles_per_step):
        q0 = (pl.program_id(1) * tiles_per_step + sub) * tq
        ks = pl.multiple_of(jnp.clip(q0 - WINDOW, 0, t_total - span), WINDOW)
        qs = _stack_gqa(q_ref[0, sub * tq:(sub + 1) * tq, :])
        kw = k_ref[0, pl.ds(ks, span), :]
        vw = v_ref[0, pl.ds(ks, span), :]
        q_pos = q0 + lax.broadcasted_iota(jnp.int32, (tq, span), 0)
        k_pos = ks + lax.broadcasted_iota(jnp.int32, (tq, span), 1)
        in_window = jnp.abs(k_pos - q_pos) <= WINDOW
        outs = []
        for h in range(N_HEADS):
            qh = qs[h * tq:(h + 1) * tq]
            s = jnp.where(in_window, _dot_nt(qh, kw), NEG_INF)
            sc = _dot_nt(qh, kc_ref[0])
            sink = sink_ref[h] * LOG2_E
            m = jnp.maximum(jnp.maximum(jnp.max(s, axis=-1, keepdims=True), jnp.max(sc, axis=-1, keepdims=True)),
                            sink)
            acc = _dot(jnp.exp2(s - m).astype(BF16), vw) + _dot(jnp.exp2(sc - m).astype(BF16), vc_ref[0])
            outs.append(acc[:, :LANES] / (acc[:, LANES:] + jnp.exp2(sink - m)))
        o_ref[0, sub * tq:(sub + 1) * tq, :] = _unstack_gqa(jnp.concatenate(outs, axis=0), tq).astype(o_ref.dtype)


def _attn_window(sink, q, k, v, kc, vc, *, tq):
    b, t, _ = q.shape
    n_ctx = kc.shape[1]
    assert t >= tq + 2 * WINDOW
    tiles = WINDOW_TILES_PER_STEP if (t // tq) % WINDOW_TILES_PER_STEP == 0 else 1
    per_b = lambda n, wd: pl.BlockSpec((1, n, wd), lambda bi, i: (bi, 0, 0))
    return pl.pallas_call(
        functools.partial(_attn_b_kernel, tq=tq, tiles_per_step=tiles, t_total=t),
        grid=(b, t // (tq * tiles)),
        in_specs=[pl.BlockSpec(memory_space=pltpu.SMEM),
                  pl.BlockSpec((1, tq * tiles, GROUP_W), lambda bi, i: (bi, i, 0)),
                  per_b(t, KV_W), per_b(t, 2 * LANES), per_b(n_ctx, KV_W), per_b(n_ctx, 2 * LANES)],
        out_specs=pl.BlockSpec((1, tq * tiles, GROUP_W), lambda bi, i: (bi, i, 0)),
        out_shape=jax.ShapeDtypeStruct((b, t, GROUP_W), BF16),
        compiler_params=_params(("parallel", "parallel"), "attn_window"),
        name="attn_window",
    )(sink, q, k, v, kc, vc)


def _head_masks(shape):
    lane = lax.broadcasted_iota(jnp.int32, shape, 1)
    return [(lane >= h * HEAD_DIM) & (lane < (h + 1) * HEAD_DIM) for h in range(N_HEADS)]


def _stack_mha(q):
    zero = jnp.zeros_like(q)
    return jnp.concatenate([jnp.where(mk, q, zero) for mk in _head_masks(q.shape)], axis=0)


def _unstack_mha(o, t):
    masks = _head_masks((t, GROUP_W))
    out = jnp.where(masks[0], o[0:t], 0.0)
    for h in range(1, N_HEADS):
        out = jnp.where(masks[h], o[h * t:(h + 1) * t], out)
    return out


def _attn_d_kernel(q_ref, k_ref, v_ref, kc_ref, vc_ref, bias_ref, o_ref, *, rows_per_step, rows):
    kc = kc_ref[0]
    vc = vc_ref[0]
    span = NA_KH * GRID_W

    def body(rr, carry):
        r = pl.program_id(1) * rows_per_step + rr
        rs = jnp.clip(r - NA_KH // 2, 0, rows - NA_KH)
        k0 = pl.multiple_of(rs * GRID_W, GRID_W)
        q0 = pl.multiple_of(rr * GRID_W, GRID_W)
        qs = _stack_mha(q_ref[0, pl.ds(q0, GRID_W), :])
        s = _dot_nt(qs, k_ref[0, pl.ds(k0, span), :]) + bias_ref[r - rs]
        sc = _dot_nt(qs, kc)
        m = jnp.maximum(jnp.max(s, axis=-1, keepdims=True), jnp.max(sc, axis=-1, keepdims=True))
        p = jnp.exp(s - m)
        pc = jnp.exp(sc - m)
        l = jnp.sum(p, axis=-1, keepdims=True) + jnp.sum(pc, axis=-1, keepdims=True)
        o = _dot(p.astype(BF16), v_ref[0, pl.ds(k0, span), :]) + _dot(pc.astype(BF16), vc)
        o_ref[0, pl.ds(q0, GRID_W), :] = _unstack_mha(o / l, GRID_W).astype(o_ref.dtype)
        return carry

    lax.fori_loop(0, rows_per_step, body, 0, unroll=True)


def _attn_neighbour(q, k, v, kc, vc, bias, *, rows_per_step):
    b, t, _ = q.shape
    n_ctx = kc.shape[1]
    rows = t // GRID_W
    assert rows >= NA_KH and rows % rows_per_step == 0
    tq = rows_per_step * GRID_W
    per_b = lambda n: pl.BlockSpec((1, n, GROUP_W), lambda bi, i: (bi, 0, 0))
    return pl.pallas_call(
        functools.partial(_attn_d_kernel, rows_per_step=rows_per_step, rows=rows),
        grid=(b, t // tq),
        in_specs=[pl.BlockSpec((1, tq, GROUP_W), lambda bi, i: (bi, i, 0)),
                  per_b(t), per_b(t), per_b(n_ctx), per_b(n_ctx),
                  _const_spec(bias.shape)],
        out_specs=pl.BlockSpec((1, tq, GROUP_W), lambda bi, i: (bi, i, 0)),
        out_shape=jax.ShapeDtypeStruct((b, t, GROUP_W), BF16),
        compiler_params=_params(("parallel", "parallel"), "attn_neighbour"),
        name="attn_neighbour",
    )(q, k, v, kc, vc, bias)


def _neighbour_bias(rpb):
    w = jnp.arange(GRID_W)
    cs = jnp.clip(w - NA_KW // 2, 0, GRID_W - NA_KW)
    col = jnp.arange(GRID_W)
    valid = (col[None, :] >= cs[:, None]) & (col[None, :] < cs[:, None] + NA_KW)
    dc = jnp.clip(col[None, :] - w[:, None] + (NA_KW - 1), 0, 2 * NA_KW - 2)
    pick_c = (jnp.arange(2 * NA_KW - 1)[:, None, None] == dc[None]).astype(F32)
    by_col = jnp.einsum('hrc,cwk->hwrk', rpb.astype(F32), pick_c, precision=lax.Precision.HIGHEST)
    by_col = jnp.where(valid[None, :, None, :], by_col, NEG_INF)
    flat = by_col.reshape(N_HEADS * GRID_W, (2 * NA_KH - 1) * GRID_W)
    return jnp.stack([flat[:, (NA_KH - 1 - o) * GRID_W:(2 * NA_KH - 1 - o) * GRID_W] for o in range(NA_KH)])


def _ctx_attn_kernel(sink_ref, aq_ref, ak_ref, av_ref, bq_ref, bk_ref, bv_ref, dq_ref, dk_ref, dv_ref,
                     ya_ref, yb_ref, yd_ref, *, n):
    s = _dot_nt(_stack_gqa(aq_ref[0]), ak_ref[0])
    acc = _dot(jnp.exp2(s - jnp.max(s, axis=-1, keepdims=True)).astype(BF16), av_ref[0])
    ya_ref[0] = _unstack_gqa(acc[:, :LANES] / acc[:, LANES:], n).astype(ya_ref.dtype)
    s = _dot_nt(_stack_gqa(bq_ref[0]), bk_ref[0])
    sink = _sink_column(sink_ref, n) * LOG2_E
    m = jnp.maximum(jnp.max(s, axis=-1, keepdims=True), sink)
    acc = _dot(jnp.exp2(s - m).astype(BF16), bv_ref[0])
    yb_ref[0] = _unstack_gqa(acc[:, :LANES] / (acc[:, LANES:] + jnp.exp2(sink - m)), n).astype(yb_ref.dtype)
    s = _dot_nt(_stack_mha(dq_ref[0]), dk_ref[0])
    p = jnp.exp(s - jnp.max(s, axis=-1, keepdims=True))
    o = _dot(p.astype(BF16), dv_ref[0]) / jnp.sum(p, axis=-1, keepdims=True)
    yd_ref[0] = _unstack_mha(o, n).astype(yd_ref.dtype)


def _ctx_attention(sink, aq, ak, av, bq, bk, bv, dq, dk, dv):
    b, n, _ = aq.shape
    spec = lambda wd: pl.BlockSpec((1, n, wd), lambda bi: (bi, 0, 0))
    return pl.pallas_call(
        functools.partial(_ctx_attn_kernel, n=n),
        grid=(b,),
        in_specs=[pl.BlockSpec(memory_space=pltpu.SMEM),
                  spec(GROUP_W), spec(KV_W), spec(2 * LANES), spec(GROUP_W), spec(KV_W), spec(2 * LANES),
                  spec(GROUP_W), spec(GROUP_W), spec(GROUP_W)],
        out_specs=[spec(GROUP_W)] * 3,
        out_shape=[jax.ShapeDtypeStruct((b, n, GROUP_W), BF16)] * 3,
        compiler_params=_params(("parallel",), "ctx_attention"),
        name="ctx_attention",
    )(sink, aq, ak, av, bq, bk, bv, dq, dk, dv)


def _log_sigmoid(x):
    return jnp.minimum(x, 0.0) - jnp.log1p(jnp.exp(-jnp.abs(x)))


def _split3(x):
    x1 = x.astype(BF16)
    r1 = x - x1.astype(F32)
    x2 = r1.astype(BF16)
    return x1, x2, (r1 - x2.astype(F32)).astype(BF16)


def _exact_dot_01(sel, x):
    return sum(_dot(sel, piece) for piece in _split3(x))


def _exact_dot_01_rhs(x, sel):
    return sum(_dot(piece, sel) for piece in _split3(x))


def _mlstm_chunk(q, k, v, gates, expand, tri, neg_mask, ones_cat, block, head_rows, s_ref, n_ref, m_ref, *,
                 reverse, L):
    ge = _exact_dot_01_rhs(gates, expand)
    li = ge[:, :GROUP_W]
    cum = _exact_dot_01(tri, ge[:, GROUP_W:])
    a_t = (li - cum).T
    m_prev = m_ref[0:1, :]
    head_b = [head_rows[h].astype(BF16) for h in range(N_HEADS)]
    k_t = k.astype(F32).T.astype(BF16)
    mu = jnp.zeros((L, GROUP_W), F32)
    scores = []
    for h in range(N_HEADS):
        lane0 = h * HEAD_DIM
        a_m = a_t[lane0:lane0 + 1, :] + neg_mask
        mu_h = jnp.maximum(jnp.max(a_m, axis=-1, keepdims=True), m_prev[:, lane0:lane0 + 1])
        qk = _dot(q * head_b[h], k_t)
        scores.append((qk * jnp.exp2(a_m - mu_h)).astype(BF16))
        mu = mu + mu_h * head_rows[h]
    s_cat = jnp.concatenate(scores, axis=1)
    v_cat = jnp.concatenate([v * hb for hb in head_b], axis=0)
    w_inter = jnp.exp2(m_prev - mu)
    num = _dot(s_cat, v_cat) + w_inter * _dot(q, s_ref[...].astype(BF16))
    den = _dot(s_cat, ones_cat) + w_inter * _dot(q, n_ref[...].astype(BF16))
    h_out = num / jnp.maximum(jnp.abs(den), jnp.exp2(-(cum + mu)))

    end_row = 0 if reverse else L - 1
    cum_end = cum[end_row:end_row + 1, :]
    log_end = cum_end - cum + li
    m_new = jnp.maximum(cum_end + m_prev, jnp.max(log_end, axis=0, keepdims=True))
    decay = jnp.exp2(cum_end + m_prev - m_new)
    w_end = jnp.exp2(log_end - m_new)
    upd =_dot(k_t, jnp.concatenate([(v.astype(F32) * w_end).astype(BF16), w_end.astype(BF16)], axis=1))
    s_ref[...] = (s_ref[...] * decay + upd[:, :GROUP_W]) * block
    n_ref[...] = (n_ref[...] * decay + upd[:, GROUP_W:]) * block
    m_ref[...] = jnp.broadcast_to(m_new, m_ref.shape)
    return h_out


def _mlstm_kernel(qf_ref, kf_ref, vf_ref, gf_ref, qb_ref, kb_ref, vb_ref, gb_ref,
                  expand_ref, tri_ref, neg_ref, ones_ref, block_ref, heads_ref, s0_ref, n0_ref, m0_ref,
                  hf_ref, hb_ref, s_out, n_out, m_out, s_ref, n_ref, m_ref, *, chunk, chunks_per_step):
    @pl.when(pl.program_id(1) == 0)
    def _():
        s_ref[...] = s0_ref[0]
        n_ref[...] = n0_ref[0]
        m_ref[...] = m0_ref[0]

    sides = ((qf_ref, kf_ref, vf_ref, gf_ref, hf_ref), (qb_ref, kb_ref, vb_ref, gb_ref, hb_ref))
    for c in range(chunks_per_step):
        for d, (q_ref, k_ref, v_ref, g_ref, h_ref) in enumerate(sides):
            rows = pl.ds((chunks_per_step - 1 - c if d else c) * chunk, chunk)
            h = _mlstm_chunk(q_ref[0, rows, :], k_ref[0, rows, :], v_ref[0, rows, :], g_ref[0, rows, :],
                             expand_ref[d], tri_ref[d], neg_ref[d], ones_ref[...], block_ref[...], heads_ref,
                             s_ref.at[d], n_ref.at[d], m_ref.at[d], reverse=bool(d), L=chunk)
            h_ref[0, rows, :] = h.astype(h_ref.dtype)
    s_out[0] = s_ref[...]
    n_out[0] = n_ref[...]
    m_out[0] = m_ref[...]


def _mlstm_consts(L):
    lane = jnp.arange(LANES)[:, None]
    col = jnp.arange(2 * GROUP_W)[None, :]
    col_head = (col % GROUP_W) // HEAD_DIM
    expand = jnp.stack([(lane == 8 * d + 4 * (col // GROUP_W) + col_head) for d in range(2)]).astype(BF16)
    t_idx = jnp.arange(L)[:, None]
    s_idx = jnp.arange(L)[None, :]
    seen = jnp.stack([s_idx <= t_idx, s_idx >= t_idx])
    head = jnp.arange(GROUP_W) // HEAD_DIM
    ones_cat = (jnp.repeat(jnp.arange(N_HEADS), L)[:, None] == head[None, :]).astype(BF16)
    block = (head[:, None] == head[None, :]).astype(F32)
    head_rows = (jnp.arange(N_HEADS)[:, None, None] == head[None, None, :]).astype(F32)
    return expand, seen.astype(BF16), jnp.where(seen, 0.0, NEG_INF).astype(F32), ones_cat, block, head_rows


def _mlstm_scan(q, k, v, gates, state, consts):
    b, t, _ = q.shape
    L = MLSTM_L
    per_step = MLSTM_CHUNKS_PER_STEP if (t // L) % MLSTM_CHUNKS_PER_STEP == 0 else 1
    nc = t // (L * per_step)
    fwd = lambda wd: pl.BlockSpec((1, L * per_step, wd), lambda bi, j: (bi, j, 0))
    bwd = lambda wd: pl.BlockSpec((1, L * per_step, wd), lambda bi, j: (bi, nc - 1 - j, 0))
    st = lambda r: pl.BlockSpec((1, 2, r, GROUP_W), lambda bi, j: (bi, 0, 0, 0))
    tok_specs = [fwd(GROUP_W), fwd(GROUP_W), fwd(GROUP_W), fwd(LANES),
                 bwd(GROUP_W), bwd(GROUP_W), bwd(GROUP_W), bwd(LANES)]
    hf, hb, s_fin, n_fin, m_fin = pl.pallas_call(
        functools.partial(_mlstm_kernel, chunk=L, chunks_per_step=per_step),
        grid=(b, nc),
        in_specs=tok_specs + [_const_spec(c.shape) for c in consts] + [st(GROUP_W), st(GROUP_W), st(8)],
        out_specs=[fwd(GROUP_W), bwd(GROUP_W), st(GROUP_W), st(GROUP_W), st(8)],
        out_shape=[jax.ShapeDtypeStruct((b, t, GROUP_W), BF16), jax.ShapeDtypeStruct((b, t, GROUP_W), BF16),
                   jax.ShapeDtypeStruct((b, 2, GROUP_W, GROUP_W), F32),
                   jax.ShapeDtypeStruct((b, 2, GROUP_W, GROUP_W), F32),
                   jax.ShapeDtypeStruct((b, 2, 8, GROUP_W), F32)],
        scratch_shapes=[pltpu.VMEM((2, GROUP_W, GROUP_W), F32), pltpu.VMEM((2, GROUP_W, GROUP_W), F32),
                        pltpu.VMEM((2, 8, GROUP_W), F32)],
        compiler_params=_params(("parallel", "arbitrary"), "mlstm"),
        name="mlstm",
    )(q, k, v, gates, q, k, v, gates, *consts, *state)
    return hf, hb, (s_fin, n_fin, m_fin)


def _outproj_kernel(x_ref, ya_ref, yb_ref, hf_ref, hb_ref, og_ref, yd_ref, mod_ref, gg_ref, w_ref, o_ref, *, d_model):
    d = d_model
    yc = jax.nn.sigmoid(og_ref[0]) * (hf_ref[0].astype(F32) + hb_ref[0].astype(F32))
    parts = []
    for i, y in enumerate((ya_ref[0], yb_ref[0], yc, yd_ref[0])):
        parts.append((_rms_rows(y.astype(F32)) * gg_ref[i:i + 1, :]).astype(BF16))
    res = _dot(jnp.concatenate(parts, axis=1), w_ref[...])
    o_ref[0] = x_ref[0] + mod_ref[0][:, 2 * d:3 * d] * res


def _out_proj(x, ya, yb, hf, hb, o_gate, yd, mod, mod_row, gg, w, *, layer, tm):
    b, t, d = x.shape
    tokd = pl.BlockSpec((1, tm, d), lambda bi, j: (bi, j, 0))
    tokg = pl.BlockSpec((1, tm, GROUP_W), lambda bi, j: (bi, j, 0))
    return pl.pallas_call(
        functools.partial(_outproj_kernel, d_model=d),
        grid=(b, t // tm),
        in_specs=[tokd, tokg, tokg, tokg, tokg, tokg, tokg,
                  pl.BlockSpec((1, 1, 6 * d), lambda bi, j: (mod_row(bi), 0, 0)),
                  _const_spec(gg.shape), _layer_spec(w.shape, layer)],
        out_specs=tokd,
        out_shape=jax.ShapeDtypeStruct((b, t, d), F32),
        compiler_params=_params(("parallel", "parallel"), "out_proj"),
        name="out_proj",
    )(x, ya, yb, hf, hb, o_gate, yd, mod, gg, w)


def _ffn_kernel(*refs, d_model, d_ff, tm, final_norm):
    if final_norm:
        x_ref, xp_ref, xn_ref, mod_ref, g2_ref, wu_ref, cw_ref, wd_ref, gf_ref, o_ref = refs
    else:
        x_ref, xp_ref, xn_ref, mod_ref, g2_ref, wu_ref, cw_ref, wd_ref, o_ref = refs
    d = d_model
    j = pl.program_id(1)
    mod = mod_ref[0]
    g2 = g2_ref[...]

    def norm_mod(x):
        return _rms_rows(x) * g2 * (1.0 + mod[:, 4 * d:5 * d]) + mod[:, 3 * d:4 * d]

    keep_prev = (j > 0).astype(F32)
    keep_next = (j < pl.num_programs(1) - 1).astype(F32)
    x = x_ref[0]
    h_ext = jnp.concatenate([norm_mod(xp_ref[0]) * keep_prev, norm_mod(x), norm_mod(xn_ref[0]) * keep_next],
                            axis=0).astype(BF16)
    n_ext = tm + 2 * HALO

    u = _dot(h_ext, wu_ref[...])
    cw = cw_ref[...]
    u = (pltpu.roll(u, 1, 0) * cw[0:1] + u * cw[1:2] + pltpu.roll(u, n_ext - 1, 0) * cw[2:3] + cw[3:4])[HALO:HALO + tm]
    gate, val = u[:, :d_ff], u[:, d_ff:]
    act = (gate * jax.nn.sigmoid(gate) * val).astype(BF16)
    y = x + mod[:, 5 * d:6 * d] * _dot(act, wd_ref[...])
    if final_norm:
        y = _rms_rows(y) * gf_ref[...]
    o_ref[0] = y


def _conv_ffn(x, mod, mod_row, g2, wu, cw, wd, g_final, *, layer, tm):
    b, t, d = x.shape
    d_ff = wd.shape[1]
    final_norm = g_final is not None
    hb = tm // HALO
    last = t // HALO - 1
    row = pl.BlockSpec((1, d), lambda bi, j: (0, 0))
    in_specs = [pl.BlockSpec((1, tm, d), lambda bi, j: (bi, j, 0)),
                pl.BlockSpec((1, HALO, d), lambda bi, j: (bi, jnp.maximum(j * hb - 1, 0), 0)),
                pl.BlockSpec((1, HALO, d), lambda bi, j: (bi, jnp.minimum((j + 1) * hb, last), 0)),
                pl.BlockSpec((1, 1, 6 * d), lambda bi, j: (mod_row(bi), 0, 0)),
                row,
                _layer_spec(wu.shape, layer), _const_spec(cw.shape), _layer_spec(wd.shape, layer)]
    args = [x, x, x, mod, g2, wu, cw, wd]
    if final_norm:
        in_specs.append(row)
        args.append(g_final)
    return pl.pallas_call(
        functools.partial(_ffn_kernel, d_model=d, d_ff=d_ff, tm=tm, final_norm=final_norm),
        grid=(b, t // tm),
        in_specs=in_specs,
        out_specs=pl.BlockSpec((1, tm, d), lambda bi, j: (bi, j, 0)),
        out_shape=jax.ShapeDtypeStruct((b, t, d), F32),
        compiler_params=_params(("parallel", "parallel"), "conv_ffn"),
        name="conv_ffn",
    )(*args)


_Q_HEAD_ORDER = (0, 2, 1, 3)


def _permute_heads(w, axis):
    parts = [lax.slice_in_dim(w, h * HEAD_DIM, (h + 1) * HEAD_DIM, axis=axis) for h in _Q_HEAD_ORDER]
    return jnp.concatenate(parts, axis=axis)


def _layout_w_in(w):
    return w.astype(BF16)


def _layout_w_out(w):
    w = w.astype(BF16)
    return jnp.concatenate([_permute_heads(w[:, 0:GROUP_W], 1), _permute_heads(w[:, GROUP_W:2 * GROUP_W], 1),
                            w[:, 2 * GROUP_W:]], axis=1)


def _rope_tables(t):
    pos = jnp.arange(t)
    row = (pos // GRID_W).astype(F32)
    col = (pos % GRID_W).astype(F32)
    n_freq = HEAD_DIM // 4
    freqs = ROPE_THETA ** (-jnp.arange(n_freq, dtype=F32) / n_freq)
    ang_r, ang_c = row[:, None] * freqs, col[:, None] * freqs
    cos = jnp.concatenate([jnp.cos(ang_r), jnp.cos(ang_r), jnp.cos(ang_c), jnp.cos(ang_c)], axis=1)
    sin = jnp.concatenate([-jnp.sin(ang_r), jnp.sin(ang_r), -jnp.sin(ang_c), jnp.sin(ang_c)], axis=1)
    reps = LANES // HEAD_DIM
    return jnp.tile(cos, (1, reps)), jnp.tile(sin, (1, reps))


def _pick_tile(t, pref):
    while t % pref:
        pref //= 2
    return pref


def kernel(x, c, ctx, c_ctx, w_mod, b_mod, g_norm1, g_norm2, w_in, a_q_gain, a_k_gain, b_sink, c_gate_bias,
           d_rel_bias, g_group, w_out, w_up, conv_w, conv_b, w_down, g_final):
    batch, t, d = x.shape
    n_ctx = ctx.shape[1]
    depth = w_in.shape[0]
    d_ff = w_down.shape[1]
    assert batch < 8 and d_ff % LANES == 0 and t % MLSTM_L == 0 and n_ctx % MLSTM_L == 0

    c_rows = jnp.zeros((8, d), F32).at[:batch].set(c).at[batch].set(c_ctx)
    mod_all = _modulation(c_rows, w_mod, b_mod).reshape(depth, 8, 1, 6 * d)
    lat_row = lambda bi: bi
    ctx_row = lambda bi: batch

    cos, sin = _rope_tables(t)
    pool = jnp.where(jnp.arange(GROUP_W)[:, None] // HEAD_DIM == jnp.arange(GROUP_W)[None, :] // HEAD_DIM,
                     1.0 / HEAD_DIM, 0.0).astype(BF16)
    tm_lat = _pick_tile(t, TOKEN_TILE)
    tm_ctx = _pick_tile(n_ctx, CTX_TILE)
    zero_state = (jnp.zeros((batch, 2, GROUP_W, GROUP_W), F32), jnp.zeros((batch, 2, GROUP_W, GROUP_W), F32),
                  jnp.zeros((batch, 2, 8, GROUP_W), F32))
    mlstm_consts = _mlstm_consts(MLSTM_L)
    w_in_b = _layout_w_in(w_in)
    w_out_b = _layout_w_out(w_out)
    w_up_b = w_up.astype(BF16)
    w_down_b = w_down.astype(BF16)

    for layer in range(depth):
        need_ctx = layer < depth - 1
        mod = mod_all[layer]
        g1 = g_norm1[layer][None]
        g2 = g_norm2[layer][None]
        gq = jnp.tile(a_q_gain[layer], N_HEADS)[None]
        gk = jnp.tile(a_k_gain[layer], 2)[None]
        gbias = jnp.pad(c_gate_bias[layer], (0, LANES - 4 * N_HEADS))[None]
        gg = g_group[layer].reshape(N_HEADS, GROUP_W)
        gg = jnp.concatenate([_permute_heads(gg[0:2], 1), gg[2:4]], axis=0)
        conv_l = jnp.concatenate([conv_w[layer], conv_b[layer][None],
                                  jnp.zeros((8 - 1 - conv_w.shape[1], 2 * d_ff), F32)], axis=0)
        sink = b_sink[layer]
        bias_tab = _neighbour_bias(d_rel_bias[layer])

        proj = functools.partial(_in_proj, g1=g1, w=w_in_b, pool=pool, gq=gq, gk=gk, gbias=gbias, layer=layer)
        (aqc, akc, avc, bqc, bkc, bvc, cqc, ckc, cvc, coc, cgc, dqc, dkc, dvc) = proj(
            ctx, mod, ctx_row, cos=cos, sin=sin, rope=False, tm=tm_ctx)
        (aq, ak, av, bq, bk, bv, cq, ck, cv, co, cg, dq, dk, dv) = proj(
            x, mod, lat_row, cos=cos, sin=sin, rope=True, tm=_pick_tile(t, 2 * TOKEN_TILE))

        score_bound = 1.02 * LOG2_E * HEAD_DIM ** 0.5 * jnp.max(jnp.abs(a_q_gain[layer])) * jnp.max(jnp.abs(a_k_gain[layer]))
        safe = (score_bound <= MAX_UNSHIFTED_LOG2_SCORE).astype(jnp.int32).reshape(1)
        ya = _attn_global(safe, aq, ak, av, akc, avc, tq=_pick_tile(t, ATTN_Q_TILE), tk=_pick_tile(t, ATTN_K_TILE))
        yb = _attn_window(sink, bq, bk, bv, bkc, bvc, tq=_pick_tile(t, WINDOW_Q_TILE))
        yd = _attn_neighbour(dq, dk, dv, dkc, dvc, bias_tab, rows_per_step=NEIGHBOUR_ROWS)

        hcf, hcb, ctx_state = _mlstm_scan(cqc, ckc, cvc, cgc, zero_state, mlstm_consts)
        hf, hb, _ = _mlstm_scan(cq, ck, cv, cg, ctx_state, mlstm_consts)

        x = _out_proj(x, ya, yb, hf, hb, co, yd, mod, lat_row, gg, w_out_b, layer=layer,
                      tm=_pick_tile(t, 2 * TOKEN_TILE))
        ffn = functools.partial(_conv_ffn, g2=g2, wu=w_up_b, cw=conv_l, wd=w_down_b, layer=layer)
        x = ffn(x, mod, lat_row, g_final=None if need_ctx else g_final[None], tm=tm_lat)
        if need_ctx:
            yac, ybc, ydc = _ctx_attention(sink, aqc, akc, avc, bqc, bkc, bvc, dqc, dkc, dvc)
            ctx = _out_proj(ctx, yac, ybc, hcf, hcb, coc, ydc, mod, ctx_row, gg, w_out_b, layer=layer, tm=tm_ctx)
            ctx = ffn(ctx, mod, ctx_row, g_final=None, tm=tm_ctx)
    return x
```

```python
import functools

import jax
import jax.numpy as jnp
from jax import lax
from jax.experimental import pallas as pl
from jax.experimental.pallas import tpu as pltpu

F32 = jnp.float32
BF16 = jnp.bfloat16

N_HEADS = 4
HEAD_DIM = 64
GROUP_W = N_HEADS * HEAD_DIM
KV_W = 2 * HEAD_DIM
GRID_W = 64
WINDOW = 128
NA_KH = 8
NA_KW = 16
ROPE_THETA = 10000.0
EPS = 1e-6
NEG_INF = -1e30
LANES = 128
MLSTM_L = 256
MLSTM_CHUNKS_PER_STEP = 2
HALO = 8
MIB = 1024 * 1024
LOG2_E = 1.4426950408889634
MAX_UNSHIFTED_LOG2_SCORE = 60.0


TOKEN_TILE = 512
CTX_TILE = 256
ATTN_Q_TILE = 1024
ATTN_K_TILE = 2048
WINDOW_Q_TILE = 256
WINDOW_TILES_PER_STEP = 8
NEIGHBOUR_ROWS = 16
VMEM_LIMIT_MIB = {"modulation": 40, "in_proj": 48, "attn_global": 48, "attn_window": 48, "attn_neighbour": 48,
                  "ctx_attention": 32, "mlstm": 48, "out_proj": 48, "conv_ffn": 56}


def _params(sem, name, n_inputs=0, fusible=()):
    fuse = [i in fusible for i in range(n_inputs)] if fusible else None
    return pltpu.CompilerParams(dimension_semantics=sem, vmem_limit_bytes=VMEM_LIMIT_MIB[name] * MIB,
                                allow_input_fusion=fuse)


def _dot(a, b):
    return jnp.dot(a, b, preferred_element_type=F32)


def _dot_nt(a, b):
    return lax.dot_general(a, b, (((1,), (1,)), ((), ())), preferred_element_type=F32)


def _const_spec(shape):
    return pl.BlockSpec(shape, lambda *_: (0,) * len(shape), pipeline_mode=pl.Buffered(1))


def _layer_spec(stacked_shape, layer):
    rest = tuple(stacked_shape[1:])
    return pl.BlockSpec((None,) + rest, lambda *_: (layer,) + (0,) * len(rest), pipeline_mode=pl.Buffered(1))


def _mod_kernel(c_ref, w_ref, b_ref, o_ref):
    c = c_ref[...]
    a = (c * jax.nn.sigmoid(c)).astype(BF16)
    o_ref[0] = _dot(a, w_ref[0].astype(BF16)) + b_ref[0]


def _modulation(c_rows, w_mod, b_mod):
    depth, d, n = w_mod.shape
    tn = 1536
    return pl.pallas_call(
        _mod_kernel,
        grid=(depth, n // tn),
        in_specs=[pl.BlockSpec((8, d), lambda l, j: (0, 0)),
                  pl.BlockSpec((1, d, tn), lambda l, j: (l, 0, j)),
                  pl.BlockSpec((1, 1, tn), lambda l, j: (l, 0, j))],
        out_specs=pl.BlockSpec((1, 8, tn), lambda l, j: (l, 0, j)),
        out_shape=jax.ShapeDtypeStruct((depth, 8, n), F32),
        compiler_params=_params(("parallel", "parallel"), "modulation"),
        name="modulation",
    )(c_rows, w_mod, b_mod.reshape(depth, 1, n))


def _rms_rows(x):
    return x * lax.rsqrt(jnp.mean(x * x, axis=-1, keepdims=True) + EPS)


def _head_rms(z, pool, gain):
    z2 = z * z
    hi = z2.astype(BF16)
    lo = (z2 - hi.astype(F32)).astype(BF16)
    ms = _dot(hi, pool) + _dot(lo, pool)
    return z * lax.rsqrt(ms + EPS) * gain


def _rope(z, cos, sin):
    lane = lax.broadcasted_iota(jnp.int32, z.shape, 1)
    first = (lane % 32) < 16
    partner = jnp.where(first, pltpu.roll(z, LANES - 16, 1), pltpu.roll(z, 16, 1))
    return z * cos + partner * sin


def _inproj_kernel(*refs, d_model, rope, sub_rows):
    x_ref, mod_ref, g1_ref, w_ref, pool_ref, gq_ref, gk_ref, cos_ref, sin_ref, gb_ref = refs[:10]
    for sub in range(x_ref.shape[1] // sub_rows):
        rows = pl.ds(sub * sub_rows, sub_rows)
        _inproj_tile(x_ref.at[:, rows, :], mod_ref, g1_ref, w_ref, pool_ref, gq_ref, gk_ref,
                     cos_ref.at[rows, :], sin_ref.at[rows, :], gb_ref, *[o.at[:, rows, :] for o in refs[10:]],
                     d_model=d_model, rope=rope)


def _inproj_tile(x_ref, mod_ref, g1_ref, w_ref, pool_ref, gq_ref, gk_ref, cos_ref, sin_ref, gb_ref,
                 aq_ref, ak_ref, av_ref, bq_ref, bk_ref, bv_ref, cq_ref, ck_ref, cv_ref, co_ref, cg_ref,
                 dq_ref, dk_ref, dv_ref, *, d_model, rope):
    d = d_model
    mod = mod_ref[0]
    xn = _rms_rows(x_ref[0]) * g1_ref[...]
    hb = (xn * (1.0 + mod[:, d:2 * d]) + mod[:, 0:d]).astype(BF16)
    projected = _dot(hb, w_ref[...])

    o_b = GROUP_W + 2 * KV_W
    o_c = 2 * o_b
    o_g = o_c + 4 * GROUP_W
    o_d = o_g + 4 * N_HEADS

    def cols(start, width=GROUP_W):
        return projected[:, start:start + width]

    def rot(z):
        if not rope:
            return z
        cos, sin = cos_ref[...], sin_ref[...]
        return jnp.concatenate([_rope(z[:, i:i + LANES], cos, sin) for i in range(0, z.shape[1], LANES)], axis=1)

    def gqa_order(z):
        a, b = z[:, :LANES], z[:, LANES:]
        left = lax.broadcasted_iota(jnp.int32, a.shape, 1) < HEAD_DIM
        return jnp.concatenate([jnp.where(left, a, pltpu.roll(b, HEAD_DIM, 1)),
                                jnp.where(left, pltpu.roll(a, HEAD_DIM, 1), b)], axis=1)

    scale = HEAD_DIM ** -0.5
    pool = pool_ref[...]
    aq_ref[0] = (gqa_order(rot(_head_rms(cols(0), pool, gq_ref[...]))) * (scale * LOG2_E)).astype(BF16)
    akv = cols(GROUP_W)
    ak_ref[0] = rot(_head_rms(akv[:, :KV_W], pool[:KV_W, :KV_W], gk_ref[...])).astype(BF16)
    ones = jnp.ones((akv.shape[0], LANES), F32)
    av_ref[0] = jnp.concatenate([akv[:, KV_W:], ones], axis=1).astype(BF16)
    bq_ref[0] = (gqa_order(rot(cols(o_b))) * (scale * LOG2_E)).astype(BF16)
    bkv = cols(o_b + GROUP_W)
    bk_ref[0] = rot(bkv[:, :KV_W]).astype(BF16)
    bv_ref[0] = jnp.concatenate([bkv[:, KV_W:], ones], axis=1).astype(BF16)
    cq_ref[0] = (cols(o_c) * scale).astype(BF16)
    ck_ref[0] = cols(o_c + GROUP_W).astype(BF16)
    cv_ref[0] = cols(o_c + 2 * GROUP_W).astype(BF16)
    co_ref[0] = cols(o_c + 3 * GROUP_W)
    dq_ref[0] = (cols(o_d) * scale).astype(BF16)
    dk_ref[0] = cols(o_d + GROUP_W).astype(BF16)
    dv_ref[0] = cols(o_d + 2 * GROUP_W).astype(BF16)
    gates = cols(o_g, LANES) + gb_ref[...]
    lane = lax.broadcasted_iota(jnp.int32, gates.shape, 1)
    gates = jnp.where((lane % 8) >= 4, _log_sigmoid(gates), gates) * LOG2_E
    cg_ref[0] = jnp.where(lane < 4 * N_HEADS, gates, 0.0)


def _in_proj(x, mod, mod_row, g1, w, pool, gq, gk, cos, sin, gbias, *, layer, rope, tm):
    b, t, d = x.shape
    widths = [(GROUP_W, BF16), (KV_W, BF16), (2 * LANES, BF16),
              (GROUP_W, BF16), (KV_W, BF16), (2 * LANES, BF16),
              (GROUP_W, BF16), (GROUP_W, BF16), (GROUP_W, BF16), (GROUP_W, F32), (LANES, F32),
              (GROUP_W, BF16), (GROUP_W, BF16), (GROUP_W, BF16)]
    row = lambda wd: pl.BlockSpec((1, wd), lambda bi, j: (0, 0))
    return pl.pallas_call(
        functools.partial(_inproj_kernel, d_model=d, rope=rope, sub_rows=min(tm, TOKEN_TILE)),
        grid=(b, t // tm),
        in_specs=[pl.BlockSpec((1, tm, d), lambda bi, j: (bi, j, 0)),
                  pl.BlockSpec((1, 1, 6 * d), lambda bi, j: (mod_row(bi), 0, 0)),
                  row(d),
                  _layer_spec(w.shape, layer),
                  _const_spec((GROUP_W, GROUP_W)),
                  row(GROUP_W), row(KV_W),
                  pl.BlockSpec((tm, LANES), lambda bi, j: (j, 0)),
                  pl.BlockSpec((tm, LANES), lambda bi, j: (j, 0)),
                  row(LANES)],
        out_specs=[pl.BlockSpec((1, tm, wd), lambda bi, j: (bi, j, 0)) for wd, _ in widths],
        out_shape=[jax.ShapeDtypeStruct((b, t, wd), dt) for wd, dt in widths],
        compiler_params=_params(("parallel", "parallel"), "in_proj", n_inputs=10, fusible=(3,)),
        name="in_proj",
    )(x, mod, g1, w, pool, gq, gk, cos, sin, gbias)


def _stack_gqa(q):
    qa, qb = q[:, :LANES], q[:, LANES:]
    left = lax.broadcasted_iota(jnp.int32, qa.shape, 1) < HEAD_DIM
    zero = jnp.zeros_like(qa)
    return jnp.concatenate([jnp.where(left, qa, zero), jnp.where(left, qb, zero),
                            jnp.where(left, zero, qa), jnp.where(left, zero, qb)], axis=0)


def _unstack_gqa(o, t):
    left = lax.broadcasted_iota(jnp.int32, (t, LANES), 1) < HEAD_DIM
    return jnp.concatenate([jnp.where(left, o[0:t], o[2 * t:3 * t]),
                            jnp.where(left, o[t:2 * t], o[3 * t:4 * t])], axis=1)


def _attn_a_kernel(safe_ref, q_ref, k_ref, v_ref, kc_ref, vc_ref, o_ref, qs_ref, m_ref, acc_ref, *, tq, tk, n_kb):
    qs_ref[...] = _stack_gqa(q_ref[0])
    heads = [slice(h * tq, (h + 1) * tq) for h in range(N_HEADS)]

    @pl.when(safe_ref[0] != 0)
    def _():
        for rows in heads:
            p = jnp.exp2(_dot_nt(qs_ref[rows], kc_ref[0]))
            acc_ref[rows] = _dot(p.astype(BF16), vc_ref[0])

        def body(kb, carry):
            start = pl.multiple_of(kb * tk, tk)
            for rows in heads:
                p = jnp.exp2(_dot_nt(qs_ref[rows], k_ref[0, pl.ds(start, tk), :]))
                acc_ref[rows] += _dot(p.astype(BF16), v_ref[0, pl.ds(start, tk), :])
            return carry

        lax.fori_loop(0, n_kb, body, 0)

    @pl.when(safe_ref[0] == 0)
    def _():
        for rows in heads:
            s = _dot_nt(qs_ref[rows], kc_ref[0])
            m0 = jnp.max(s, axis=-1, keepdims=True)
            m_ref[rows] = m0
            acc_ref[rows] = _dot(jnp.exp2(s - m0).astype(BF16), vc_ref[0])

        def body(kb, carry):
            start = pl.multiple_of(kb * tk, tk)
            for rows in heads:
                s = _dot_nt(qs_ref[rows], k_ref[0, pl.ds(start, tk), :])
                m_prev = m_ref[rows]
                m_new = jnp.maximum(m_prev, jnp.max(s, axis=-1, keepdims=True))
                p = jnp.exp2(s - m_new)
                acc_ref[rows] = jnp.exp2(m_prev - m_new) * acc_ref[rows] + _dot(
                    p.astype(BF16), v_ref[0, pl.ds(start, tk), :])
                m_ref[rows] = m_new
            return carry

        lax.fori_loop(0, n_kb, body, 0)

    acc = acc_ref[...]
    o_ref[0] = _unstack_gqa(acc[:, :LANES] / acc[:, LANES:], tq).astype(o_ref.dtype)


def _attn_global(safe, q, k, v, kc, vc, *, tq, tk):
    b, t, _ = q.shape
    n_ctx = kc.shape[1]
    per_b = lambda n, wd: pl.BlockSpec((1, n, wd), lambda bi, i: (bi, 0, 0))
    return pl.pallas_call(
        functools.partial(_attn_a_kernel, tq=tq, tk=tk, n_kb=t // tk),
        grid=(b, t // tq),
        in_specs=[pl.BlockSpec(memory_space=pltpu.SMEM),
                  pl.BlockSpec((1, tq, GROUP_W), lambda bi, i: (bi, i, 0)),
                  per_b(t, KV_W), per_b(t, 2 * LANES), per_b(n_ctx, KV_W), per_b(n_ctx, 2 * LANES)],
        out_specs=pl.BlockSpec((1, tq, GROUP_W), lambda bi, i: (bi, i, 0)),
        out_shape=jax.ShapeDtypeStruct((b, t, GROUP_W), BF16),
        scratch_shapes=[pltpu.VMEM((4 * tq, LANES), BF16), pltpu.VMEM((4 * tq, 1), F32),
                        pltpu.VMEM((4 * tq, 2 * LANES), F32)],
        compiler_params=_params(("parallel", "parallel"), "attn_global"),
        name="attn_global",
    )(safe, q, k, v, kc, vc)


def _sink_column(sink_ref, t):
    row = lax.broadcasted_iota(jnp.int32, (4 * t, 1), 0)
    col = jnp.full((4 * t, 1), sink_ref[3], F32)
    for h in (2, 1, 0):
        col = jnp.where(row < (h + 1) * t, sink_ref[h], col)
    return col


def _attn_b_kernel(sink_ref, q_ref, k_ref, v_ref, kc_ref, vc_ref, o_ref, *, tq, tiles_per_step, t_total):
    span = tq + 2 * WINDOW
    for sub in range(tiles_per_step):
        q0 = (pl.program_id(1) * tiles_per_step + sub) * tq
        ks = pl.multiple_of(jnp.clip(q0 - WINDOW, 0, t_total - span), WINDOW)
        qs = _stack_gqa(q_ref[0, sub * tq:(sub + 1) * tq, :])
        kw = k_ref[0, pl.ds(ks, span), :]
        vw = v_ref[0, pl.ds(ks, span), :]
        q_pos = q0 + lax.broadcasted_iota(jnp.int32, (tq, span), 0)
        k_pos = ks + lax.broadcasted_iota(jnp.int32, (tq, span), 1)
        in_window = jnp.abs(k_pos - q_pos) <= WINDOW
        outs = []
        for h in range(N_HEADS):
            qh = qs[h * tq:(h + 1) * tq]
            s = jnp.where(in_window, _dot_nt(qh, kw), NEG_INF)
            sc = _dot_nt(qh, kc_ref[0])
            sink = sink_ref[h] * LOG2_E
            m = jnp.maximum(jnp.maximum(jnp.max(s, axis=-1, keepdims=True), jnp.max(sc, axis=-1, keepdims=True)),
                            sink)
            acc = _dot(jnp.exp2(s - m).astype(BF16), vw) + _dot(jnp.exp2(sc - m).astype(BF16), vc_ref[0])
            outs.append(acc[:, :LANES] / (acc[:, LANES:] + jnp.exp2(sink - m)))
        o_ref[0, sub * tq:(sub + 1) * tq, :] = _unstack_gqa(jnp.concatenate(outs, axis=0), tq).astype(o_ref.dtype)


def _attn_window(sink, q, k, v, kc, vc, *, tq):
    b, t, _ = q.shape
    n_ctx = kc.shape[1]
    assert t >= tq + 2 * WINDOW
    tiles = WINDOW_TILES_PER_STEP if (t // tq) % WINDOW_TILES_PER_STEP == 0 else 1
    per_b = lambda n, wd: pl.BlockSpec((1, n, wd), lambda bi, i: (bi, 0, 0))
    return pl.pallas_call(
        functools.partial(_attn_b_kernel, tq=tq, tiles_per_step=tiles, t_total=t),
        grid=(b, t // (tq * tiles)),
        in_specs=[pl.BlockSpec(memory_space=pltpu.SMEM),
                  pl.BlockSpec((1, tq * tiles, GROUP_W), lambda bi, i: (bi, i, 0)),
                  per_b(t, KV_W), per_b(t, 2 * LANES), per_b(n_ctx, KV_W), per_b(n_ctx, 2 * LANES)],
        out_specs=pl.BlockSpec((1, tq * tiles, GROUP_W), lambda bi, i: (bi, i, 0)),
        out_shape=jax.ShapeDtypeStruct((b, t, GROUP_W), BF16),
        compiler_params=_params(("parallel", "parallel"), "attn_window"),
        name="attn_window",
    )(sink, q, k, v, kc, vc)


def _head_masks(shape):
    lane = lax.broadcasted_iota(jnp.int32, shape, 1)
    return [(lane >= h * HEAD_DIM) & (lane < (h + 1) * HEAD_DIM) for h in range(N_HEADS)]


def _stack_mha(q):
    zero = jnp.zeros_like(q)
    return jnp.concatenate([jnp.where(mk, q, zero) for mk in _head_masks(q.shape)], axis=0)


def _unstack_mha(o, t):
    masks = _head_masks((t, GROUP_W))
    out = jnp.where(masks[0], o[0:t], 0.0)
    for h in range(1, N_HEADS):
        out = jnp.where(masks[h], o[h * t:(h + 1) * t], out)
    return out


def _attn_d_kernel(q_ref, k_ref, v_ref, kc_ref, vc_ref, bias_ref, o_ref, *, rows_per_step, rows):
    kc = kc_ref[0]
    vc = vc_ref[0]
    span = NA_KH * GRID_W

    def body(rr, carry):
        r = pl.program_id(1) * rows_per_step + rr
        rs = jnp.clip(r - NA_KH // 2, 0, rows - NA_KH)
        k0 = pl.multiple_of(rs * GRID_W, GRID_W)
        q0 = pl.multiple_of(rr * GRID_W, GRID_W)
        qs = _stack_mha(q_ref[0, pl.ds(q0, GRID_W), :])
        s = _dot_nt(qs, k_ref[0, pl.ds(k0, span), :]) + bias_ref[r - rs]
        sc = _dot_nt(qs, kc)
        m = jnp.maximum(jnp.max(s, axis=-1, keepdims=True), jnp.max(sc, axis=-1, keepdims=True))
        p = jnp.exp(s - m)
        pc = jnp.exp(sc - m)
        l = jnp.sum(p, axis=-1, keepdims=True) + jnp.sum(pc, axis=-1, keepdims=True)
        o = _dot(p.astype(BF16), v_ref[0, pl.ds(k0, span), :]) + _dot(pc.astype(BF16), vc)
        o_ref[0, pl.ds(q0, GRID_W), :] = _unstack_mha(o / l, GRID_W).astype(o_ref.dtype)
        return carry

    lax.fori_loop(0, rows_per_step, body, 0, unroll=True)


def _attn_neighbour(q, k, v, kc, vc, bias, *, rows_per_step):
    b, t, _ = q.shape
    n_ctx = kc.shape[1]
    rows = t // GRID_W
    assert rows >= NA_KH and rows % rows_per_step == 0
    tq = rows_per_step * GRID_W
    per_b = lambda n: pl.BlockSpec((1, n, GROUP_W), lambda bi, i: (bi, 0, 0))
    return pl.pallas_call(
        functools.partial(_attn_d_kernel, rows_per_step=rows_per_step, rows=rows),
        grid=(b, t // tq),
        in_specs=[pl.BlockSpec((1, tq, GROUP_W), lambda bi, i: (bi, i, 0)),
                  per_b(t), per_b(t), per_b(n_ctx), per_b(n_ctx),
                  _const_spec(bias.shape)],
        out_specs=pl.BlockSpec((1, tq, GROUP_W), lambda bi, i: (bi, i, 0)),
        out_shape=jax.ShapeDtypeStruct((b, t, GROUP_W), BF16),
        compiler_params=_params(("parallel", "parallel"), "attn_neighbour"),
        name="attn_neighbour",
    )(q, k, v, kc, vc, bias)


def _neighbour_bias(rpb):
    w = jnp.arange(GRID_W)
    cs = jnp.clip(w - NA_KW // 2, 0, GRID_W - NA_KW)
    col = jnp.arange(GRID_W)
    valid = (col[None, :] >= cs[:, None]) & (col[None, :] < cs[:, None] + NA_KW)
    dc = jnp.clip(col[None, :] - w[:, None] + (NA_KW - 1), 0, 2 * NA_KW - 2)
    pick_c = (jnp.arange(2 * NA_KW - 1)[:, None, None] == dc[None]).astype(F32)
    by_col = jnp.einsum('hrc,cwk->hwrk', rpb.astype(F32), pick_c, precision=lax.Precision.HIGHEST)
    by_col = jnp.where(valid[None, :, None, :], by_col, NEG_INF)
    flat = by_col.reshape(N_HEADS * GRID_W, (2 * NA_KH - 1) * GRID_W)
    return jnp.stack([flat[:, (NA_KH - 1 - o) * GRID_W:(2 * NA_KH - 1 - o) * GRID_W] for o in range(NA_KH)])


def _ctx_attn_kernel(sink_ref, aq_ref, ak_ref, av_ref, bq_ref, bk_ref, bv_ref, dq_ref, dk_ref, dv_ref,
                     ya_ref, yb_ref, yd_ref, *, n):
    s = _dot_nt(_stack_gqa(aq_ref[0]), ak_ref[0])
    acc = _dot(jnp.exp2(s - jnp.max(s, axis=-1, keepdims=True)).astype(BF16), av_ref[0])
    ya_ref[0] = _unstack_gqa(acc[:, :LANES] / acc[:, LANES:], n).astype(ya_ref.dtype)
    s = _dot_nt(_stack_gqa(bq_ref[0]), bk_ref[0])
    sink = _sink_column(sink_ref, n) * LOG2_E
    m = jnp.maximum(jnp.max(s, axis=-1, keepdims=True), sink)
    acc = _dot(jnp.exp2(s - m).astype(BF16), bv_ref[0])
    yb_ref[0] = _unstack_gqa(acc[:, :LANES] / (acc[:, LANES:] + jnp.exp2(sink - m)), n).astype(yb_ref.dtype)
    s = _dot_nt(_stack_mha(dq_ref[0]), dk_ref[0])
    p = jnp.exp(s - jnp.max(s, axis=-1, keepdims=True))
    o = _dot(p.astype(BF16), dv_ref[0]) / jnp.sum(p, axis=-1, keepdims=True)
    yd_ref[0] = _unstack_mha(o, n).astype(yd_ref.dtype)


def _ctx_attention(sink, aq, ak, av, bq, bk, bv, dq, dk, dv):
    b, n, _ = aq.shape
    spec = lambda wd: pl.BlockSpec((1, n, wd), lambda bi: (bi, 0, 0))
    return pl.pallas_call(
        functools.partial(_ctx_attn_kernel, n=n),
        grid=(b,),
        in_specs=[pl.BlockSpec(memory_space=pltpu.SMEM),
                  spec(GROUP_W), spec(KV_W), spec(2 * LANES), spec(GROUP_W), spec(KV_W), spec(2 * LANES),
                  spec(GROUP_W), spec(GROUP_W), spec(GROUP_W)],
        out_specs=[spec(GROUP_W)] * 3,
        out_shape=[jax.ShapeDtypeStruct((b, n, GROUP_W), BF16)] * 3,
        compiler_params=_params(("parallel",), "ctx_attention"),
        name="ctx_attention",
    )(sink, aq, ak, av, bq, bk, bv, dq, dk, dv)


def _log_sigmoid(x):
    return jnp.minimum(x, 0.0) - jnp.log1p(jnp.exp(-jnp.abs(x)))


def _split3(x):
    x1 = x.astype(BF16)
    r1 = x - x1.astype(F32)
    x2 = r1.astype(BF16)
    return x1, x2, (r1 - x2.astype(F32)).astype(BF16)


def _exact_dot_01(sel, x):
    return sum(_dot(sel, piece) for piece in _split3(x))


def _exact_dot_01_rhs(x, sel):
    return sum(_dot(piece, sel) for piece in _split3(x))


def _mlstm_chunk(q, k, v, gates, expand, tri, neg_mask, ones_cat, block, head_rows, s_ref, n_ref, m_ref, *,
                 reverse, L):
    ge = _exact_dot_01_rhs(gates, expand)
    li = ge[:, :GROUP_W]
    cum = _exact_dot_01(tri, ge[:, GROUP_W:])
    a_t = (li - cum).T
    m_prev = m_ref[0:1, :]
    head_b = [head_rows[h].astype(BF16) for h in range(N_HEADS)]
    k_t = k.astype(F32).T.astype(BF16)
    mu = jnp.zeros((L, GROUP_W), F32)
    scores = []
    for h in range(N_HEADS):
        lane0 = h * HEAD_DIM
        a_m = a_t[lane0:lane0 + 1, :] + neg_mask
        mu_h = jnp.maximum(jnp.max(a_m, axis=-1, keepdims=True), m_prev[:, lane0:lane0 + 1])
        qk = _dot(q * head_b[h], k_t)
        scores.append((qk * jnp.exp2(a_m - mu_h)).astype(BF16))
        mu = mu + mu_h * head_rows[h]
    s_cat = jnp.concatenate(scores, axis=1)
    v_cat = jnp.concatenate([v * hb for hb in head_b], axis=0)
    w_inter = jnp.exp2(m_prev - mu)
    num = _dot(s_cat, v_cat) + w_inter * _dot(q, s_ref[...].astype(BF16))
    den = _dot(s_cat, ones_cat) + w_inter * _dot(q, n_ref[...].astype(BF16))
    h_out = num / jnp.maximum(jnp.abs(den), jnp.exp2(-(cum + mu)))

    end_row = 0 if reverse else L - 1
    cum_end = cum[end_row:end_row + 1, :]
    log_end = cum_end - cum + li
    m_new = jnp.maximum(cum_end + m_prev, jnp.max(log_end, axis=0, keepdims=True))
    decay = jnp.exp2(cum_end + m_prev - m_new)
    w_end = jnp.exp2(log_end - m_new)
    upd =_dot(k_t, jnp.concatenate([(v.astype(F32) * w_end).astype(BF16), w_end.astype(BF16)], axis=1))
    s_ref[...] = (s_ref[...] * decay + upd[:, :GROUP_W]) * block
    n_ref[...] = (n_ref[...] * decay + upd[:, GROUP_W:]) * block
    m_ref[...] = jnp.broadcast_to(m_new, m_ref.shape)
    return h_out


def _mlstm_kernel(qf_ref, kf_ref, vf_ref, gf_ref, qb_ref, kb_ref, vb_ref, gb_ref,
                  expand_ref, tri_ref, neg_ref, ones_ref, block_ref, heads_ref, s0_ref, n0_ref, m0_ref,
                  hf_ref, hb_ref, s_out, n_out, m_out, s_ref, n_ref, m_ref, *, chunk, chunks_per_step):
    @pl.when(pl.program_id(1) == 0)
    def _():
        s_ref[...] = s0_ref[0]
        n_ref[...] = n0_ref[0]
        m_ref[...] = m0_ref[0]

    sides = ((qf_ref, kf_ref, vf_ref, gf_ref, hf_ref), (qb_ref, kb_ref, vb_ref, gb_ref, hb_ref))
    for c in range(chunks_per_step):
        for d, (q_ref, k_ref, v_ref, g_ref, h_ref) in enumerate(sides):
            rows = pl.ds((chunks_per_step - 1 - c if d else c) * chunk, chunk)
            h = _mlstm_chunk(q_ref[0, rows, :], k_ref[0, rows, :], v_ref[0, rows, :], g_ref[0, rows, :],
                             expand_ref[d], tri_ref[d], neg_ref[d], ones_ref[...], block_ref[...], heads_ref,
                             s_ref.at[d], n_ref.at[d], m_ref.at[d], reverse=bool(d), L=chunk)
            h_ref[0, rows, :] = h.astype(h_ref.dtype)
    s_out[0] = s_ref[...]
    n_out[0] = n_ref[...]
    m_out[0] = m_ref[...]


def _mlstm_consts(L):
    lane = jnp.arange(LANES)[:, None]
    col = jnp.arange(2 * GROUP_W)[None, :]
    col_head = (col % GROUP_W) // HEAD_DIM
    expand = jnp.stack([(lane == 8 * d + 4 * (col // GROUP_W) + col_head) for d in range(2)]).astype(BF16)
    t_idx = jnp.arange(L)[:, None]
    s_idx = jnp.arange(L)[None, :]
    seen = jnp.stack([s_idx <= t_idx, s_idx >= t_idx])
    head = jnp.arange(GROUP_W) // HEAD_DIM
    ones_cat = (jnp.repeat(jnp.arange(N_HEADS), L)[:, None] == head[None, :]).astype(BF16)
    block = (head[:, None] == head[None, :]).astype(F32)
    head_rows = (jnp.arange(N_HEADS)[:, None, None] == head[None, None, :]).astype(F32)
    return expand, seen.astype(BF16), jnp.where(seen, 0.0, NEG_INF).astype(F32), ones_cat, block, head_rows


def _mlstm_scan(q, k, v, gates, state, consts):
    b, t, _ = q.shape
    L = MLSTM_L
    per_step = MLSTM_CHUNKS_PER_STEP if (t // L) % MLSTM_CHUNKS_PER_STEP == 0 else 1
    nc = t // (L * per_step)
    fwd = lambda wd: pl.BlockSpec((1, L * per_step, wd), lambda bi, j: (bi, j, 0))
    bwd = lambda wd: pl.BlockSpec((1, L * per_step, wd), lambda bi, j: (bi, nc - 1 - j, 0))
    st = lambda r: pl.BlockSpec((1, 2, r, GROUP_W), lambda bi, j: (bi, 0, 0, 0))
    tok_specs = [fwd(GROUP_W), fwd(GROUP_W), fwd(GROUP_W), fwd(LANES),
                 bwd(GROUP_W), bwd(GROUP_W), bwd(GROUP_W), bwd(LANES)]
    hf, hb, s_fin, n_fin, m_fin = pl.pallas_call(
        functools.partial(_mlstm_kernel, chunk=L, chunks_per_step=per_step),
        grid=(b, nc),
        in_specs=tok_specs + [_const_spec(c.shape) for c in consts] + [st(GROUP_W), st(GROUP_W), st(8)],
        out_specs=[fwd(GROUP_W), bwd(GROUP_W), st(GROUP_W), st(GROUP_W), st(8)],
        out_shape=[jax.ShapeDtypeStruct((b, t, GROUP_W), BF16), jax.ShapeDtypeStruct((b, t, GROUP_W), BF16),
                   jax.ShapeDtypeStruct((b, 2, GROUP_W, GROUP_W), F32),
                   jax.ShapeDtypeStruct((b, 2, GROUP_W, GROUP_W), F32),
                   jax.ShapeDtypeStruct((b, 2, 8, GROUP_W), F32)],
        scratch_shapes=[pltpu.VMEM((2, GROUP_W, GROUP_W), F32), pltpu.VMEM((2, GROUP_W, GROUP_W), F32),
                        pltpu.VMEM((2, 8, GROUP_W), F32)],
        compiler_params=_params(("parallel", "arbitrary"), "mlstm"),
        name="mlstm",
    )(q, k, v, gates, q, k, v, gates, *consts, *state)
    return hf, hb, (s_fin, n_fin, m_fin)


def _outproj_kernel(x_ref, ya_ref, yb_ref, hf_ref, hb_ref, og_ref, yd_ref, mod_ref, gg_ref, w_ref, o_ref, *, d_model):
    d = d_model
    yc = jax.nn.sigmoid(og_ref[0]) * (hf_ref[0].astype(F32) + hb_ref[0].astype(F32))
    parts = []
    for i, y in enumerate((ya_ref[0], yb_ref[0], yc, yd_ref[0])):
        parts.append((_rms_rows(y.astype(F32)) * gg_ref[i:i + 1, :]).astype(BF16))
    res = _dot(jnp.concatenate(parts, axis=1), w_ref[...])
    o_ref[0] = x_ref[0] + mod_ref[0][:, 2 * d:3 * d] * res


def _out_proj(x, ya, yb, hf, hb, o_gate, yd, mod, mod_row, gg, w, *, layer, tm):
    b, t, d = x.shape
    tokd = pl.BlockSpec((1, tm, d), lambda bi, j: (bi, j, 0))
    tokg = pl.BlockSpec((1, tm, GROUP_W), lambda bi, j: (bi, j, 0))
    return pl.pallas_call(
        functools.partial(_outproj_kernel, d_model=d),
        grid=(b, t // tm),
        in_specs=[tokd, tokg, tokg, tokg, tokg, tokg, tokg,
                  pl.BlockSpec((1, 1, 6 * d), lambda bi, j: (mod_row(bi), 0, 0)),
                  _const_spec(gg.shape), _layer_spec(w.shape, layer)],
        out_specs=tokd,
        out_shape=jax.ShapeDtypeStruct((b, t, d), F32),
        compiler_params=_params(("parallel", "parallel"), "out_proj", n_inputs=10, fusible=(9,)),
        name="out_proj",
    )(x, ya, yb, hf, hb, o_gate, yd, mod, gg, w)


def _ffn_kernel(*refs, d_model, d_ff, tm, final_norm):
    if final_norm:
        x_ref, xp_ref, xn_ref, mod_ref, g2_ref, wu_ref, cw_ref, wd_ref, gf_ref, o_ref = refs
    else:
        x_ref, xp_ref, xn_ref, mod_ref, g2_ref, wu_ref, cw_ref, wd_ref, o_ref = refs
    d = d_model
    j = pl.program_id(1)
    mod = mod_ref[0]
    g2 = g2_ref[...]

    def norm_mod(x):
        return _rms_rows(x) * g2 * (1.0 + mod[:, 4 * d:5 * d]) + mod[:, 3 * d:4 * d]

    keep_prev = (j > 0).astype(F32)
    keep_next = (j < pl.num_programs(1) - 1).astype(F32)
    x = x_ref[0]
    h_ext = jnp.concatenate([norm_mod(xp_ref[0]) * keep_prev, norm_mod(x), norm_mod(xn_ref[0]) * keep_next],
                            axis=0).astype(BF16)
    n_ext = tm + 2 * HALO

    u = _dot(h_ext, wu_ref[...])
    cw = cw_ref[...]
    u = (pltpu.roll(u, 1, 0) * cw[0:1] + u * cw[1:2] + pltpu.roll(u, n_ext - 1, 0) * cw[2:3] + cw[3:4])[HALO:HALO + tm]
    gate, val = u[:, :d_ff], u[:, d_ff:]
    act = (gate * jax.nn.sigmoid(gate) * val).astype(BF16)
    y = x + mod[:, 5 * d:6 * d] * _dot(act, wd_ref[...])
    if final_norm:
        y = _rms_rows(y) * gf_ref[...]
    o_ref[0] = y


def _conv_ffn(x, mod, mod_row, g2, wu, cw, wd, g_final, *, layer, tm):
    b, t, d = x.shape
    d_ff = wd.shape[1]
    final_norm = g_final is not None
    hb = tm // HALO
    last = t // HALO - 1
    row = pl.BlockSpec((1, d), lambda bi, j: (0, 0))
    in_specs = [pl.BlockSpec((1, tm, d), lambda bi, j: (bi, j, 0)),
                pl.BlockSpec((1, HALO, d), lambda bi, j: (bi, jnp.maximum(j * hb - 1, 0), 0)),
                pl.BlockSpec((1, HALO, d), lambda bi, j: (bi, jnp.minimum((j + 1) * hb, last), 0)),
                pl.BlockSpec((1, 1, 6 * d), lambda bi, j: (mod_row(bi), 0, 0)),
                row,
                _layer_spec(wu.shape, layer), _const_spec(cw.shape), _layer_spec(wd.shape, layer)]
    args = [x, x, x, mod, g2, wu, cw, wd]
    if final_norm:
        in_specs.append(row)
        args.append(g_final)
    return pl.pallas_call(
        functools.partial(_ffn_kernel, d_model=d, d_ff=d_ff, tm=tm, final_norm=final_norm),
        grid=(b, t // tm),
        in_specs=in_specs,
        out_specs=pl.BlockSpec((1, tm, d), lambda bi, j: (bi, j, 0)),
        out_shape=jax.ShapeDtypeStruct((b, t, d), F32),
        compiler_params=_params(("parallel", "parallel"), "conv_ffn", n_inputs=len(args), fusible=(5, 7)),
        name="conv_ffn",
    )(*args)


_Q_HEAD_ORDER = (0, 2, 1, 3)


def _permute_heads(w, axis):
    parts = [lax.slice_in_dim(w, h * HEAD_DIM, (h + 1) * HEAD_DIM, axis=axis) for h in _Q_HEAD_ORDER]
    return jnp.concatenate(parts, axis=axis)


def _layout_w_in(w):
    return w.astype(BF16)


def _layout_w_out(w):
    w = w.astype(BF16)
    return jnp.concatenate([_permute_heads(w[:, 0:GROUP_W], 1), _permute_heads(w[:, GROUP_W:2 * GROUP_W], 1),
                            w[:, 2 * GROUP_W:]], axis=1)


def _rope_tables(t):
    pos = jnp.arange(t)
    row = (pos // GRID_W).astype(F32)
    col = (pos % GRID_W).astype(F32)
    n_freq = HEAD_DIM // 4
    freqs = ROPE_THETA ** (-jnp.arange(n_freq, dtype=F32) / n_freq)
    ang_r, ang_c = row[:, None] * freqs, col[:, None] * freqs
    cos = jnp.concatenate([jnp.cos(ang_r), jnp.cos(ang_r), jnp.cos(ang_c), jnp.cos(ang_c)], axis=1)
    sin = jnp.concatenate([-jnp.sin(ang_r), jnp.sin(ang_r), -jnp.sin(ang_c), jnp.sin(ang_c)], axis=1)
    reps = LANES // HEAD_DIM
    return jnp.tile(cos, (1, reps)), jnp.tile(sin, (1, reps))


def _pick_tile(t, pref):
    while t % pref:
        pref //= 2
    return pref


def kernel(x, c, ctx, c_ctx, w_mod, b_mod, g_norm1, g_norm2, w_in, a_q_gain, a_k_gain, b_sink, c_gate_bias,
           d_rel_bias, g_group, w_out, w_up, conv_w, conv_b, w_down, g_final):
    batch, t, d = x.shape
    n_ctx = ctx.shape[1]
    depth = w_in.shape[0]
    d_ff = w_down.shape[1]
    assert batch < 8 and d_ff % LANES == 0 and t % MLSTM_L == 0 and n_ctx % MLSTM_L == 0

    c_rows = jnp.zeros((8, d), F32).at[:batch].set(c).at[batch].set(c_ctx)
    mod_all = _modulation(c_rows, w_mod, b_mod).reshape(depth, 8, 1, 6 * d)
    lat_row = lambda bi: bi
    ctx_row = lambda bi: batch

    cos, sin = _rope_tables(t)
    pool = jnp.where(jnp.arange(GROUP_W)[:, None] // HEAD_DIM == jnp.arange(GROUP_W)[None, :] // HEAD_DIM,
                     1.0 / HEAD_DIM, 0.0).astype(BF16)
    tm_lat = _pick_tile(t, TOKEN_TILE)
    tm_ctx = _pick_tile(n_ctx, CTX_TILE)
    zero_state = (jnp.zeros((batch, 2, GROUP_W, GROUP_W), F32), jnp.zeros((batch, 2, GROUP_W, GROUP_W), F32),
                  jnp.zeros((batch, 2, 8, GROUP_W), F32))
    mlstm_consts = _mlstm_consts(MLSTM_L)
    w_in_b = _layout_w_in(w_in)
    w_out_b = _layout_w_out(w_out)
    w_up_b = w_up.astype(BF16)
    w_down_b = w_down.astype(BF16)

    for layer in range(depth):
        need_ctx = layer < depth - 1
        mod = mod_all[layer]
        g1 = g_norm1[layer][None]
        g2 = g_norm2[layer][None]
        gq = jnp.tile(a_q_gain[layer], N_HEADS)[None]
        gk = jnp.tile(a_k_gain[layer], 2)[None]
        gbias = jnp.pad(c_gate_bias[layer], (0, LANES - 4 * N_HEADS))[None]
        gg = g_group[layer].reshape(N_HEADS, GROUP_W)
        gg = jnp.concatenate([_permute_heads(gg[0:2], 1), gg[2:4]], axis=0)
        conv_l = jnp.concatenate([conv_w[layer], conv_b[layer][None],
                                  jnp.zeros((8 - 1 - conv_w.shape[1], 2 * d_ff), F32)], axis=0)
        sink = b_sink[layer]
        bias_tab = _neighbour_bias(d_rel_bias[layer])

        proj = functools.partial(_in_proj, g1=g1, w=w_in_b, pool=pool, gq=gq, gk=gk, gbias=gbias, layer=layer)
        (aqc, akc, avc, bqc, bkc, bvc, cqc, ckc, cvc, coc, cgc, dqc, dkc, dvc) = proj(
            ctx, mod, ctx_row, cos=cos, sin=sin, rope=False, tm=tm_ctx)
        (aq, ak, av, bq, bk, bv, cq, ck, cv, co, cg, dq, dk, dv) = proj(
            x, mod, lat_row, cos=cos, sin=sin, rope=True, tm=_pick_tile(t, 2 * TOKEN_TILE))

        score_bound = 1.02 * LOG2_E * HEAD_DIM ** 0.5 * jnp.max(jnp.abs(a_q_gain[layer])) * jnp.max(jnp.abs(a_k_gain[layer]))
        safe = (score_bound <= MAX_UNSHIFTED_LOG2_SCORE).astype(jnp.int32).reshape(1)
        ya = _attn_global(safe, aq, ak, av, akc, avc, tq=_pick_tile(t, ATTN_Q_TILE), tk=_pick_tile(t, ATTN_K_TILE))
        yb = _attn_window(sink, bq, bk, bv, bkc, bvc, tq=_pick_tile(t, WINDOW_Q_TILE))
        yd = _attn_neighbour(dq, dk, dv, dkc, dvc, bias_tab, rows_per_step=NEIGHBOUR_ROWS)

        hcf, hcb, ctx_state = _mlstm_scan(cqc, ckc, cvc, cgc, zero_state, mlstm_consts)
        hf, hb, _ = _mlstm_scan(cq, ck, cv, cg, ctx_state, mlstm_consts)

        x = _out_proj(x, ya, yb, hf, hb, co, yd, mod, lat_row, gg, w_out_b, layer=layer,
                      tm=_pick_tile(t, 2 * TOKEN_TILE))
        ffn = functools.partial(_conv_ffn, g2=g2, wu=w_up_b, cw=conv_l, wd=w_down_b, layer=layer)
        x = ffn(x, mod, lat_row, g_final=None if need_ctx else g_final[None], tm=tm_lat)
        if need_ctx:
            yac, ybc, ydc = _ctx_attention(sink, aqc, akc, avc, bqc, bkc, bvc, dqc, dkc, dvc)
            ctx = _out_proj(ctx, yac, ybc, hcf, hcb, coc, ydc, mod, ctx_row, gg, w_out_b, layer=layer, tm=tm_ctx)
            ctx = ffn(ctx, mod, ctx_row, g_final=None, tm=tm_ctx)
    return x
```

```python
import functools

import jax
import jax.numpy as jnp
from jax import lax
from jax.experimental import pallas as pl
from jax.experimental.pallas import tpu as pltpu

F32 = jnp.float32
BF16 = jnp.bfloat16

N_HEADS = 4
HEAD_DIM = 64
GROUP_W = N_HEADS * HEAD_DIM
KV_W = 2 * HEAD_DIM
GRID_W = 64
WINDOW = 128
NA_KH = 8
NA_KW = 16
ROPE_THETA = 10000.0
EPS = 1e-6
NEG_INF = -1e30
LANES = 128
MLSTM_L = 256
MLSTM_CHUNKS_PER_STEP = 2
HALO = 8
MIB = 1024 * 1024
LOG2_E = 1.4426950408889634
MAX_UNSHIFTED_LOG2_SCORE = 60.0


TOKEN_TILE = 512
CTX_TILE = 256
ATTN_Q_TILE = 1024
ATTN_K_TILE = 2048
WINDOW_Q_TILE = 256
WINDOW_TILES_PER_STEP = 16
NEIGHBOUR_ROWS = 32
VMEM_LIMIT_MIB = {"modulation": 40, "in_proj": 48, "attn_global": 48, "attn_window": 48, "attn_neighbour": 48,
                  "ctx_attention": 32, "mlstm": 48, "out_proj": 48, "conv_ffn": 56}


def _params(sem, name, n_inputs=0, fusible=()):
    fuse = [i in fusible for i in range(n_inputs)] if fusible else None
    return pltpu.CompilerParams(dimension_semantics=sem, vmem_limit_bytes=VMEM_LIMIT_MIB[name] * MIB,
                                allow_input_fusion=fuse)


def _dot(a, b):
    return jnp.dot(a, b, preferred_element_type=F32)


def _dot_nt(a, b):
    return lax.dot_general(a, b, (((1,), (1,)), ((), ())), preferred_element_type=F32)


def _const_spec(shape):
    return pl.BlockSpec(shape, lambda *_: (0,) * len(shape), pipeline_mode=pl.Buffered(1))


def _layer_spec(stacked_shape, layer):
    rest = tuple(stacked_shape[1:])
    return pl.BlockSpec((None,) + rest, lambda *_: (layer,) + (0,) * len(rest), pipeline_mode=pl.Buffered(1))


def _mod_kernel(c_ref, w_ref, b_ref, o_ref):
    c = c_ref[...]
    a = (c * jax.nn.sigmoid(c)).astype(BF16)
    o_ref[0] = _dot(a, w_ref[0].astype(BF16)) + b_ref[0]


def _modulation(c_rows, w_mod, b_mod):
    depth, d, n = w_mod.shape
    tn = 1536
    return pl.pallas_call(
        _mod_kernel,
        grid=(depth, n // tn),
        in_specs=[pl.BlockSpec((8, d), lambda l, j: (0, 0)),
                  pl.BlockSpec((1, d, tn), lambda l, j: (l, 0, j)),
                  pl.BlockSpec((1, 1, tn), lambda l, j: (l, 0, j))],
        out_specs=pl.BlockSpec((1, 8, tn), lambda l, j: (l, 0, j)),
        out_shape=jax.ShapeDtypeStruct((depth, 8, n), F32),
        compiler_params=_params(("parallel", "parallel"), "modulation"),
        name="modulation",
    )(c_rows, w_mod, b_mod.reshape(depth, 1, n))


def _rms_rows(x):
    return x * lax.rsqrt(jnp.mean(x * x, axis=-1, keepdims=True) + EPS)


def _head_rms(z, pool, gain):
    z2 = z * z
    hi = z2.astype(BF16)
    lo = (z2 - hi.astype(F32)).astype(BF16)
    ms = _dot(hi, pool) + _dot(lo, pool)
    return z * lax.rsqrt(ms + EPS) * gain


def _rope(z, cos, sin):
    lane = lax.broadcasted_iota(jnp.int32, z.shape, 1)
    first = (lane % 32) < 16
    partner = jnp.where(first, pltpu.roll(z, LANES - 16, 1), pltpu.roll(z, 16, 1))
    return z * cos + partner * sin


def _inproj_kernel(*refs, d_model, rope, sub_rows):
    x_ref, mod_ref, g1_ref, w_ref, pool_ref, gq_ref, gk_ref, cos_ref, sin_ref, gb_ref = refs[:10]
    for sub in range(x_ref.shape[1] // sub_rows):
        rows = pl.ds(sub * sub_rows, sub_rows)
        _inproj_tile(x_ref.at[:, rows, :], mod_ref, g1_ref, w_ref, pool_ref, gq_ref, gk_ref,
                     cos_ref.at[rows, :], sin_ref.at[rows, :], gb_ref, *[o.at[:, rows, :] for o in refs[10:]],
                     d_model=d_model, rope=rope)


def _inproj_tile(x_ref, mod_ref, g1_ref, w_ref, pool_ref, gq_ref, gk_ref, cos_ref, sin_ref, gb_ref,
                 aq_ref, ak_ref, av_ref, bq_ref, bk_ref, bv_ref, cq_ref, ck_ref, cv_ref, co_ref, cg_ref,
                 dq_ref, dk_ref, dv_ref, *, d_model, rope):
    d = d_model
    mod = mod_ref[0]
    xn = _rms_rows(x_ref[0]) * g1_ref[...]
    hb = (xn * (1.0 + mod[:, d:2 * d]) + mod[:, 0:d]).astype(BF16)
    projected = _dot(hb, w_ref[...])

    o_b = GROUP_W + 2 * KV_W
    o_c = 2 * o_b
    o_g = o_c + 4 * GROUP_W
    o_d = o_g + 4 * N_HEADS

    def cols(start, width=GROUP_W):
        return projected[:, start:start + width]

    def rot(z):
        if not rope:
            return z
        cos, sin = cos_ref[...], sin_ref[...]
        return jnp.concatenate([_rope(z[:, i:i + LANES], cos, sin) for i in range(0, z.shape[1], LANES)], axis=1)

    def gqa_order(z):
        a, b = z[:, :LANES], z[:, LANES:]
        left = lax.broadcasted_iota(jnp.int32, a.shape, 1) < HEAD_DIM
        return jnp.concatenate([jnp.where(left, a, pltpu.roll(b, HEAD_DIM, 1)),
                                jnp.where(left, pltpu.roll(a, HEAD_DIM, 1), b)], axis=1)

    scale = HEAD_DIM ** -0.5
    pool = pool_ref[...]
    aq_ref[0] = (gqa_order(rot(_head_rms(cols(0), pool, gq_ref[...]))) * (scale * LOG2_E)).astype(BF16)
    akv = cols(GROUP_W)
    ak_ref[0] = rot(_head_rms(akv[:, :KV_W], pool[:KV_W, :KV_W], gk_ref[...])).astype(BF16)
    ones = jnp.ones((akv.shape[0], LANES), F32)
    av_ref[0] = jnp.concatenate([akv[:, KV_W:], ones], axis=1).astype(BF16)
    bq_ref[0] = (gqa_order(rot(cols(o_b))) * (scale * LOG2_E)).astype(BF16)
    bkv = cols(o_b + GROUP_W)
    bk_ref[0] = rot(bkv[:, :KV_W]).astype(BF16)
    bv_ref[0] = jnp.concatenate([bkv[:, KV_W:], ones], axis=1).astype(BF16)
    cq_ref[0] = (cols(o_c) * scale).astype(BF16)
    ck_ref[0] = cols(o_c + GROUP_W).astype(BF16)
    cv_ref[0] = cols(o_c + 2 * GROUP_W).astype(BF16)
    co_ref[0] = cols(o_c + 3 * GROUP_W)
    dq_ref[0] = (cols(o_d) * scale).astype(BF16)
    dk_ref[0] = cols(o_d + GROUP_W).astype(BF16)
    dv_ref[0] = cols(o_d + 2 * GROUP_W).astype(BF16)
    gates = cols(o_g, LANES) + gb_ref[...]
    lane = lax.broadcasted_iota(jnp.int32, gates.shape, 1)
    gates = jnp.where((lane % 8) >= 4, _log_sigmoid(gates), gates) * LOG2_E
    cg_ref[0] = jnp.where(lane < 4 * N_HEADS, gates, 0.0)


def _in_proj(x, mod, mod_row, g1, w, pool, gq, gk, cos, sin, gbias, *, layer, rope, tm):
    b, t, d = x.shape
    widths = [(GROUP_W, BF16), (KV_W, BF16), (2 * LANES, BF16),
              (GROUP_W, BF16), (KV_W, BF16), (2 * LANES, BF16),
              (GROUP_W, BF16), (GROUP_W, BF16), (GROUP_W, BF16), (GROUP_W, F32), (LANES, F32),
              (GROUP_W, BF16), (GROUP_W, BF16), (GROUP_W, BF16)]
    row = lambda wd: pl.BlockSpec((1, wd), lambda bi, j: (0, 0))
    return pl.pallas_call(
        functools.partial(_inproj_kernel, d_model=d, rope=rope, sub_rows=min(tm, TOKEN_TILE)),
        grid=(b, t // tm),
        in_specs=[pl.BlockSpec((1, tm, d), lambda bi, j: (bi, j, 0)),
                  pl.BlockSpec((1, 1, 6 * d), lambda bi, j: (mod_row(bi), 0, 0)),
                  row(d),
                  _layer_spec(w.shape, layer),
                  _const_spec((GROUP_W, GROUP_W)),
                  row(GROUP_W), row(KV_W),
                  pl.BlockSpec((tm, LANES), lambda bi, j: (j, 0)),
                  pl.BlockSpec((tm, LANES), lambda bi, j: (j, 0)),
                  row(LANES)],
        out_specs=[pl.BlockSpec((1, tm, wd), lambda bi, j: (bi, j, 0)) for wd, _ in widths],
        out_shape=[jax.ShapeDtypeStruct((b, t, wd), dt) for wd, dt in widths],
        compiler_params=_params(("parallel", "parallel"), "in_proj", n_inputs=10, fusible=(3,)),
        name="in_proj",
    )(x, mod, g1, w, pool, gq, gk, cos, sin, gbias)


def _stack_gqa(q):
    qa, qb = q[:, :LANES], q[:, LANES:]
    left = lax.broadcasted_iota(jnp.int32, qa.shape, 1) < HEAD_DIM
    zero = jnp.zeros_like(qa)
    return jnp.concatenate([jnp.where(left, qa, zero), jnp.where(left, qb, zero),
                            jnp.where(left, zero, qa), jnp.where(left, zero, qb)], axis=0)


def _unstack_gqa(o, t):
    left = lax.broadcasted_iota(jnp.int32, (t, LANES), 1) < HEAD_DIM
    return jnp.concatenate([jnp.where(left, o[0:t], o[2 * t:3 * t]),
                            jnp.where(left, o[t:2 * t], o[3 * t:4 * t])], axis=1)


def _attn_a_kernel(safe_ref, q_ref, k_ref, v_ref, kc_ref, vc_ref, o_ref, qs_ref, m_ref, acc_ref, *, tq, tk, n_kb):
    qs_ref[...] = _stack_gqa(q_ref[0])
    heads = [slice(h * tq, (h + 1) * tq) for h in range(N_HEADS)]

    @pl.when(safe_ref[0] != 0)
    def _():
        for rows in heads:
            p = jnp.exp2(_dot_nt(qs_ref[rows], kc_ref[0]))
            acc_ref[rows] = _dot(p.astype(BF16), vc_ref[0])

        def body(kb, carry):
            start = pl.multiple_of(kb * tk, tk)
            for rows in heads:
                p = jnp.exp2(_dot_nt(qs_ref[rows], k_ref[0, pl.ds(start, tk), :]))
                acc_ref[rows] += _dot(p.astype(BF16), v_ref[0, pl.ds(start, tk), :])
            return carry

        lax.fori_loop(0, n_kb, body, 0)

    @pl.when(safe_ref[0] == 0)
    def _():
        for rows in heads:
            s = _dot_nt(qs_ref[rows], kc_ref[0])
            m0 = jnp.max(s, axis=-1, keepdims=True)
            m_ref[rows] = m0
            acc_ref[rows] = _dot(jnp.exp2(s - m0).astype(BF16), vc_ref[0])

        def body(kb, carry):
            start = pl.multiple_of(kb * tk, tk)
            for rows in heads:
                s = _dot_nt(qs_ref[rows], k_ref[0, pl.ds(start, tk), :])
                m_prev = m_ref[rows]
                m_new = jnp.maximum(m_prev, jnp.max(s, axis=-1, keepdims=True))
                p = jnp.exp2(s - m_new)
                acc_ref[rows] = jnp.exp2(m_prev - m_new) * acc_ref[rows] + _dot(
                    p.astype(BF16), v_ref[0, pl.ds(start, tk), :])
                m_ref[rows] = m_new
            return carry

        lax.fori_loop(0, n_kb, body, 0)

    acc = acc_ref[...]
    o_ref[0] = _unstack_gqa(acc[:, :LANES] / acc[:, LANES:], tq).astype(o_ref.dtype)


def _attn_global(safe, q, k, v, kc, vc, *, tq, tk):
    b, t, _ = q.shape
    n_ctx = kc.shape[1]
    per_b = lambda n, wd: pl.BlockSpec((1, n, wd), lambda bi, i: (bi, 0, 0))
    return pl.pallas_call(
        functools.partial(_attn_a_kernel, tq=tq, tk=tk, n_kb=t // tk),
        grid=(b, t // tq),
        in_specs=[pl.BlockSpec(memory_space=pltpu.SMEM),
                  pl.BlockSpec((1, tq, GROUP_W), lambda bi, i: (bi, i, 0)),
                  per_b(t, KV_W), per_b(t, 2 * LANES), per_b(n_ctx, KV_W), per_b(n_ctx, 2 * LANES)],
        out_specs=pl.BlockSpec((1, tq, GROUP_W), lambda bi, i: (bi, i, 0)),
        out_shape=jax.ShapeDtypeStruct((b, t, GROUP_W), BF16),
        scratch_shapes=[pltpu.VMEM((4 * tq, LANES), BF16), pltpu.VMEM((4 * tq, 1), F32),
                        pltpu.VMEM((4 * tq, 2 * LANES), F32)],
        compiler_params=_params(("parallel", "parallel"), "attn_global"),
        name="attn_global",
    )(safe, q, k, v, kc, vc)


def _sink_column(sink_ref, t):
    row = lax.broadcasted_iota(jnp.int32, (4 * t, 1), 0)
    col = jnp.full((4 * t, 1), sink_ref[3], F32)
    for h in (2, 1, 0):
        col = jnp.where(row < (h + 1) * t, sink_ref[h], col)
    return col


def _attn_b_kernel(sink_ref, q_ref, k_ref, v_ref, kc_ref, vc_ref, o_ref, *, tq, tiles_per_step, t_total):
    span = tq + 2 * WINDOW
    for sub in range(tiles_per_step):
        q0 = (pl.program_id(1) * tiles_per_step + sub) * tq
        ks = pl.multiple_of(jnp.clip(q0 - WINDOW, 0, t_total - span), WINDOW)
        qs = _stack_gqa(q_ref[0, sub * tq:(sub + 1) * tq, :])
        kw = k_ref[0, pl.ds(ks, span), :]
        vw = v_ref[0, pl.ds(ks, span), :]
        q_pos = q0 + lax.broadcasted_iota(jnp.int32, (tq, span), 0)
        k_pos = ks + lax.broadcasted_iota(jnp.int32, (tq, span), 1)
        in_window = jnp.abs(k_pos - q_pos) <= WINDOW
        outs = []
        for h in range(N_HEADS):
            qh = qs[h * tq:(h + 1) * tq]
            s = jnp.where(in_window, _dot_nt(qh, kw), NEG_INF)
            sc = _dot_nt(qh, kc_ref[0])
            sink = sink_ref[h] * LOG2_E
            m = jnp.maximum(jnp.maximum(jnp.max(s, axis=-1, keepdims=True), jnp.max(sc, axis=-1, keepdims=True)),
                            sink)
            acc = _dot(jnp.exp2(s - m).astype(BF16), vw) + _dot(jnp.exp2(sc - m).astype(BF16), vc_ref[0])
            outs.append(acc[:, :LANES] / (acc[:, LANES:] + jnp.exp2(sink - m)))
        o_ref[0, sub * tq:(sub + 1) * tq, :] = _unstack_gqa(jnp.concatenate(outs, axis=0), tq).astype(o_ref.dtype)


def _attn_window(sink, q, k, v, kc, vc, *, tq):
    b, t, _ = q.shape
    n_ctx = kc.shape[1]
    assert t >= tq + 2 * WINDOW
    tiles = WINDOW_TILES_PER_STEP if (t // tq) % WINDOW_TILES_PER_STEP == 0 else 1
    per_b = lambda n, wd: pl.BlockSpec((1, n, wd), lambda bi, i: (bi, 0, 0))
    return pl.pallas_call(
        functools.partial(_attn_b_kernel, tq=tq, tiles_per_step=tiles, t_total=t),
        grid=(b, t // (tq * tiles)),
        in_specs=[pl.BlockSpec(memory_space=pltpu.SMEM),
                  pl.BlockSpec((1, tq * tiles, GROUP_W), lambda bi, i: (bi, i, 0)),
                  per_b(t, KV_W), per_b(t, 2 * LANES), per_b(n_ctx, KV_W), per_b(n_ctx, 2 * LANES)],
        out_specs=pl.BlockSpec((1, tq * tiles, GROUP_W), lambda bi, i: (bi, i, 0)),
        out_shape=jax.ShapeDtypeStruct((b, t, GROUP_W), BF16),
        compiler_params=_params(("parallel", "parallel"), "attn_window"),
        name="attn_window",
    )(sink, q, k, v, kc, vc)


def _head_masks(shape):
    lane = lax.broadcasted_iota(jnp.int32, shape, 1)
    return [(lane >= h * HEAD_DIM) & (lane < (h + 1) * HEAD_DIM) for h in range(N_HEADS)]


def _stack_mha(q):
    zero = jnp.zeros_like(q)
    return jnp.concatenate([jnp.where(mk, q, zero) for mk in _head_masks(q.shape)], axis=0)


def _unstack_mha(o, t):
    masks = _head_masks((t, GROUP_W))
    out = jnp.where(masks[0], o[0:t], 0.0)
    for h in range(1, N_HEADS):
        out = jnp.where(masks[h], o[h * t:(h + 1) * t], out)
    return out


def _attn_d_kernel(q_ref, k_ref, v_ref, kc_ref, vc_ref, bias_ref, o_ref, *, rows_per_step, rows):
    kc = kc_ref[0]
    vc = vc_ref[0]
    span = NA_KH * GRID_W

    def body(rr, carry):
        r = pl.program_id(1) * rows_per_step + rr
        rs = jnp.clip(r - NA_KH // 2, 0, rows - NA_KH)
        k0 = pl.multiple_of(rs * GRID_W, GRID_W)
        q0 = pl.multiple_of(rr * GRID_W, GRID_W)
        qs = _stack_mha(q_ref[0, pl.ds(q0, GRID_W), :])
        s = _dot_nt(qs, k_ref[0, pl.ds(k0, span), :]) + bias_ref[r - rs]
        sc = _dot_nt(qs, kc)
        m = jnp.maximum(jnp.max(s, axis=-1, keepdims=True), jnp.max(sc, axis=-1, keepdims=True))
        p = jnp.exp(s - m)
        pc = jnp.exp(sc - m)
        l = jnp.sum(p, axis=-1, keepdims=True) + jnp.sum(pc, axis=-1, keepdims=True)
        o = _dot(p.astype(BF16), v_ref[0, pl.ds(k0, span), :]) + _dot(pc.astype(BF16), vc)
        o_ref[0, pl.ds(q0, GRID_W), :] = _unstack_mha(o / l, GRID_W).astype(o_ref.dtype)
        return carry

    lax.fori_loop(0, rows_per_step, body, 0, unroll=True)


def _attn_neighbour(q, k, v, kc, vc, bias, *, rows_per_step):
    b, t, _ = q.shape
    n_ctx = kc.shape[1]
    rows = t // GRID_W
    assert rows >= NA_KH and rows % rows_per_step == 0
    tq = rows_per_step * GRID_W
    per_b = lambda n: pl.BlockSpec((1, n, GROUP_W), lambda bi, i: (bi, 0, 0))
    return pl.pallas_call(
        functools.partial(_attn_d_kernel, rows_per_step=rows_per_step, rows=rows),
        grid=(b, t // tq),
        in_specs=[pl.BlockSpec((1, tq, GROUP_W), lambda bi, i: (bi, i, 0)),
                  per_b(t), per_b(t), per_b(n_ctx), per_b(n_ctx),
                  _const_spec(bias.shape)],
        out_specs=pl.BlockSpec((1, tq, GROUP_W), lambda bi, i: (bi, i, 0)),
        out_shape=jax.ShapeDtypeStruct((b, t, GROUP_W), BF16),
        compiler_params=_params(("parallel", "parallel"), "attn_neighbour"),
        name="attn_neighbour",
    )(q, k, v, kc, vc, bias)


def _neighbour_bias(rpb):
    w = jnp.arange(GRID_W)
    cs = jnp.clip(w - NA_KW // 2, 0, GRID_W - NA_KW)
    col = jnp.arange(GRID_W)
    valid = (col[None, :] >= cs[:, None]) & (col[None, :] < cs[:, None] + NA_KW)
    dc = jnp.clip(col[None, :] - w[:, None] + (NA_KW - 1), 0, 2 * NA_KW - 2)
    pick_c = (jnp.arange(2 * NA_KW - 1)[:, None, None] == dc[None]).astype(F32)
    by_col = jnp.einsum('hrc,cwk->hwrk', rpb.astype(F32), pick_c, precision=lax.Precision.HIGHEST)
    by_col = jnp.where(valid[None, :, None, :], by_col, NEG_INF)
    flat = by_col.reshape(N_HEADS * GRID_W, (2 * NA_KH - 1) * GRID_W)
    return jnp.stack([flat[:, (NA_KH - 1 - o) * GRID_W:(2 * NA_KH - 1 - o) * GRID_W] for o in range(NA_KH)])


def _ctx_attn_kernel(sink_ref, aq_ref, ak_ref, av_ref, bq_ref, bk_ref, bv_ref, dq_ref, dk_ref, dv_ref,
                     ya_ref, yb_ref, yd_ref, *, n):
    s = _dot_nt(_stack_gqa(aq_ref[0]), ak_ref[0])
    acc = _dot(jnp.exp2(s - jnp.max(s, axis=-1, keepdims=True)).astype(BF16), av_ref[0])
    ya_ref[0] = _unstack_gqa(acc[:, :LANES] / acc[:, LANES:], n).astype(ya_ref.dtype)
    s = _dot_nt(_stack_gqa(bq_ref[0]), bk_ref[0])
    sink = _sink_column(sink_ref, n) * LOG2_E
    m = jnp.maximum(jnp.max(s, axis=-1, keepdims=True), sink)
    acc = _dot(jnp.exp2(s - m).astype(BF16), bv_ref[0])
    yb_ref[0] = _unstack_gqa(acc[:, :LANES] / (acc[:, LANES:] + jnp.exp2(sink - m)), n).astype(yb_ref.dtype)
    s = _dot_nt(_stack_mha(dq_ref[0]), dk_ref[0])
    p = jnp.exp(s - jnp.max(s, axis=-1, keepdims=True))
    o = _dot(p.astype(BF16), dv_ref[0]) / jnp.sum(p, axis=-1, keepdims=True)
    yd_ref[0] = _unstack_mha(o, n).astype(yd_ref.dtype)


def _ctx_attention(sink, aq, ak, av, bq, bk, bv, dq, dk, dv):
    b, n, _ = aq.shape
    spec = lambda wd: pl.BlockSpec((1, n, wd), lambda bi: (bi, 0, 0))
    return pl.pallas_call(
        functools.partial(_ctx_attn_kernel, n=n),
        grid=(b,),
        in_specs=[pl.BlockSpec(memory_space=pltpu.SMEM),
                  spec(GROUP_W), spec(KV_W), spec(2 * LANES), spec(GROUP_W), spec(KV_W), spec(2 * LANES),
                  spec(GROUP_W), spec(GROUP_W), spec(GROUP_W)],
        out_specs=[spec(GROUP_W)] * 3,
        out_shape=[jax.ShapeDtypeStruct((b, n, GROUP_W), BF16)] * 3,
        compiler_params=_params(("parallel",), "ctx_attention"),
        name="ctx_attention",
    )(sink, aq, ak, av, bq, bk, bv, dq, dk, dv)


def _log_sigmoid(x):
    return jnp.minimum(x, 0.0) - jnp.log1p(jnp.exp(-jnp.abs(x)))


def _split3(x):
    x1 = x.astype(BF16)
    r1 = x - x1.astype(F32)
    x2 = r1.astype(BF16)
    return x1, x2, (r1 - x2.astype(F32)).astype(BF16)


def _exact_dot_01(sel, x):
    return sum(_dot(sel, piece) for piece in _split3(x))


def _exact_dot_01_rhs(x, sel):
    return sum(_dot(piece, sel) for piece in _split3(x))


def _mlstm_chunk(q, k, v, gates, expand, tri, neg_mask, ones_cat, block, head_rows, s_ref, n_ref, m_ref, *,
                 reverse, L):
    ge = _exact_dot_01_rhs(gates, expand)
    li = ge[:, :GROUP_W]
    cum = _exact_dot_01(tri, ge[:, GROUP_W:])
    a_t = (li - cum).T
    m_prev = m_ref[0:1, :]
    head_b = [head_rows[h].astype(BF16) for h in range(N_HEADS)]
    k_t = k.astype(F32).T.astype(BF16)
    mu = jnp.zeros((L, GROUP_W), F32)
    scores = []
    for h in range(N_HEADS):
        lane0 = h * HEAD_DIM
        a_m = a_t[lane0:lane0 + 1, :] + neg_mask
        mu_h = jnp.maximum(jnp.max(a_m, axis=-1, keepdims=True), m_prev[:, lane0:lane0 + 1])
        qk = _dot(q * head_b[h], k_t)
        scores.append((qk * jnp.exp2(a_m - mu_h)).astype(BF16))
        mu = mu + mu_h * head_rows[h]
    s_cat = jnp.concatenate(scores, axis=1)
    v_cat = jnp.concatenate([v * hb for hb in head_b], axis=0)
    w_inter = jnp.exp2(m_prev - mu)
    num = _dot(s_cat, v_cat) + w_inter * _dot(q, s_ref[...].astype(BF16))
    den = _dot(s_cat, ones_cat) + w_inter * _dot(q, n_ref[...].astype(BF16))
    h_out = num / jnp.maximum(jnp.abs(den), jnp.exp2(-(cum + mu)))

    end_row = 0 if reverse else L - 1
    cum_end = cum[end_row:end_row + 1, :]
    log_end = cum_end - cum + li
    m_new = jnp.maximum(cum_end + m_prev, jnp.max(log_end, axis=0, keepdims=True))
    decay = jnp.exp2(cum_end + m_prev - m_new)
    w_end = jnp.exp2(log_end - m_new)
    upd =_dot(k_t, jnp.concatenate([(v.astype(F32) * w_end).astype(BF16), w_end.astype(BF16)], axis=1))
    s_ref[...] = (s_ref[...] * decay + upd[:, :GROUP_W]) * block
    n_ref[...] = (n_ref[...] * decay + upd[:, GROUP_W:]) * block
    m_ref[...] = jnp.broadcast_to(m_new, m_ref.shape)
    return h_out


def _mlstm_kernel(qf_ref, kf_ref, vf_ref, gf_ref, qb_ref, kb_ref, vb_ref, gb_ref,
                  expand_ref, tri_ref, neg_ref, ones_ref, block_ref, heads_ref, s0_ref, n0_ref, m0_ref,
                  hf_ref, hb_ref, s_out, n_out, m_out, s_ref, n_ref, m_ref, *, chunk, chunks_per_step):
    @pl.when(pl.program_id(1) == 0)
    def _():
        s_ref[...] = s0_ref[0]
        n_ref[...] = n0_ref[0]
        m_ref[...] = m0_ref[0]

    sides = ((qf_ref, kf_ref, vf_ref, gf_ref, hf_ref), (qb_ref, kb_ref, vb_ref, gb_ref, hb_ref))
    for c in range(chunks_per_step):
        for d, (q_ref, k_ref, v_ref, g_ref, h_ref) in enumerate(sides):
            rows = pl.ds((chunks_per_step - 1 - c if d else c) * chunk, chunk)
            h = _mlstm_chunk(q_ref[0, rows, :], k_ref[0, rows, :], v_ref[0, rows, :], g_ref[0, rows, :],
                             expand_ref[d], tri_ref[d], neg_ref[d], ones_ref[...], block_ref[...], heads_ref,
                             s_ref.at[d], n_ref.at[d], m_ref.at[d], reverse=bool(d), L=chunk)
            h_ref[0, rows, :] = h.astype(h_ref.dtype)
    s_out[0] = s_ref[...]
    n_out[0] = n_ref[...]
    m_out[0] = m_ref[...]


def _mlstm_consts(L):
    lane = jnp.arange(LANES)[:, None]
    col = jnp.arange(2 * GROUP_W)[None, :]
    col_head = (col % GROUP_W) // HEAD_DIM
    expand = jnp.stack([(lane == 8 * d + 4 * (col // GROUP_W) + col_head) for d in range(2)]).astype(BF16)
    t_idx = jnp.arange(L)[:, None]
    s_idx = jnp.arange(L)[None, :]
    seen = jnp.stack([s_idx <= t_idx, s_idx >= t_idx])
    head = jnp.arange(GROUP_W) // HEAD_DIM
    ones_cat = (jnp.repeat(jnp.arange(N_HEADS), L)[:, None] == head[None, :]).astype(BF16)
    block = (head[:, None] == head[None, :]).astype(F32)
    head_rows = (jnp.arange(N_HEADS)[:, None, None] == head[None, None, :]).astype(F32)
    return expand, seen.astype(BF16), jnp.where(seen, 0.0, NEG_INF).astype(F32), ones_cat, block, head_rows


def _mlstm_scan(q, k, v, gates, state, consts):
    b, t, _ = q.shape
    L = MLSTM_L
    per_step = MLSTM_CHUNKS_PER_STEP if (t // L) % MLSTM_CHUNKS_PER_STEP == 0 else 1
    nc = t // (L * per_step)
    fwd = lambda wd: pl.BlockSpec((1, L * per_step, wd), lambda bi, j: (bi, j, 0))
    bwd = lambda wd: pl.BlockSpec((1, L * per_step, wd), lambda bi, j: (bi, nc - 1 - j, 0))
    st = lambda r: pl.BlockSpec((1, 2, r, GROUP_W), lambda bi, j: (bi, 0, 0, 0))
    tok_specs = [fwd(GROUP_W), fwd(GROUP_W), fwd(GROUP_W), fwd(LANES),
                 bwd(GROUP_W), bwd(GROUP_W), bwd(GROUP_W), bwd(LANES)]
    hf, hb, s_fin, n_fin, m_fin = pl.pallas_call(
        functools.partial(_mlstm_kernel, chunk=L, chunks_per_step=per_step),
        grid=(b, nc),
        in_specs=tok_specs + [_const_spec(c.shape) for c in consts] + [st(GROUP_W), st(GROUP_W), st(8)],
        out_specs=[fwd(GROUP_W), bwd(GROUP_W), st(GROUP_W), st(GROUP_W), st(8)],
        out_shape=[jax.ShapeDtypeStruct((b, t, GROUP_W), BF16), jax.ShapeDtypeStruct((b, t, GROUP_W), BF16),
                   jax.ShapeDtypeStruct((b, 2, GROUP_W, GROUP_W), F32),
                   jax.ShapeDtypeStruct((b, 2, GROUP_W, GROUP_W), F32),
                   jax.ShapeDtypeStruct((b, 2, 8, GROUP_W), F32)],
        scratch_shapes=[pltpu.VMEM((2, GROUP_W, GROUP_W), F32), pltpu.VMEM((2, GROUP_W, GROUP_W), F32),
                        pltpu.VMEM((2, 8, GROUP_W), F32)],
        compiler_params=_params(("parallel", "arbitrary"), "mlstm"),
        name="mlstm",
    )(q, k, v, gates, q, k, v, gates, *consts, *state)
    return hf, hb, (s_fin, n_fin, m_fin)


def _outproj_kernel(x_ref, ya_ref, yb_ref, hf_ref, hb_ref, og_ref, yd_ref, mod_ref, gg_ref, w_ref, o_ref, *, d_model):
    d = d_model
    yc = jax.nn.sigmoid(og_ref[0]) * (hf_ref[0].astype(F32) + hb_ref[0].astype(F32))
    parts = []
    for i, y in enumerate((ya_ref[0], yb_ref[0], yc, yd_ref[0])):
        parts.append((_rms_rows(y.astype(F32)) * gg_ref[i:i + 1, :]).astype(BF16))
    res = _dot(jnp.concatenate(parts, axis=1), w_ref[...])
    o_ref[0] = x_ref[0] + mod_ref[0][:, 2 * d:3 * d] * res


def _out_proj(x, ya, yb, hf, hb, o_gate, yd, mod, mod_row, gg, w, *, layer, tm):
    b, t, d = x.shape
    tokd = pl.BlockSpec((1, tm, d), lambda bi, j: (bi, j, 0))
    tokg = pl.BlockSpec((1, tm, GROUP_W), lambda bi, j: (bi, j, 0))
    return pl.pallas_call(
        functools.partial(_outproj_kernel, d_model=d),
        grid=(b, t // tm),
        in_specs=[tokd, tokg, tokg, tokg, tokg, tokg, tokg,
                  pl.BlockSpec((1, 1, 6 * d), lambda bi, j: (mod_row(bi), 0, 0)),
                  _const_spec(gg.shape), _layer_spec(w.shape, layer)],
        out_specs=tokd,
        out_shape=jax.ShapeDtypeStruct((b, t, d), F32),
        compiler_params=_params(("parallel", "parallel"), "out_proj", n_inputs=10, fusible=(9,)),
        name="out_proj",
    )(x, ya, yb, hf, hb, o_gate, yd, mod, gg, w)


def _ffn_kernel(*refs, d_model, d_ff, tm, final_norm):
    if final_norm:
        x_ref, xp_ref, xn_ref, mod_ref, g2_ref, wu_ref, cw_ref, wd_ref, gf_ref, o_ref = refs
    else:
        x_ref, xp_ref, xn_ref, mod_ref, g2_ref, wu_ref, cw_ref, wd_ref, o_ref = refs
    d = d_model
    j = pl.program_id(1)
    mod = mod_ref[0]
    g2 = g2_ref[...]

    def norm_mod(x):
        return _rms_rows(x) * g2 * (1.0 + mod[:, 4 * d:5 * d]) + mod[:, 3 * d:4 * d]

    keep_prev = (j > 0).astype(F32)
    keep_next = (j < pl.num_programs(1) - 1).astype(F32)
    x = x_ref[0]
    h_ext = jnp.concatenate([norm_mod(xp_ref[0]) * keep_prev, norm_mod(x), norm_mod(xn_ref[0]) * keep_next],
                            axis=0).astype(BF16)
    n_ext = tm + 2 * HALO

    u = _dot(h_ext, wu_ref[...])
    cw = cw_ref[...]
    u = (pltpu.roll(u, 1, 0) * cw[0:1] + u * cw[1:2] + pltpu.roll(u, n_ext - 1, 0) * cw[2:3] + cw[3:4])[HALO:HALO + tm]
    gate, val = u[:, :d_ff], u[:, d_ff:]
    act = (gate * jax.nn.sigmoid(gate) * val).astype(BF16)
    y = x + mod[:, 5 * d:6 * d] * _dot(act, wd_ref[...])
    if final_norm:
        y = _rms_rows(y) * gf_ref[...]
    o_ref[0] = y


def _conv_ffn(x, mod, mod_row, g2, wu, cw, wd, g_final, *, layer, tm):
    b, t, d = x.shape
    d_ff = wd.shape[1]
    final_norm = g_final is not None
    hb = tm // HALO
    last = t // HALO - 1
    row = pl.BlockSpec((1, d), lambda bi, j: (0, 0))
    in_specs = [pl.BlockSpec((1, tm, d), lambda bi, j: (bi, j, 0)),
                pl.BlockSpec((1, HALO, d), lambda bi, j: (bi, jnp.maximum(j * hb - 1, 0), 0)),
                pl.BlockSpec((1, HALO, d), lambda bi, j: (bi, jnp.minimum((j + 1) * hb, last), 0)),
                pl.BlockSpec((1, 1, 6 * d), lambda bi, j: (mod_row(bi), 0, 0)),
                row,
                _layer_spec(wu.shape, layer), _const_spec(cw.shape), _layer_spec(wd.shape, layer)]
    args = [x, x, x, mod, g2, wu, cw, wd]
    if final_norm:
        in_specs.append(row)
        args.append(g_final)
    return pl.pallas_call(
        functools.partial(_ffn_kernel, d_model=d, d_ff=d_ff, tm=tm, final_norm=final_norm),
        grid=(b, t // tm),
        in_specs=in_specs,
        out_specs=pl.BlockSpec((1, tm, d), lambda bi, j: (bi, j, 0)),
        out_shape=jax.ShapeDtypeStruct((b, t, d), F32),
        compiler_params=_params(("parallel", "parallel"), "conv_ffn", n_inputs=len(args), fusible=(5, 7)),
        name="conv_ffn",
    )(*args)


_Q_HEAD_ORDER = (0, 2, 1, 3)


def _permute_heads(w, axis):
    parts = [lax.slice_in_dim(w, h * HEAD_DIM, (h + 1) * HEAD_DIM, axis=axis) for h in _Q_HEAD_ORDER]
    return jnp.concatenate(parts, axis=axis)


def _layout_w_in(w):
    return w.astype(BF16)


def _layout_w_out(w):
    w = w.astype(BF16)
    return jnp.concatenate([_permute_heads(w[:, 0:GROUP_W], 1), _permute_heads(w[:, GROUP_W:2 * GROUP_W], 1),
                            w[:, 2 * GROUP_W:]], axis=1)


def _rope_tables(t):
    pos = jnp.arange(t)
    row = (pos // GRID_W).astype(F32)
    col = (pos % GRID_W).astype(F32)
    n_freq = HEAD_DIM // 4
    freqs = ROPE_THETA ** (-jnp.arange(n_freq, dtype=F32) / n_freq)
    ang_r, ang_c = row[:, None] * freqs, col[:, None] * freqs
    cos = jnp.concatenate([jnp.cos(ang_r), jnp.cos(ang_r), jnp.cos(ang_c), jnp.cos(ang_c)], axis=1)
    sin = jnp.concatenate([-jnp.sin(ang_r), jnp.sin(ang_r), -jnp.sin(ang_c), jnp.sin(ang_c)], axis=1)
    reps = LANES // HEAD_DIM
    return jnp.tile(cos, (1, reps)), jnp.tile(sin, (1, reps))


def _pick_tile(t, pref):
    while t % pref:
        pref //= 2
    return pref


def kernel(x, c, ctx, c_ctx, w_mod, b_mod, g_norm1, g_norm2, w_in, a_q_gain, a_k_gain, b_sink, c_gate_bias,
           d_rel_bias, g_group, w_out, w_up, conv_w, conv_b, w_down, g_final):
    batch, t, d = x.shape
    n_ctx = ctx.shape[1]
    depth = w_in.shape[0]
    d_ff = w_down.shape[1]
    assert batch < 8 and d_ff % LANES == 0 and t % MLSTM_L == 0 and n_ctx % MLSTM_L == 0

    c_rows = jnp.zeros((8, d), F32).at[:batch].set(c).at[batch].set(c_ctx)
    mod_all = _modulation(c_rows, w_mod, b_mod).reshape(depth, 8, 1, 6 * d)
    lat_row = lambda bi: bi
    ctx_row = lambda bi: batch

    cos, sin = _rope_tables(t)
    pool = jnp.where(jnp.arange(GROUP_W)[:, None] // HEAD_DIM == jnp.arange(GROUP_W)[None, :] // HEAD_DIM,
                     1.0 / HEAD_DIM, 0.0).astype(BF16)
    tm_lat = _pick_tile(t, TOKEN_TILE)
    tm_ctx = _pick_tile(n_ctx, CTX_TILE)
    zero_state = (jnp.zeros((batch, 2, GROUP_W, GROUP_W), F32), jnp.zeros((batch, 2, GROUP_W, GROUP_W), F32),
                  jnp.zeros((batch, 2, 8, GROUP_W), F32))
    mlstm_consts = _mlstm_consts(MLSTM_L)
    w_in_b = _layout_w_in(w_in)
    w_out_b = _layout_w_out(w_out)
    w_up_b = w_up.astype(BF16)
    w_down_b = w_down.astype(BF16)

    for layer in range(depth):
        need_ctx = layer < depth - 1
        mod = mod_all[layer]
        g1 = g_norm1[layer][None]
        g2 = g_norm2[layer][None]
        gq = jnp.tile(a_q_gain[layer], N_HEADS)[None]
        gk = jnp.tile(a_k_gain[layer], 2)[None]
        gbias = jnp.pad(c_gate_bias[layer], (0, LANES - 4 * N_HEADS))[None]
        gg = g_group[layer].reshape(N_HEADS, GROUP_W)
        gg = jnp.concatenate([_permute_heads(gg[0:2], 1), gg[2:4]], axis=0)
        conv_l = jnp.concatenate([conv_w[layer], conv_b[layer][None],
                                  jnp.zeros((8 - 1 - conv_w.shape[1], 2 * d_ff), F32)], axis=0)
        sink = b_sink[layer]
        bias_tab = _neighbour_bias(d_rel_bias[layer])

        proj = functools.partial(_in_proj, g1=g1, w=w_in_b, pool=pool, gq=gq, gk=gk, gbias=gbias, layer=layer)
        (aqc, akc, avc, bqc, bkc, bvc, cqc, ckc, cvc, coc, cgc, dqc, dkc, dvc) = proj(
            ctx, mod, ctx_row, cos=cos, sin=sin, rope=False, tm=tm_ctx)
        (aq, ak, av, bq, bk, bv, cq, ck, cv, co, cg, dq, dk, dv) = proj(
            x, mod, lat_row, cos=cos, sin=sin, rope=True, tm=_pick_tile(t, 2 * TOKEN_TILE))

        score_bound = 1.02 * LOG2_E * HEAD_DIM ** 0.5 * jnp.max(jnp.abs(a_q_gain[layer])) * jnp.max(jnp.abs(a_k_gain[layer]))
        safe = (score_bound <= MAX_UNSHIFTED_LOG2_SCORE).astype(jnp.int32).reshape(1)
        ya = _attn_global(safe, aq, ak, av, akc, avc, tq=_pick_tile(t, ATTN_Q_TILE), tk=_pick_tile(t, ATTN_K_TILE))
        yb = _attn_window(sink, bq, bk, bv, bkc, bvc, tq=_pick_tile(t, WINDOW_Q_TILE))
        yd = _attn_neighbour(dq, dk, dv, dkc, dvc, bias_tab, rows_per_step=NEIGHBOUR_ROWS)

        hcf, hcb, ctx_state = _mlstm_scan(cqc, ckc, cvc, cgc, zero_state, mlstm_consts)
        hf, hb, _ = _mlstm_scan(cq, ck, cv, cg, ctx_state, mlstm_consts)

        x = _out_proj(x, ya, yb, hf, hb, co, yd, mod, lat_row, gg, w_out_b, layer=layer,
                      tm=_pick_tile(t, 2 * TOKEN_TILE))
        ffn = functools.partial(_conv_ffn, g2=g2, wu=w_up_b, cw=conv_l, wd=w_down_b, layer=layer)
        x = ffn(x, mod, lat_row, g_final=None if need_ctx else g_final[None], tm=tm_lat)
        if need_ctx:
            yac, ybc, ydc = _ctx_attention(sink, aqc, akc, avc, bqc, bkc, bvc, dqc, dkc, dvc)
            ctx = _out_proj(ctx, yac, ybc, hcf, hcb, coc, ydc, mod, ctx_row, gg, w_out_b, layer=layer, tm=tm_ctx)
            ctx = ffn(ctx, mod, ctx_row, g_final=None, tm=tm_ctx)
    return x
```
